```python
import jax
import jax.numpy as jnp
from jax import lax
import numpy as np

D_MODEL = 2048
BATCH = 2
SEQ = 8192
DEPTH = 4

GRID_W = 64
CTX_LEN = 256
N_MIXERS = 3
N_RET_LAYERS = (DEPTH + 2) // 3
N_WIN_LAYERS = (DEPTH + 1) // 3
N_FNO_LAYERS = DEPTH // 3
RET_HEADS = 8
RET_DK = D_MODEL // RET_HEADS
RET_DV = 2 * RET_DK
RET_CHUNK = 128
RET_IN_WIDTH = 2 * RET_HEADS * RET_DK + 3 * RET_HEADS * RET_DV
WIN_HEADS = 16
WIN_KV_HEADS = 4
WIN_HEAD_DIM = D_MODEL // WIN_HEADS
WINDOW = 128
WIN_BLOCK = 128
FOURIER_GROUPS = 4
N_EXPERTS = 16
EXPERT_FF = D_MODEL // 2
CAPACITY_FACTOR = 2
ROPE_BASE = 10000.0
NORM_EPS = 1e-6
NEG_INF = -1e30
F32 = jnp.float32

kernel_name = 'hybrid_retention_window_fourier_ec_moe_dit'


def rms_norm(x, gain):
    x32 = x.astype(F32)
    y = x32 * lax.rsqrt(jnp.mean(x32 * x32, axis=-1, keepdims=True) + NORM_EPS)
    return (y * gain.astype(F32)).astype(x.dtype)


def modulate(x, gain, shift, scale):
    return rms_norm(x, gain) * (1 + scale[..., None, :]) + shift[..., None, :]


def rope_1d(x, pos):
    half = x.shape[-1] // 2
    inv = ROPE_BASE ** (-jnp.arange(half, dtype=F32) / half)
    ang = pos.astype(F32)[:, None] * inv[None, :]
    cos, sin = jnp.cos(ang)[:, None, :], jnp.sin(ang)[:, None, :]
    x32 = x.astype(F32)
    x1, x2 = x32[..., :half], x32[..., half:]
    return jnp.concatenate([x1 * cos - x2 * sin, x1 * sin + x2 * cos], axis=-1).astype(x.dtype)


def axial_rope(x, row, col):
    half = x.shape[-1] // 2
    return jnp.concatenate([rope_1d(x[..., :half], row), rope_1d(x[..., half:], col)], axis=-1)


def group_norm(o):
    mu = jnp.mean(o, axis=-1, keepdims=True)
    var = jnp.mean(jnp.square(o - mu), axis=-1, keepdims=True)
    return (o - mu) * lax.rsqrt(var + NORM_EPS)


def retention_scan(q, k, v, log_gamma, state0):
    b, l, h, _ = q.shape
    dv = v.shape[-1]
    n = l // RET_CHUNK

    def chunks(t):
        return t.reshape(b, n, RET_CHUNK, h, t.shape[-1]).transpose(1, 0, 3, 2, 4)

    pos = jnp.arange(RET_CHUNK, dtype=F32)
    lg = log_gamma[:, None]
    q_decay = jnp.exp(lg * (pos + 1.0))
    k_decay = jnp.exp(lg * (RET_CHUNK - 1.0 - pos))
    chunk_decay = jnp.exp(log_gamma * RET_CHUNK)
    diff = pos[:, None] - pos[None, :]
    intra = jnp.where(diff >= 0, jnp.exp(lg[:, :, None] * jnp.maximum(diff, 0.0)), 0.0)

    def step(s, qkv):
        qc, kc, vc = qkv
        cross = jnp.einsum('bhid,bhde->bhie', qc * q_decay[None, :, :, None], s)
        scores = jnp.einsum('bhid,bhjd->bhij', qc, kc) * intra[None]
        local = jnp.einsum('bhij,bhje->bhie', scores, vc)
        s = s * chunk_decay[None, :, None, None] + jnp.einsum('bhjd,bhje->bhde', kc * k_decay[None, :, :, None], vc)
        return s, cross + local

    s_final, out = lax.scan(step, state0, (chunks(q), chunks(k), chunks(v)))
    return out.transpose(1, 0, 3, 2, 4).reshape(b, l, h, dv), s_final


def retention_mixer(h_ctx, h_lat, w_in, w_out, decay_param, row, col, with_ctx):
    b = h_lat.shape[0]
    hk, hv = RET_HEADS * RET_DK, RET_HEADS * RET_DV
    log_gamma = -jnp.exp(decay_param.astype(F32))

    def project(hh, rotate):
        l = hh.shape[1]
        q, k, v, g_f, g_b = jnp.split(hh @ w_in, [hk, 2 * hk, 2 * hk + hv, 2 * hk + 2 * hv], axis=-1)
        q = q.reshape(b, l, RET_HEADS, RET_DK)
        k = k.reshape(b, l, RET_HEADS, RET_DK) * (RET_DK ** -0.5)
        if rotate:
            q, k = axial_rope(q, row, col), axial_rope(k, row, col)
        v = v.reshape(b, l, RET_HEADS, RET_DV)
        return (q.astype(F32), k.astype(F32), v.astype(F32),
                g_f.reshape(b, l, RET_HEADS, RET_DV), g_b.reshape(b, l, RET_HEADS, RET_DV))

    qc, kc, vc, gfc, gbc = project(h_ctx, False)
    ql, kl, vl, gfl, gbl = project(h_lat, True)
    zeros = jnp.zeros((b, RET_HEADS, RET_DK, RET_DV), F32)
    flip = lambda t: jnp.flip(t, axis=1)
    oc_f, sc_f = retention_scan(qc, kc, vc, log_gamma[0], zeros)
    ol_f, _ = retention_scan(ql, kl, vl, log_gamma[0], sc_f)
    oc_b, sc_b = retention_scan(flip(qc), flip(kc), flip(vc), log_gamma[1], zeros)
    ol_b, _ = retention_scan(flip(ql), flip(kl), flip(vl), log_gamma[1], sc_b)

    def merge(o_f, o_b, g_f, g_b, dtype):
        y = (group_norm(o_f) * jax.nn.silu(g_f.astype(F32))
             + group_norm(o_b) * jax.nn.silu(g_b.astype(F32)))
        return y.reshape(y.shape[0], y.shape[1], hv).astype(dtype) @ w_out

    y_lat = merge(ol_f, flip(ol_b), gfl, gbl, h_lat.dtype)
    y_ctx = merge(oc_f, flip(oc_b), gfc, gbc, h_ctx.dtype) if with_ctx else None
    return y_ctx, y_lat


def window_attention_mixer(h_ctx, h_lat, w_qkv, w_o, sink, row, col, with_ctx):
    b, l, _ = h_lat.shape
    kv, hd = WIN_KV_HEADS, WIN_HEAD_DIM
    g = WIN_HEADS // kv
    scale = hd ** -0.5

    def project(hh):
        n = hh.shape[1]
        q, k, v = jnp.split(hh @ w_qkv, [WIN_HEADS * hd, (WIN_HEADS + kv) * hd], axis=-1)
        return q.reshape(b, n, WIN_HEADS, hd), k.reshape(b, n, kv, hd), v.reshape(b, n, kv, hd)

    qc, kc, vc = project(h_ctx)
    ql, kl, vl = project(h_lat)
    ql, kl = axial_rope(ql, row, col), axial_rope(kl, row, col)
    sink_logit = sink.astype(F32).reshape(kv, g)

    def softmax_with_sink(logits):
        sk = jnp.broadcast_to(sink_logit[:, :, None, None], logits.shape[:-1] + (1,))
        return jax.nn.softmax(jnp.concatenate([logits, sk], axis=-1), axis=-1)[..., :-1]

    nb = l // WIN_BLOCK
    pad = ((0, 0), (WIN_BLOCK, WIN_BLOCK), (0, 0), (0, 0))
    kb = jnp.pad(kl, pad).reshape(b, nb + 2, WIN_BLOCK, kv, hd)
    vb = jnp.pad(vl, pad).reshape(b, nb + 2, WIN_BLOCK, kv, hd)
    band = lambda t: jnp.concatenate([t[:, :-2], t[:, 1:-1], t[:, 2:]], axis=2).transpose(1, 0, 2, 3, 4)
    k_band, v_band = band(kb), band(vb)
    q_blocks = ql.reshape(b, nb, WIN_BLOCK, kv, g, hd).transpose(1, 0, 2, 3, 4, 5)
    qpos = jnp.arange(nb)[:, None] * WIN_BLOCK + jnp.arange(WIN_BLOCK)[None, :]
    kpos = (jnp.arange(nb)[:, None] - 1) * WIN_BLOCK + jnp.arange(3 * WIN_BLOCK)[None, :]
    kp = kpos[:, None, :]
    valid = (jnp.abs(kp - qpos[:, :, None]) <= WINDOW) & (kp >= 0) & (kp < l)
    bias = jnp.where(valid, 0.0, NEG_INF).astype(F32)
    n_ctx = kc.shape[1]

    def block(args):
        qb, kbd, vbd, bb = args
        s_ctx = jnp.einsum('bqkgd,bckd->bkgqc', qb, kc).astype(F32) * scale
        s_loc = jnp.einsum('bqkgd,bskd->bkgqs', qb, kbd).astype(F32) * scale + bb
        p = softmax_with_sink(jnp.concatenate([s_ctx, s_loc], axis=-1))
        p_ctx, p_loc = p[..., :n_ctx].astype(vc.dtype), p[..., n_ctx:].astype(vbd.dtype)
        return jnp.einsum('bkgqc,bckd->bqkgd', p_ctx, vc) + jnp.einsum('bkgqs,bskd->bqkgd', p_loc, vbd)

    o = lax.map(block, (q_blocks, k_band, v_band, bias))
    y_lat = o.transpose(1, 0, 2, 3, 4, 5).reshape(b, l, WIN_HEADS * hd) @ w_o
    y_ctx = None
    if with_ctx:
        qcg = qc.reshape(b, n_ctx, kv, g, hd)
        p = softmax_with_sink(jnp.einsum('bqkgd,bckd->bkgqc', qcg, kc).astype(F32) * scale).astype(vc.dtype)
        y_ctx = jnp.einsum('bkgqc,bckd->bqkgd', p, vc).reshape(b, n_ctx, WIN_HEADS * hd) @ w_o
    return y_ctx, y_lat


def fourier_mixer(h_ctx, h_lat, w_o, with_ctx):
    def mix(hh):
        b, n, d = hh.shape
        z = hh.astype(F32).reshape(b, n, FOURIER_GROUPS, d // FOURIER_GROUPS)
        z = jnp.fft.fftn(z, axes=(1, 3), norm='ortho').real
        return z.reshape(b, n, d).astype(hh.dtype) @ w_o

    return (mix(h_ctx) if with_ctx else None), mix(h_lat)


def ec_moe(h, w_router, w_gate, w_up, w_down):
    b, n, d = h.shape
    cap = CAPACITY_FACTOR * n // N_EXPERTS
    aff = jax.nn.softmax((h @ w_router).astype(F32), axis=-1)
    gate, idx = lax.top_k(jnp.swapaxes(aff, 1, 2), cap)
    xe = jax.vmap(lambda hb, ib: hb[ib])(h, idx)
    a = jnp.einsum('becd,edf->becf', xe, w_gate)
    u = jnp.einsum('becd,edf->becf', xe, w_up)
    ye = jnp.einsum('becf,efd->becd', jax.nn.silu(a) * u, w_down) * gate[..., None].astype(h.dtype)
    return jax.vmap(lambda yb, ib: jnp.zeros((n, d), yb.dtype).at[ib.reshape(-1)].add(yb.reshape(-1, d)))(ye, idx)


def setup_inputs(seed: int = 0) -> dict:
    key = jax.random.key(seed)
    ks = jax.random.split(key, 19)
    nrm = lambda k, shape, s: jax.random.normal(k, shape, F32) * s
    d, f = D_MODEL, EXPERT_FF
    gamma = 1.0 - jnp.power(2.0, -(5.0 + jnp.arange(RET_HEADS, dtype=F32)))
    decay_base = jnp.log(-jnp.log(gamma))
    return {
        'x': nrm(ks[0], (BATCH, SEQ, d), 1.0),
        'c': nrm(ks[1], (BATCH, d), 1.0),
        'ctx': nrm(ks[2], (BATCH, CTX_LEN, d), 1.0),
        'c_ctx': nrm(ks[3], (d,), 1.0),
        'w_mod': nrm(ks[4], (DEPTH, d, 6 * d), 0.5 * d ** -0.5),
        'b_mod': nrm(ks[5], (DEPTH, 6 * d), 0.02),
        'norm_gain': 1.0 + nrm(ks[6], (DEPTH, 2, d), 0.02),
        'final_gain': 1.0 + nrm(ks[7], (d,), 0.02),
        'ret_w_in': nrm(ks[8], (N_RET_LAYERS, d, RET_IN_WIDTH), d ** -0.5),
        'ret_w_out': nrm(ks[9], (N_RET_LAYERS, RET_HEADS * RET_DV, d), (RET_HEADS * RET_DV) ** -0.5),
        'ret_decay': decay_base[None, None, :] + nrm(ks[10], (N_RET_LAYERS, 2, RET_HEADS), 0.1),
        'win_w_qkv': nrm(ks[11], (N_WIN_LAYERS, d, (WIN_HEADS + 2 * WIN_KV_HEADS) * WIN_HEAD_DIM), d ** -0.5),
        'win_w_o': nrm(ks[12], (N_WIN_LAYERS, WIN_HEADS * WIN_HEAD_DIM, d), (WIN_HEADS * WIN_HEAD_DIM) ** -0.5),
        'win_sink': nrm(ks[13], (N_WIN_LAYERS, WIN_HEADS), 1.0),
        'fno_w_o': nrm(ks[14], (N_FNO_LAYERS, d, d), d ** -0.5),
        'router_w': nrm(ks[15], (DEPTH, d, N_EXPERTS), d ** -0.5),
        'exp_w_gate': nrm(ks[16], (DEPTH, N_EXPERTS, d, f), d ** -0.5),
        'exp_w_up': nrm(ks[17], (DEPTH, N_EXPERTS, d, f), d ** -0.5),
        'exp_w_down': nrm(ks[18], (DEPTH, N_EXPERTS, f, d), f ** -0.5),
    }


def reference(x, c, ctx, c_ctx, w_mod, b_mod, norm_gain, final_gain, ret_w_in, ret_w_out, ret_decay,
              win_w_qkv, win_w_o, win_sink, fno_w_o, router_w, exp_w_gate, exp_w_up, exp_w_down):
    l = x.shape[1]
    rows = l // GRID_W
    row = jnp.repeat(jnp.arange(rows), GRID_W)
    col = jnp.tile(jnp.arange(GRID_W), rows)
    silu_c, silu_cc = jax.nn.silu(c), jax.nn.silu(c_ctx)
    for i in range(DEPTH):
        kind, j = i % N_MIXERS, i // N_MIXERS
        last = i == DEPTH - 1
        sh1, sc1, g1, sh2, sc2, g2 = jnp.split(silu_c @ w_mod[i] + b_mod[i], 6, axis=-1)
        csh1, csc1, cg1, csh2, csc2, cg2 = jnp.split(silu_cc @ w_mod[i] + b_mod[i], 6, axis=-1)
        h_lat = modulate(x, norm_gain[i, 0], sh1, sc1)
        h_ctx = modulate(ctx, norm_gain[i, 0], csh1, csc1)
        if kind == 0:
            y_ctx, y_lat = retention_mixer(h_ctx, h_lat, ret_w_in[j], ret_w_out[j], ret_decay[j], row, col, not last)
        elif kind == 1:
            y_ctx, y_lat = window_attention_mixer(h_ctx, h_lat, win_w_qkv[j], win_w_o[j], win_sink[j], row, col, not last)
        else:
            y_ctx, y_lat = fourier_mixer(h_ctx, h_lat, fno_w_o[j], not last)
        x = x + g1[:, None, :] * y_lat
        h2 = modulate(x, norm_gain[i, 1], sh2, sc2)
        x = x + g2[:, None, :] * ec_moe(h2, router_w[i], exp_w_gate[i], exp_w_up[i], exp_w_down[i])
        if not last:
            ctx = ctx + cg1 * y_ctx
            h2c = modulate(ctx, norm_gain[i, 1], csh2, csc2)
            ctx = ctx + cg2 * ec_moe(h2c, router_w[i], exp_w_gate[i], exp_w_up[i], exp_w_down[i])
    return rms_norm(x, final_gain)
```

```python
import functools
import math

import numpy as np
import jax
import jax.numpy as jnp
from jax import lax
from jax.experimental import pallas as pl
from jax.experimental.pallas import tpu as pltpu

F32 = jnp.float32
BF16 = jnp.bfloat16
I32 = jnp.int32

GRID_W = 64
N_MIXERS = 3
RET_HEADS = 8
RET_CHUNK = 128
WIN_HEADS = 16
WIN_KV_HEADS = 4
WIN_BLOCK = 128
FOURIER_GROUPS = 4
FFT_INNER = 64
N_EXPERTS = 16
CAPACITY_FACTOR = 2
ROPE_BASE = 10000.0
NORM_EPS = 1e-6
NEG_INF = -1e30

LANES = 128
ROUTE_TILE = 128
SLOT_ALIGN = 16
MIN_CAP_ROWS = 256
ROW_TILE = 768
COL_TILE = 512
NORM_TILE = 256
VMEM_LIMIT_BYTES = 56 * 1024 * 1024


def _params(sem):
    return pltpu.CompilerParams(dimension_semantics=sem, vmem_limit_bytes=VMEM_LIMIT_BYTES)


def _dot(a, b):
    return jnp.dot(a, b, preferred_element_type=F32)


def _dot_nt(a, b):
    return lax.dot_general(a, b, (((1,), (1,)), ((), ())), preferred_element_type=F32)


def _split_bf16(x):
    hi = x.astype(BF16)
    lo = (x - hi.astype(F32)).astype(BF16)
    return hi, lo


def _silu(x):
    return x / (1.0 + jnp.exp(-x))


def _mod_kernel(c_ref, w_ref, b_ref, o_ref):
    s = _silu(c_ref[...])
    sh, sl = _split_bf16(s)
    wh, wl = _split_bf16(w_ref[0])
    o_ref[0] = _dot(sh, wh) + _dot(sl, wh) + _dot(sh, wl) + b_ref[0]


def _modulation(cvec, w_mod, b_mod):
    depth, d, n = w_mod.shape
    tn = 1024
    return pl.pallas_call(
        _mod_kernel,
        grid=(depth, n // tn),
        in_specs=[
            pl.BlockSpec((8, d), lambda i, j: (0, 0)),
            pl.BlockSpec((1, d, tn), lambda i, j: (i, 0, j)),
            pl.BlockSpec((1, 1, tn), lambda i, j: (i, 0, j)),
        ],
        out_specs=pl.BlockSpec((1, 8, tn), lambda i, j: (i, 0, j)),
        out_shape=jax.ShapeDtypeStruct((depth, 8, n), F32),
        compiler_params=_params(("parallel", "parallel")),
    )(cvec, w_mod, b_mod.reshape(depth, 1, n))


def _normed(x_ref, gain_ref, shift_ref, scale_ref):
    x = x_ref[0]
    ms = jnp.mean(x * x, axis=-1, keepdims=True)
    y = x * lax.rsqrt(ms + NORM_EPS) * gain_ref[...]
    return y * (1.0 + scale_ref[0, 0]) + shift_ref[0, 0]


def _norm_mod_kernel(x_ref, gain_ref, shift_ref, scale_ref, o_ref):
    o_ref[0] = _normed(x_ref, gain_ref, shift_ref, scale_ref).astype(o_ref.dtype)


def _norm_router_kernel(x_ref, gain_ref, shift_ref, scale_ref, wr_ref, o_ref, aff_ref):
    h = _normed(x_ref, gain_ref, shift_ref, scale_ref)
    o_ref[0] = h.astype(o_ref.dtype)
    hh, hl = _split_bf16(h)
    wh, wl = _split_bf16(wr_ref[...])
    logits = _dot(hh, wh) + _dot(hl, wh) + _dot(hh, wl)
    lane = lax.broadcasted_iota(I32, logits.shape, 1)
    valid = lane < N_EXPERTS
    logits = jnp.where(valid, logits, -jnp.inf)
    m = jnp.max(logits, axis=-1, keepdims=True)
    p = jnp.exp(logits - m)
    aff = p / jnp.sum(p, axis=-1, keepdims=True)
    aff_ref[0] = jnp.where(valid, aff, 0.0)


def _norm_mod(x, gain, shift, scale, n_lat, *, out_dtype=BF16, rows=None, router_w=None):
    b, s, d = x.shape
    rows = s if rows is None else rows
    tr = NORM_TILE
    lat_tiles = n_lat // tr
    region = lambda bi, t: (bi, jnp.where(t >= lat_tiles, 1, 0), 0, 0)
    in_specs = [
        pl.BlockSpec((1, tr, d), lambda bi, t: (bi, t, 0)),
        pl.BlockSpec((1, d), lambda bi, t: (0, 0)),
        pl.BlockSpec((1, 1, 1, d), region),
        pl.BlockSpec((1, 1, 1, d), region),
    ]
    args = [x, gain.reshape(1, d), shift, scale]
    out_specs = pl.BlockSpec((1, tr, d), lambda bi, t: (bi, t, 0))
    out_shape = jax.ShapeDtypeStruct((b, rows, d), out_dtype)
    kern = _norm_mod_kernel
    if router_w is not None:
        wr = jnp.zeros((d, LANES), F32).at[:, :N_EXPERTS].set(router_w)
        in_specs.append(pl.BlockSpec((d, LANES), lambda bi, t: (0, 0)))
        args.append(wr)
        out_specs = [out_specs, pl.BlockSpec((1, tr, LANES), lambda bi, t: (bi, t, 0))]
        out_shape = [out_shape, jax.ShapeDtypeStruct((b, rows, LANES), F32)]
        kern = _norm_router_kernel
    return pl.pallas_call(
        kern,
        grid=(b, rows // tr),
        in_specs=in_specs,
        out_specs=out_specs,
        out_shape=out_shape,
        compiler_params=_params(("parallel", "parallel")),
    )(*args)


def _rope_partner(xs, quarter):
    if 2 * quarter == LANES:
        return pltpu.roll(xs, quarter, 1)
    back = pltpu.roll(xs, quarter, 1)
    fwd = pltpu.roll(xs, LANES - quarter, 1)
    lane = lax.broadcasted_iota(I32, xs.shape, 1)
    return jnp.where((lane % (2 * quarter)) < quarter, fwd, back)


def _mm_rope_kernel(a_ref, w_ref, cos_ref, sin_ref, o_ref, *, n_rope, n_q, kscale, head_dim):
    j = pl.program_id(2)
    acc = _dot(a_ref[0], w_ref[...])

    @pl.when(j >= n_rope)
    def _():
        o_ref[0] = acc.astype(o_ref.dtype)

    @pl.when(j < n_rope)
    def _():
        sc = jnp.where(j >= n_q, kscale, 1.0).astype(F32)
        parts = []
        for s in range(acc.shape[1] // LANES):
            xs = acc[:, s * LANES:(s + 1) * LANES]
            off = (s * LANES) % head_dim
            c = cos_ref[:, off:off + LANES]
            sn = sin_ref[:, off:off + LANES]
            parts.append((xs * c + _rope_partner(xs, head_dim // 4) * sn) * sc)
        o_ref[0] = jnp.concatenate(parts, axis=1).astype(o_ref.dtype)


def _mm_rope(a, w, cos_t, sin_t, *, n_rope_cols, n_q_cols, kscale, head_dim):
    b, s, k = a.shape
    n = w.shape[1]
    tm, tn = ROW_TILE, COL_TILE
    kern = functools.partial(_mm_rope_kernel, n_rope=n_rope_cols // tn, n_q=n_q_cols // tn,
                             kscale=kscale, head_dim=head_dim)
    return pl.pallas_call(
        kern,
        grid=(b, s // tm, n // tn),
        in_specs=[
            pl.BlockSpec((1, tm, k), lambda bi, i, j: (bi, i, 0)),
            pl.BlockSpec((k, tn), lambda bi, i, j: (0, j)),
            pl.BlockSpec((tm, head_dim), lambda bi, i, j: (i, 0)),
            pl.BlockSpec((tm, head_dim), lambda bi, i, j: (i, 0)),
        ],
        out_specs=pl.BlockSpec((1, tm, tn), lambda bi, i, j: (bi, i, j)),
        out_shape=jax.ShapeDtypeStruct((b, s, n), BF16),
        compiler_params=_params(("parallel", "parallel", "arbitrary")),
    )(a, w, cos_t, sin_t)


def _mm_res_kernel(a_ref, w_ref, x_ref, gl_ref, gc_ref, o_ref, *, n_lat):
    acc = _dot(a_ref[0], w_ref[...])
    tm = acc.shape[0]
    row = pl.program_id(1) * tm + lax.broadcasted_iota(I32, (tm, 1), 0)
    gate = jnp.where(row < n_lat, gl_ref[0], gc_ref[0])
    o_ref[0] = x_ref[0] + gate * acc


def _mm_res(a, w, x, gate_lat, gate_ctx, n_lat):
    b, s, k = a.shape
    n = w.shape[1]
    tm, tn = ROW_TILE, COL_TILE
    return pl.pallas_call(
        functools.partial(_mm_res_kernel, n_lat=n_lat),
        grid=(b, s // tm, n // tn),
        in_specs=[
            pl.BlockSpec((1, tm, k), lambda bi, i, j: (bi, i, 0)),
            pl.BlockSpec((k, tn), lambda bi, i, j: (0, j)),
            pl.BlockSpec((1, tm, tn), lambda bi, i, j: (bi, i, j)),
            pl.BlockSpec((1, 1, tn), lambda bi, i, j: (bi, 0, j)),
            pl.BlockSpec((1, 1, tn), lambda bi, i, j: (bi, 0, j)),
        ],
        out_specs=pl.BlockSpec((1, tm, tn), lambda bi, i, j: (bi, i, j)),
        out_shape=jax.ShapeDtypeStruct((b, s, n), F32),
        compiler_params=_params(("parallel", "parallel", "arbitrary")),
    )(a, w, x, gate_lat, gate_ctx)


def _mm_groups_kernel(a_ref, w_ref, o_ref):
    o_ref[0] = _dot(a_ref[0], w_ref[...]).astype(o_ref.dtype)


def _mm_groups(a, w, groups):
    b, s, d = a.shape
    cg = d // groups
    tm = ROW_TILE
    return pl.pallas_call(
        _mm_groups_kernel,
        grid=(b, s // tm, 2 * groups),
        in_specs=[
            pl.BlockSpec((1, tm, cg), lambda bi, i, j: (bi, i, j % groups)),
            pl.BlockSpec((cg, cg), lambda bi, i, j: (0, j // groups)),
        ],
        out_specs=pl.BlockSpec((1, tm, cg), lambda bi, i, j: (bi, i, j)),
        out_shape=jax.ShapeDtypeStruct((b, s, 2 * d), BF16),
        compiler_params=_params(("parallel", "parallel", "arbitrary")),
    )(a, w)


def _ret_scan_kernel(*refs, heads, dk, dv, reverse, add_in):
    if add_in:
        dec_ref, q_ref, k_ref, v_ref, g_ref, yin_ref, o_ref, s_ref, qd_ref, kd_ref, in_ref, cd_ref = refs
    else:
        dec_ref, q_ref, k_ref, v_ref, g_ref, o_ref, s_ref, qd_ref, kd_ref, in_ref, cd_ref = refs
        yin_ref = None
    c = q_ref.shape[1]
    j = pl.program_id(1)

    @pl.when(j == 0)
    def _():
        s_ref[...] = jnp.zeros_like(s_ref)
        m_col = lax.broadcasted_iota(I32, (c, LANES), 0).astype(F32)
        m_row = lax.broadcasted_iota(I32, (c, c), 0).astype(F32)
        n_row = lax.broadcasted_iota(I32, (c, c), 1).astype(F32)
        for h in range(heads):
            lg = -jnp.exp(dec_ref[:, h:h + 1])
            if reverse:
                q_pow, k_pow, diff = c - m_col, m_col, n_row - m_row
            else:
                q_pow, k_pow, diff = m_col + 1.0, c - 1.0 - m_col, m_row - n_row
            qd_ref[h] = jnp.exp(lg * q_pow)
            kd_ref[h] = jnp.exp(lg * k_pow)
            in_ref[h] = jnp.where(diff >= 0, jnp.exp(lg * jnp.maximum(diff, 0.0)), 0.0)
            cd_ref[h] = jnp.exp(jnp.broadcast_to(lg, (8, LANES)) * float(c))

    for h in range(heads):
        q = q_ref[0, :, h * dk:(h + 1) * dk]
        k = k_ref[0, :, h * dk:(h + 1) * dk]
        v = v_ref[0, :, h * dv:(h + 1) * dv]
        qdec = jnp.concatenate([qd_ref[h]] * (dk // LANES), axis=1)
        kdec = jnp.concatenate([kd_ref[h]] * (dk // LANES), axis=1)
        state = s_ref[h]
        cross = _dot((q.astype(F32) * qdec).astype(BF16), state.astype(BF16))
        scores = _dot_nt(q, k) * in_ref[h]
        o = cross + _dot(scores.astype(BF16), v)
        k_t = (k.astype(F32) * kdec).T.astype(BF16)
        s_ref[h] = state * cd_ref[h][0:1, 0:1] + _dot(k_t, v)
        mu = jnp.mean(o, axis=-1, keepdims=True)
        cen = o - mu
        var = jnp.mean(cen * cen, axis=-1, keepdims=True)
        g = g_ref[0, :, h * dv:(h + 1) * dv].astype(F32)
        y = cen * lax.rsqrt(var + NORM_EPS) * _silu(g)
        if add_in:
            y = y + yin_ref[0, :, h * dv:(h + 1) * dv].astype(F32)
        o_ref[0, :, h * dv:(h + 1) * dv] = y.astype(o_ref.dtype)


def _ret_scan(proj, decay_row, y_in, *, n_lat, n_ctx, heads, dk, dv, reverse):
    b, s, _ = proj.shape
    c = RET_CHUNK
    lat_chunks, ctx_chunks = n_lat // c, n_ctx // c
    steps = lat_chunks + ctx_chunks
    hk, hv = heads * dk, heads * dv
    if reverse:
        chunk = lambda j: steps - 1 - j
    else:
        chunk = lambda j: jnp.where(j < ctx_chunks, lat_chunks + j, j - ctx_chunks)
    gate_blk = (2 * hk) // hv + 1 + (1 if reverse else 0)
    in_specs = [
        pl.BlockSpec((1, heads), lambda bi, j: (0, 0)),
        pl.BlockSpec((1, c, hk), lambda bi, j: (bi, chunk(j), 0)),
        pl.BlockSpec((1, c, hk), lambda bi, j: (bi, chunk(j), 1)),
        pl.BlockSpec((1, c, hv), lambda bi, j: (bi, chunk(j), (2 * hk) // hv)),
        pl.BlockSpec((1, c, hv), lambda bi, j: (bi, chunk(j), gate_blk)),
    ]
    args = [decay_row, proj, proj, proj, proj]
    if y_in is not None:
        in_specs.append(pl.BlockSpec((1, c, hv), lambda bi, j: (bi, chunk(j), 0)))
        args.append(y_in)
    kern = functools.partial(_ret_scan_kernel, heads=heads, dk=dk, dv=dv, reverse=reverse,
                             add_in=y_in is not None)
    return pl.pallas_call(
        kern,
        grid=(b, steps),
        in_specs=in_specs,
        out_specs=pl.BlockSpec((1, c, hv), lambda bi, j: (bi, chunk(j), 0)),
        out_shape=jax.ShapeDtypeStruct((b, s, hv), BF16),
        scratch_shapes=[
            pltpu.VMEM((heads, dk, dv), F32),
            pltpu.VMEM((heads, c, LANES), F32),
            pltpu.VMEM((heads, c, LANES), F32),
            pltpu.VMEM((heads, c, c), F32),
            pltpu.VMEM((heads, 8, LANES), F32),
        ],
        compiler_params=_params(("parallel", "arbitrary")),
    )(*args)


def _win_attn_kernel(sink_ref, q_ref, kc_ref, vc_ref, kp_ref, kq_ref, kn_ref, vp_ref, vq_ref, vn_ref,
                     o_ref, *, lat_tiles, heads, kv_heads, hd):
    qt = pl.program_id(1)
    blk = q_ref.shape[1]
    n_ctx = kc_ref.shape[1]
    grp = heads // kv_heads
    scale = hd ** -0.5
    rows = grp * blk
    qi = lax.broadcasted_iota(I32, (rows, blk), 0) % blk
    kj = lax.broadcasted_iota(I32, (rows, blk), 1)
    tq = qt + jnp.zeros((rows, blk), I32)
    ok_cur = tq < lat_tiles
    ok_prev = (kj >= qi) & ok_cur & (tq >= 1)
    ok_next = (kj <= qi) & (tq + 1 < lat_tiles)
    bias = jnp.concatenate(
        [jnp.zeros((rows, n_ctx), F32)]
        + [jnp.where(ok, 0.0, NEG_INF).astype(F32) for ok in (ok_prev, ok_cur, ok_next)], axis=1)
    head_row = lax.broadcasted_iota(I32, (rows, 1), 0) // blk
    for kv in range(kv_heads):
        cs = slice(kv * hd, (kv + 1) * hd)
        keys = jnp.concatenate([kc_ref[0, :, cs], kp_ref[0, :, cs], kq_ref[0, :, cs], kn_ref[0, :, cs]], axis=0)
        vals = jnp.concatenate([vc_ref[0, :, cs], vp_ref[0, :, cs], vq_ref[0, :, cs], vn_ref[0, :, cs]], axis=0)
        q = jnp.concatenate([q_ref[0, :, (kv * grp + g) * hd:(kv * grp + g + 1) * hd] for g in range(grp)], axis=0)
        sink = jnp.zeros((rows, 1), F32)
        for g in range(grp):
            h = kv * grp + g
            sink = jnp.where(head_row == g, sink_ref[:, h:h + 1], sink)
        s = _dot_nt(q, keys) * scale + bias
        m = jnp.maximum(jnp.max(s, axis=-1, keepdims=True), sink)
        p = jnp.exp(s - m)
        den = jnp.sum(p, axis=-1, keepdims=True) + jnp.exp(sink - m)
        o = _dot(p.astype(BF16), vals) / den
        for g in range(grp):
            h = kv * grp + g
            o_ref[0, :, h * hd:(h + 1) * hd] = o[g * blk:(g + 1) * blk].astype(o_ref.dtype)


def _win_attn(proj, sink, *, n_lat, n_ctx, heads, kv_heads, hd):
    b, s, _ = proj.shape
    blk = WIN_BLOCK
    lat_tiles = n_lat // blk
    tiles = s // blk
    kvw = kv_heads * hd
    k_col = (heads * hd) // kvw
    v_col = k_col + 1
    ctx_blk = n_lat // n_ctx
    prev = lambda t: jnp.maximum(t - 1, 0)
    nxt = lambda t: jnp.minimum(t + 1, tiles - 1)
    sink_row = jnp.zeros((1, LANES), F32).at[0, :heads].set(sink.astype(F32))
    kern = functools.partial(_win_attn_kernel, lat_tiles=lat_tiles, heads=heads, kv_heads=kv_heads, hd=hd)
    return pl.pallas_call(
        kern,
        grid=(b, tiles),
        in_specs=[
            pl.BlockSpec((1, LANES), lambda bi, t: (0, 0)),
            pl.BlockSpec((1, blk, heads * hd), lambda bi, t: (bi, t, 0)),
            pl.BlockSpec((1, n_ctx, kvw), lambda bi, t: (bi, ctx_blk, k_col)),
            pl.BlockSpec((1, n_ctx, kvw), lambda bi, t: (bi, ctx_blk, v_col)),
            pl.BlockSpec((1, blk, kvw), lambda bi, t: (bi, prev(t), k_col)),
            pl.BlockSpec((1, blk, kvw), lambda bi, t: (bi, t, k_col)),
            pl.BlockSpec((1, blk, kvw), lambda bi, t: (bi, nxt(t), k_col)),
            pl.BlockSpec((1, blk, kvw), lambda bi, t: (bi, prev(t), v_col)),
            pl.BlockSpec((1, blk, kvw), lambda bi, t: (bi, t, v_col)),
            pl.BlockSpec((1, blk, kvw), lambda bi, t: (bi, nxt(t), v_col)),
        ],
        out_specs=pl.BlockSpec((1, blk, heads * hd), lambda bi, t: (bi, t, 0)),
        out_shape=jax.ShapeDtypeStruct((b, s, heads * hd), BF16),
        compiler_params=_params(("parallel", "parallel")),
    )(sink_row, *([proj] * 9))


def _dft_stage_a_kernel(m_ref, x_ref, re_ref, im_ref, *, d):
    x = x_ref[0]
    stacked = jnp.concatenate([x[:, :d], x[:, d:]], axis=0)
    out = _dot(m_ref[...], stacked)
    r = out.shape[0] // 2
    re_ref[0] = out[:r].astype(re_ref.dtype)
    im_ref[0] = out[r:].astype(im_ref.dtype)


def _dft_stage_b_kernel(m_ref, re_ref, im_ref, twc_ref, tws_ref, o_ref, *, inner, d):
    per_step = re_ref.shape[1] // inner
    reps = d // LANES
    for jj in range(per_step):
        rs = slice(jj * inner, (jj + 1) * inner)
        ar = re_ref[0, rs, :].astype(F32)
        ai = im_ref[0, rs, :].astype(F32)
        twc = jnp.concatenate([twc_ref[rs, :]] * reps, axis=1)
        tws = jnp.concatenate([tws_ref[rs, :]] * reps, axis=1)
        br = ar * twc + ai * tws
        bi = ai * twc - ar * tws
        stacked = jnp.concatenate([br, bi], axis=0).astype(BF16)
        o_ref[0, :, jj * d:(jj + 1) * d] = _dot(m_ref[...], stacked).astype(o_ref.dtype)


def _dft_ctx_kernel(m_ref, x_ref, o_ref, *, d):
    x = x_ref[0]
    stacked = jnp.concatenate([x[:, :d], x[:, d:]], axis=0)
    o_ref[0] = _dot(m_ref[...], stacked).astype(o_ref.dtype)


def _cos_sin(n, m=None):
    m = n if m is None else m
    ang = 2.0 * np.pi * np.outer(np.arange(n), np.arange(m)) / float(max(n, m))
    return np.cos(ang), np.sin(ang)


def _fourier_positions(uv, n_lat, n_ctx, d):
    b, s, _ = uv.shape
    n2 = FFT_INNER
    n1 = n_lat // n2
    ca, sa = _cos_sin(n1)
    ca, sa = ca / math.sqrt(n1), sa / math.sqrt(n1)
    mat_a = jnp.asarray(np.block([[ca, -sa], [-sa, -ca]]), BF16)
    uv_view = uv.reshape(b, s // n2, n2 * 2 * d)
    a_re, a_im = pl.pallas_call(
        functools.partial(_dft_stage_a_kernel, d=d),
        grid=(b, n2),
        in_specs=[
            pl.BlockSpec((2 * n1, 2 * n1), lambda bi, c: (0, 0)),
            pl.BlockSpec((1, n1, 2 * d), lambda bi, c: (bi, 0, c)),
        ],
        out_specs=[pl.BlockSpec((1, n1, d), lambda bi, c: (bi, 0, c))] * 2,
        out_shape=[jax.ShapeDtypeStruct((b, n1, n2 * d), BF16)] * 2,
        compiler_params=_params(("parallel", "parallel")),
    )(mat_a, uv_view)
    cb, sb = _cos_sin(n2)
    mat_b = jnp.asarray(np.concatenate([cb, sb], axis=1) / math.sqrt(n2), BF16)
    phi = 2.0 * np.pi * np.outer(np.arange(n1), np.arange(n2)) / float(n_lat)
    twc = jnp.asarray(np.repeat(np.cos(phi).reshape(-1, 1), LANES, axis=1), F32)
    tws = jnp.asarray(np.repeat(np.sin(phi).reshape(-1, 1), LANES, axis=1), F32)
    per_step = 4
    rows = per_step * n2
    y_lat = pl.pallas_call(
        functools.partial(_dft_stage_b_kernel, inner=n2, d=d),
        grid=(b, n1 // per_step),
        in_specs=[
            pl.BlockSpec((n2, 2 * n2), lambda bi, kb: (0, 0)),
            pl.BlockSpec((1, rows, d), lambda bi, kb: (bi, kb, 0)),
            pl.BlockSpec((1, rows, d), lambda bi, kb: (bi, kb, 0)),
            pl.BlockSpec((rows, LANES), lambda bi, kb: (kb, 0)),
            pl.BlockSpec((rows, LANES), lambda bi, kb: (kb, 0)),
        ],
        out_specs=pl.BlockSpec((1, n2, per_step * d), lambda bi, kb: (bi, 0, kb)),
        out_shape=jax.ShapeDtypeStruct((b, n2, n1 * d), BF16),
        compiler_params=_params(("parallel", "parallel")),
    )(mat_b, a_re.reshape(b, n_lat, d), a_im.reshape(b, n_lat, d), twc, tws)
    cc, sc = _cos_sin(n_ctx)
    mat_c = jnp.asarray(np.concatenate([cc, -sc], axis=1) / math.sqrt(n_ctx), BF16)
    y_ctx = pl.pallas_call(
        functools.partial(_dft_ctx_kernel, d=d),
        grid=(b,),
        in_specs=[
            pl.BlockSpec((n_ctx, 2 * n_ctx), lambda bi: (0, 0)),
            pl.BlockSpec((1, n_ctx, 2 * d), lambda bi: (bi, n_lat // n_ctx, 0)),
        ],
        out_specs=pl.BlockSpec((1, n_ctx, d), lambda bi: (bi, 0, 0)),
        out_shape=jax.ShapeDtypeStruct((b, n_ctx, d), BF16),
        compiler_params=_params(("parallel",)),
    )(mat_c, uv)
    return jnp.concatenate([y_lat.reshape(b, n_lat, d), y_ctx], axis=1)


def _route_kernel(aff_ref, tri_ref, posc_ref, gate_ref, posr_ref, tst_ref,
                  thr_ref, need_ref, ctie_ref, cpos_ref, *, cap):
    t = pl.program_id(1)
    tile = posc_ref.shape[1]

    @pl.when(t == 0)
    def _():
        def body(it, thr):
            bits = lax.bitcast_convert_type(aff_ref[0], I32)
            cand = thr | jnp.left_shift(jnp.int32(1), 30 - it)
            cnt = jnp.sum(jnp.where(bits >= cand, 1.0, 0.0), axis=0, keepdims=True)
            return jnp.where(cnt >= cap, cand, thr)

        thr = lax.fori_loop(0, 31, body, jnp.zeros((1, LANES), I32))
        bits = lax.bitcast_convert_type(aff_ref[0], I32)
        above = jnp.sum(jnp.where(bits > thr, 1.0, 0.0), axis=0, keepdims=True)
        thr_ref[...] = thr
        need_ref[...] = float(cap) - above
        ctie_ref[...] = jnp.zeros_like(ctie_ref)
        cpos_ref[...] = jnp.zeros_like(cpos_ref)

    a = aff_ref[0, pl.ds(pl.multiple_of(t * tile, tile), tile), :]
    bits = lax.bitcast_convert_type(a, I32)
    thr = thr_ref[...]
    gt = bits > thr
    eq = bits == thr
    eqf = jnp.where(eq, 1.0, 0.0)
    tie_rank = _dot(tri_ref[...], eqf.astype(BF16)) + ctie_ref[...]
    sel = gt | (eq & (tie_rank < need_ref[...]))
    self_ = jnp.where(sel, 1.0, 0.0)
    start = cpos_ref[...]
    pos = jnp.where(sel, _dot(tri_ref[...], self_.astype(BF16)) + start, -1.0)
    posc_ref[0] = pos.astype(I32)
    gate_ref[0] = jnp.where(sel, a, 0.0)
    posr_ref[0] = pos.T[:N_EXPERTS].astype(I32)
    tst_ref[0, 0] = jnp.broadcast_to(start, (8, LANES)).astype(I32)
    ctie_ref[...] = ctie_ref[...] + jnp.sum(eqf, axis=0, keepdims=True)
    cpos_ref[...] = start + jnp.sum(self_, axis=0, keepdims=True)


def _route(aff, *, row_off, n, cap):
    b = aff.shape[0]
    tile = ROUTE_TILE
    nt = n // tile
    tri = jnp.asarray(np.tril(np.ones((tile, tile)), -1), BF16)
    return pl.pallas_call(
        functools.partial(_route_kernel, cap=cap),
        grid=(b, nt),
        in_specs=[
            pl.BlockSpec((1, n, LANES), lambda bi, t: (bi, row_off // n, 0)),
            pl.BlockSpec((tile, tile), lambda bi, t: (0, 0)),
        ],
        out_specs=[
            pl.BlockSpec((1, tile, LANES), lambda bi, t: (bi, t, 0)),
            pl.BlockSpec((1, tile, LANES), lambda bi, t: (bi, t, 0)),
            pl.BlockSpec((1, N_EXPERTS, tile), lambda bi, t: (bi, 0, t)),
            pl.BlockSpec((1, 1, 8, LANES), lambda bi, t: (bi, t, 0, 0)),
        ],
        out_shape=[
            jax.ShapeDtypeStruct((b, n, LANES), I32),
            jax.ShapeDtypeStruct((b, n, LANES), F32),
            jax.ShapeDtypeStruct((b, N_EXPERTS, n), I32),
            jax.ShapeDtypeStruct((b, nt, 8, LANES), I32),
        ],
        scratch_shapes=[pltpu.VMEM((1, LANES), I32)] + [pltpu.VMEM((1, LANES), F32)] * 3,
        compiler_params=_params(("parallel", "arbitrary")),
    )(aff, tri)


def _window_start(ts_ref, idx, cap_rows, win):
    a0 = jnp.minimum(ts_ref[idx] & (-SLOT_ALIGN), cap_rows - win)
    return pl.multiple_of(a0, SLOT_ALIGN)


def _gather_kernel(ts_ref, h_ref, posr_ref, xe_ref, *, group, nt, win):
    bi, eg, t = pl.program_id(0), pl.program_id(1), pl.program_id(2)
    cap_rows = xe_ref.shape[2]

    @pl.when(t == 0)
    def _():
        xe_ref[...] = jnp.zeros_like(xe_ref)

    h = h_ref[0]
    tile = h.shape[0]
    for jx in range(group):
        e = eg * group + jx
        a0 = _window_start(ts_ref, (bi * nt + t) * N_EXPERTS + e, cap_rows, win)
        prow = posr_ref[0, pl.ds(e, 1), :]
        slot = a0 + lax.broadcasted_iota(I32, (win, tile), 0)
        onehot = jnp.where(prow == slot, 1.0, 0.0).astype(BF16)
        cur = xe_ref[0, jx, pl.ds(a0, win), :].astype(F32)
        xe_ref[0, jx, pl.ds(a0, win), :] = (cur + _dot(onehot, h)).astype(xe_ref.dtype)


def _gather(tstart, h, posr, *, row_off, n, cap_rows):
    b, _, d = h.shape
    tile = ROUTE_TILE
    nt = n // tile
    group = 2
    win = min(tile + SLOT_ALIGN, cap_rows)
    off = row_off // tile
    grid_spec = pltpu.PrefetchScalarGridSpec(
        num_scalar_prefetch=1,
        grid=(b, N_EXPERTS // group, nt),
        in_specs=[
            pl.BlockSpec((1, tile, d), lambda bi, eg, t, ts: (bi, off + t, 0)),
            pl.BlockSpec((1, N_EXPERTS, tile), lambda bi, eg, t, ts: (bi, 0, t)),
        ],
        out_specs=pl.BlockSpec((1, group, cap_rows, d), lambda bi, eg, t, ts: (bi, eg, 0, 0)),
    )
    return pl.pallas_call(
        functools.partial(_gather_kernel, group=group, nt=nt, win=win),
        grid_spec=grid_spec,
        out_shape=jax.ShapeDtypeStruct((b, N_EXPERTS, cap_rows, d), BF16),
        compiler_params=_params(("parallel", "parallel", "arbitrary")),
    )(tstart, h, posr)


def _ffn_kernel(xe_ref, wg_ref, wu_ref, wd_ref, ye_ref, acc_ref):
    fk = pl.program_id(2)
    x = xe_ref[0, 0]
    a = _dot(x, wg_ref[0].astype(BF16))
    u = _dot(x, wu_ref[0].astype(BF16))
    y = _dot((_silu(a) * u).astype(BF16), wd_ref[0].astype(BF16))

    @pl.when(fk == 0)
    def _():
        acc_ref[...] = y

    @pl.when(fk > 0)
    def _():
        acc_ref[...] = acc_ref[...] + y

    @pl.when(fk == pl.num_programs(2) - 1)
    def _():
        ye_ref[0, 0] = acc_ref[...].astype(ye_ref.dtype)


def _ffn(xe, w_gate, w_up, w_down):
    b, e, cap_rows, d = xe.shape
    f = w_gate.shape[2]
    tf = min(256, f)
    return pl.pallas_call(
        _ffn_kernel,
        grid=(b, e, f // tf),
        in_specs=[
            pl.BlockSpec((1, 1, cap_rows, d), lambda bi, ei, fk: (bi, ei, 0, 0)),
            pl.BlockSpec((1, d, tf), lambda bi, ei, fk: (ei, 0, fk)),
            pl.BlockSpec((1, d, tf), lambda bi, ei, fk: (ei, 0, fk)),
            pl.BlockSpec((1, tf, d), lambda bi, ei, fk: (ei, fk, 0)),
        ],
        out_specs=pl.BlockSpec((1, 1, cap_rows, d), lambda bi, ei, fk: (bi, ei, 0, 0)),
        out_shape=jax.ShapeDtypeStruct((b, e, cap_rows, d), BF16),
        scratch_shapes=[pltpu.VMEM((cap_rows, d), F32)],
        compiler_params=_params(("parallel", "parallel", "arbitrary")),
    )(xe, w_gate, w_up, w_down)


def _combine_kernel(ts_ref, ye_ref, posc_ref, gate_ref, x_ref, g_ref, o_ref, *, nt, win):
    bi, t = pl.program_id(0), pl.program_id(2)
    cap_rows = ye_ref.shape[2]
    tile = posc_ref.shape[1]
    acc = jnp.zeros(o_ref.shape[1:], F32)
    lane = lax.broadcasted_iota(I32, (tile, win), 1)
    for e in range(N_EXPERTS):
        a0 = _window_start(ts_ref, (bi * nt + t) * N_EXPERTS + e, cap_rows, win)
        rows = ye_ref[0, e, pl.ds(a0, win), :]
        pcol = posc_ref[0, :, e:e + 1]
        gcol = gate_ref[0, :, e:e + 1]
        weights = jnp.where(pcol == a0 + lane, gcol, 0.0).astype(BF16)
        acc = acc + _dot(weights, rows)
    o_ref[0] = x_ref[0] + g_ref[0, 0] * acc


def _combine(tstart, ye, posc, gate, x, g2, *, row_off, n, region):
    b, s, d = x.shape
    cap_rows = ye.shape[2]
    tile = ROUTE_TILE
    nt = n // tile
    dc = 512 if d % 512 == 0 else d
    win = min(2 * tile, cap_rows)
    off = row_off // tile
    grid_spec = pltpu.PrefetchScalarGridSpec(
        num_scalar_prefetch=1,
        grid=(b, d // dc, nt),
        in_specs=[
            pl.BlockSpec((1, N_EXPERTS, cap_rows, dc), lambda bi, c, t, ts: (bi, 0, 0, c)),
            pl.BlockSpec((1, tile, LANES), lambda bi, c, t, ts: (bi, t, 0)),
            pl.BlockSpec((1, tile, LANES), lambda bi, c, t, ts: (bi, t, 0)),
            pl.BlockSpec((1, tile, dc), lambda bi, c, t, ts: (bi, off + t, c)),
            pl.BlockSpec((1, 1, 1, dc), lambda bi, c, t, ts: (bi, region, 0, c)),
        ],
        out_specs=pl.BlockSpec((1, tile, dc), lambda bi, c, t, ts: (bi, off + t, c)),
    )
    return pl.pallas_call(
        functools.partial(_combine_kernel, nt=nt, win=win),
        grid_spec=grid_spec,
        out_shape=jax.ShapeDtypeStruct((b, s, d), F32),
        input_output_aliases={4: 0},
        compiler_params=_params(("parallel", "parallel", "arbitrary")),
    )(tstart, ye, posc, gate, x, g2)


def _moe(x, h, aff, g2, w_gate, w_up, w_down, *, row_off, n, region):
    cap = CAPACITY_FACTOR * n // N_EXPERTS
    cap_rows = max(cap, MIN_CAP_ROWS)
    posc, gate, posr, tst = _route(aff, row_off=row_off, n=n, cap=cap)
    tstart = tst[:, :, 0, :N_EXPERTS].reshape(-1)
    xe = _gather(tstart, h, posr, row_off=row_off, n=n, cap_rows=cap_rows)
    ye = _ffn(xe, w_gate, w_up, w_down)
    return _combine(tstart, ye, posc, gate, x, g2, row_off=row_off, n=n, region=region)


def _rope_tables(n_lat, n_ctx, head_dim):
    quarter = head_dim // 4
    inv = ROPE_BASE ** (-jnp.arange(quarter, dtype=F32) / quarter)
    pos = jnp.arange(n_lat)
    row = (pos // GRID_W).astype(F32)[:, None] * inv[None, :]
    col = (pos % GRID_W).astype(F32)[:, None] * inv[None, :]
    cos = jnp.concatenate([jnp.cos(row), jnp.cos(row), jnp.cos(col), jnp.cos(col)], axis=1)
    sin = jnp.concatenate([-jnp.sin(row), jnp.sin(row), -jnp.sin(col), jnp.sin(col)], axis=1)
    cos = jnp.concatenate([cos, jnp.ones((n_ctx, head_dim), F32)], axis=0)
    sin = jnp.concatenate([sin, jnp.zeros((n_ctx, head_dim), F32)], axis=0)
    return cos, sin


def kernel(x, c, ctx, c_ctx, w_mod, b_mod, norm_gain, final_gain, ret_w_in, ret_w_out, ret_decay,
           win_w_qkv, win_w_o, win_sink, fno_w_o, router_w, exp_w_gate, exp_w_up, exp_w_down):
    b, n_lat, d = x.shape
    n_ctx = ctx.shape[1]
    depth = w_mod.shape[0]
    s = n_lat + n_ctx
    assert s % ROW_TILE == 0 and n_lat % NORM_TILE == 0 and n_ctx % NORM_TILE == 0 and n_lat % n_ctx == 0
    assert n_lat % (FFT_INNER * 4) == 0 and b + 1 <= 8 and n_ctx % ROUTE_TILE == 0

    xs = jnp.concatenate([x, ctx], axis=1)
    cvec = jnp.zeros((8, d), F32).at[:b].set(c).at[b].set(c_ctx)
    mod = _modulation(cvec, w_mod, b_mod)

    def mod_pair(i, k):
        lat = mod[i, :b, k * d:(k + 1) * d]
        cx = jnp.broadcast_to(mod[i, b, k * d:(k + 1) * d], (b, d))
        return jnp.stack([lat, cx], axis=1).reshape(b, 2, 1, d)

    ret_dk = d // RET_HEADS
    ret_dv = 2 * ret_dk
    win_hd = d // WIN_HEADS
    ret_cos, ret_sin = _rope_tables(n_lat, n_ctx, ret_dk)
    win_cos, win_sin = _rope_tables(n_lat, n_ctx, win_hd)

    for i in range(depth):
        kind, j = i % N_MIXERS, i // N_MIXERS
        last = i == depth - 1
        sh1, sc1, g1, sh2, sc2, g2 = [mod_pair(i, k) for k in range(6)]
        h = _norm_mod(xs, norm_gain[i, 0], sh1, sc1, n_lat)
        if kind == 0:
            hk = RET_HEADS * ret_dk
            proj = _mm_rope(h, ret_w_in[j].astype(BF16), ret_cos, ret_sin, n_rope_cols=2 * hk, n_q_cols=hk,
                            kscale=ret_dk ** -0.5, head_dim=ret_dk)
            scan = functools.partial(_ret_scan, n_lat=n_lat, n_ctx=n_ctx, heads=RET_HEADS, dk=ret_dk, dv=ret_dv)
            y_b = scan(proj, ret_decay[j, 1:2], None, reverse=True)
            y = scan(proj, ret_decay[j, 0:1], y_b, reverse=False)
            w_out = ret_w_out[j]
        elif kind == 1:
            proj = _mm_rope(h, win_w_qkv[j].astype(BF16), win_cos, win_sin,
                            n_rope_cols=(WIN_HEADS + WIN_KV_HEADS) * win_hd, n_q_cols=WIN_HEADS * win_hd,
                            kscale=1.0, head_dim=win_hd)
            y = _win_attn(proj, win_sink[j], n_lat=n_lat, n_ctx=n_ctx, heads=WIN_HEADS,
                          kv_heads=WIN_KV_HEADS, hd=win_hd)
            w_out = win_w_o[j]
        else:
            cg = d // FOURIER_GROUPS
            cc, sc = _cos_sin(cg)
            w_ch = jnp.asarray(np.concatenate([cc, sc], axis=1) / math.sqrt(cg), BF16)
            uv = _mm_groups(h, w_ch, FOURIER_GROUPS)
            y = _fourier_positions(uv, n_lat, n_ctx, d)
            w_out = fno_w_o[j]
        xs = _mm_res(y, w_out.astype(BF16), xs, g1[:, 0], g1[:, 1], n_lat)
        h2, aff = _norm_mod(xs, norm_gain[i, 1], sh2, sc2, n_lat, router_w=router_w[i])
        xs = _moe(xs, h2, aff, g2, exp_w_gate[i], exp_w_up[i], exp_w_down[i], row_off=0, n=n_lat, region=0)
        if not last:
            xs = _moe(xs, h2, aff, g2, exp_w_gate[i], exp_w_up[i], exp_w_down[i], row_off=n_lat, n=n_ctx, region=1)

    zeros = jnp.zeros((b, 2, 1, d), F32)
    return _norm_mod(xs, final_gain, zeros, zeros, n_lat, out_dtype=F32, rows=n_lat)
```

```python
import functools
import math

import numpy as np
import jax
import jax.numpy as jnp
from jax import lax
from jax.experimental import pallas as pl
from jax.experimental.pallas import tpu as pltpu

F32 = jnp.float32
BF16 = jnp.bfloat16
I32 = jnp.int32

GRID_W = 64
N_MIXERS = 3
RET_HEADS = 8
RET_CHUNK = 128
WIN_HEADS = 16
WIN_KV_HEADS = 4
WIN_BLOCK = 128
FOURIER_GROUPS = 4
FFT_INNER = 64
N_EXPERTS = 16
CAPACITY_FACTOR = 2
ROPE_BASE = 10000.0
NORM_EPS = 1e-6
NEG_INF = -1e30

LANES = 128
ROUTE_TILE = 128
SLOT_ALIGN = 16
MIN_CAP_ROWS = 256
ROW_TILE = 768
MM_ROW_CHUNK = 256
COL_TILE = 512
PROJ_COL_TILE = 1024
NORM_TILE = 256
VMEM_LIMIT_BYTES = 56 * 1024 * 1024


def _params(sem):
    return pltpu.CompilerParams(dimension_semantics=sem, vmem_limit_bytes=VMEM_LIMIT_BYTES)


def _dot(a, b):
    return jnp.dot(a, b, preferred_element_type=F32)


def _dot_nt(a, b):
    return lax.dot_general(a, b, (((1,), (1,)), ((), ())), preferred_element_type=F32)


def _split_bf16(x):
    hi = x.astype(BF16)
    lo = (x - hi.astype(F32)).astype(BF16)
    return hi, lo


def _silu(x):
    return x / (1.0 + jnp.exp(-x))


def _mod_kernel(c_ref, w_ref, b_ref, o_ref):
    s = _silu(c_ref[...])
    sh, sl = _split_bf16(s)
    wh, wl = _split_bf16(w_ref[0])
    o_ref[0] = _dot(sh, wh) + _dot(sl, wh) + _dot(sh, wl) + b_ref[0]


def _modulation(cvec, w_mod, b_mod):
    depth, d, n = w_mod.shape
    tn = 1024
    return pl.pallas_call(
        _mod_kernel,
        name="modulation",
        grid=(depth, n // tn),
        in_specs=[
            pl.BlockSpec((8, d), lambda i, j: (0, 0)),
            pl.BlockSpec((1, d, tn), lambda i, j: (i, 0, j)),
            pl.BlockSpec((1, 1, tn), lambda i, j: (i, 0, j)),
        ],
        out_specs=pl.BlockSpec((1, 8, tn), lambda i, j: (i, 0, j)),
        out_shape=jax.ShapeDtypeStruct((depth, 8, n), F32),
        compiler_params=_params(("parallel", "parallel")),
    )(cvec, w_mod, b_mod.reshape(depth, 1, n))


def _normed(x_ref, gain_ref, shift_ref, scale_ref):
    x = x_ref[0]
    ms = jnp.mean(x * x, axis=-1, keepdims=True)
    y = x * lax.rsqrt(ms + NORM_EPS) * gain_ref[...]
    return y * (1.0 + scale_ref[0, 0]) + shift_ref[0, 0]


def _norm_mod_kernel(x_ref, gain_ref, shift_ref, scale_ref, o_ref):
    o_ref[0] = _normed(x_ref, gain_ref, shift_ref, scale_ref).astype(o_ref.dtype)


def _norm_router_kernel(x_ref, gain_ref, shift_ref, scale_ref, wr_ref, o_ref, aff_ref):
    h = _normed(x_ref, gain_ref, shift_ref, scale_ref)
    o_ref[0] = h.astype(o_ref.dtype)
    hh, hl = _split_bf16(h)
    wh, wl = _split_bf16(wr_ref[...])
    logits = _dot(hh, wh) + _dot(hl, wh) + _dot(hh, wl)
    lane = lax.broadcasted_iota(I32, logits.shape, 1)
    valid = lane < N_EXPERTS
    logits = jnp.where(valid, logits, -jnp.inf)
    m = jnp.max(logits, axis=-1, keepdims=True)
    p = jnp.exp(logits - m)
    aff = p / jnp.sum(p, axis=-1, keepdims=True)
    aff_ref[0] = jnp.where(valid, aff, 0.0)


def _norm_mod(x, gain, shift, scale, n_lat, *, out_dtype=BF16, rows=None, router_w=None):
    b, s, d = x.shape
    rows = s if rows is None else rows
    tr = NORM_TILE
    lat_tiles = n_lat // tr
    region = lambda bi, t: (bi, jnp.where(t >= lat_tiles, 1, 0), 0, 0)
    in_specs = [
        pl.BlockSpec((1, tr, d), lambda bi, t: (bi, t, 0)),
        pl.BlockSpec((1, d), lambda bi, t: (0, 0)),
        pl.BlockSpec((1, 1, 1, d), region),
        pl.BlockSpec((1, 1, 1, d), region),
    ]
    args = [x, gain.reshape(1, d), shift, scale]
    out_specs = pl.BlockSpec((1, tr, d), lambda bi, t: (bi, t, 0))
    out_shape = jax.ShapeDtypeStruct((b, rows, d), out_dtype)
    kern = _norm_mod_kernel
    if router_w is not None:
        wr = jnp.zeros((d, LANES), F32).at[:, :N_EXPERTS].set(router_w)
        in_specs.append(pl.BlockSpec((d, LANES), lambda bi, t: (0, 0)))
        args.append(wr)
        out_specs = [out_specs, pl.BlockSpec((1, tr, LANES), lambda bi, t: (bi, t, 0))]
        out_shape = [out_shape, jax.ShapeDtypeStruct((b, rows, LANES), F32)]
        kern = _norm_router_kernel
    return pl.pallas_call(
        kern,
        name="norm_mod" if router_w is None else "norm_router",
        grid=(b, rows // tr),
        in_specs=in_specs,
        out_specs=out_specs,
        out_shape=out_shape,
        compiler_params=_params(("parallel", "parallel")),
    )(*args)


def _rope_partner(xs, quarter):
    if 2 * quarter == LANES:
        return pltpu.roll(xs, quarter, 1)
    back = pltpu.roll(xs, quarter, 1)
    fwd = pltpu.roll(xs, LANES - quarter, 1)
    lane = lax.broadcasted_iota(I32, xs.shape, 1)
    return jnp.where((lane % (2 * quarter)) < quarter, fwd, back)


def _row_chunks(tm):
    return [slice(r, r + MM_ROW_CHUNK) for r in range(0, tm, MM_ROW_CHUNK)]


def _mm_rope_kernel(a_ref, w_ref, cos_ref, sin_ref, o_ref, *, n_q, kscale, head_dim):
    sc = jnp.where(pl.program_id(2) >= n_q, kscale, 1.0).astype(F32)
    for rows in _row_chunks(a_ref.shape[1]):
        acc = _dot(a_ref[0, rows, :], w_ref[...])
        for s in range(acc.shape[1] // LANES):
            cols = slice(s * LANES, (s + 1) * LANES)
            off = (s * LANES) % head_dim
            xs = acc[:, cols]
            rot = xs * cos_ref[rows, off:off + LANES] + _rope_partner(xs, head_dim // 4) * sin_ref[rows, off:off + LANES]
            o_ref[0, rows, cols] = (rot * sc).astype(o_ref.dtype)


def _mm_plain_kernel(a_ref, w_ref, o_ref):
    for rows in _row_chunks(a_ref.shape[1]):
        o_ref[0, rows, :] = _dot(a_ref[0, rows, :], w_ref[...]).astype(o_ref.dtype)


def _mm_proj(a, w, col0, ncols, tn, rope=None):
    b, s, k = a.shape
    tm = ROW_TILE
    assert col0 % tn == 0 and ncols % tn == 0
    in_specs = [
        pl.BlockSpec((1, tm, k), lambda bi, i, j: (bi, i, 0)),
        pl.BlockSpec((k, tn), lambda bi, i, j: (0, col0 // tn + j)),
    ]
    args = [a, w]
    if rope is None:
        kern, name = _mm_plain_kernel, "mm_proj"
    else:
        cos_t, sin_t, n_q_cols, kscale, head_dim = rope
        assert tn % head_dim == 0 and n_q_cols % tn == 0
        kern = functools.partial(_mm_rope_kernel, n_q=n_q_cols // tn, kscale=kscale, head_dim=head_dim)
        name = "mm_proj_rope"
        in_specs += [pl.BlockSpec((tm, head_dim), lambda bi, i, j: (i, 0))] * 2
        args += [cos_t, sin_t]
    return pl.pallas_call(
        kern,
        name=name,
        grid=(b, s // tm, ncols // tn),
        in_specs=in_specs,
        out_specs=pl.BlockSpec((1, tm, tn), lambda bi, i, j: (bi, i, j)),
        out_shape=jax.ShapeDtypeStruct((b, s, ncols), BF16),
        compiler_params=_params(("parallel", "parallel", "arbitrary")),
    )(*args)


def _mm_res_kernel(a_ref, w_ref, x_ref, gl_ref, gc_ref, o_ref, *, n_lat):
    tm = a_ref.shape[1]
    for rows in _row_chunks(tm):
        acc = _dot(a_ref[0, rows, :], w_ref[...])
        row = pl.program_id(1) * tm + rows.start + lax.broadcasted_iota(I32, (acc.shape[0], 1), 0)
        gate = jnp.where(row < n_lat, gl_ref[0], gc_ref[0])
        o_ref[0, rows, :] = x_ref[0, rows, :] + gate * acc


def _mm_res(a, w, x, gate_lat, gate_ctx, n_lat):
    b, s, k = a.shape
    n = w.shape[1]
    tm, tn = ROW_TILE, COL_TILE
    return pl.pallas_call(
        functools.partial(_mm_res_kernel, n_lat=n_lat),
        name="mm_residual",
        grid=(b, s // tm, n // tn),
        in_specs=[
            pl.BlockSpec((1, tm, k), lambda bi, i, j: (bi, i, 0)),
            pl.BlockSpec((k, tn), lambda bi, i, j: (0, j)),
            pl.BlockSpec((1, tm, tn), lambda bi, i, j: (bi, i, j)),
            pl.BlockSpec((1, 1, tn), lambda bi, i, j: (bi, 0, j)),
            pl.BlockSpec((1, 1, tn), lambda bi, i, j: (bi, 0, j)),
        ],
        out_specs=pl.BlockSpec((1, tm, tn), lambda bi, i, j: (bi, i, j)),
        out_shape=jax.ShapeDtypeStruct((b, s, n), F32),
        compiler_params=_params(("parallel", "parallel", "arbitrary")),
    )(a, w, x, gate_lat, gate_ctx)


def _mm_groups_kernel(a_ref, w_ref, o_ref):
    o_ref[0] = _dot(a_ref[0], w_ref[...]).astype(o_ref.dtype)


def _mm_groups(a, w, groups):
    b, s, d = a.shape
    cg = d // groups
    tm = ROW_TILE
    return pl.pallas_call(
        _mm_groups_kernel,
        name="mm_channel_dft",
        grid=(b, s // tm, 2 * groups),
        in_specs=[
            pl.BlockSpec((1, tm, cg), lambda bi, i, j: (bi, i, j % groups)),
            pl.BlockSpec((cg, cg), lambda bi, i, j: (0, j // groups)),
        ],
        out_specs=pl.BlockSpec((1, tm, cg), lambda bi, i, j: (bi, i, j)),
        out_shape=jax.ShapeDtypeStruct((b, s, 2 * d), F32),
        compiler_params=_params(("parallel", "parallel", "arbitrary")),
    )(a, w)


def _ret_scan_kernel(*refs, heads, dk, dv, reverse, add_in):
    if add_in:
        dec_ref, q_ref, k_ref, v_ref, g_ref, yin_ref, o_ref, s_ref, qd_ref, kd_ref, in_ref, cd_ref = refs
    else:
        dec_ref, q_ref, k_ref, v_ref, g_ref, o_ref, s_ref, qd_ref, kd_ref, in_ref, cd_ref = refs
        yin_ref = None
    c = q_ref.shape[1]
    j = pl.program_id(1)

    @pl.when(j == 0)
    def _():
        s_ref[...] = jnp.zeros_like(s_ref)
        m_col = lax.broadcasted_iota(I32, (c, LANES), 0).astype(F32)
        m_row = lax.broadcasted_iota(I32, (c, c), 0).astype(F32)
        n_row = lax.broadcasted_iota(I32, (c, c), 1).astype(F32)
        for h in range(heads):
            lg = -jnp.exp(dec_ref[:, h:h + 1])
            if reverse:
                q_pow, k_pow, diff = c - m_col, m_col, n_row - m_row
            else:
                q_pow, k_pow, diff = m_col + 1.0, c - 1.0 - m_col, m_row - n_row
            qd_ref[h] = jnp.exp(lg * q_pow)
            kd_ref[h] = jnp.exp(lg * k_pow)
            in_ref[h] = jnp.where(diff >= 0, jnp.exp(lg * jnp.maximum(diff, 0.0)), 0.0)
            cd_ref[h] = jnp.exp(jnp.broadcast_to(lg, (8, LANES)) * float(c))

    for h in range(heads):
        q = q_ref[0, :, h * dk:(h + 1) * dk]
        k = k_ref[0, :, h * dk:(h + 1) * dk]
        v = v_ref[0, :, h * dv:(h + 1) * dv]
        qdec = jnp.concatenate([qd_ref[h]] * (dk // LANES), axis=1)
        kdec = jnp.concatenate([kd_ref[h]] * (dk // LANES), axis=1)
        state = s_ref[h]
        cross = _dot((q.astype(F32) * qdec).astype(BF16), state.astype(BF16))
        scores = _dot_nt(q, k) * in_ref[h]
        o = cross + _dot(scores.astype(BF16), v)
        k_t = (k.astype(F32) * kdec).T.astype(BF16)
        s_ref[h] = state * cd_ref[h][0:1, 0:1] + _dot(k_t, v)
        mu = jnp.mean(o, axis=-1, keepdims=True)
        cen = o - mu
        var = jnp.mean(cen * cen, axis=-1, keepdims=True)
        g = g_ref[0, :, h * dv:(h + 1) * dv].astype(F32)
        y = cen * lax.rsqrt(var + NORM_EPS) * _silu(g)
        if add_in:
            y = y + yin_ref[0, :, h * dv:(h + 1) * dv].astype(F32)
        o_ref[0, :, h * dv:(h + 1) * dv] = y.astype(o_ref.dtype)


def _ret_scan(qk, vg, decay_row, y_in, *, n_lat, n_ctx, heads, dk, dv, reverse):
    b, s, _ = qk.shape
    c = RET_CHUNK
    lat_chunks, ctx_chunks = n_lat // c, n_ctx // c
    steps = lat_chunks + ctx_chunks
    hk, hv = heads * dk, heads * dv
    if reverse:
        chunk = lambda j: steps - 1 - j
    else:
        chunk = lambda j: jnp.where(j < ctx_chunks, lat_chunks + j, j - ctx_chunks)
    gate_blk = 2 if reverse else 1
    in_specs = [
        pl.BlockSpec((1, heads), lambda bi, j: (0, 0)),
        pl.BlockSpec((1, c, hk), lambda bi, j: (bi, chunk(j), 0)),
        pl.BlockSpec((1, c, hk), lambda bi, j: (bi, chunk(j), 1)),
        pl.BlockSpec((1, c, hv), lambda bi, j: (bi, chunk(j), 0)),
        pl.BlockSpec((1, c, hv), lambda bi, j: (bi, chunk(j), gate_blk)),
    ]
    args = [decay_row, qk, qk, vg, vg]
    if y_in is not None:
        in_specs.append(pl.BlockSpec((1, c, hv), lambda bi, j: (bi, chunk(j), 0)))
        args.append(y_in)
    kern = functools.partial(_ret_scan_kernel, heads=heads, dk=dk, dv=dv, reverse=reverse,
                             add_in=y_in is not None)
    return pl.pallas_call(
        kern,
        name="ret_scan_bwd" if reverse else "ret_scan_fwd",
        grid=(b, steps),
        in_specs=in_specs,
        out_specs=pl.BlockSpec((1, c, hv), lambda bi, j: (bi, chunk(j), 0)),
        out_shape=jax.ShapeDtypeStruct((b, s, hv), BF16),
        scratch_shapes=[
            pltpu.VMEM((heads, dk, dv), F32),
            pltpu.VMEM((heads, c, LANES), F32),
            pltpu.VMEM((heads, c, LANES), F32),
            pltpu.VMEM((heads, c, c), F32),
            pltpu.VMEM((heads, 8, LANES), F32),
        ],
        compiler_params=_params(("parallel", "arbitrary")),
    )(*args)


def _win_attn_kernel(sink_ref, q_ref, kc_ref, vc_ref, kp_ref, kq_ref, kn_ref, vp_ref, vq_ref, vn_ref,
                     o_ref, *, lat_tiles, heads, kv_heads, hd):
    qt = pl.program_id(1)
    blk = q_ref.shape[1]
    n_ctx = kc_ref.shape[1]
    grp = heads // kv_heads
    scale = hd ** -0.5
    rows = grp * blk
    qi = lax.broadcasted_iota(I32, (rows, blk), 0) % blk
    kj = lax.broadcasted_iota(I32, (rows, blk), 1)
    tq = qt + jnp.zeros((rows, blk), I32)
    ok_cur = tq < lat_tiles
    ok_prev = (kj >= qi) & ok_cur & (tq >= 1)
    ok_next = (kj <= qi) & (tq + 1 < lat_tiles)
    bias = jnp.concatenate(
        [jnp.zeros((rows, n_ctx), F32)]
        + [jnp.where(ok, 0.0, NEG_INF).astype(F32) for ok in (ok_prev, ok_cur, ok_next)], axis=1)
    head_row = lax.broadcasted_iota(I32, (rows, 1), 0) // blk
    for kv in range(kv_heads):
        cs = slice(kv * hd, (kv + 1) * hd)
        keys = jnp.concatenate([kc_ref[0, :, cs], kp_ref[0, :, cs], kq_ref[0, :, cs], kn_ref[0, :, cs]], axis=0)
        vals = jnp.concatenate([vc_ref[0, :, cs], vp_ref[0, :, cs], vq_ref[0, :, cs], vn_ref[0, :, cs]], axis=0)
        q = jnp.concatenate([q_ref[0, :, (kv * grp + g) * hd:(kv * grp + g + 1) * hd] for g in range(grp)], axis=0)
        sink = jnp.zeros((rows, 1), F32)
        for g in range(grp):
            h = kv * grp + g
            sink = jnp.where(head_row == g, sink_ref[:, h:h + 1], sink)
        s = _dot_nt(q, keys) * scale + bias
        m = jnp.maximum(jnp.max(s, axis=-1, keepdims=True), sink)
        p = jnp.exp(s - m)
        den = jnp.sum(p, axis=-1, keepdims=True) + jnp.exp(sink - m)
        o = _dot(p.astype(BF16), vals) / den
        for g in range(grp):
            h = kv * grp + g
            o_ref[0, :, h * hd:(h + 1) * hd] = o[g * blk:(g + 1) * blk].astype(o_ref.dtype)


def _win_attn(qk, v, sink, *, n_lat, n_ctx, heads, kv_heads, hd):
    b, s, _ = qk.shape
    blk = WIN_BLOCK
    lat_tiles = n_lat // blk
    tiles = s // blk
    kvw = kv_heads * hd
    k_col = (heads * hd) // kvw
    v_col = 0
    ctx_blk = n_lat // n_ctx
    prev = lambda t: jnp.maximum(t - 1, 0)
    nxt = lambda t: jnp.minimum(t + 1, tiles - 1)
    sink_row = jnp.zeros((1, LANES), F32).at[0, :heads].set(sink.astype(F32))
    kern = functools.partial(_win_attn_kernel, lat_tiles=lat_tiles, heads=heads, kv_heads=kv_heads, hd=hd)
    return pl.pallas_call(
        kern,
        name="win_attn",
        grid=(b, tiles),
        in_specs=[
            pl.BlockSpec((1, LANES), lambda bi, t: (0, 0)),
            pl.BlockSpec((1, blk, heads * hd), lambda bi, t: (bi, t, 0)),
            pl.BlockSpec((1, n_ctx, kvw), lambda bi, t: (bi, ctx_blk, k_col)),
            pl.BlockSpec((1, n_ctx, kvw), lambda bi, t: (bi, ctx_blk, v_col)),
            pl.BlockSpec((1, blk, kvw), lambda bi, t: (bi, prev(t), k_col)),
            pl.BlockSpec((1, blk, kvw), lambda bi, t: (bi, t, k_col)),
            pl.BlockSpec((1, blk, kvw), lambda bi, t: (bi, nxt(t), k_col)),
            pl.BlockSpec((1, blk, kvw), lambda bi, t: (bi, prev(t), v_col)),
            pl.BlockSpec((1, blk, kvw), lambda bi, t: (bi, t, v_col)),
            pl.BlockSpec((1, blk, kvw), lambda bi, t: (bi, nxt(t), v_col)),
        ],
        out_specs=pl.BlockSpec((1, blk, heads * hd), lambda bi, t: (bi, t, 0)),
        out_shape=jax.ShapeDtypeStruct((b, s, heads * hd), BF16),
        compiler_params=_params(("parallel", "parallel")),
    )(sink_row, qk, qk, v, qk, qk, qk, v, v, v)


SUBLANES = 8
PACKED_ROWS = 16


def _dft_stage_a_kernel(m_ref, u_ref, v_ref, re_ref, im_ref):
    n1, sub, tc = u_ref.shape[1:]
    rows = n1 * sub
    stacked = jnp.concatenate([u_ref[0].reshape(rows, tc), v_ref[0].reshape(rows, tc)], axis=0)
    out = _dot(m_ref[...], stacked.astype(BF16))
    re_ref[0] = out[:rows].reshape(n1, sub, tc)
    im_ref[0] = out[rows:].reshape(n1, sub, tc)


def _dft_stage_b_kernel(m_ref, re_ref, im_ref, twc_ref, tws_ref, o_ref):
    n2, sub, tc = o_ref.shape[1:]
    reps = tc // LANES
    ar, ai = re_ref[0], im_ref[0]
    twc = jnp.concatenate([twc_ref[...]] * reps, axis=1)
    tws = jnp.concatenate([tws_ref[...]] * reps, axis=1)
    stacked = jnp.concatenate([ar * twc + ai * tws, ai * twc - ar * tws], axis=0).astype(BF16)
    out = _dot(m_ref[...], stacked)
    o_ref[0] = out.reshape(n2, sub, tc).astype(o_ref.dtype)


def _dft_ctx_kernel(m_ref, x_ref, y_hbm_ref, o_ref, *, d):
    del y_hbm_ref
    x = x_ref[0]
    stacked = jnp.concatenate([x[:, :d], x[:, d:]], axis=0).astype(BF16)
    o_ref[0] = _dot(m_ref[...], stacked).astype(o_ref.dtype)


def _cos_sin(n):
    ang = 2.0 * np.pi * np.outer(np.arange(n), np.arange(n)) / float(n)
    return np.cos(ang), np.sin(ang)


def _fourier_positions(uv, n_lat, n_ctx, d):
    b, s, _ = uv.shape
    n2 = FFT_INNER
    n1 = n_lat // n2
    tc = min(512, d)
    ca, sa = _cos_sin(n1)
    eye = np.eye(SUBLANES)
    ka, ks = np.kron(ca, eye) / math.sqrt(n1), np.kron(sa, eye) / math.sqrt(n1)
    mat_a = jnp.asarray(np.block([[ka, -ks], [-ks, -ka]]), BF16)
    uv4 = uv.reshape(b, s // n2, n2, 2 * d)
    blk_a = (1, n1, SUBLANES, tc)
    a_re, a_im = pl.pallas_call(
        _dft_stage_a_kernel,
        name="dft_stage_a",
        grid=(b, n2 // SUBLANES, d // tc),
        in_specs=[
            pl.BlockSpec(mat_a.shape, lambda bi, cg, jc: (0, 0)),
            pl.BlockSpec(blk_a, lambda bi, cg, jc: (bi, 0, cg, jc)),
            pl.BlockSpec(blk_a, lambda bi, cg, jc: (bi, 0, cg, d // tc + jc)),
        ],
        out_specs=[pl.BlockSpec(blk_a, lambda bi, cg, jc: (bi, 0, cg, jc))] * 2,
        out_shape=[jax.ShapeDtypeStruct((b, n1, n2, d), F32)] * 2,
        compiler_params=_params(("parallel", "parallel", "parallel")),
    )(mat_a, uv4, uv4)
    cb, sb = _cos_sin(n2)
    eye = np.eye(PACKED_ROWS)
    kron_b = lambda m: np.einsum("kc,ab->kabc", m, eye).reshape(n2 * PACKED_ROWS, PACKED_ROWS * n2)
    mat_b = jnp.asarray(np.concatenate([kron_b(cb), kron_b(sb)], axis=1) / math.sqrt(n2), BF16)
    phi = 2.0 * np.pi * np.outer(np.arange(n1), np.arange(n2)) / float(n_lat)
    twc = jnp.asarray(np.repeat(np.cos(phi).reshape(-1, 1), LANES, axis=1), F32)
    tws = jnp.asarray(np.repeat(np.sin(phi).reshape(-1, 1), LANES, axis=1), F32)
    rows = PACKED_ROWS * n2
    y = pl.pallas_call(
        _dft_stage_b_kernel,
        name="dft_stage_b",
        grid=(b, n1 // PACKED_ROWS, d // tc),
        in_specs=[
            pl.BlockSpec(mat_b.shape, lambda bi, kb, jc: (0, 0)),
            pl.BlockSpec((1, rows, tc), lambda bi, kb, jc: (bi, kb, jc)),
            pl.BlockSpec((1, rows, tc), lambda bi, kb, jc: (bi, kb, jc)),
            pl.BlockSpec((rows, LANES), lambda bi, kb, jc: (kb, 0)),
            pl.BlockSpec((rows, LANES), lambda bi, kb, jc: (kb, 0)),
        ],
        out_specs=pl.BlockSpec((1, n2, PACKED_ROWS, tc), lambda bi, kb, jc: (bi, 0, kb, jc)),
        out_shape=jax.ShapeDtypeStruct((b, s // n1, n1, d), BF16),
        compiler_params=_params(("parallel", "parallel", "parallel")),
    )(mat_b, a_re.reshape(b, n_lat, d), a_im.reshape(b, n_lat, d), twc, tws)
    cc, sc = _cos_sin(n_ctx)
    mat_c = jnp.asarray(np.concatenate([cc, -sc], axis=1) / math.sqrt(n_ctx), BF16)
    return pl.pallas_call(
        functools.partial(_dft_ctx_kernel, d=d),
        name="dft_ctx",
        grid=(b,),
        in_specs=[
            pl.BlockSpec(mat_c.shape, lambda bi: (0, 0)),
            pl.BlockSpec((1, n_ctx, 2 * d), lambda bi: (bi, n_lat // n_ctx, 0)),
            pl.BlockSpec(memory_space=pl.ANY),
        ],
        out_specs=pl.BlockSpec((1, n_ctx, d), lambda bi: (bi, n_lat // n_ctx, 0)),
        out_shape=jax.ShapeDtypeStruct((b, s, d), BF16),
        input_output_aliases={2: 0},
        compiler_params=_params(("parallel",)),
    )(mat_c, uv, y.reshape(b, s, d))


def _route_kernel(aff_ref, tri_ref, posc_ref, gate_ref, posr_ref, tst_ref,
                  thr_ref, need_ref, ctie_ref, cpos_ref, *, cap):
    t = pl.program_id(1)
    tile = posc_ref.shape[1]

    @pl.when(t == 0)
    def _():
        def body(it, thr):
            bits = lax.bitcast_convert_type(aff_ref[0], I32)
            cand = thr | jnp.left_shift(jnp.int32(1), 30 - it)
            cnt = jnp.sum(jnp.where(bits >= cand, 1.0, 0.0), axis=0, keepdims=True)
            return jnp.where(cnt >= cap, cand, thr)

        thr = lax.fori_loop(0, 31, body, jnp.zeros((1, LANES), I32))
        bits = lax.bitcast_convert_type(aff_ref[0], I32)
        above = jnp.sum(jnp.where(bits > thr, 1.0, 0.0), axis=0, keepdims=True)
        thr_ref[...] = thr
        need_ref[...] = float(cap) - above
        ctie_ref[...] = jnp.zeros_like(ctie_ref)
        cpos_ref[...] = jnp.zeros_like(cpos_ref)

    a = aff_ref[0, pl.ds(pl.multiple_of(t * tile, tile), tile), :]
    bits = lax.bitcast_convert_type(a, I32)
    thr = thr_ref[...]
    gt = bits > thr
    eq = bits == thr
    eqf = jnp.where(eq, 1.0, 0.0)
    tie_rank = _dot(tri_ref[...], eqf.astype(BF16)) + ctie_ref[...]
    sel = gt | (eq & (tie_rank < need_ref[...]))
    self_ = jnp.where(sel, 1.0, 0.0)
    start = cpos_ref[...]
    pos = jnp.where(sel, _dot(tri_ref[...], self_.astype(BF16)) + start, -1.0)
    posc_ref[0] = pos.astype(I32)
    gate_ref[0] = jnp.where(sel, a, 0.0)
    posr_ref[0] = pos.T[:N_EXPERTS].astype(I32)
    tst_ref[0, 0] = jnp.broadcast_to(start, (8, LANES)).astype(I32)
    ctie_ref[...] = ctie_ref[...] + jnp.sum(eqf, axis=0, keepdims=True)
    cpos_ref[...] = start + jnp.sum(self_, axis=0, keepdims=True)


def _route(aff, *, row_off, n, cap):
    b = aff.shape[0]
    tile = ROUTE_TILE
    nt = n // tile
    tri = jnp.asarray(np.tril(np.ones((tile, tile)), -1), BF16)
    return pl.pallas_call(
        functools.partial(_route_kernel, cap=cap),
        name="moe_route",
        grid=(b, nt),
        in_specs=[
            pl.BlockSpec((1, n, LANES), lambda bi, t: (bi, row_off // n, 0)),
            pl.BlockSpec((tile, tile), lambda bi, t: (0, 0)),
        ],
        out_specs=[
            pl.BlockSpec((1, tile, LANES), lambda bi, t: (bi, t, 0)),
            pl.BlockSpec((1, tile, LANES), lambda bi, t: (bi, t, 0)),
            pl.BlockSpec((1, N_EXPERTS, tile), lambda bi, t: (bi, 0, t)),
            pl.BlockSpec((1, 1, 8, LANES), lambda bi, t: (bi, t, 0, 0)),
        ],
        out_shape=[
            jax.ShapeDtypeStruct((b, n, LANES), I32),
            jax.ShapeDtypeStruct((b, n, LANES), F32),
            jax.ShapeDtypeStruct((b, N_EXPERTS, n), I32),
            jax.ShapeDtypeStruct((b, nt, 8, LANES), I32),
        ],
        scratch_shapes=[pltpu.VMEM((1, LANES), I32)] + [pltpu.VMEM((1, LANES), F32)] * 3,
        compiler_params=_params(("parallel", "arbitrary")),
    )(aff, tri)


def _window_start(ts_ref, idx, cap_rows, win):
    a0 = jnp.minimum(ts_ref[idx] & (-SLOT_ALIGN), cap_rows - win)
    return pl.multiple_of(a0, SLOT_ALIGN)


def _gather_kernel(ts_ref, h_ref, posr_ref, xe_ref, *, group, nt, win):
    bi, eg, t = pl.program_id(0), pl.program_id(1), pl.program_id(2)
    cap_rows = xe_ref.shape[2]

    @pl.when(t == 0)
    def _():
        xe_ref[...] = jnp.zeros_like(xe_ref)

    h = h_ref[0]
    tile = h.shape[0]
    for jx in range(group):
        e = eg * group + jx
        a0 = _window_start(ts_ref, (bi * nt + t) * N_EXPERTS + e, cap_rows, win)
        prow = posr_ref[0, pl.ds(e, 1), :]
        slot = a0 + lax.broadcasted_iota(I32, (win, tile), 0)
        onehot = jnp.where(prow == slot, 1.0, 0.0).astype(BF16)
        cur = xe_ref[0, jx, pl.ds(a0, win), :].astype(F32)
        xe_ref[0, jx, pl.ds(a0, win), :] = (cur + _dot(onehot, h)).astype(xe_ref.dtype)


def _gather(tstart, h, posr, *, row_off, n, cap_rows):
    b, _, d = h.shape
    tile = ROUTE_TILE
    nt = n // tile
    group = 2
    win = min(tile + SLOT_ALIGN, cap_rows)
    off = row_off // tile
    grid_spec = pltpu.PrefetchScalarGridSpec(
        num_scalar_prefetch=1,
        grid=(b, N_EXPERTS // group, nt),
        in_specs=[
            pl.BlockSpec((1, tile, d), lambda bi, eg, t, ts: (bi, off + t, 0)),
            pl.BlockSpec((1, N_EXPERTS, tile), lambda bi, eg, t, ts: (bi, 0, t)),
        ],
        out_specs=pl.BlockSpec((1, group, cap_rows, d), lambda bi, eg, t, ts: (bi, eg, 0, 0)),
    )
    return pl.pallas_call(
        functools.partial(_gather_kernel, group=group, nt=nt, win=win),
        name="moe_gather",
        grid_spec=grid_spec,
        out_shape=jax.ShapeDtypeStruct((b, N_EXPERTS, cap_rows, d), BF16),
        compiler_params=_params(("parallel", "parallel", "arbitrary")),
    )(tstart, h, posr)


def _ffn_kernel(x_ref, wg_ref, wu_ref, wd_ref, ye_ref, acc_ref, hm_ref, *, n_up, n_down):
    st = pl.program_id(2)
    f = wg_ref.shape[3]
    fc, d = wd_ref.shape[2:]

    @pl.when(st == 0)
    def _():
        acc_ref[...] = jnp.zeros_like(acc_ref)

    @pl.when(st < n_up)
    def _():
        x = x_ref[0, 0]
        acc_ref[:, :f] += _dot(x, wg_ref[0, 0].astype(BF16))
        acc_ref[:, f:2 * f] += _dot(x, wu_ref[0, 0].astype(BF16))

    @pl.when(st == n_up)
    def _():
        hm = (_silu(acc_ref[:, :f]) * acc_ref[:, f:2 * f]).astype(BF16)
        for c in range(n_down):
            hm_ref[c] = hm[:, c * fc:(c + 1) * fc]
        acc_ref[...] = jnp.zeros_like(acc_ref)

    @pl.when(st >= n_up)
    def _():
        acc_ref[:, :d] += _dot(hm_ref[st - n_up], wd_ref[0, 0].astype(BF16))

    @pl.when(st == n_up + n_down - 1)
    def _():
        ye_ref[0, 0] = acc_ref[:, :d].astype(ye_ref.dtype)


def _ffn(xe, w_gate, w_up, w_down, layer):
    b, e, cap_rows, d = xe.shape
    f = w_gate.shape[3]
    n_up, n_down = 4, 2
    kc, fc = d // n_up, f // n_down
    up = lambda st: jnp.minimum(st, n_up - 1)
    down = lambda st: jnp.maximum(st - n_up, 0)
    return pl.pallas_call(
        functools.partial(_ffn_kernel, n_up=n_up, n_down=n_down),
        name="moe_ffn",
        grid=(b, e, n_up + n_down),
        in_specs=[
            pl.BlockSpec((1, 1, cap_rows, kc), lambda bi, ei, st: (bi, ei, 0, up(st))),
            pl.BlockSpec((1, 1, kc, f), lambda bi, ei, st: (layer, ei, up(st), 0)),
            pl.BlockSpec((1, 1, kc, f), lambda bi, ei, st: (layer, ei, up(st), 0)),
            pl.BlockSpec((1, 1, fc, d), lambda bi, ei, st: (layer, ei, down(st), 0)),
        ],
        out_specs=pl.BlockSpec((1, 1, cap_rows, d), lambda bi, ei, st: (bi, ei, 0, 0)),
        out_shape=jax.ShapeDtypeStruct((b, e, cap_rows, d), BF16),
        scratch_shapes=[pltpu.VMEM((cap_rows, max(2 * f, d)), F32), pltpu.VMEM((n_down, cap_rows, fc), BF16)],
        compiler_params=_params(("parallel", "parallel", "arbitrary")),
    )(xe, w_gate, w_up, w_down)


def _combine_kernel(ts_ref, ye_ref, posc_ref, gate_ref, x_ref, g_ref, o_ref, *, nt, win):
    bi, t = pl.program_id(0), pl.program_id(2)
    cap_rows = ye_ref.shape[2]
    tile = posc_ref.shape[1]
    acc = jnp.zeros(o_ref.shape[1:], F32)
    lane = lax.broadcasted_iota(I32, (tile, win), 1)
    for e in range(N_EXPERTS):
        a0 = _window_start(ts_ref, (bi * nt + t) * N_EXPERTS + e, cap_rows, win)
        rows = ye_ref[0, e, pl.ds(a0, win), :]
        pcol = posc_ref[0, :, e:e + 1]
        gcol = gate_ref[0, :, e:e + 1]
        weights = jnp.where(pcol == a0 + lane, gcol, 0.0).astype(BF16)
        acc = acc + _dot(weights, rows)
    o_ref[0] = x_ref[0] + g_ref[0, 0] * acc


def _combine(tstart, ye, posc, gate, x, g2, *, row_off, n, region):
    b, s, d = x.shape
    cap_rows = ye.shape[2]
    tile = ROUTE_TILE
    nt = n // tile
    dc = 512 if d % 512 == 0 else d
    win = min(2 * tile, cap_rows)
    off = row_off // tile
    grid_spec = pltpu.PrefetchScalarGridSpec(
        num_scalar_prefetch=1,
        grid=(b, d // dc, nt),
        in_specs=[
            pl.BlockSpec((1, N_EXPERTS, cap_rows, dc), lambda bi, c, t, ts: (bi, 0, 0, c)),
            pl.BlockSpec((1, tile, LANES), lambda bi, c, t, ts: (bi, t, 0)),
            pl.BlockSpec((1, tile, LANES), lambda bi, c, t, ts: (bi, t, 0)),
            pl.BlockSpec((1, tile, dc), lambda bi, c, t, ts: (bi, off + t, c)),
            pl.BlockSpec((1, 1, 1, dc), lambda bi, c, t, ts: (bi, region, 0, c)),
        ],
        out_specs=pl.BlockSpec((1, tile, dc), lambda bi, c, t, ts: (bi, off + t, c)),
    )
    return pl.pallas_call(
        functools.partial(_combine_kernel, nt=nt, win=win),
        name="moe_combine",
        grid_spec=grid_spec,
        out_shape=jax.ShapeDtypeStruct((b, s, d), F32),
        input_output_aliases={4: 0},
        compiler_params=_params(("parallel", "parallel", "arbitrary")),
    )(tstart, ye, posc, gate, x, g2)


def _moe(x, h, aff, g2, w_gate, w_up, w_down, layer, *, row_off, n, region):
    cap = CAPACITY_FACTOR * n // N_EXPERTS
    cap_rows = max(cap, MIN_CAP_ROWS)
    posc, gate, posr, tst = _route(aff, row_off=row_off, n=n, cap=cap)
    tstart = tst[:, :, 0, :N_EXPERTS].reshape(-1)
    xe = _gather(tstart, h, posr, row_off=row_off, n=n, cap_rows=cap_rows)
    ye = _ffn(xe, w_gate, w_up, w_down, layer)
    return _combine(tstart, ye, posc, gate, x, g2, row_off=row_off, n=n, region=region)


def _rope_tables(n_lat, n_ctx, head_dim):
    quarter = head_dim // 4
    inv = ROPE_BASE ** (-jnp.arange(quarter, dtype=F32) / quarter)
    pos = jnp.arange(n_lat)
    row = (pos // GRID_W).astype(F32)[:, None] * inv[None, :]
    col = (pos % GRID_W).astype(F32)[:, None] * inv[None, :]
    cos = jnp.concatenate([jnp.cos(row), jnp.cos(row), jnp.cos(col), jnp.cos(col)], axis=1)
    sin = jnp.concatenate([-jnp.sin(row), jnp.sin(row), -jnp.sin(col), jnp.sin(col)], axis=1)
    cos = jnp.concatenate([cos, jnp.ones((n_ctx, head_dim), F32)], axis=0)
    sin = jnp.concatenate([sin, jnp.zeros((n_ctx, head_dim), F32)], axis=0)
    return cos, sin


def kernel(x, c, ctx, c_ctx, w_mod, b_mod, norm_gain, final_gain, ret_w_in, ret_w_out, ret_decay,
           win_w_qkv, win_w_o, win_sink, fno_w_o, router_w, exp_w_gate, exp_w_up, exp_w_down):
    b, n_lat, d = x.shape
    n_ctx = ctx.shape[1]
    depth = w_mod.shape[0]
    s = n_lat + n_ctx
    assert s % ROW_TILE == 0 and n_lat % NORM_TILE == 0 and n_ctx % NORM_TILE == 0 and n_lat % n_ctx == 0
    assert n_lat % (FFT_INNER * 4) == 0 and b + 1 <= 8 and n_ctx % ROUTE_TILE == 0

    xs = jnp.concatenate([x, ctx], axis=1)
    cvec = jnp.zeros((8, d), F32).at[:b].set(c).at[b].set(c_ctx)
    mod = _modulation(cvec, w_mod, b_mod)

    def mod_pair(i, k):
        lat = mod[i, :b, k * d:(k + 1) * d]
        cx = jnp.broadcast_to(mod[i, b, k * d:(k + 1) * d], (b, d))
        return jnp.stack([lat, cx], axis=1).reshape(b, 2, 1, d)

    ret_dk = d // RET_HEADS
    ret_dv = 2 * ret_dk
    win_hd = d // WIN_HEADS
    ret_cos, ret_sin = _rope_tables(n_lat, n_ctx, ret_dk)
    win_cos, win_sin = _rope_tables(n_lat, n_ctx, win_hd)

    for i in range(depth):
        kind, j = i % N_MIXERS, i // N_MIXERS
        last = i == depth - 1
        sh1, sc1, g1, sh2, sc2, g2 = [mod_pair(i, k) for k in range(6)]
        h = _norm_mod(xs, norm_gain[i, 0], sh1, sc1, n_lat)
        if kind == 0:
            hk, hv = RET_HEADS * ret_dk, RET_HEADS * ret_dv
            w_in = ret_w_in[j].astype(BF16)
            qk = _mm_proj(h, w_in, 0, 2 * hk, PROJ_COL_TILE,
                          rope=(ret_cos, ret_sin, hk, ret_dk ** -0.5, ret_dk))
            vg = _mm_proj(h, w_in, 2 * hk, 3 * hv, PROJ_COL_TILE)
            scan = functools.partial(_ret_scan, n_lat=n_lat, n_ctx=n_ctx, heads=RET_HEADS, dk=ret_dk, dv=ret_dv)
            y_b = scan(qk, vg, ret_decay[j, 1:2], None, reverse=True)
            y = scan(qk, vg, ret_decay[j, 0:1], y_b, reverse=False)
            w_out = ret_w_out[j]
        elif kind == 1:
            nq, nkv = WIN_HEADS * win_hd, WIN_KV_HEADS * win_hd
            w_qkv = win_w_qkv[j].astype(BF16)
            qk = _mm_proj(h, w_qkv, 0, nq + nkv, COL_TILE, rope=(win_cos, win_sin, nq, 1.0, win_hd))
            v = _mm_proj(h, w_qkv, nq + nkv, nkv, COL_TILE)
            y = _win_attn(qk, v, win_sink[j], n_lat=n_lat, n_ctx=n_ctx, heads=WIN_HEADS,
                          kv_heads=WIN_KV_HEADS, hd=win_hd)
            w_out = win_w_o[j]
        else:
            cg = d // FOURIER_GROUPS
            cc, sc = _cos_sin(cg)
            w_ch = jnp.asarray(np.concatenate([cc, sc], axis=1) / math.sqrt(cg), BF16)
            uv = _mm_groups(h, w_ch, FOURIER_GROUPS)
            y = _fourier_positions(uv, n_lat, n_ctx, d)
            w_out = fno_w_o[j]
        xs = _mm_res(y, w_out.astype(BF16), xs, g1[:, 0], g1[:, 1], n_lat)
        h2, aff = _norm_mod(xs, norm_gain[i, 1], sh2, sc2, n_lat, router_w=router_w[i])
        xs = _moe(xs, h2, aff, g2, exp_w_gate, exp_w_up, exp_w_down, i, row_off=0, n=n_lat, region=0)
        if not last:
            xs = _moe(xs, h2, aff, g2, exp_w_gate, exp_w_up, exp_w_down, i, row_off=n_lat, n=n_ctx, region=1)

    zeros = jnp.zeros((b, 2, 1, d), F32)
    return _norm_mod(xs, final_gain, zeros, zeros, n_lat, out_dtype=F32, rows=n_lat)
```

```python
import functools
import math

import numpy as np
import jax
import jax.numpy as jnp
from jax import lax
from jax.experimental import pallas as pl
from jax.experimental.pallas import tpu as pltpu

F32 = jnp.float32
BF16 = jnp.bfloat16
I32 = jnp.int32

GRID_W = 64
N_MIXERS = 3
RET_HEADS = 8
RET_CHUNK = 128
WIN_HEADS = 16
WIN_KV_HEADS = 4
WIN_BLOCK = 128
FOURIER_GROUPS = 4
FFT_INNER = 64
N_EXPERTS = 16
CAPACITY_FACTOR = 2
ROPE_BASE = 10000.0
NORM_EPS = 1e-6
NEG_INF = -1e30

LANES = 128
ROUTE_TILE = 128
SLOT_ALIGN = 16
ROW_TILE = 768
MM_ROW_CHUNK = 256
COL_TILE = 512
PROJ_COL_TILE = 1024
NORM_TILE = 256
VMEM_LIMIT_BYTES = 56 * 1024 * 1024


def _params(sem):
    return pltpu.CompilerParams(dimension_semantics=sem, vmem_limit_bytes=VMEM_LIMIT_BYTES)


def _dot(a, b):
    return jnp.dot(a, b, preferred_element_type=F32)


def _dot_nt(a, b):
    return lax.dot_general(a, b, (((1,), (1,)), ((), ())), preferred_element_type=F32)


def _split_bf16(x):
    hi = x.astype(BF16)
    lo = (x - hi.astype(F32)).astype(BF16)
    return hi, lo


def _silu(x):
    return x / (1.0 + jnp.exp(-x))


def _mod_kernel(c_ref, w_ref, b_ref, o_ref):
    s = _silu(c_ref[...])
    sh, sl = _split_bf16(s)
    wh, wl = _split_bf16(w_ref[0])
    o_ref[0] = _dot(sh, wh) + _dot(sl, wh) + _dot(sh, wl) + b_ref[0]


def _modulation(cvec, w_mod, b_mod):
    depth, d, n = w_mod.shape
    tn = 1024
    return pl.pallas_call(
        _mod_kernel,
        name="modulation",
        grid=(depth, n // tn),
        in_specs=[
            pl.BlockSpec((8, d), lambda i, j: (0, 0)),
            pl.BlockSpec((1, d, tn), lambda i, j: (i, 0, j)),
            pl.BlockSpec((1, 1, tn), lambda i, j: (i, 0, j)),
        ],
        out_specs=pl.BlockSpec((1, 8, tn), lambda i, j: (i, 0, j)),
        out_shape=jax.ShapeDtypeStruct((depth, 8, n), F32),
        compiler_params=_params(("parallel", "parallel")),
    )(cvec, w_mod, b_mod.reshape(depth, 1, n))


def _normed(x_ref, gain_ref, shift_ref, scale_ref):
    x = x_ref[0]
    ms = jnp.mean(x * x, axis=-1, keepdims=True)
    y = x * lax.rsqrt(ms + NORM_EPS) * gain_ref[...]
    return y * (1.0 + scale_ref[0, 0]) + shift_ref[0, 0]


def _norm_mod_kernel(x_ref, gain_ref, shift_ref, scale_ref, o_ref):
    o_ref[0] = _normed(x_ref, gain_ref, shift_ref, scale_ref).astype(o_ref.dtype)


def _norm_router_kernel(x_ref, gain_ref, shift_ref, scale_ref, wr_ref, o_ref, aff_ref):
    h = _normed(x_ref, gain_ref, shift_ref, scale_ref)
    o_ref[0] = h.astype(o_ref.dtype)
    hh, hl = _split_bf16(h)
    wh, wl = _split_bf16(wr_ref[...])
    logits = _dot(hh, wh) + _dot(hl, wh) + _dot(hh, wl)
    lane = lax.broadcasted_iota(I32, logits.shape, 1)
    valid = lane < N_EXPERTS
    logits = jnp.where(valid, logits, -jnp.inf)
    m = jnp.max(logits, axis=-1, keepdims=True)
    p = jnp.exp(logits - m)
    aff = p / jnp.sum(p, axis=-1, keepdims=True)
    aff_ref[0] = jnp.where(valid, aff, 0.0)


def _norm_mod(x, gain, shift, scale, n_lat, *, out_dtype=BF16, rows=None, router_w=None):
    b, s, d = x.shape
    rows = s if rows is None else rows
    tr = NORM_TILE
    lat_tiles = n_lat // tr
    region = lambda bi, t: (bi, jnp.where(t >= lat_tiles, 1, 0), 0, 0)
    in_specs = [
        pl.BlockSpec((1, tr, d), lambda bi, t: (bi, t, 0)),
        pl.BlockSpec((1, d), lambda bi, t: (0, 0)),
        pl.BlockSpec((1, 1, 1, d), region),
        pl.BlockSpec((1, 1, 1, d), region),
    ]
    args = [x, gain.reshape(1, d), shift, scale]
    out_specs = pl.BlockSpec((1, tr, d), lambda bi, t: (bi, t, 0))
    out_shape = jax.ShapeDtypeStruct((b, rows, d), out_dtype)
    kern = _norm_mod_kernel
    if router_w is not None:
        wr = jnp.zeros((d, LANES), F32).at[:, :N_EXPERTS].set(router_w)
        in_specs.append(pl.BlockSpec((d, LANES), lambda bi, t: (0, 0)))
        args.append(wr)
        out_specs = [out_specs, pl.BlockSpec((1, tr, LANES), lambda bi, t: (bi, t, 0))]
        out_shape = [out_shape, jax.ShapeDtypeStruct((b, rows, LANES), F32)]
        kern = _norm_router_kernel
    return pl.pallas_call(
        kern,
        name="norm_mod" if router_w is None else "norm_router",
        grid=(b, rows // tr),
        in_specs=in_specs,
        out_specs=out_specs,
        out_shape=out_shape,
        compiler_params=_params(("parallel", "parallel")),
    )(*args)


def _rope_partner(xs, quarter):
    if 2 * quarter == LANES:
        return pltpu.roll(xs, quarter, 1)
    back = pltpu.roll(xs, quarter, 1)
    fwd = pltpu.roll(xs, LANES - quarter, 1)
    lane = lax.broadcasted_iota(I32, xs.shape, 1)
    return jnp.where((lane % (2 * quarter)) < quarter, fwd, back)


def _row_chunks(tm):
    return [slice(r, r + MM_ROW_CHUNK) for r in range(0, tm, MM_ROW_CHUNK)]


def _mm_rope_kernel(a_ref, w_ref, cos_ref, sin_ref, o_ref, *, n_q, kscale, head_dim):
    sc = jnp.where(pl.program_id(2) >= n_q, kscale, 1.0).astype(F32)
    for rows in _row_chunks(a_ref.shape[1]):
        acc = _dot(a_ref[0, rows, :], w_ref[...])
        for s in range(acc.shape[1] // LANES):
            cols = slice(s * LANES, (s + 1) * LANES)
            off = (s * LANES) % head_dim
            xs = acc[:, cols]
            rot = xs * cos_ref[rows, off:off + LANES] + _rope_partner(xs, head_dim // 4) * sin_ref[rows, off:off + LANES]
            o_ref[0, rows, cols] = (rot * sc).astype(o_ref.dtype)


def _mm_plain_kernel(a_ref, w_ref, o_ref):
    for rows in _row_chunks(a_ref.shape[1]):
        o_ref[0, rows, :] = _dot(a_ref[0, rows, :], w_ref[...]).astype(o_ref.dtype)


def _mm_proj(a, w, col0, ncols, tn, rope=None):
    b, s, k = a.shape
    tm = ROW_TILE
    assert col0 % tn == 0 and ncols % tn == 0
    in_specs = [
        pl.BlockSpec((1, tm, k), lambda bi, i, j: (bi, i, 0)),
        pl.BlockSpec((k, tn), lambda bi, i, j: (0, col0 // tn + j)),
    ]
    args = [a, w]
    if rope is None:
        kern, name = _mm_plain_kernel, "mm_proj"
    else:
        cos_t, sin_t, n_q_cols, kscale, head_dim = rope
        assert tn % head_dim == 0 and n_q_cols % tn == 0
        kern = functools.partial(_mm_rope_kernel, n_q=n_q_cols // tn, kscale=kscale, head_dim=head_dim)
        name = "mm_proj_rope"
        in_specs += [pl.BlockSpec((tm, head_dim), lambda bi, i, j: (i, 0))] * 2
        args += [cos_t, sin_t]
    return pl.pallas_call(
        kern,
        name=name,
        grid=(b, s // tm, ncols // tn),
        in_specs=in_specs,
        out_specs=pl.BlockSpec((1, tm, tn), lambda bi, i, j: (bi, i, j)),
        out_shape=jax.ShapeDtypeStruct((b, s, ncols), BF16),
        compiler_params=_params(("parallel", "parallel", "arbitrary")),
    )(*args)


def _mm_res_kernel(a_ref, w_ref, x_ref, gl_ref, gc_ref, o_ref, *, n_lat):
    tm = a_ref.shape[1]
    for rows in _row_chunks(tm):
        acc = _dot(a_ref[0, rows, :], w_ref[...])
        row = pl.program_id(1) * tm + rows.start + lax.broadcasted_iota(I32, (acc.shape[0], 1), 0)
        gate = jnp.where(row < n_lat, gl_ref[0], gc_ref[0])
        o_ref[0, rows, :] = x_ref[0, rows, :] + gate * acc


def _mm_res(a, w, x, gate_lat, gate_ctx, n_lat):
    b, s, k = a.shape
    n = w.shape[1]
    tm, tn = ROW_TILE, COL_TILE
    return pl.pallas_call(
        functools.partial(_mm_res_kernel, n_lat=n_lat),
        name="mm_residual",
        grid=(b, s // tm, n // tn),
        in_specs=[
            pl.BlockSpec((1, tm, k), lambda bi, i, j: (bi, i, 0)),
            pl.BlockSpec((k, tn), lambda bi, i, j: (0, j)),
            pl.BlockSpec((1, tm, tn), lambda bi, i, j: (bi, i, j)),
            pl.BlockSpec((1, 1, tn), lambda bi, i, j: (bi, 0, j)),
            pl.BlockSpec((1, 1, tn), lambda bi, i, j: (bi, 0, j)),
        ],
        out_specs=pl.BlockSpec((1, tm, tn), lambda bi, i, j: (bi, i, j)),
        out_shape=jax.ShapeDtypeStruct((b, s, n), F32),
        compiler_params=_params(("parallel", "parallel", "arbitrary")),
    )(a, w, x, gate_lat, gate_ctx)


def _mm_groups_kernel(a_ref, w_ref, o_ref):
    o_ref[0] = _dot(a_ref[0], w_ref[...]).astype(o_ref.dtype)


def _mm_groups(a, w, groups):
    b, s, d = a.shape
    cg = d // groups
    tm = ROW_TILE
    return pl.pallas_call(
        _mm_groups_kernel,
        name="mm_channel_dft",
        grid=(b, s // tm, 2 * groups),
        in_specs=[
            pl.BlockSpec((1, tm, cg), lambda bi, i, j: (bi, i, j % groups)),
            pl.BlockSpec((cg, cg), lambda bi, i, j: (0, j // groups)),
        ],
        out_specs=pl.BlockSpec((1, tm, cg), lambda bi, i, j: (bi, i, j)),
        out_shape=jax.ShapeDtypeStruct((b, s, 2 * d), F32),
        compiler_params=_params(("parallel", "parallel", "arbitrary")),
    )(a, w)


def _ret_scan_kernel(*refs, heads, dk, dv, reverse, add_in):
    if add_in:
        dec_ref, q_ref, k_ref, v_ref, g_ref, yin_ref, o_ref, s_ref, qd_ref, kd_ref, in_ref, cd_ref = refs
    else:
        dec_ref, q_ref, k_ref, v_ref, g_ref, o_ref, s_ref, qd_ref, kd_ref, in_ref, cd_ref = refs
        yin_ref = None
    c = q_ref.shape[1]
    j = pl.program_id(1)

    @pl.when(j == 0)
    def _():
        s_ref[...] = jnp.zeros_like(s_ref)
        m_col = lax.broadcasted_iota(I32, (c, LANES), 0).astype(F32)
        m_row = lax.broadcasted_iota(I32, (c, c), 0).astype(F32)
        n_row = lax.broadcasted_iota(I32, (c, c), 1).astype(F32)
        for h in range(heads):
            lg = -jnp.exp(dec_ref[:, h:h + 1])
            if reverse:
                q_pow, k_pow, diff = c - m_col, m_col, n_row - m_row
            else:
                q_pow, k_pow, diff = m_col + 1.0, c - 1.0 - m_col, m_row - n_row
            qd_ref[h] = jnp.exp(lg * q_pow)
            kd_ref[h] = jnp.exp(lg * k_pow)
            in_ref[h] = jnp.where(diff >= 0, jnp.exp(lg * jnp.maximum(diff, 0.0)), 0.0)
            cd_ref[h] = jnp.exp(jnp.broadcast_to(lg, (8, LANES)) * float(c))

    for h in range(heads):
        q = q_ref[0, :, h * dk:(h + 1) * dk]
        k = k_ref[0, :, h * dk:(h + 1) * dk]
        v = v_ref[0, :, h * dv:(h + 1) * dv]
        qdec = jnp.concatenate([qd_ref[h]] * (dk // LANES), axis=1)
        kdec = jnp.concatenate([kd_ref[h]] * (dk // LANES), axis=1)
        state = s_ref[h]
        cross = _dot((q.astype(F32) * qdec).astype(BF16), state.astype(BF16))
        scores = _dot_nt(q, k) * in_ref[h]
        o = cross + _dot(scores.astype(BF16), v)
        k_t = (k.astype(F32) * kdec).T.astype(BF16)
        s_ref[h] = state * cd_ref[h][0:1, 0:1] + _dot(k_t, v)
        mu = jnp.mean(o, axis=-1, keepdims=True)
        cen = o - mu
        var = jnp.mean(cen * cen, axis=-1, keepdims=True)
        g = g_ref[0, :, h * dv:(h + 1) * dv].astype(F32)
        y = cen * lax.rsqrt(var + NORM_EPS) * _silu(g)
        if add_in:
            y = y + yin_ref[0, :, h * dv:(h + 1) * dv].astype(F32)
        o_ref[0, :, h * dv:(h + 1) * dv] = y.astype(o_ref.dtype)


def _ret_scan(qk, vg, decay_row, y_in, *, n_lat, n_ctx, heads, dk, dv, reverse):
    b, s, _ = qk.shape
    c = RET_CHUNK
    lat_chunks, ctx_chunks = n_lat // c, n_ctx // c
    steps = lat_chunks + ctx_chunks
    hk, hv = heads * dk, heads * dv
    if reverse:
        chunk = lambda j: steps - 1 - j
    else:
        chunk = lambda j: jnp.where(j < ctx_chunks, lat_chunks + j, j - ctx_chunks)
    gate_blk = 2 if reverse else 1
    in_specs = [
        pl.BlockSpec((1, heads), lambda bi, j: (0, 0)),
        pl.BlockSpec((1, c, hk), lambda bi, j: (bi, chunk(j), 0)),
        pl.BlockSpec((1, c, hk), lambda bi, j: (bi, chunk(j), 1)),
        pl.BlockSpec((1, c, hv), lambda bi, j: (bi, chunk(j), 0)),
        pl.BlockSpec((1, c, hv), lambda bi, j: (bi, chunk(j), gate_blk)),
    ]
    args = [decay_row, qk, qk, vg, vg]
    if y_in is not None:
        in_specs.append(pl.BlockSpec((1, c, hv), lambda bi, j: (bi, chunk(j), 0)))
        args.append(y_in)
    kern = functools.partial(_ret_scan_kernel, heads=heads, dk=dk, dv=dv, reverse=reverse,
                             add_in=y_in is not None)
    return pl.pallas_call(
        kern,
        name="ret_scan_bwd" if reverse else "ret_scan_fwd",
        grid=(b, steps),
        in_specs=in_specs,
        out_specs=pl.BlockSpec((1, c, hv), lambda bi, j: (bi, chunk(j), 0)),
        out_shape=jax.ShapeDtypeStruct((b, s, hv), BF16),
        scratch_shapes=[
            pltpu.VMEM((heads, dk, dv), F32),
            pltpu.VMEM((heads, c, LANES), F32),
            pltpu.VMEM((heads, c, LANES), F32),
            pltpu.VMEM((heads, c, c), F32),
            pltpu.VMEM((heads, 8, LANES), F32),
        ],
        compiler_params=_params(("parallel", "arbitrary")),
    )(*args)


def _win_attn_kernel(sink_ref, q_ref, kc_ref, vc_ref, kp_ref, kq_ref, kn_ref, vp_ref, vq_ref, vn_ref,
                     o_ref, *, lat_tiles, heads, kv_heads, hd):
    qt = pl.program_id(1)
    blk = q_ref.shape[1]
    n_ctx = kc_ref.shape[1]
    grp = heads // kv_heads
    scale = hd ** -0.5
    rows = grp * blk
    qi = lax.broadcasted_iota(I32, (rows, blk), 0) % blk
    kj = lax.broadcasted_iota(I32, (rows, blk), 1)
    tq = qt + jnp.zeros((rows, blk), I32)
    ok_cur = tq < lat_tiles
    ok_prev = (kj >= qi) & ok_cur & (tq >= 1)
    ok_next = (kj <= qi) & (tq + 1 < lat_tiles)
    bias = jnp.concatenate(
        [jnp.zeros((rows, n_ctx), F32)]
        + [jnp.where(ok, 0.0, NEG_INF).astype(F32) for ok in (ok_prev, ok_cur, ok_next)], axis=1)
    head_row = lax.broadcasted_iota(I32, (rows, 1), 0) // blk
    for kv in range(kv_heads):
        cs = slice(kv * hd, (kv + 1) * hd)
        keys = jnp.concatenate([kc_ref[0, :, cs], kp_ref[0, :, cs], kq_ref[0, :, cs], kn_ref[0, :, cs]], axis=0)
        vals = jnp.concatenate([vc_ref[0, :, cs], vp_ref[0, :, cs], vq_ref[0, :, cs], vn_ref[0, :, cs]], axis=0)
        q = jnp.concatenate([q_ref[0, :, (kv * grp + g) * hd:(kv * grp + g + 1) * hd] for g in range(grp)], axis=0)
        sink = jnp.zeros((rows, 1), F32)
        for g in range(grp):
            h = kv * grp + g
            sink = jnp.where(head_row == g, sink_ref[:, h:h + 1], sink)
        s = _dot_nt(q, keys) * scale + bias
        m = jnp.maximum(jnp.max(s, axis=-1, keepdims=True), sink)
        p = jnp.exp(s - m)
        den = jnp.sum(p, axis=-1, keepdims=True) + jnp.exp(sink - m)
        o = _dot(p.astype(BF16), vals) / den
        for g in range(grp):
            h = kv * grp + g
            o_ref[0, :, h * hd:(h + 1) * hd] = o[g * blk:(g + 1) * blk].astype(o_ref.dtype)


def _win_attn(qk, v, sink, *, n_lat, n_ctx, heads, kv_heads, hd):
    b, s, _ = qk.shape
    blk = WIN_BLOCK
    lat_tiles = n_lat // blk
    tiles = s // blk
    kvw = kv_heads * hd
    k_col = (heads * hd) // kvw
    v_col = 0
    ctx_blk = n_lat // n_ctx
    prev = lambda t: jnp.maximum(t - 1, 0)
    nxt = lambda t: jnp.minimum(t + 1, tiles - 1)
    sink_row = jnp.zeros((1, LANES), F32).at[0, :heads].set(sink.astype(F32))
    kern = functools.partial(_win_attn_kernel, lat_tiles=lat_tiles, heads=heads, kv_heads=kv_heads, hd=hd)
    return pl.pallas_call(
        kern,
        name="win_attn",
        grid=(b, tiles),
        in_specs=[
            pl.BlockSpec((1, LANES), lambda bi, t: (0, 0)),
            pl.BlockSpec((1, blk, heads * hd), lambda bi, t: (bi, t, 0)),
            pl.BlockSpec((1, n_ctx, kvw), lambda bi, t: (bi, ctx_blk, k_col)),
            pl.BlockSpec((1, n_ctx, kvw), lambda bi, t: (bi, ctx_blk, v_col)),
            pl.BlockSpec((1, blk, kvw), lambda bi, t: (bi, prev(t), k_col)),
            pl.BlockSpec((1, blk, kvw), lambda bi, t: (bi, t, k_col)),
            pl.BlockSpec((1, blk, kvw), lambda bi, t: (bi, nxt(t), k_col)),
            pl.BlockSpec((1, blk, kvw), lambda bi, t: (bi, prev(t), v_col)),
            pl.BlockSpec((1, blk, kvw), lambda bi, t: (bi, t, v_col)),
            pl.BlockSpec((1, blk, kvw), lambda bi, t: (bi, nxt(t), v_col)),
        ],
        out_specs=pl.BlockSpec((1, blk, heads * hd), lambda bi, t: (bi, t, 0)),
        out_shape=jax.ShapeDtypeStruct((b, s, heads * hd), BF16),
        compiler_params=_params(("parallel", "parallel")),
    )(sink_row, qk, qk, v, qk, qk, qk, v, v, v)


SUBLANES = 8
PACKED_ROWS = 16


def _dft_stage_a_kernel(m_ref, u_ref, v_ref, re_ref, im_ref):
    n1, sub, tc = u_ref.shape[1:]
    rows = n1 * sub
    stacked = jnp.concatenate([u_ref[0].reshape(rows, tc), v_ref[0].reshape(rows, tc)], axis=0)
    out = _dot(m_ref[...], stacked.astype(BF16))
    re_ref[0] = out[:rows].reshape(n1, sub, tc)
    im_ref[0] = out[rows:].reshape(n1, sub, tc)


def _dft_stage_b_kernel(m_ref, re_ref, im_ref, twc_ref, tws_ref, o_ref):
    n2, sub, tc = o_ref.shape[1:]
    reps = tc // LANES
    ar, ai = re_ref[0], im_ref[0]
    twc = jnp.concatenate([twc_ref[...]] * reps, axis=1)
    tws = jnp.concatenate([tws_ref[...]] * reps, axis=1)
    stacked = jnp.concatenate([ar * twc + ai * tws, ai * twc - ar * tws], axis=0).astype(BF16)
    out = _dot(m_ref[...], stacked)
    o_ref[0] = out.reshape(n2, sub, tc).astype(o_ref.dtype)


def _dft_ctx_kernel(m_ref, x_ref, o_ref, *, d):
    x = x_ref[0]
    stacked = jnp.concatenate([x[:, :d], x[:, d:]], axis=0).astype(BF16)
    o_ref[0] = _dot(m_ref[...], stacked).astype(o_ref.dtype)


def _cos_sin(n):
    ang = 2.0 * np.pi * np.outer(np.arange(n), np.arange(n)) / float(n)
    return np.cos(ang), np.sin(ang)


def _fourier_positions(uv, n_lat, n_ctx, d):
    b, s, _ = uv.shape
    n2 = FFT_INNER
    n1 = n_lat // n2
    tc = min(512, d)
    ca, sa = _cos_sin(n1)
    eye = np.eye(SUBLANES)
    ka, ks = np.kron(ca, eye) / math.sqrt(n1), np.kron(sa, eye) / math.sqrt(n1)
    mat_a = jnp.asarray(np.block([[ka, -ks], [-ks, -ka]]), BF16)
    uv4 = uv.reshape(b, s // n2, n2, 2 * d)
    blk_a = (1, n1, SUBLANES, tc)
    a_re, a_im = pl.pallas_call(
        _dft_stage_a_kernel,
        name="dft_stage_a",
        grid=(b, n2 // SUBLANES, d // tc),
        in_specs=[
            pl.BlockSpec(mat_a.shape, lambda bi, cg, jc: (0, 0)),
            pl.BlockSpec(blk_a, lambda bi, cg, jc: (bi, 0, cg, jc)),
            pl.BlockSpec(blk_a, lambda bi, cg, jc: (bi, 0, cg, d // tc + jc)),
        ],
        out_specs=[pl.BlockSpec(blk_a, lambda bi, cg, jc: (bi, 0, cg, jc))] * 2,
        out_shape=[jax.ShapeDtypeStruct((b, n1, n2, d), F32)] * 2,
        compiler_params=_params(("parallel", "parallel", "parallel")),
    )(mat_a, uv4, uv4)
    cb, sb = _cos_sin(n2)
    eye = np.eye(PACKED_ROWS)
    kron_b = lambda m: np.einsum("kc,ab->kabc", m, eye).reshape(n2 * PACKED_ROWS, PACKED_ROWS * n2)
    mat_b = jnp.asarray(np.concatenate([kron_b(cb), kron_b(sb)], axis=1) / math.sqrt(n2), BF16)
    phi = 2.0 * np.pi * np.outer(np.arange(n1), np.arange(n2)) / float(n_lat)
    twc = jnp.asarray(np.repeat(np.cos(phi).reshape(-1, 1), LANES, axis=1), F32)
    tws = jnp.asarray(np.repeat(np.sin(phi).reshape(-1, 1), LANES, axis=1), F32)
    rows = PACKED_ROWS * n2
    y = pl.pallas_call(
        _dft_stage_b_kernel,
        name="dft_stage_b",
        grid=(b, n1 // PACKED_ROWS, d // tc),
        in_specs=[
            pl.BlockSpec(mat_b.shape, lambda bi, kb, jc: (0, 0)),
            pl.BlockSpec((1, rows, tc), lambda bi, kb, jc: (bi, kb, jc)),
            pl.BlockSpec((1, rows, tc), lambda bi, kb, jc: (bi, kb, jc)),
            pl.BlockSpec((rows, LANES), lambda bi, kb, jc: (kb, 0)),
            pl.BlockSpec((rows, LANES), lambda bi, kb, jc: (kb, 0)),
        ],
        out_specs=pl.BlockSpec((1, n2, PACKED_ROWS, tc), lambda bi, kb, jc: (bi, 0, kb, jc)),
        out_shape=jax.ShapeDtypeStruct((b, n2, n1, d), BF16),
        compiler_params=_params(("parallel", "parallel", "parallel")),
    )(mat_b, a_re.reshape(b, n_lat, d), a_im.reshape(b, n_lat, d), twc, tws)
    cc, sc = _cos_sin(n_ctx)
    mat_c = jnp.asarray(np.concatenate([cc, -sc], axis=1) / math.sqrt(n_ctx), BF16)
    y_ctx = pl.pallas_call(
        functools.partial(_dft_ctx_kernel, d=d),
        name="dft_ctx",
        grid=(b,),
        in_specs=[
            pl.BlockSpec(mat_c.shape, lambda bi: (0, 0)),
            pl.BlockSpec((1, n_ctx, 2 * d), lambda bi: (bi, n_lat // n_ctx, 0)),
        ],
        out_specs=pl.BlockSpec((1, n_ctx, d), lambda bi: (bi, 0, 0)),
        out_shape=jax.ShapeDtypeStruct((b, n_ctx, d), BF16),
        compiler_params=_params(("parallel",)),
    )(mat_c, uv)
    return jnp.concatenate([y.reshape(b, n_lat, d), y_ctx], axis=1)


LANE_SHIFT = 7


def _route_kernel(aff_ref, tri_ref, posc_ref, gate_ref, tst_ref,
                  thr_ref, need_ref, ctie_ref, cpos_ref, *, cap):
    t = pl.program_id(1)
    tile = posc_ref.shape[1]

    @pl.when(t == 0)
    def _():
        def body(it, thr):
            bits = lax.bitcast_convert_type(aff_ref[0], I32)
            cand = thr | jnp.left_shift(jnp.int32(1), 30 - it)
            cnt = jnp.sum(jnp.where(bits >= cand, 1.0, 0.0), axis=0, keepdims=True)
            return jnp.where(cnt >= cap, cand, thr)

        thr = lax.fori_loop(0, 31, body, jnp.zeros((1, LANES), I32))
        bits = lax.bitcast_convert_type(aff_ref[0], I32)
        above = jnp.sum(jnp.where(bits > thr, 1.0, 0.0), axis=0, keepdims=True)
        thr_ref[...] = thr
        need_ref[...] = float(cap) - above
        ctie_ref[...] = jnp.zeros_like(ctie_ref)
        cpos_ref[...] = jnp.zeros_like(cpos_ref)

    a = aff_ref[0, pl.ds(pl.multiple_of(t * tile, tile), tile), :]
    bits = lax.bitcast_convert_type(a, I32)
    thr = thr_ref[...]
    gt = bits > thr
    eq = bits == thr
    eqf = jnp.where(eq, 1.0, 0.0)
    tie_rank = _dot(tri_ref[...], eqf.astype(BF16)) + ctie_ref[...]
    sel = gt | (eq & (tie_rank < need_ref[...]))
    self_ = jnp.where(sel, 1.0, 0.0)
    start = cpos_ref[...]
    pos = jnp.where(sel, _dot(tri_ref[...], self_.astype(BF16)) + start, -1.0)
    posc_ref[0] = pos.astype(I32)
    gate_ref[0] = jnp.where(sel, a, 0.0)
    tst_ref[0, 0] = jnp.broadcast_to(start, (8, LANES)).astype(I32)
    ctie_ref[...] = ctie_ref[...] + jnp.sum(eqf, axis=0, keepdims=True)
    cpos_ref[...] = start + jnp.sum(self_, axis=0, keepdims=True)


def _route(aff, *, row_off, n, cap):
    b = aff.shape[0]
    tile = ROUTE_TILE
    nt = n // tile
    tri = jnp.asarray(np.tril(np.ones((tile, tile)), -1), BF16)
    return pl.pallas_call(
        functools.partial(_route_kernel, cap=cap),
        name="moe_route",
        grid=(b, nt),
        in_specs=[
            pl.BlockSpec((1, n, LANES), lambda bi, t: (bi, row_off // n, 0)),
            pl.BlockSpec((tile, tile), lambda bi, t: (0, 0)),
        ],
        out_specs=[
            pl.BlockSpec((1, tile, LANES), lambda bi, t: (bi, t, 0)),
            pl.BlockSpec((1, tile, LANES), lambda bi, t: (bi, t, 0)),
            pl.BlockSpec((1, 1, 8, LANES), lambda bi, t: (bi, t, 0, 0)),
        ],
        out_shape=[
            jax.ShapeDtypeStruct((b, n, LANES), I32),
            jax.ShapeDtypeStruct((b, n, LANES), F32),
            jax.ShapeDtypeStruct((b, nt, 8, LANES), I32),
        ],
        scratch_shapes=[pltpu.VMEM((1, LANES), I32)] + [pltpu.VMEM((1, LANES), F32)] * 3,
        compiler_params=_params(("parallel", "arbitrary")),
    )(aff, tri)


def _slot_index_kernel(ts_ref, posc_ref, idx_ref, *, nt, row_off):
    bi, t = pl.program_id(0), pl.program_id(1)

    @pl.when(t == 0)
    def _():
        idx_ref[...] = jnp.zeros_like(idx_ref)

    tile = posc_ref.shape[1]
    slot_rows = idx_ref.shape[2]
    tok = (row_off + t * tile + lax.broadcasted_iota(I32, (tile, 1), 0)).astype(F32)
    lane = lax.broadcasted_iota(I32, (tile, 2 * LANES), 1)
    for e in range(N_EXPERTS):
        h0 = jnp.minimum(ts_ref[(bi * nt + t) * N_EXPERTS + e] >> LANE_SHIFT, slot_rows - 2)
        hit = (posc_ref[0, :, e:e + 1] - h0 * LANES) == lane
        vals = jnp.sum(jnp.where(hit, tok, 0.0), axis=0, keepdims=True)
        two_rows = jnp.concatenate([vals[:, :LANES], vals[:, LANES:]], axis=0).astype(I32)
        idx_ref[0, e, pl.ds(h0, 2), :] = idx_ref[0, e, pl.ds(h0, 2), :] + two_rows


def _slot_index(tstart, posc, *, row_off, n, cap):
    b = posc.shape[0]
    tile = ROUTE_TILE
    nt = n // tile
    slot_rows = max(cap, 2 * LANES) // LANES
    grid_spec = pltpu.PrefetchScalarGridSpec(
        num_scalar_prefetch=1,
        grid=(b, nt),
        in_specs=[pl.BlockSpec((1, tile, LANES), lambda bi, t, ts: (bi, t, 0))],
        out_specs=pl.BlockSpec((1, N_EXPERTS, slot_rows, LANES), lambda bi, t, ts: (bi, 0, 0, 0)),
    )
    idx = pl.pallas_call(
        functools.partial(_slot_index_kernel, nt=nt, row_off=row_off),
        name="moe_slot_index",
        grid_spec=grid_spec,
        out_shape=jax.ShapeDtypeStruct((b, N_EXPERTS, slot_rows, LANES), I32),
        compiler_params=_params(("parallel", "arbitrary")),
    )(tstart, posc)
    return idx.reshape(b, N_EXPERTS, slot_rows * LANES)


def _gather_kernel(idx_ref, h_ref, o_ref, buf_ref, sem):
    bi = pl.program_id(1)
    n_rows = buf_ref.shape[0]

    def row_copy(src_row, dst_row, rows):
        return pltpu.make_async_copy(h_ref.at[bi, pl.ds(src_row, rows), :], buf_ref.at[pl.ds(dst_row, rows), :], sem)

    def issue(s, carry):
        row_copy(idx_ref[0, 0, s], s, 1).start()
        return carry

    lax.fori_loop(0, n_rows, issue, 0)
    row_copy(0, 0, n_rows).wait()
    o_ref[0, 0] = buf_ref[...].astype(o_ref.dtype)


def _gather(idx, h):
    b, _, d = h.shape
    slots = idx.shape[2]
    return pl.pallas_call(
        _gather_kernel,
        name="moe_gather",
        grid=(N_EXPERTS, b),
        in_specs=[
            pl.BlockSpec((1, 1, slots), lambda ei, bi: (bi * N_EXPERTS + ei, 0, 0), memory_space=pltpu.SMEM),
            pl.BlockSpec(memory_space=pl.ANY),
        ],
        out_specs=pl.BlockSpec((1, 1, slots, d), lambda ei, bi: (ei, bi, 0, 0)),
        out_shape=jax.ShapeDtypeStruct((N_EXPERTS, b, slots, d), BF16),
        scratch_shapes=[pltpu.VMEM((slots, d), F32), pltpu.SemaphoreType.DMA(())],
        compiler_params=_params(("arbitrary", "arbitrary")),
    )(idx.reshape(b * N_EXPERTS, 1, slots), h)


def _ffn_kernel(x_ref, wg_ref, wu_ref, wd_ref, ye_ref, hm_ref, *, n_up):
    st = pl.program_id(1)
    bsz, rows, d = x_ref.shape[1:]
    tf = wg_ref.shape[3]

    @pl.when(st < n_up)
    def _():
        x = x_ref[0].reshape(bsz * rows, d)
        a = _dot(x, wg_ref[0, 0].astype(BF16))
        u = _dot(x, wu_ref[0, 0].astype(BF16))
        hm_ref[st] = (_silu(a) * u).astype(BF16)

    @pl.when(st >= n_up)
    def _():
        y = _dot(hm_ref[0], wd_ref[0, 0, 0:tf, :].astype(BF16))
        for c in range(1, n_up):
            y = y + _dot(hm_ref[c], wd_ref[0, 0, c * tf:(c + 1) * tf, :].astype(BF16))
        ye_ref[0] = y.reshape(bsz, rows, y.shape[1]).astype(ye_ref.dtype)


def _ffn(xe, w_gate, w_up, w_down, layer):
    e, b, rows, d = xe.shape
    f = w_gate.shape[3]
    tf = min(256, f)
    tdc = min(512, d)
    n_up, n_down = f // tf, d // tdc
    up = lambda st: jnp.minimum(st, n_up - 1)
    down = lambda st: jnp.maximum(st - n_up, 0)
    return pl.pallas_call(
        functools.partial(_ffn_kernel, n_up=n_up),
        name="moe_ffn",
        grid=(e, n_up + n_down),
        in_specs=[
            pl.BlockSpec((1, b, rows, d), lambda ei, st: (ei, 0, 0, 0)),
            pl.BlockSpec((1, 1, d, tf), lambda ei, st: (layer, ei, 0, up(st))),
            pl.BlockSpec((1, 1, d, tf), lambda ei, st: (layer, ei, 0, up(st))),
            pl.BlockSpec((1, 1, f, tdc), lambda ei, st: (layer, ei, 0, down(st))),
        ],
        out_specs=pl.BlockSpec((1, b, rows, tdc), lambda ei, st: (ei, 0, 0, down(st))),
        out_shape=jax.ShapeDtypeStruct((e, b, rows, d), BF16),
        scratch_shapes=[pltpu.VMEM((n_up, b * rows, tf), BF16)],
        compiler_params=_params(("parallel", "arbitrary")),
    )(xe, w_gate, w_up, w_down)


def _window_start(ts_ref, idx, cap_rows, win):
    a0 = jnp.minimum(ts_ref[idx] & (-SLOT_ALIGN), cap_rows - win)
    return pl.multiple_of(a0, SLOT_ALIGN)


def _combine_kernel(ts_ref, ye_ref, posc_ref, gate_ref, x_ref, g_ref, o_ref, *, nt, win):
    bi, t = pl.program_id(0), pl.program_id(2)
    cap_rows = ye_ref.shape[2]
    tile = posc_ref.shape[1]
    acc = jnp.zeros(o_ref.shape[1:], F32)
    lane = lax.broadcasted_iota(I32, (tile, win), 1)
    for e in range(N_EXPERTS):
        a0 = _window_start(ts_ref, (bi * nt + t) * N_EXPERTS + e, cap_rows, win)
        rows = ye_ref[e, 0, pl.ds(a0, win), :]
        pcol = posc_ref[0, :, e:e + 1]
        gcol = gate_ref[0, :, e:e + 1]
        weights = jnp.where(pcol == a0 + lane, gcol, 0.0).astype(BF16)
        acc = acc + _dot(weights, rows)
    o_ref[0] = x_ref[0] + g_ref[0, 0] * acc


def _combine(tstart, ye, posc, gate, x, g2, *, row_off, n, region, row0, cap_rows):
    b, s, d = x.shape
    tile = ROUTE_TILE
    nt = n // tile
    dc = 512 if d % 512 == 0 else d
    win = min(2 * tile, cap_rows)
    off = row_off // tile
    grid_spec = pltpu.PrefetchScalarGridSpec(
        num_scalar_prefetch=1,
        grid=(b, d // dc, nt),
        in_specs=[
            pl.BlockSpec((N_EXPERTS, 1, cap_rows, dc), lambda bi, c, t, ts: (0, bi, row0 // cap_rows, c)),
            pl.BlockSpec((1, tile, LANES), lambda bi, c, t, ts: (bi, t, 0)),
            pl.BlockSpec((1, tile, LANES), lambda bi, c, t, ts: (bi, t, 0)),
            pl.BlockSpec((1, tile, dc), lambda bi, c, t, ts: (bi, off + t, c)),
            pl.BlockSpec((1, 1, 1, dc), lambda bi, c, t, ts: (bi, region, 0, c)),
        ],
        out_specs=pl.BlockSpec((1, tile, dc), lambda bi, c, t, ts: (bi, off + t, c)),
    )
    return pl.pallas_call(
        functools.partial(_combine_kernel, nt=nt, win=win),
        name="moe_combine",
        grid_spec=grid_spec,
        out_shape=jax.ShapeDtypeStruct((b, s, d), F32),
        input_output_aliases={4: 0},
        compiler_params=_params(("parallel", "parallel", "arbitrary")),
    )(tstart, ye, posc, gate, x, g2)


def _moe(x, h, aff, g2, w_gate, w_up, w_down, layer, token_sets):
    routed, row0 = [], 0
    for row_off, n, region in token_sets:
        cap = CAPACITY_FACTOR * n // N_EXPERTS
        cap_rows = -(-cap // LANES) * LANES
        posc, gate, tst = _route(aff, row_off=row_off, n=n, cap=cap)
        tstart = tst[:, :, 0, :N_EXPERTS].reshape(-1)
        idx = _slot_index(tstart, posc, row_off=row_off, n=n, cap=cap)
        routed.append((row_off, n, region, row0, cap_rows, tstart, posc, gate, idx))
        row0 += cap_rows
    idx_all = jnp.concatenate([r[8][:, :, :r[4]] for r in routed], axis=2)
    ye = _ffn(_gather(idx_all, h), w_gate, w_up, w_down, layer)
    for row_off, n, region, r0, cap_rows, tstart, posc, gate, _ in routed:
        x = _combine(tstart, ye, posc, gate, x, g2, row_off=row_off, n=n, region=region, row0=r0, cap_rows=cap_rows)
    return x


def _rope_tables(n_lat, n_ctx, head_dim):
    quarter = head_dim // 4
    inv = ROPE_BASE ** (-jnp.arange(quarter, dtype=F32) / quarter)
    pos = jnp.arange(n_lat)
    row = (pos // GRID_W).astype(F32)[:, None] * inv[None, :]
    col = (pos % GRID_W).astype(F32)[:, None] * inv[None, :]
    cos = jnp.concatenate([jnp.cos(row), jnp.cos(row), jnp.cos(col), jnp.cos(col)], axis=1)
    sin = jnp.concatenate([-jnp.sin(row), jnp.sin(row), -jnp.sin(col), jnp.sin(col)], axis=1)
    cos = jnp.concatenate([cos, jnp.ones((n_ctx, head_dim), F32)], axis=0)
    sin = jnp.concatenate([sin, jnp.zeros((n_ctx, head_dim), F32)], axis=0)
    return cos, sin


def kernel(x, c, ctx, c_ctx, w_mod, b_mod, norm_gain, final_gain, ret_w_in, ret_w_out, ret_decay,
           win_w_qkv, win_w_o, win_sink, fno_w_o, router_w, exp_w_gate, exp_w_up, exp_w_down):
    b, n_lat, d = x.shape
    n_ctx = ctx.shape[1]
    depth = w_mod.shape[0]
    s = n_lat + n_ctx
    assert s % ROW_TILE == 0 and n_lat % NORM_TILE == 0 and n_ctx % NORM_TILE == 0 and n_lat % n_ctx == 0
    assert n_lat % (FFT_INNER * 4) == 0 and b + 1 <= 8 and n_ctx % ROUTE_TILE == 0

    xs = jnp.concatenate([x, ctx], axis=1)
    cvec = jnp.zeros((8, d), F32).at[:b].set(c).at[b].set(c_ctx)
    mod = _modulation(cvec, w_mod, b_mod)

    def mod_pair(i, k):
        lat = mod[i, :b, k * d:(k + 1) * d]
        cx = jnp.broadcast_to(mod[i, b, k * d:(k + 1) * d], (b, d))
        return jnp.stack([lat, cx], axis=1).reshape(b, 2, 1, d)

    ret_dk = d // RET_HEADS
    ret_dv = 2 * ret_dk
    win_hd = d // WIN_HEADS
    ret_cos, ret_sin = _rope_tables(n_lat, n_ctx, ret_dk)
    win_cos, win_sin = _rope_tables(n_lat, n_ctx, win_hd)

    for i in range(depth):
        kind, j = i % N_MIXERS, i // N_MIXERS
        last = i == depth - 1
        sh1, sc1, g1, sh2, sc2, g2 = [mod_pair(i, k) for k in range(6)]
        h = _norm_mod(xs, norm_gain[i, 0], sh1, sc1, n_lat)
        if kind == 0:
            hk, hv = RET_HEADS * ret_dk, RET_HEADS * ret_dv
            w_in = ret_w_in[j].astype(BF16)
            qk = _mm_proj(h, w_in, 0, 2 * hk, PROJ_COL_TILE,
                          rope=(ret_cos, ret_sin, hk, ret_dk ** -0.5, ret_dk))
            vg = _mm_proj(h, w_in, 2 * hk, 3 * hv, PROJ_COL_TILE)
            scan = functools.partial(_ret_scan, n_lat=n_lat, n_ctx=n_ctx, heads=RET_HEADS, dk=ret_dk, dv=ret_dv)
            y_b = scan(qk, vg, ret_decay[j, 1:2], None, reverse=True)
            y = scan(qk, vg, ret_decay[j, 0:1], y_b, reverse=False)
            w_out = ret_w_out[j]
        elif kind == 1:
            nq, nkv = WIN_HEADS * win_hd, WIN_KV_HEADS * win_hd
            w_qkv = win_w_qkv[j].astype(BF16)
            qk = _mm_proj(h, w_qkv, 0, nq + nkv, COL_TILE, rope=(win_cos, win_sin, nq, 1.0, win_hd))
            v = _mm_proj(h, w_qkv, nq + nkv, nkv, COL_TILE)
            y = _win_attn(qk, v, win_sink[j], n_lat=n_lat, n_ctx=n_ctx, heads=WIN_HEADS,
                          kv_heads=WIN_KV_HEADS, hd=win_hd)
            w_out = win_w_o[j]
        else:
            cg = d // FOURIER_GROUPS
            cc, sc = _cos_sin(cg)
            w_ch = jnp.asarray(np.concatenate([cc, sc], axis=1) / math.sqrt(cg), BF16)
            uv = _mm_groups(h, w_ch, FOURIER_GROUPS)
            y = _fourier_positions(uv, n_lat, n_ctx, d)
            w_out = fno_w_o[j]
        xs = _mm_res(y, w_out.astype(BF16), xs, g1[:, 0], g1[:, 1], n_lat)
        h2, aff = _norm_mod(xs, norm_gain[i, 1], sh2, sc2, n_lat, out_dtype=F32, router_w=router_w[i])
        token_sets = [(0, n_lat, 0)] + ([] if last else [(n_lat, n_ctx, 1)])
        xs = _moe(xs, h2, aff, g2, exp_w_gate, exp_w_up, exp_w_down, i, token_sets)

    zeros = jnp.zeros((b, 2, 1, d), F32)
    return _norm_mod(xs, final_gain, zeros, zeros, n_lat, out_dtype=F32, rows=n_lat)
```

```python
import functools
import math

import numpy as np
import jax
import jax.numpy as jnp
from jax import lax
from jax.experimental import pallas as pl
from jax.experimental.pallas import tpu as pltpu

F32 = jnp.float32
BF16 = jnp.bfloat16
I32 = jnp.int32

GRID_W = 64
N_MIXERS = 3
RET_HEADS = 8
RET_CHUNK = 128
WIN_HEADS = 16
WIN_KV_HEADS = 4
WIN_BLOCK = 128
FOURIER_GROUPS = 4
FFT_INNER = 64
N_EXPERTS = 16
CAPACITY_FACTOR = 2
ROPE_BASE = 10000.0
NORM_EPS = 1e-6
NEG_INF = -1e30

LANES = 128
ROUTE_TILE = 128
SLOT_ALIGN = 16
ROW_TILE = 768
MM_ROW_CHUNK = 256
COL_TILE = 512
PROJ_COL_TILE = 1024
NORM_TILE = 256
VMEM_LIMIT_BYTES = 56 * 1024 * 1024


def _params(sem):
    return pltpu.CompilerParams(dimension_semantics=sem, vmem_limit_bytes=VMEM_LIMIT_BYTES)


def _dot(a, b):
    return jnp.dot(a, b, preferred_element_type=F32)


def _dot_nt(a, b):
    return lax.dot_general(a, b, (((1,), (1,)), ((), ())), preferred_element_type=F32)


def _split_bf16(x):
    hi = x.astype(BF16)
    lo = (x - hi.astype(F32)).astype(BF16)
    return hi, lo


def _silu(x):
    return x / (1.0 + jnp.exp(-x))


def _mod_kernel(c_ref, w_ref, b_ref, o_ref):
    s = _silu(c_ref[...])
    sh, sl = _split_bf16(s)
    wh, wl = _split_bf16(w_ref[0])
    o_ref[0] = _dot(sh, wh) + _dot(sl, wh) + _dot(sh, wl) + b_ref[0]


def _modulation(cvec, w_mod, b_mod):
    depth, d, n = w_mod.shape
    tn = 1024
    return pl.pallas_call(
        _mod_kernel,
        name="modulation",
        grid=(depth, n // tn),
        in_specs=[
            pl.BlockSpec((8, d), lambda i, j: (0, 0)),
            pl.BlockSpec((1, d, tn), lambda i, j: (i, 0, j)),
            pl.BlockSpec((1, 1, tn), lambda i, j: (i, 0, j)),
        ],
        out_specs=pl.BlockSpec((1, 8, tn), lambda i, j: (i, 0, j)),
        out_shape=jax.ShapeDtypeStruct((depth, 8, n), F32),
        compiler_params=_params(("parallel", "parallel")),
    )(cvec, w_mod, b_mod.reshape(depth, 1, n))


def _normed(x_ref, gain_ref, shift_ref, scale_ref):
    x = x_ref[0]
    ms = jnp.mean(x * x, axis=-1, keepdims=True)
    y = x * lax.rsqrt(ms + NORM_EPS) * gain_ref[...]
    return y * (1.0 + scale_ref[0, 0]) + shift_ref[0, 0]


def _norm_mod_kernel(x_ref, gain_ref, shift_ref, scale_ref, o_ref):
    o_ref[0] = _normed(x_ref, gain_ref, shift_ref, scale_ref).astype(o_ref.dtype)


def _pack_bf16_pairs(h):
    half = h.shape[1] // 2
    bits = lax.bitcast_convert_type(h.astype(BF16).astype(F32), jnp.uint32)
    word = lax.shift_right_logical(bits[:, :half], jnp.uint32(16)) | (bits[:, half:] & jnp.uint32(0xFFFF0000))
    return lax.bitcast_convert_type(word, I32)


def _unpack_bf16_pairs(word):
    bits = lax.bitcast_convert_type(word, jnp.uint32)
    lo = lax.bitcast_convert_type(lax.shift_left(bits, jnp.uint32(16)), F32)
    hi = lax.bitcast_convert_type(bits & jnp.uint32(0xFFFF0000), F32)
    return jnp.concatenate([lo, hi], axis=1).astype(BF16)


def _norm_router_kernel(x_ref, gain_ref, shift_ref, scale_ref, wr_ref, o_ref, aff_ref):
    h = _normed(x_ref, gain_ref, shift_ref, scale_ref)
    o_ref[0] = _pack_bf16_pairs(h)
    hh, hl = _split_bf16(h)
    wh, wl = _split_bf16(wr_ref[...])
    logits = _dot(hh, wh) + _dot(hl, wh) + _dot(hh, wl)
    lane = lax.broadcasted_iota(I32, logits.shape, 1)
    valid = lane < N_EXPERTS
    logits = jnp.where(valid, logits, -jnp.inf)
    m = jnp.max(logits, axis=-1, keepdims=True)
    p = jnp.exp(logits - m)
    aff = p / jnp.sum(p, axis=-1, keepdims=True)
    aff_ref[0] = jnp.where(valid, aff, 0.0)


def _norm_mod(x, gain, shift, scale, n_lat, *, out_dtype=BF16, rows=None, router_w=None):
    b, s, d = x.shape
    rows = s if rows is None else rows
    tr = NORM_TILE
    lat_tiles = n_lat // tr
    region = lambda bi, t: (bi, jnp.where(t >= lat_tiles, 1, 0), 0, 0)
    in_specs = [
        pl.BlockSpec((1, tr, d), lambda bi, t: (bi, t, 0)),
        pl.BlockSpec((1, d), lambda bi, t: (0, 0)),
        pl.BlockSpec((1, 1, 1, d), region),
        pl.BlockSpec((1, 1, 1, d), region),
    ]
    args = [x, gain.reshape(1, d), shift, scale]
    out_specs = pl.BlockSpec((1, tr, d), lambda bi, t: (bi, t, 0))
    out_shape = jax.ShapeDtypeStruct((b, rows, d), out_dtype)
    kern = _norm_mod_kernel
    if router_w is not None:
        wr = jnp.zeros((d, LANES), F32).at[:, :N_EXPERTS].set(router_w)
        in_specs.append(pl.BlockSpec((d, LANES), lambda bi, t: (0, 0)))
        args.append(wr)
        out_specs = [pl.BlockSpec((1, tr, d // 2), lambda bi, t: (bi, t, 0)),
                     pl.BlockSpec((1, tr, LANES), lambda bi, t: (bi, t, 0))]
        out_shape = [jax.ShapeDtypeStruct((b, rows, d // 2), I32),
                     jax.ShapeDtypeStruct((b, rows, LANES), F32)]
        kern = _norm_router_kernel
    return pl.pallas_call(
        kern,
        name="norm_mod" if router_w is None else "norm_router",
        grid=(b, rows // tr),
        in_specs=in_specs,
        out_specs=out_specs,
        out_shape=out_shape,
        compiler_params=_params(("parallel", "parallel")),
    )(*args)


def _rope_partner(xs, quarter):
    if 2 * quarter == LANES:
        return pltpu.roll(xs, quarter, 1)
    back = pltpu.roll(xs, quarter, 1)
    fwd = pltpu.roll(xs, LANES - quarter, 1)
    lane = lax.broadcasted_iota(I32, xs.shape, 1)
    return jnp.where((lane % (2 * quarter)) < quarter, fwd, back)


def _row_chunks(tm):
    return [slice(r, r + MM_ROW_CHUNK) for r in range(0, tm, MM_ROW_CHUNK)]


def _mm_rope_kernel(a_ref, w_ref, cos_ref, sin_ref, o_ref, *, n_q, kscale, head_dim):
    sc = jnp.where(pl.program_id(2) >= n_q, kscale, 1.0).astype(F32)
    for rows in _row_chunks(a_ref.shape[1]):
        acc = _dot(a_ref[0, rows, :], w_ref[...])
        for s in range(acc.shape[1] // LANES):
            cols = slice(s * LANES, (s + 1) * LANES)
            off = (s * LANES) % head_dim
            xs = acc[:, cols]
            rot = xs * cos_ref[rows, off:off + LANES] + _rope_partner(xs, head_dim // 4) * sin_ref[rows, off:off + LANES]
            o_ref[0, rows, cols] = (rot * sc).astype(o_ref.dtype)


def _mm_plain_kernel(a_ref, w_ref, o_ref):
    for rows in _row_chunks(a_ref.shape[1]):
        o_ref[0, rows, :] = _dot(a_ref[0, rows, :], w_ref[...]).astype(o_ref.dtype)


def _mm_proj(a, w, col0, ncols, tn, rope=None):
    b, s, k = a.shape
    tm = ROW_TILE
    assert col0 % tn == 0 and ncols % tn == 0
    in_specs = [
        pl.BlockSpec((1, tm, k), lambda bi, i, j: (bi, i, 0)),
        pl.BlockSpec((k, tn), lambda bi, i, j: (0, col0 // tn + j)),
    ]
    args = [a, w]
    if rope is None:
        kern, name = _mm_plain_kernel, "mm_proj"
    else:
        cos_t, sin_t, n_q_cols, kscale, head_dim = rope
        assert tn % head_dim == 0 and n_q_cols % tn == 0
        kern = functools.partial(_mm_rope_kernel, n_q=n_q_cols // tn, kscale=kscale, head_dim=head_dim)
        name = "mm_proj_rope"
        in_specs += [pl.BlockSpec((tm, head_dim), lambda bi, i, j: (i, 0))] * 2
        args += [cos_t, sin_t]
    return pl.pallas_call(
        kern,
        name=name,
        grid=(b, s // tm, ncols // tn),
        in_specs=in_specs,
        out_specs=pl.BlockSpec((1, tm, tn), lambda bi, i, j: (bi, i, j)),
        out_shape=jax.ShapeDtypeStruct((b, s, ncols), BF16),
        compiler_params=_params(("parallel", "parallel", "arbitrary")),
    )(*args)


def _mm_res_kernel(a_ref, w_ref, x_ref, gl_ref, gc_ref, o_ref, *, n_lat):
    tm = a_ref.shape[1]
    for rows in _row_chunks(tm):
        acc = _dot(a_ref[0, rows, :], w_ref[...])
        row = pl.program_id(1) * tm + rows.start + lax.broadcasted_iota(I32, (acc.shape[0], 1), 0)
        gate = jnp.where(row < n_lat, gl_ref[0], gc_ref[0])
        o_ref[0, rows, :] = x_ref[0, rows, :] + gate * acc


def _mm_res(a, w, x, gate_lat, gate_ctx, n_lat):
    b, s, k = a.shape
    n = w.shape[1]
    tm, tn = ROW_TILE, COL_TILE
    return pl.pallas_call(
        functools.partial(_mm_res_kernel, n_lat=n_lat),
        name="mm_residual",
        grid=(b, s // tm, n // tn),
        in_specs=[
            pl.BlockSpec((1, tm, k), lambda bi, i, j: (bi, i, 0)),
            pl.BlockSpec((k, tn), lambda bi, i, j: (0, j)),
            pl.BlockSpec((1, tm, tn), lambda bi, i, j: (bi, i, j)),
            pl.BlockSpec((1, 1, tn), lambda bi, i, j: (bi, 0, j)),
            pl.BlockSpec((1, 1, tn), lambda bi, i, j: (bi, 0, j)),
        ],
        out_specs=pl.BlockSpec((1, tm, tn), lambda bi, i, j: (bi, i, j)),
        out_shape=jax.ShapeDtypeStruct((b, s, n), F32),
        compiler_params=_params(("parallel", "parallel", "arbitrary")),
    )(a, w, x, gate_lat, gate_ctx)


def _mm_groups_kernel(a_ref, w_ref, o_ref):
    o_ref[0] = _dot(a_ref[0], w_ref[...]).astype(o_ref.dtype)


def _mm_groups(a, w, groups):
    b, s, d = a.shape
    cg = d // groups
    tm = ROW_TILE
    return pl.pallas_call(
        _mm_groups_kernel,
        name="mm_channel_dft",
        grid=(b, s // tm, 2 * groups),
        in_specs=[
            pl.BlockSpec((1, tm, cg), lambda bi, i, j: (bi, i, j % groups)),
            pl.BlockSpec((cg, cg), lambda bi, i, j: (0, j // groups)),
        ],
        out_specs=pl.BlockSpec((1, tm, cg), lambda bi, i, j: (bi, i, j)),
        out_shape=jax.ShapeDtypeStruct((b, s, 2 * d), F32),
        compiler_params=_params(("parallel", "parallel", "arbitrary")),
    )(a, w)


def _ret_scan_kernel(*refs, heads, dk, dv, reverse, add_in):
    if add_in:
        dec_ref, q_ref, k_ref, v_ref, g_ref, yin_ref, o_ref, s_ref, qd_ref, kd_ref, in_ref, cd_ref = refs
    else:
        dec_ref, q_ref, k_ref, v_ref, g_ref, o_ref, s_ref, qd_ref, kd_ref, in_ref, cd_ref = refs
        yin_ref = None
    c = q_ref.shape[1]
    j = pl.program_id(1)

    @pl.when(j == 0)
    def _():
        s_ref[...] = jnp.zeros_like(s_ref)
        m_col = lax.broadcasted_iota(I32, (c, LANES), 0).astype(F32)
        m_row = lax.broadcasted_iota(I32, (c, c), 0).astype(F32)
        n_row = lax.broadcasted_iota(I32, (c, c), 1).astype(F32)
        for h in range(heads):
            lg = -jnp.exp(dec_ref[:, h:h + 1])
            if reverse:
                q_pow, k_pow, diff = c - m_col, m_col, n_row - m_row
            else:
                q_pow, k_pow, diff = m_col + 1.0, c - 1.0 - m_col, m_row - n_row
            qd_ref[h] = jnp.exp(lg * q_pow)
            kd_ref[h] = jnp.exp(lg * k_pow)
            in_ref[h] = jnp.where(diff >= 0, jnp.exp(lg * jnp.maximum(diff, 0.0)), 0.0)
            cd_ref[h] = jnp.exp(jnp.broadcast_to(lg, (8, LANES)) * float(c))

    for h in range(heads):
        q = q_ref[0, :, h * dk:(h + 1) * dk]
        k = k_ref[0, :, h * dk:(h + 1) * dk]
        v = v_ref[0, :, h * dv:(h + 1) * dv]
        qdec = jnp.concatenate([qd_ref[h]] * (dk // LANES), axis=1)
        kdec = jnp.concatenate([kd_ref[h]] * (dk // LANES), axis=1)
        state = s_ref[h]
        cross = _dot((q.astype(F32) * qdec).astype(BF16), state.astype(BF16))
        scores = _dot_nt(q, k) * in_ref[h]
        o = cross + _dot(scores.astype(BF16), v)
        k_t = (k.astype(F32) * kdec).T.astype(BF16)
        s_ref[h] = state * cd_ref[h][0:1, 0:1] + _dot(k_t, v)
        mu = jnp.mean(o, axis=-1, keepdims=True)
        cen = o - mu
        var = jnp.mean(cen * cen, axis=-1, keepdims=True)
        g = g_ref[0, :, h * dv:(h + 1) * dv].astype(F32)
        y = cen * lax.rsqrt(var + NORM_EPS) * _silu(g)
        if add_in:
            y = y + yin_ref[0, :, h * dv:(h + 1) * dv].astype(F32)
        o_ref[0, :, h * dv:(h + 1) * dv] = y.astype(o_ref.dtype)


def _ret_scan(qk, vg, decay_row, y_in, *, n_lat, n_ctx, heads, dk, dv, reverse):
    b, s, _ = qk.shape
    c = RET_CHUNK
    lat_chunks, ctx_chunks = n_lat // c, n_ctx // c
    steps = lat_chunks + ctx_chunks
    hk, hv = heads * dk, heads * dv
    if reverse:
        chunk = lambda j: steps - 1 - j
    else:
        chunk = lambda j: jnp.where(j < ctx_chunks, lat_chunks + j, j - ctx_chunks)
    gate_blk = 2 if reverse else 1
    in_specs = [
        pl.BlockSpec((1, heads), lambda bi, j: (0, 0)),
        pl.BlockSpec((1, c, hk), lambda bi, j: (bi, chunk(j), 0)),
        pl.BlockSpec((1, c, hk), lambda bi, j: (bi, chunk(j), 1)),
        pl.BlockSpec((1, c, hv), lambda bi, j: (bi, chunk(j), 0)),
        pl.BlockSpec((1, c, hv), lambda bi, j: (bi, chunk(j), gate_blk)),
    ]
    args = [decay_row, qk, qk, vg, vg]
    if y_in is not None:
        in_specs.append(pl.BlockSpec((1, c, hv), lambda bi, j: (bi, chunk(j), 0)))
        args.append(y_in)
    kern = functools.partial(_ret_scan_kernel, heads=heads, dk=dk, dv=dv, reverse=reverse,
                             add_in=y_in is not None)
    return pl.pallas_call(
        kern,
        name="ret_scan_bwd" if reverse else "ret_scan_fwd",
        grid=(b, steps),
        in_specs=in_specs,
        out_specs=pl.BlockSpec((1, c, hv), lambda bi, j: (bi, chunk(j), 0)),
        out_shape=jax.ShapeDtypeStruct((b, s, hv), BF16),
        scratch_shapes=[
            pltpu.VMEM((heads, dk, dv), F32),
            pltpu.VMEM((heads, c, LANES), F32),
            pltpu.VMEM((heads, c, LANES), F32),
            pltpu.VMEM((heads, c, c), F32),
            pltpu.VMEM((heads, 8, LANES), F32),
        ],
        compiler_params=_params(("parallel", "arbitrary")),
    )(*args)


def _win_attn_kernel(sink_ref, q_ref, kc_ref, vc_ref, kp_ref, kq_ref, kn_ref, vp_ref, vq_ref, vn_ref,
                     o_ref, *, lat_tiles, heads, kv_heads, hd):
    qt = pl.program_id(1)
    blk = q_ref.shape[1]
    n_ctx = kc_ref.shape[1]
    grp = heads // kv_heads
    scale = hd ** -0.5
    rows = grp * blk
    qi = lax.broadcasted_iota(I32, (rows, blk), 0) % blk
    kj = lax.broadcasted_iota(I32, (rows, blk), 1)
    tq = qt + jnp.zeros((rows, blk), I32)
    ok_cur = tq < lat_tiles
    ok_prev = (kj >= qi) & ok_cur & (tq >= 1)
    ok_next = (kj <= qi) & (tq + 1 < lat_tiles)
    bias = jnp.concatenate(
        [jnp.zeros((rows, n_ctx), F32)]
        + [jnp.where(ok, 0.0, NEG_INF).astype(F32) for ok in (ok_prev, ok_cur, ok_next)], axis=1)
    head_row = lax.broadcasted_iota(I32, (rows, 1), 0) // blk
    for kv in range(kv_heads):
        cs = slice(kv * hd, (kv + 1) * hd)
        keys = jnp.concatenate([kc_ref[0, :, cs], kp_ref[0, :, cs], kq_ref[0, :, cs], kn_ref[0, :, cs]], axis=0)
        vals = jnp.concatenate([vc_ref[0, :, cs], vp_ref[0, :, cs], vq_ref[0, :, cs], vn_ref[0, :, cs]], axis=0)
        q = jnp.concatenate([q_ref[0, :, (kv * grp + g) * hd:(kv * grp + g + 1) * hd] for g in range(grp)], axis=0)
        sink = jnp.zeros((rows, 1), F32)
        for g in range(grp):
            h = kv * grp + g
            sink = jnp.where(head_row == g, sink_ref[:, h:h + 1], sink)
        s = _dot_nt(q, keys) * scale + bias
        m = jnp.maximum(jnp.max(s, axis=-1, keepdims=True), sink)
        p = jnp.exp(s - m)
        den = jnp.sum(p, axis=-1, keepdims=True) + jnp.exp(sink - m)
        o = _dot(p.astype(BF16), vals) / den
        for g in range(grp):
            h = kv * grp + g
            o_ref[0, :, h * hd:(h + 1) * hd] = o[g * blk:(g + 1) * blk].astype(o_ref.dtype)


def _win_attn(qk, v, sink, *, n_lat, n_ctx, heads, kv_heads, hd):
    b, s, _ = qk.shape
    blk = WIN_BLOCK
    lat_tiles = n_lat // blk
    tiles = s // blk
    kvw = kv_heads * hd
    k_col = (heads * hd) // kvw
    v_col = 0
    ctx_blk = n_lat // n_ctx
    prev = lambda t: jnp.maximum(t - 1, 0)
    nxt = lambda t: jnp.minimum(t + 1, tiles - 1)
    sink_row = jnp.zeros((1, LANES), F32).at[0, :heads].set(sink.astype(F32))
    kern = functools.partial(_win_attn_kernel, lat_tiles=lat_tiles, heads=heads, kv_heads=kv_heads, hd=hd)
    return pl.pallas_call(
        kern,
        name="win_attn",
        grid=(b, tiles),
        in_specs=[
            pl.BlockSpec((1, LANES), lambda bi, t: (0, 0)),
            pl.BlockSpec((1, blk, heads * hd), lambda bi, t: (bi, t, 0)),
            pl.BlockSpec((1, n_ctx, kvw), lambda bi, t: (bi, ctx_blk, k_col)),
            pl.BlockSpec((1, n_ctx, kvw), lambda bi, t: (bi, ctx_blk, v_col)),
            pl.BlockSpec((1, blk, kvw), lambda bi, t: (bi, prev(t), k_col)),
            pl.BlockSpec((1, blk, kvw), lambda bi, t: (bi, t, k_col)),
            pl.BlockSpec((1, blk, kvw), lambda bi, t: (bi, nxt(t), k_col)),
            pl.BlockSpec((1, blk, kvw), lambda bi, t: (bi, prev(t), v_col)),
            pl.BlockSpec((1, blk, kvw), lambda bi, t: (bi, t, v_col)),
            pl.BlockSpec((1, blk, kvw), lambda bi, t: (bi, nxt(t), v_col)),
        ],
        out_specs=pl.BlockSpec((1, blk, heads * hd), lambda bi, t: (bi, t, 0)),
        out_shape=jax.ShapeDtypeStruct((b, s, heads * hd), BF16),
        compiler_params=_params(("parallel", "parallel")),
    )(sink_row, qk, qk, v, qk, qk, qk, v, v, v)


SUBLANES = 8
PACKED_ROWS = 16


def _dft_stage_a_kernel(m_ref, u_ref, v_ref, re_ref, im_ref):
    n1, sub, tc = u_ref.shape[1:]
    rows = n1 * sub
    stacked = jnp.concatenate([u_ref[0].reshape(rows, tc), v_ref[0].reshape(rows, tc)], axis=0)
    out = _dot(m_ref[...], stacked.astype(BF16))
    re_ref[0] = out[:rows].reshape(n1, sub, tc)
    im_ref[0] = out[rows:].reshape(n1, sub, tc)


def _dft_stage_b_kernel(m_ref, re_ref, im_ref, twc_ref, tws_ref, o_ref):
    n2, sub, tc = o_ref.shape[1:]
    reps = tc // LANES
    ar, ai = re_ref[0], im_ref[0]
    twc = jnp.concatenate([twc_ref[...]] * reps, axis=1)
    tws = jnp.concatenate([tws_ref[...]] * reps, axis=1)
    stacked = jnp.concatenate([ar * twc + ai * tws, ai * twc - ar * tws], axis=0).astype(BF16)
    out = _dot(m_ref[...], stacked)
    o_ref[0] = out.reshape(n2, sub, tc).astype(o_ref.dtype)


def _dft_ctx_kernel(m_ref, x_ref, o_ref, *, d):
    x = x_ref[0]
    stacked = jnp.concatenate([x[:, :d], x[:, d:]], axis=0).astype(BF16)
    o_ref[0] = _dot(m_ref[...], stacked).astype(o_ref.dtype)


def _cos_sin(n):
    ang = 2.0 * np.pi * np.outer(np.arange(n), np.arange(n)) / float(n)
    return np.cos(ang), np.sin(ang)


def _fourier_positions(uv, n_lat, n_ctx, d):
    b, s, _ = uv.shape
    n2 = FFT_INNER
    n1 = n_lat // n2
    tc = min(512, d)
    ca, sa = _cos_sin(n1)
    eye = np.eye(SUBLANES)
    ka, ks = np.kron(ca, eye) / math.sqrt(n1), np.kron(sa, eye) / math.sqrt(n1)
    mat_a = jnp.asarray(np.block([[ka, -ks], [-ks, -ka]]), BF16)
    uv4 = uv.reshape(b, s // n2, n2, 2 * d)
    blk_a = (1, n1, SUBLANES, tc)
    a_re, a_im = pl.pallas_call(
        _dft_stage_a_kernel,
        name="dft_stage_a",
        grid=(b, n2 // SUBLANES, d // tc),
        in_specs=[
            pl.BlockSpec(mat_a.shape, lambda bi, cg, jc: (0, 0)),
            pl.BlockSpec(blk_a, lambda bi, cg, jc: (bi, 0, cg, jc)),
            pl.BlockSpec(blk_a, lambda bi, cg, jc: (bi, 0, cg, d // tc + jc)),
        ],
        out_specs=[pl.BlockSpec(blk_a, lambda bi, cg, jc: (bi, 0, cg, jc))] * 2,
        out_shape=[jax.ShapeDtypeStruct((b, n1, n2, d), F32)] * 2,
        compiler_params=_params(("parallel", "parallel", "parallel")),
    )(mat_a, uv4, uv4)
    cb, sb = _cos_sin(n2)
    eye = np.eye(PACKED_ROWS)
    kron_b = lambda m: np.einsum("kc,ab->kabc", m, eye).reshape(n2 * PACKED_ROWS, PACKED_ROWS * n2)
    mat_b = jnp.asarray(np.concatenate([kron_b(cb), kron_b(sb)], axis=1) / math.sqrt(n2), BF16)
    phi = 2.0 * np.pi * np.outer(np.arange(n1), np.arange(n2)) / float(n_lat)
    twc = jnp.asarray(np.repeat(np.cos(phi).reshape(-1, 1), LANES, axis=1), F32)
    tws = jnp.asarray(np.repeat(np.sin(phi).reshape(-1, 1), LANES, axis=1), F32)
    rows = PACKED_ROWS * n2
    y = pl.pallas_call(
        _dft_stage_b_kernel,
        name="dft_stage_b",
        grid=(b, n1 // PACKED_ROWS, d // tc),
        in_specs=[
            pl.BlockSpec(mat_b.shape, lambda bi, kb, jc: (0, 0)),
            pl.BlockSpec((1, rows, tc), lambda bi, kb, jc: (bi, kb, jc)),
            pl.BlockSpec((1, rows, tc), lambda bi, kb, jc: (bi, kb, jc)),
            pl.BlockSpec((rows, LANES), lambda bi, kb, jc: (kb, 0)),
            pl.BlockSpec((rows, LANES), lambda bi, kb, jc: (kb, 0)),
        ],
        out_specs=pl.BlockSpec((1, n2, PACKED_ROWS, tc), lambda bi, kb, jc: (bi, 0, kb, jc)),
        out_shape=jax.ShapeDtypeStruct((b, n2, n1, d), BF16),
        compiler_params=_params(("parallel", "parallel", "parallel")),
    )(mat_b, a_re.reshape(b, n_lat, d), a_im.reshape(b, n_lat, d), twc, tws)
    cc, sc = _cos_sin(n_ctx)
    mat_c = jnp.asarray(np.concatenate([cc, -sc], axis=1) / math.sqrt(n_ctx), BF16)
    y_ctx = pl.pallas_call(
        functools.partial(_dft_ctx_kernel, d=d),
        name="dft_ctx",
        grid=(b,),
        in_specs=[
            pl.BlockSpec(mat_c.shape, lambda bi: (0, 0)),
            pl.BlockSpec((1, n_ctx, 2 * d), lambda bi: (bi, n_lat // n_ctx, 0)),
        ],
        out_specs=pl.BlockSpec((1, n_ctx, d), lambda bi: (bi, 0, 0)),
        out_shape=jax.ShapeDtypeStruct((b, n_ctx, d), BF16),
        compiler_params=_params(("parallel",)),
    )(mat_c, uv)
    return jnp.concatenate([y.reshape(b, n_lat, d), y_ctx], axis=1)


LANE_SHIFT = 7
GATHER_UNROLL = 8
COMBINE_REGION = 64


def _route_kernel(aff_ref, tri_ref, posc_ref, gate_ref, tst_ref,
                  thr_ref, need_ref, ctie_ref, cpos_ref, *, cap):
    t = pl.program_id(1)
    tile = posc_ref.shape[1]

    @pl.when(t == 0)
    def _():
        def body(it, thr):
            bits = lax.bitcast_convert_type(aff_ref[0], I32)
            cand = thr | jnp.left_shift(jnp.int32(1), 30 - it)
            cnt = jnp.sum(jnp.where(bits >= cand, 1.0, 0.0), axis=0, keepdims=True)
            return jnp.where(cnt >= cap, cand, thr)

        thr = lax.fori_loop(0, 31, body, jnp.zeros((1, LANES), I32))
        bits = lax.bitcast_convert_type(aff_ref[0], I32)
        above = jnp.sum(jnp.where(bits > thr, 1.0, 0.0), axis=0, keepdims=True)
        thr_ref[...] = thr
        need_ref[...] = float(cap) - above
        ctie_ref[...] = jnp.zeros_like(ctie_ref)
        cpos_ref[...] = jnp.zeros_like(cpos_ref)

    a = aff_ref[0, pl.ds(pl.multiple_of(t * tile, tile), tile), :]
    bits = lax.bitcast_convert_type(a, I32)
    thr = thr_ref[...]
    gt = bits > thr
    eq = bits == thr
    eqf = jnp.where(eq, 1.0, 0.0)
    tie_rank = _dot(tri_ref[...], eqf.astype(BF16)) + ctie_ref[...]
    sel = gt | (eq & (tie_rank < need_ref[...]))
    self_ = jnp.where(sel, 1.0, 0.0)
    start = cpos_ref[...]
    pos = jnp.where(sel, _dot(tri_ref[...], self_.astype(BF16)) + start, -1.0)
    posc_ref[0] = pos.astype(I32)
    gate_ref[0] = jnp.where(sel, a, 0.0)
    tst_ref[0, 0] = jnp.broadcast_to(start, (8, LANES)).astype(I32)
    ctie_ref[...] = ctie_ref[...] + jnp.sum(eqf, axis=0, keepdims=True)
    cpos_ref[...] = start + jnp.sum(self_, axis=0, keepdims=True)


def _route(aff, *, row_off, n, cap):
    b = aff.shape[0]
    tile = ROUTE_TILE
    nt = n // tile
    tri = jnp.asarray(np.tril(np.ones((tile, tile)), -1), BF16)
    return pl.pallas_call(
        functools.partial(_route_kernel, cap=cap),
        name="moe_route",
        grid=(b, nt),
        in_specs=[
            pl.BlockSpec((1, n, LANES), lambda bi, t: (bi, row_off // n, 0)),
            pl.BlockSpec((tile, tile), lambda bi, t: (0, 0)),
        ],
        out_specs=[
            pl.BlockSpec((1, tile, LANES), lambda bi, t: (bi, t, 0)),
            pl.BlockSpec((1, tile, LANES), lambda bi, t: (bi, t, 0)),
            pl.BlockSpec((1, 1, 8, LANES), lambda bi, t: (bi, t, 0, 0)),
        ],
        out_shape=[
            jax.ShapeDtypeStruct((b, n, LANES), I32),
            jax.ShapeDtypeStruct((b, n, LANES), F32),
            jax.ShapeDtypeStruct((b, nt, 8, LANES), I32),
        ],
        scratch_shapes=[pltpu.VMEM((1, LANES), I32)] + [pltpu.VMEM((1, LANES), F32)] * 3,
        compiler_params=_params(("parallel", "arbitrary")),
    )(aff, tri)


def _slot_index_kernel(ts_ref, posc_ref, idx_ref, *, nt, row_off):
    bi, t = pl.program_id(0), pl.program_id(1)

    @pl.when(t == 0)
    def _():
        idx_ref[...] = jnp.zeros_like(idx_ref)

    tile = posc_ref.shape[1]
    slot_rows = idx_ref.shape[2]
    tok = (row_off + t * tile + lax.broadcasted_iota(I32, (tile, 1), 0)).astype(F32)
    lane = lax.broadcasted_iota(I32, (tile, 2 * LANES), 1)
    for e in range(N_EXPERTS):
        h0 = jnp.minimum(ts_ref[(bi * nt + t) * N_EXPERTS + e] >> LANE_SHIFT, slot_rows - 2)
        hit = (posc_ref[0, :, e:e + 1] - h0 * LANES) == lane
        vals = jnp.sum(jnp.where(hit, tok, 0.0), axis=0, keepdims=True)
        two_rows = jnp.concatenate([vals[:, :LANES], vals[:, LANES:]], axis=0).astype(I32)
        idx_ref[0, e, pl.ds(h0, 2), :] = idx_ref[0, e, pl.ds(h0, 2), :] + two_rows


def _slot_index(tstart, posc, *, row_off, n, cap):
    b = posc.shape[0]
    tile = ROUTE_TILE
    nt = n // tile
    slot_rows = max(cap, 2 * LANES) // LANES
    grid_spec = pltpu.PrefetchScalarGridSpec(
        num_scalar_prefetch=1,
        grid=(b, nt),
        in_specs=[pl.BlockSpec((1, tile, LANES), lambda bi, t, ts: (bi, t, 0))],
        out_specs=pl.BlockSpec((1, N_EXPERTS, slot_rows, LANES), lambda bi, t, ts: (bi, 0, 0, 0)),
    )
    idx = pl.pallas_call(
        functools.partial(_slot_index_kernel, nt=nt, row_off=row_off),
        name="moe_slot_index",
        grid_spec=grid_spec,
        out_shape=jax.ShapeDtypeStruct((b, N_EXPERTS, slot_rows, LANES), I32),
        compiler_params=_params(("parallel", "arbitrary")),
    )(tstart, posc)
    return idx.reshape(b, N_EXPERTS, slot_rows * LANES)


def _gather_kernel(idx_ref, h_ref, o_ref, buf_ref, sem):
    bi = pl.program_id(1)
    n_rows = buf_ref.shape[0]

    def row_copy(src_row, dst_row, rows):
        return pltpu.make_async_copy(h_ref.at[bi, pl.ds(src_row, rows), :], buf_ref.at[pl.ds(dst_row, rows), :], sem)

    def issue(g, carry):
        for k in range(GATHER_UNROLL):
            s = g * GATHER_UNROLL + k
            row_copy(idx_ref[0, 0, s], s, 1).start(priority=k % 2)
        return carry

    lax.fori_loop(0, n_rows // GATHER_UNROLL, issue, 0)
    row_copy(0, 0, n_rows).wait()
    o_ref[0, 0] = _unpack_bf16_pairs(buf_ref[...])


def _gather(idx, h):
    b, _, half = h.shape
    d = 2 * half
    slots = idx.shape[2]
    assert slots % GATHER_UNROLL == 0
    return pl.pallas_call(
        _gather_kernel,
        name="moe_gather",
        grid=(N_EXPERTS, b),
        in_specs=[
            pl.BlockSpec((1, 1, slots), lambda ei, bi: (bi * N_EXPERTS + ei, 0, 0), memory_space=pltpu.SMEM),
            pl.BlockSpec(memory_space=pl.ANY),
        ],
        out_specs=pl.BlockSpec((1, 1, slots, d), lambda ei, bi: (ei, bi, 0, 0)),
        out_shape=jax.ShapeDtypeStruct((N_EXPERTS, b, slots, d), BF16),
        scratch_shapes=[pltpu.VMEM((slots, half), I32), pltpu.SemaphoreType.DMA(())],
        compiler_params=_params(("arbitrary", "arbitrary")),
    )(idx.reshape(b * N_EXPERTS, 1, slots), h)


def _ffn_kernel(x_ref, wg_ref, wu_ref, wd_ref, ye_ref, hm_ref, *, n_up):
    st = pl.program_id(1)
    bsz, rows, d = x_ref.shape[1:]
    tf = wg_ref.shape[3]

    @pl.when(st < n_up)
    def _():
        x = x_ref[0].reshape(bsz * rows, d)
        a = _dot(x, wg_ref[0, 0].astype(BF16))
        u = _dot(x, wu_ref[0, 0].astype(BF16))
        hm_ref[st] = (_silu(a) * u).astype(BF16)

    @pl.when(st >= n_up)
    def _():
        y = _dot(hm_ref[0], wd_ref[0, 0, 0:tf, :].astype(BF16))
        for c in range(1, n_up):
            y = y + _dot(hm_ref[c], wd_ref[0, 0, c * tf:(c + 1) * tf, :].astype(BF16))
        ye_ref[0] = y.reshape(bsz, rows, y.shape[1]).astype(ye_ref.dtype)


def _ffn(xe, w_gate, w_up, w_down, layer):
    e, b, rows, d = xe.shape
    f = w_gate.shape[3]
    tf = min(256, f)
    tdc = min(512, d)
    n_up, n_down = f // tf, d // tdc
    up = lambda st: jnp.minimum(st, n_up - 1)
    down = lambda st: jnp.maximum(st - n_up, 0)
    return pl.pallas_call(
        functools.partial(_ffn_kernel, n_up=n_up),
        name="moe_ffn",
        grid=(e, n_up + n_down),
        in_specs=[
            pl.BlockSpec((1, b, rows, d), lambda ei, st: (ei, 0, 0, 0)),
            pl.BlockSpec((1, 1, d, tf), lambda ei, st: (layer, ei, 0, up(st))),
            pl.BlockSpec((1, 1, d, tf), lambda ei, st: (layer, ei, 0, up(st))),
            pl.BlockSpec((1, 1, f, tdc), lambda ei, st: (layer, ei, 0, down(st))),
        ],
        out_specs=pl.BlockSpec((1, b, rows, tdc), lambda ei, st: (ei, 0, 0, down(st))),
        out_shape=jax.ShapeDtypeStruct((e, b, rows, d), BF16),
        scratch_shapes=[pltpu.VMEM((n_up, b * rows, tf), BF16)],
        compiler_params=_params(("parallel", "arbitrary")),
    )(xe, w_gate, w_up, w_down)


def _window_start(ts_ref, idx, cap_rows, win):
    a0 = jnp.minimum(ts_ref[idx] & (-SLOT_ALIGN), cap_rows - win)
    return pl.multiple_of(a0, SLOT_ALIGN)


def _combine_kernel(ts_ref, ye_ref, posc_ref, gate_ref, x_ref, g_ref, o_ref, *, nt, win, cap, n_starts):
    bi, t = pl.program_id(0), pl.program_id(2)
    cap_rows = ye_ref.shape[2]
    tile = posc_ref.shape[1]
    base = (bi * nt + t) * N_EXPERTS
    reg = COMBINE_REGION
    assert 2 * reg == LANES and reg <= cap_rows

    starts, fits = [], None
    for e in range(N_EXPERTS):
        a0 = _window_start(ts_ref, base + e, cap_rows, reg)
        nxt = ts_ref[jnp.minimum(base + N_EXPERTS + e, n_starts - 1)]
        end = jnp.where(t + 1 < nt, nxt, cap)
        ok = end - a0 <= reg
        fits = ok if fits is None else jnp.logical_and(fits, ok)
        starts.append(a0)

    @pl.when(fits)
    def _():
        lane = lax.broadcasted_iota(I32, (tile, LANES), 1)
        upper = lane >= reg
        weights, rows = [], []
        for p in range(N_EXPERTS // 2):
            e0, e1 = 2 * p, 2 * p + 1
            slot = jnp.where(upper, starts[e1] - reg, starts[e0]) + lane
            pcol = jnp.where(upper, posc_ref[0, :, e1:e1 + 1], posc_ref[0, :, e0:e0 + 1])
            gcol = jnp.where(upper, gate_ref[0, :, e1:e1 + 1], gate_ref[0, :, e0:e0 + 1])
            weights.append(jnp.where(pcol == slot, gcol, 0.0).astype(BF16))
            rows += [ye_ref[e0, 0, pl.ds(starts[e0], reg), :], ye_ref[e1, 0, pl.ds(starts[e1], reg), :]]
        acc = _dot(jnp.concatenate(weights, axis=1), jnp.concatenate(rows, axis=0))
        o_ref[0] = x_ref[0] + g_ref[0, 0] * acc

    @pl.when(jnp.logical_not(fits))
    def _():
        acc = jnp.zeros(o_ref.shape[1:], F32)
        lane = lax.broadcasted_iota(I32, (tile, win), 1)
        for e in range(N_EXPERTS):
            a0 = _window_start(ts_ref, base + e, cap_rows, win)
            pcol = posc_ref[0, :, e:e + 1]
            gcol = gate_ref[0, :, e:e + 1]
            w = jnp.where(pcol == a0 + lane, gcol, 0.0).astype(BF16)
            acc = acc + _dot(w, ye_ref[e, 0, pl.ds(a0, win), :])
        o_ref[0] = x_ref[0] + g_ref[0, 0] * acc


def _combine(tstart, ye, posc, gate, x, g2, *, row_off, n, region, row0, cap_rows):
    b, s, d = x.shape
    tile = ROUTE_TILE
    nt = n // tile
    dc = 512 if d % 512 == 0 else d
    win = min(2 * tile, cap_rows)
    off = row_off // tile
    grid_spec = pltpu.PrefetchScalarGridSpec(
        num_scalar_prefetch=1,
        grid=(b, d // dc, nt),
        in_specs=[
            pl.BlockSpec((N_EXPERTS, 1, cap_rows, dc), lambda bi, c, t, ts: (0, bi, row0 // cap_rows, c)),
            pl.BlockSpec((1, tile, LANES), lambda bi, c, t, ts: (bi, t, 0)),
            pl.BlockSpec((1, tile, LANES), lambda bi, c, t, ts: (bi, t, 0)),
            pl.BlockSpec((1, tile, dc), lambda bi, c, t, ts: (bi, off + t, c)),
            pl.BlockSpec((1, 1, 1, dc), lambda bi, c, t, ts: (bi, region, 0, c)),
        ],
        out_specs=pl.BlockSpec((1, tile, dc), lambda bi, c, t, ts: (bi, off + t, c)),
    )
    return pl.pallas_call(
        functools.partial(_combine_kernel, nt=nt, win=win, cap=CAPACITY_FACTOR * n // N_EXPERTS,
                          n_starts=b * nt * N_EXPERTS),
        name="moe_combine",
        grid_spec=grid_spec,
        out_shape=jax.ShapeDtypeStruct((b, s, d), F32),
        input_output_aliases={4: 0},
        compiler_params=_params(("parallel", "parallel", "arbitrary")),
    )(tstart, ye, posc, gate, x, g2)


def _moe(x, h, aff, g2, w_gate, w_up, w_down, layer, token_sets):
    routed, row0 = [], 0
    for row_off, n, region in token_sets:
        cap = CAPACITY_FACTOR * n // N_EXPERTS
        cap_rows = -(-cap // LANES) * LANES
        posc, gate, tst = _route(aff, row_off=row_off, n=n, cap=cap)
        tstart = tst[:, :, 0, :N_EXPERTS].reshape(-1)
        idx = _slot_index(tstart, posc, row_off=row_off, n=n, cap=cap)
        routed.append((row_off, n, region, row0, cap_rows, tstart, posc, gate, idx))
        row0 += cap_rows
    idx_all = jnp.concatenate([r[8][:, :, :r[4]] for r in routed], axis=2)
    ye = _ffn(_gather(idx_all, h), w_gate, w_up, w_down, layer)
    for row_off, n, region, r0, cap_rows, tstart, posc, gate, _ in routed:
        x = _combine(tstart, ye, posc, gate, x, g2, row_off=row_off, n=n, region=region, row0=r0, cap_rows=cap_rows)
    return x


def _rope_tables(n_lat, n_ctx, head_dim):
    quarter = head_dim // 4
    inv = ROPE_BASE ** (-jnp.arange(quarter, dtype=F32) / quarter)
    pos = jnp.arange(n_lat)
    row = (pos // GRID_W).astype(F32)[:, None] * inv[None, :]
    col = (pos % GRID_W).astype(F32)[:, None] * inv[None, :]
    cos = jnp.concatenate([jnp.cos(row), jnp.cos(row), jnp.cos(col), jnp.cos(col)], axis=1)
    sin = jnp.concatenate([-jnp.sin(row), jnp.sin(row), -jnp.sin(col), jnp.sin(col)], axis=1)
    cos = jnp.concatenate([cos, jnp.ones((n_ctx, head_dim), F32)], axis=0)
    sin = jnp.concatenate([sin, jnp.zeros((n_ctx, head_dim), F32)], axis=0)
    return cos, sin


def kernel(x, c, ctx, c_ctx, w_mod, b_mod, norm_gain, final_gain, ret_w_in, ret_w_out, ret_decay,
           win_w_qkv, win_w_o, win_sink, fno_w_o, router_w, exp_w_gate, exp_w_up, exp_w_down):
    b, n_lat, d = x.shape
    n_ctx = ctx.shape[1]
    depth = w_mod.shape[0]
    s = n_lat + n_ctx
    assert s % ROW_TILE == 0 and n_lat % NORM_TILE == 0 and n_ctx % NORM_TILE == 0 and n_lat % n_ctx == 0
    assert n_lat % (FFT_INNER * 4) == 0 and b + 1 <= 8 and n_ctx % ROUTE_TILE == 0

    xs = jnp.concatenate([x, ctx], axis=1)
    cvec = jnp.zeros((8, d), F32).at[:b].set(c).at[b].set(c_ctx)
    mod = _modulation(cvec, w_mod, b_mod)

    def mod_pair(i, k):
        lat = mod[i, :b, k * d:(k + 1) * d]
        cx = jnp.broadcast_to(mod[i, b, k * d:(k + 1) * d], (b, d))
        return jnp.stack([lat, cx], axis=1).reshape(b, 2, 1, d)

    ret_dk = d // RET_HEADS
    ret_dv = 2 * ret_dk
    win_hd = d // WIN_HEADS
    ret_cos, ret_sin = _rope_tables(n_lat, n_ctx, ret_dk)
    win_cos, win_sin = _rope_tables(n_lat, n_ctx, win_hd)

    for i in range(depth):
        kind, j = i % N_MIXERS, i // N_MIXERS
        last = i == depth - 1
        sh1, sc1, g1, sh2, sc2, g2 = [mod_pair(i, k) for k in range(6)]
        h = _norm_mod(xs, norm_gain[i, 0], sh1, sc1, n_lat)
        if kind == 0:
            hk, hv = RET_HEADS * ret_dk, RET_HEADS * ret_dv
            w_in = ret_w_in[j].astype(BF16)
            qk = _mm_proj(h, w_in, 0, 2 * hk, PROJ_COL_TILE,
                          rope=(ret_cos, ret_sin, hk, ret_dk ** -0.5, ret_dk))
            vg = _mm_proj(h, w_in, 2 * hk, 3 * hv, PROJ_COL_TILE)
            scan = functools.partial(_ret_scan, n_lat=n_lat, n_ctx=n_ctx, heads=RET_HEADS, dk=ret_dk, dv=ret_dv)
            y_b = scan(qk, vg, ret_decay[j, 1:2], None, reverse=True)
            y = scan(qk, vg, ret_decay[j, 0:1], y_b, reverse=False)
            w_out = ret_w_out[j]
        elif kind == 1:
            nq, nkv = WIN_HEADS * win_hd, WIN_KV_HEADS * win_hd
            w_qkv = win_w_qkv[j].astype(BF16)
            qk = _mm_proj(h, w_qkv, 0, nq + nkv, COL_TILE, rope=(win_cos, win_sin, nq, 1.0, win_hd))
            v = _mm_proj(h, w_qkv, nq + nkv, nkv, COL_TILE)
            y = _win_attn(qk, v, win_sink[j], n_lat=n_lat, n_ctx=n_ctx, heads=WIN_HEADS,
                          kv_heads=WIN_KV_HEADS, hd=win_hd)
            w_out = win_w_o[j]
        else:
            cg = d // FOURIER_GROUPS
            cc, sc = _cos_sin(cg)
            w_ch = jnp.asarray(np.concatenate([cc, sc], axis=1) / math.sqrt(cg), BF16)
            uv = _mm_groups(h, w_ch, FOURIER_GROUPS)
            y = _fourier_positions(uv, n_lat, n_ctx, d)
            w_out = fno_w_o[j]
        xs = _mm_res(y, w_out.astype(BF16), xs, g1[:, 0], g1[:, 1], n_lat)
        h2, aff = _norm_mod(xs, norm_gain[i, 1], sh2, sc2, n_lat, router_w=router_w[i])
        token_sets = [(0, n_lat, 0)] + ([] if last else [(n_lat, n_ctx, 1)])
        xs = _moe(xs, h2, aff, g2, exp_w_gate, exp_w_up, exp_w_down, i, token_sets)

    zeros = jnp.zeros((b, 2, 1, d), F32)
    return _norm_mod(xs, final_gain, zeros, zeros, n_lat, out_dtype=F32, rows=n_lat)
```

```python
import functools
import math

import numpy as np
import jax
import jax.numpy as jnp
from jax import lax
from jax.experimental import pallas as pl
from jax.experimental.pallas import tpu as pltpu

F32 = jnp.float32
BF16 = jnp.bfloat16
I32 = jnp.int32

GRID_W = 64
N_MIXERS = 3
RET_HEADS = 8
RET_CHUNK = 128
WIN_HEADS = 16
WIN_KV_HEADS = 4
WIN_BLOCK = 128
FOURIER_GROUPS = 4
FFT_INNER = 64
N_EXPERTS = 16
CAPACITY_FACTOR = 2
ROPE_BASE = 10000.0
NORM_EPS = 1e-6
NEG_INF = -1e30

LANES = 128
ROUTE_TILE = 128
SLOT_ALIGN = 16
ROW_TILE = 768
MM_ROW_CHUNK = 256
COL_TILE = 512
PROJ_COL_TILE = 1024
NORM_TILES = (768, 512, 256)
VMEM_LIMIT_BYTES = 56 * 1024 * 1024


def _params(sem):
    return pltpu.CompilerParams(dimension_semantics=sem, vmem_limit_bytes=VMEM_LIMIT_BYTES)


def _dot(a, b):
    return jnp.dot(a, b, preferred_element_type=F32)


def _dot_nt(a, b):
    return lax.dot_general(a, b, (((1,), (1,)), ((), ())), preferred_element_type=F32)


def _split_bf16(x):
    hi = x.astype(BF16)
    lo = (x - hi.astype(F32)).astype(BF16)
    return hi, lo


def _silu(x):
    return x / (1.0 + jnp.exp(-x))


def _mod_kernel(c_ref, w_ref, b_ref, o_ref):
    s = _silu(c_ref[...])
    sh, sl = _split_bf16(s)
    wh, wl = _split_bf16(w_ref[0])
    o_ref[0] = _dot(sh, wh) + _dot(sl, wh) + _dot(sh, wl) + b_ref[0]


def _modulation(cvec, w_mod, b_mod):
    depth, d, n = w_mod.shape
    tn = 1024
    return pl.pallas_call(
        _mod_kernel,
        name="modulation",
        grid=(depth, n // tn),
        in_specs=[
            pl.BlockSpec((8, d), lambda i, j: (0, 0)),
            pl.BlockSpec((1, d, tn), lambda i, j: (i, 0, j)),
            pl.BlockSpec((1, 1, tn), lambda i, j: (i, 0, j)),
        ],
        out_specs=pl.BlockSpec((1, 8, tn), lambda i, j: (i, 0, j)),
        out_shape=jax.ShapeDtypeStruct((depth, 8, n), F32),
        compiler_params=_params(("parallel", "parallel")),
    )(cvec, w_mod, b_mod.reshape(depth, 1, n))


def _normed(x_ref, gain_ref, shift_ref, scale_ref, n_lat):
    x = x_ref[0]
    tr = x.shape[0]
    ms = jnp.mean(x * x, axis=-1, keepdims=True)
    y = x * lax.rsqrt(ms + NORM_EPS) * gain_ref[...]
    is_lat = pl.program_id(1) * tr + lax.broadcasted_iota(I32, (tr, 1), 0) < n_lat
    scale = jnp.where(is_lat, scale_ref[0, 0], scale_ref[0, 1])
    shift = jnp.where(is_lat, shift_ref[0, 0], shift_ref[0, 1])
    return y * (1.0 + scale) + shift


def _norm_mod_kernel(x_ref, gain_ref, shift_ref, scale_ref, o_ref, *, n_lat):
    o_ref[0] = _normed(x_ref, gain_ref, shift_ref, scale_ref, n_lat).astype(o_ref.dtype)


def _pack_bf16_pairs(h):
    half = h.shape[1] // 2
    bits = lax.bitcast_convert_type(h.astype(BF16).astype(F32), jnp.uint32)
    word = lax.shift_right_logical(bits[:, :half], jnp.uint32(16)) | (bits[:, half:] & jnp.uint32(0xFFFF0000))
    return lax.bitcast_convert_type(word, I32)


def _unpack_bf16_pairs(word):
    bits = lax.bitcast_convert_type(word, jnp.uint32)
    lo = lax.bitcast_convert_type(lax.shift_left(bits, jnp.uint32(16)), F32)
    hi = lax.bitcast_convert_type(bits & jnp.uint32(0xFFFF0000), F32)
    return jnp.concatenate([lo, hi], axis=1).astype(BF16)


def _norm_router_kernel(x_ref, gain_ref, shift_ref, scale_ref, wr_ref, o_ref, aff_ref, *, n_lat):
    h = _normed(x_ref, gain_ref, shift_ref, scale_ref, n_lat)
    o_ref[0] = _pack_bf16_pairs(h)
    hh, hl = _split_bf16(h)
    wh, wl = _split_bf16(wr_ref[...])
    logits = _dot(hh, wh) + _dot(hl, wh) + _dot(hh, wl)
    lane = lax.broadcasted_iota(I32, logits.shape, 1)
    valid = lane < N_EXPERTS
    logits = jnp.where(valid, logits, -jnp.inf)
    m = jnp.max(logits, axis=-1, keepdims=True)
    p = jnp.exp(logits - m)
    aff = p / jnp.sum(p, axis=-1, keepdims=True)
    aff_ref[0] = jnp.where(valid, aff, 0.0)


def _norm_mod(x, gain, shift, scale, n_lat, *, out_dtype=BF16, rows=None, router_w=None):
    b, s, d = x.shape
    rows = s if rows is None else rows
    tr = next(t for t in NORM_TILES if rows % t == 0 and s % t == 0)
    both = lambda bi, t: (bi, 0, 0, 0)
    in_specs = [
        pl.BlockSpec((1, tr, d), lambda bi, t: (bi, t, 0)),
        pl.BlockSpec((1, d), lambda bi, t: (0, 0)),
        pl.BlockSpec((1, 2, 1, d), both),
        pl.BlockSpec((1, 2, 1, d), both),
    ]
    args = [x, gain.reshape(1, d), shift, scale]
    out_specs = pl.BlockSpec((1, tr, d), lambda bi, t: (bi, t, 0))
    out_shape = jax.ShapeDtypeStruct((b, rows, d), out_dtype)
    kern = functools.partial(_norm_mod_kernel, n_lat=n_lat)
    if router_w is not None:
        wr = jnp.zeros((d, LANES), F32).at[:, :N_EXPERTS].set(router_w)
        in_specs.append(pl.BlockSpec((d, LANES), lambda bi, t: (0, 0)))
        args.append(wr)
        out_specs = [pl.BlockSpec((1, tr, d // 2), lambda bi, t: (bi, t, 0)),
                     pl.BlockSpec((1, tr, LANES), lambda bi, t: (bi, t, 0))]
        out_shape = [jax.ShapeDtypeStruct((b, rows, d // 2), I32),
                     jax.ShapeDtypeStruct((b, rows, LANES), F32)]
        kern = functools.partial(_norm_router_kernel, n_lat=n_lat)
    return pl.pallas_call(
        kern,
        name="norm_mod" if router_w is None else "norm_router",
        grid=(b, rows // tr),
        in_specs=in_specs,
        out_specs=out_specs,
        out_shape=out_shape,
        compiler_params=_params(("parallel", "parallel")),
    )(*args)


def _rope_partner(xs, quarter):
    if 2 * quarter == LANES:
        return pltpu.roll(xs, quarter, 1)
    back = pltpu.roll(xs, quarter, 1)
    fwd = pltpu.roll(xs, LANES - quarter, 1)
    lane = lax.broadcasted_iota(I32, xs.shape, 1)
    return jnp.where((lane % (2 * quarter)) < quarter, fwd, back)


def _row_chunks(tm):
    return [slice(r, r + MM_ROW_CHUNK) for r in range(0, tm, MM_ROW_CHUNK)]


def _mm_rope_kernel(a_ref, w_ref, cos_ref, sin_ref, o_ref, *, n_q, kscale, head_dim):
    sc = jnp.where(pl.program_id(2) >= n_q, kscale, 1.0).astype(F32)
    for rows in _row_chunks(a_ref.shape[1]):
        acc = _dot(a_ref[0, rows, :], w_ref[...])
        for s in range(acc.shape[1] // LANES):
            cols = slice(s * LANES, (s + 1) * LANES)
            off = (s * LANES) % head_dim
            xs = acc[:, cols]
            rot = xs * cos_ref[rows, off:off + LANES] + _rope_partner(xs, head_dim // 4) * sin_ref[rows, off:off + LANES]
            o_ref[0, rows, cols] = (rot * sc).astype(o_ref.dtype)


def _mm_plain_kernel(a_ref, w_ref, o_ref):
    for rows in _row_chunks(a_ref.shape[1]):
        o_ref[0, rows, :] = _dot(a_ref[0, rows, :], w_ref[...]).astype(o_ref.dtype)


def _mm_proj(a, w, col0, ncols, tn, rope=None):
    b, s, k = a.shape
    tm = ROW_TILE
    assert col0 % tn == 0 and ncols % tn == 0
    in_specs = [
        pl.BlockSpec((1, tm, k), lambda bi, i, j: (bi, i, 0)),
        pl.BlockSpec((k, tn), lambda bi, i, j: (0, col0 // tn + j)),
    ]
    args = [a, w]
    if rope is None:
        kern, name = _mm_plain_kernel, "mm_proj"
    else:
        cos_t, sin_t, n_q_cols, kscale, head_dim = rope
        assert tn % head_dim == 0 and n_q_cols % tn == 0
        kern = functools.partial(_mm_rope_kernel, n_q=n_q_cols // tn, kscale=kscale, head_dim=head_dim)
        name = "mm_proj_rope"
        in_specs += [pl.BlockSpec((tm, head_dim), lambda bi, i, j: (i, 0))] * 2
        args += [cos_t, sin_t]
    return pl.pallas_call(
        kern,
        name=name,
        grid=(b, s // tm, ncols // tn),
        in_specs=in_specs,
        out_specs=pl.BlockSpec((1, tm, tn), lambda bi, i, j: (bi, i, j)),
        out_shape=jax.ShapeDtypeStruct((b, s, ncols), BF16),
        compiler_params=_params(("parallel", "parallel", "arbitrary")),
    )(*args)


def _mm_res_kernel(a_ref, w_ref, x_ref, gl_ref, gc_ref, o_ref, *, n_lat):
    tm = a_ref.shape[1]
    for rows in _row_chunks(tm):
        acc = _dot(a_ref[0, rows, :], w_ref[...])
        row = pl.program_id(1) * tm + rows.start + lax.broadcasted_iota(I32, (acc.shape[0], 1), 0)
        gate = jnp.where(row < n_lat, gl_ref[0], gc_ref[0])
        o_ref[0, rows, :] = x_ref[0, rows, :] + gate * acc


def _mm_res(a, w, x, gate_lat, gate_ctx, n_lat):
    b, s, k = a.shape
    n = w.shape[1]
    tm, tn = ROW_TILE, COL_TILE
    return pl.pallas_call(
        functools.partial(_mm_res_kernel, n_lat=n_lat),
        name="mm_residual",
        grid=(b, s // tm, n // tn),
        in_specs=[
            pl.BlockSpec((1, tm, k), lambda bi, i, j: (bi, i, 0)),
            pl.BlockSpec((k, tn), lambda bi, i, j: (0, j)),
            pl.BlockSpec((1, tm, tn), lambda bi, i, j: (bi, i, j)),
            pl.BlockSpec((1, 1, tn), lambda bi, i, j: (bi, 0, j)),
            pl.BlockSpec((1, 1, tn), lambda bi, i, j: (bi, 0, j)),
        ],
        out_specs=pl.BlockSpec((1, tm, tn), lambda bi, i, j: (bi, i, j)),
        out_shape=jax.ShapeDtypeStruct((b, s, n), F32),
        compiler_params=_params(("parallel", "parallel", "arbitrary")),
    )(a, w, x, gate_lat, gate_ctx)


def _mm_groups_kernel(a_ref, w_ref, o_ref):
    o_ref[0] = _dot(a_ref[0], w_ref[...]).astype(o_ref.dtype)


def _mm_groups(a, w, groups):
    b, s, d = a.shape
    cg = d // groups
    tm = ROW_TILE
    return pl.pallas_call(
        _mm_groups_kernel,
        name="mm_channel_dft",
        grid=(b, s // tm, 2 * groups),
        in_specs=[
            pl.BlockSpec((1, tm, cg), lambda bi, i, j: (bi, i, j % groups)),
            pl.BlockSpec((cg, cg), lambda bi, i, j: (0, j // groups)),
        ],
        out_specs=pl.BlockSpec((1, tm, cg), lambda bi, i, j: (bi, i, j)),
        out_shape=jax.ShapeDtypeStruct((b, s, 2 * d), F32),
        compiler_params=_params(("parallel", "parallel", "arbitrary")),
    )(a, w)


def _ret_scan_kernel(*refs, heads, dk, dv, reverse, add_in):
    if add_in:
        dec_ref, q_ref, k_ref, v_ref, g_ref, yin_ref, o_ref, s_ref, qd_ref, kd_ref, in_ref, cd_ref = refs
    else:
        dec_ref, q_ref, k_ref, v_ref, g_ref, o_ref, s_ref, qd_ref, kd_ref, in_ref, cd_ref = refs
        yin_ref = None
    c = q_ref.shape[1]
    j = pl.program_id(1)

    @pl.when(j == 0)
    def _():
        s_ref[...] = jnp.zeros_like(s_ref)
        m_col = lax.broadcasted_iota(I32, (c, LANES), 0).astype(F32)
        m_row = lax.broadcasted_iota(I32, (c, c), 0).astype(F32)
        n_row = lax.broadcasted_iota(I32, (c, c), 1).astype(F32)
        for h in range(heads):
            lg = -jnp.exp(dec_ref[:, h:h + 1])
            if reverse:
                q_pow, k_pow, diff = c - m_col, m_col, n_row - m_row
            else:
                q_pow, k_pow, diff = m_col + 1.0, c - 1.0 - m_col, m_row - n_row
            qd_ref[h] = jnp.exp(lg * q_pow)
            kd_ref[h] = jnp.exp(lg * k_pow)
            in_ref[h] = jnp.where(diff >= 0, jnp.exp(lg * jnp.maximum(diff, 0.0)), 0.0)
            cd_ref[h] = jnp.exp(jnp.broadcast_to(lg, (8, LANES)) * float(c))

    for h in range(heads):
        q = q_ref[0, :, h * dk:(h + 1) * dk]
        k = k_ref[0, :, h * dk:(h + 1) * dk]
        v = v_ref[0, :, h * dv:(h + 1) * dv]
        qdec = jnp.concatenate([qd_ref[h]] * (dk // LANES), axis=1)
        kdec = jnp.concatenate([kd_ref[h]] * (dk // LANES), axis=1)
        state = s_ref[h]
        cross = _dot((q.astype(F32) * qdec).astype(BF16), state.astype(BF16))
        scores = _dot_nt(q, k) * in_ref[h]
        o = cross + _dot(scores.astype(BF16), v)
        k_t = (k.astype(F32) * kdec).T.astype(BF16)
        s_ref[h] = state * cd_ref[h][0:1, 0:1] + _dot(k_t, v)
        mu = jnp.mean(o, axis=-1, keepdims=True)
        cen = o - mu
        var = jnp.mean(cen * cen, axis=-1, keepdims=True)
        g = g_ref[0, :, h * dv:(h + 1) * dv].astype(F32)
        y = cen * lax.rsqrt(var + NORM_EPS) * _silu(g)
        if add_in:
            y = y + yin_ref[0, :, h * dv:(h + 1) * dv].astype(F32)
        o_ref[0, :, h * dv:(h + 1) * dv] = y.astype(o_ref.dtype)


def _ret_scan(qk, vg, decay_row, y_in, *, n_lat, n_ctx, heads, dk, dv, reverse):
    b, s, _ = qk.shape
    c = RET_CHUNK
    lat_chunks, ctx_chunks = n_lat // c, n_ctx // c
    steps = lat_chunks + ctx_chunks
    hk, hv = heads * dk, heads * dv
    if reverse:
        chunk = lambda j: steps - 1 - j
    else:
        chunk = lambda j: jnp.where(j < ctx_chunks, lat_chunks + j, j - ctx_chunks)
    gate_blk = 2 if reverse else 1
    in_specs = [
        pl.BlockSpec((1, heads), lambda bi, j: (0, 0)),
        pl.BlockSpec((1, c, hk), lambda bi, j: (bi, chunk(j), 0)),
        pl.BlockSpec((1, c, hk), lambda bi, j: (bi, chunk(j), 1)),
        pl.BlockSpec((1, c, hv), lambda bi, j: (bi, chunk(j), 0)),
        pl.BlockSpec((1, c, hv), lambda bi, j: (bi, chunk(j), gate_blk)),
    ]
    args = [decay_row, qk, qk, vg, vg]
    if y_in is not None:
        in_specs.append(pl.BlockSpec((1, c, hv), lambda bi, j: (bi, chunk(j), 0)))
        args.append(y_in)
    kern = functools.partial(_ret_scan_kernel, heads=heads, dk=dk, dv=dv, reverse=reverse,
                             add_in=y_in is not None)
    return pl.pallas_call(
        kern,
        name="ret_scan_bwd" if reverse else "ret_scan_fwd",
        grid=(b, steps),
        in_specs=in_specs,
        out_specs=pl.BlockSpec((1, c, hv), lambda bi, j: (bi, chunk(j), 0)),
        out_shape=jax.ShapeDtypeStruct((b, s, hv), BF16),
        scratch_shapes=[
            pltpu.VMEM((heads, dk, dv), F32),
            pltpu.VMEM((heads, c, LANES), F32),
            pltpu.VMEM((heads, c, LANES), F32),
            pltpu.VMEM((heads, c, c), F32),
            pltpu.VMEM((heads, 8, LANES), F32),
        ],
        compiler_params=_params(("parallel", "arbitrary")),
    )(*args)


def _win_attn_kernel(sink_ref, q_ref, kc_ref, vc_ref, kp_ref, kq_ref, kn_ref, vp_ref, vq_ref, vn_ref,
                     o_ref, *, lat_tiles, heads, kv_heads, hd):
    qt = pl.program_id(1)
    blk = q_ref.shape[1]
    n_ctx = kc_ref.shape[1]
    grp = heads // kv_heads
    scale = hd ** -0.5
    rows = grp * blk
    qi = lax.broadcasted_iota(I32, (rows, blk), 0) % blk
    kj = lax.broadcasted_iota(I32, (rows, blk), 1)
    tq = qt + jnp.zeros((rows, blk), I32)
    ok_cur = tq < lat_tiles
    ok_prev = (kj >= qi) & ok_cur & (tq >= 1)
    ok_next = (kj <= qi) & (tq + 1 < lat_tiles)
    bias = jnp.concatenate(
        [jnp.zeros((rows, n_ctx), F32)]
        + [jnp.where(ok, 0.0, NEG_INF).astype(F32) for ok in (ok_prev, ok_cur, ok_next)], axis=1)
    head_row = lax.broadcasted_iota(I32, (rows, 1), 0) // blk
    for kv in range(kv_heads):
        cs = slice(kv * hd, (kv + 1) * hd)
        keys = jnp.concatenate([kc_ref[0, :, cs], kp_ref[0, :, cs], kq_ref[0, :, cs], kn_ref[0, :, cs]], axis=0)
        vals = jnp.concatenate([vc_ref[0, :, cs], vp_ref[0, :, cs], vq_ref[0, :, cs], vn_ref[0, :, cs]], axis=0)
        q = jnp.concatenate([q_ref[0, :, (kv * grp + g) * hd:(kv * grp + g + 1) * hd] for g in range(grp)], axis=0)
        sink = jnp.zeros((rows, 1), F32)
        for g in range(grp):
            h = kv * grp + g
            sink = jnp.where(head_row == g, sink_ref[:, h:h + 1], sink)
        s = _dot_nt(q, keys) * scale + bias
        m = jnp.maximum(jnp.max(s, axis=-1, keepdims=True), sink)
        p = jnp.exp(s - m)
        den = jnp.sum(p, axis=-1, keepdims=True) + jnp.exp(sink - m)
        o = _dot(p.astype(BF16), vals) / den
        for g in range(grp):
            h = kv * grp + g
            o_ref[0, :, h * hd:(h + 1) * hd] = o[g * blk:(g + 1) * blk].astype(o_ref.dtype)


def _win_attn(qk, v, sink, *, n_lat, n_ctx, heads, kv_heads, hd):
    b, s, _ = qk.shape
    blk = WIN_BLOCK
    lat_tiles = n_lat // blk
    tiles = s // blk
    kvw = kv_heads * hd
    k_col = (heads * hd) // kvw
    v_col = 0
    ctx_blk = n_lat // n_ctx
    prev = lambda t: jnp.maximum(t - 1, 0)
    nxt = lambda t: jnp.minimum(t + 1, tiles - 1)
    sink_row = jnp.zeros((1, LANES), F32).at[0, :heads].set(sink.astype(F32))
    kern = functools.partial(_win_attn_kernel, lat_tiles=lat_tiles, heads=heads, kv_heads=kv_heads, hd=hd)
    return pl.pallas_call(
        kern,
        name="win_attn",
        grid=(b, tiles),
        in_specs=[
            pl.BlockSpec((1, LANES), lambda bi, t: (0, 0)),
            pl.BlockSpec((1, blk, heads * hd), lambda bi, t: (bi, t, 0)),
            pl.BlockSpec((1, n_ctx, kvw), lambda bi, t: (bi, ctx_blk, k_col)),
            pl.BlockSpec((1, n_ctx, kvw), lambda bi, t: (bi, ctx_blk, v_col)),
            pl.BlockSpec((1, blk, kvw), lambda bi, t: (bi, prev(t), k_col)),
            pl.BlockSpec((1, blk, kvw), lambda bi, t: (bi, t, k_col)),
            pl.BlockSpec((1, blk, kvw), lambda bi, t: (bi, nxt(t), k_col)),
            pl.BlockSpec((1, blk, kvw), lambda bi, t: (bi, prev(t), v_col)),
            pl.BlockSpec((1, blk, kvw), lambda bi, t: (bi, t, v_col)),
            pl.BlockSpec((1, blk, kvw), lambda bi, t: (bi, nxt(t), v_col)),
        ],
        out_specs=pl.BlockSpec((1, blk, heads * hd), lambda bi, t: (bi, t, 0)),
        out_shape=jax.ShapeDtypeStruct((b, s, heads * hd), BF16),
        compiler_params=_params(("parallel", "parallel")),
    )(sink_row, qk, qk, v, qk, qk, qk, v, v, v)


SUBLANES = 8
PACKED_ROWS = 16


def _dft_stage_a_kernel(m_ref, u_ref, v_ref, re_ref, im_ref):
    n1, sub, tc = u_ref.shape[1:]
    rows = n1 * sub
    stacked = jnp.concatenate([u_ref[0].reshape(rows, tc), v_ref[0].reshape(rows, tc)], axis=0)
    out = _dot(m_ref[...], stacked.astype(BF16))
    re_ref[0] = out[:rows].reshape(n1, sub, tc)
    im_ref[0] = out[rows:].reshape(n1, sub, tc)


def _dft_stage_b_kernel(m_ref, re_ref, im_ref, twc_ref, tws_ref, o_ref):
    n2, sub, tc = o_ref.shape[1:]
    reps = tc // LANES
    ar, ai = re_ref[0], im_ref[0]
    twc = jnp.concatenate([twc_ref[...]] * reps, axis=1)
    tws = jnp.concatenate([tws_ref[...]] * reps, axis=1)
    stacked = jnp.concatenate([ar * twc + ai * tws, ai * twc - ar * tws], axis=0).astype(BF16)
    out = _dot(m_ref[...], stacked)
    o_ref[0] = out.reshape(n2, sub, tc).astype(o_ref.dtype)


def _dft_ctx_kernel(m_ref, x_ref, o_ref, *, d):
    x = x_ref[0]
    stacked = jnp.concatenate([x[:, :d], x[:, d:]], axis=0).astype(BF16)
    o_ref[0] = _dot(m_ref[...], stacked).astype(o_ref.dtype)


def _cos_sin(n):
    ang = 2.0 * np.pi * np.outer(np.arange(n), np.arange(n)) / float(n)
    return np.cos(ang), np.sin(ang)


def _fourier_positions(uv, n_lat, n_ctx, d):
    b, s, _ = uv.shape
    n2 = FFT_INNER
    n1 = n_lat // n2
    tc = min(512, d)
    ca, sa = _cos_sin(n1)
    eye = np.eye(SUBLANES)
    ka, ks = np.kron(ca, eye) / math.sqrt(n1), np.kron(sa, eye) / math.sqrt(n1)
    mat_a = jnp.asarray(np.block([[ka, -ks], [-ks, -ka]]), BF16)
    uv4 = uv.reshape(b, s // n2, n2, 2 * d)
    blk_a = (1, n1, SUBLANES, tc)
    a_re, a_im = pl.pallas_call(
        _dft_stage_a_kernel,
        name="dft_stage_a",
        grid=(b, n2 // SUBLANES, d // tc),
        in_specs=[
            pl.BlockSpec(mat_a.shape, lambda bi, cg, jc: (0, 0)),
            pl.BlockSpec(blk_a, lambda bi, cg, jc: (bi, 0, cg, jc)),
            pl.BlockSpec(blk_a, lambda bi, cg, jc: (bi, 0, cg, d // tc + jc)),
        ],
        out_specs=[pl.BlockSpec(blk_a, lambda bi, cg, jc: (bi, 0, cg, jc))] * 2,
        out_shape=[jax.ShapeDtypeStruct((b, n1, n2, d), F32)] * 2,
        compiler_params=_params(("parallel", "parallel", "parallel")),
    )(mat_a, uv4, uv4)
    cb, sb = _cos_sin(n2)
    eye = np.eye(PACKED_ROWS)
    kron_b = lambda m: np.einsum("kc,ab->kabc", m, eye).reshape(n2 * PACKED_ROWS, PACKED_ROWS * n2)
    mat_b = jnp.asarray(np.concatenate([kron_b(cb), kron_b(sb)], axis=1) / math.sqrt(n2), BF16)
    phi = 2.0 * np.pi * np.outer(np.arange(n1), np.arange(n2)) / float(n_lat)
    twc = jnp.asarray(np.repeat(np.cos(phi).reshape(-1, 1), LANES, axis=1), F32)
    tws = jnp.asarray(np.repeat(np.sin(phi).reshape(-1, 1), LANES, axis=1), F32)
    rows = PACKED_ROWS * n2
    y = pl.pallas_call(
        _dft_stage_b_kernel,
        name="dft_stage_b",
        grid=(b, n1 // PACKED_ROWS, d // tc),
        in_specs=[
            pl.BlockSpec(mat_b.shape, lambda bi, kb, jc: (0, 0)),
            pl.BlockSpec((1, rows, tc), lambda bi, kb, jc: (bi, kb, jc)),
            pl.BlockSpec((1, rows, tc), lambda bi, kb, jc: (bi, kb, jc)),
            pl.BlockSpec((rows, LANES), lambda bi, kb, jc: (kb, 0)),
            pl.BlockSpec((rows, LANES), lambda bi, kb, jc: (kb, 0)),
        ],
        out_specs=pl.BlockSpec((1, n2, PACKED_ROWS, tc), lambda bi, kb, jc: (bi, 0, kb, jc)),
        out_shape=jax.ShapeDtypeStruct((b, n2, n1, d), BF16),
        compiler_params=_params(("parallel", "parallel", "parallel")),
    )(mat_b, a_re.reshape(b, n_lat, d), a_im.reshape(b, n_lat, d), twc, tws)
    cc, sc = _cos_sin(n_ctx)
    mat_c = jnp.asarray(np.concatenate([cc, -sc], axis=1) / math.sqrt(n_ctx), BF16)
    y_ctx = pl.pallas_call(
        functools.partial(_dft_ctx_kernel, d=d),
        name="dft_ctx",
        grid=(b,),
        in_specs=[
            pl.BlockSpec(mat_c.shape, lambda bi: (0, 0)),
            pl.BlockSpec((1, n_ctx, 2 * d), lambda bi: (bi, n_lat // n_ctx, 0)),
        ],
        out_specs=pl.BlockSpec((1, n_ctx, d), lambda bi: (bi, 0, 0)),
        out_shape=jax.ShapeDtypeStruct((b, n_ctx, d), BF16),
        compiler_params=_params(("parallel",)),
    )(mat_c, uv)
    return jnp.concatenate([y.reshape(b, n_lat, d), y_ctx], axis=1)


LANE_SHIFT = 7
GATHER_UNROLL = 8
TILES_PER_STEP = 4
COMBINE_REGION = 64


def _route_kernel(aff_ref, tri_ref, posc_ref, gate_ref, tst_ref,
                  thr_ref, need_ref, ctie_ref, cpos_ref, *, cap):
    step = pl.program_id(1)
    tile = tri_ref.shape[0]
    tps = posc_ref.shape[1] // tile

    @pl.when(step == 0)
    def _():
        def body(it, thr):
            bits = lax.bitcast_convert_type(aff_ref[0], I32)
            cand = thr | jnp.left_shift(jnp.int32(1), 30 - it)
            cnt = jnp.sum(jnp.where(bits >= cand, 1.0, 0.0), axis=0, keepdims=True)
            return jnp.where(cnt >= cap, cand, thr)

        thr = lax.fori_loop(0, 31, body, jnp.zeros((1, LANES), I32))
        bits = lax.bitcast_convert_type(aff_ref[0], I32)
        above = jnp.sum(jnp.where(bits > thr, 1.0, 0.0), axis=0, keepdims=True)
        thr_ref[...] = thr
        need_ref[...] = float(cap) - above
        ctie_ref[...] = jnp.zeros_like(ctie_ref)
        cpos_ref[...] = jnp.zeros_like(cpos_ref)

    thr = thr_ref[...]
    for u in range(tps):
        rows = slice(u * tile, (u + 1) * tile)
        a = aff_ref[0, pl.ds(pl.multiple_of((step * tps + u) * tile, tile), tile), :]
        bits = lax.bitcast_convert_type(a, I32)
        gt = bits > thr
        eq = bits == thr
        eqf = jnp.where(eq, 1.0, 0.0)
        tie_rank = _dot(tri_ref[...], eqf.astype(BF16)) + ctie_ref[...]
        sel = gt | (eq & (tie_rank < need_ref[...]))
        self_ = jnp.where(sel, 1.0, 0.0)
        start = cpos_ref[...]
        pos = jnp.where(sel, _dot(tri_ref[...], self_.astype(BF16)) + start, -1.0)
        posc_ref[0, rows, :] = pos.astype(I32)
        gate_ref[0, rows, :] = jnp.where(sel, a, 0.0)
        tst_ref[0, u] = jnp.broadcast_to(start, (8, LANES)).astype(I32)
        ctie_ref[...] = ctie_ref[...] + jnp.sum(eqf, axis=0, keepdims=True)
        cpos_ref[...] = start + jnp.sum(self_, axis=0, keepdims=True)


def _route(aff, *, row_off, n, cap):
    b = aff.shape[0]
    tile = ROUTE_TILE
    nt = n // tile
    tps = min(TILES_PER_STEP, nt)
    tri = jnp.asarray(np.tril(np.ones((tile, tile)), -1), BF16)
    return pl.pallas_call(
        functools.partial(_route_kernel, cap=cap),
        name="moe_route",
        grid=(b, nt // tps),
        in_specs=[
            pl.BlockSpec((1, n, LANES), lambda bi, t: (bi, row_off // n, 0)),
            pl.BlockSpec((tile, tile), lambda bi, t: (0, 0)),
        ],
        out_specs=[
            pl.BlockSpec((1, tps * tile, LANES), lambda bi, t: (bi, t, 0)),
            pl.BlockSpec((1, tps * tile, LANES), lambda bi, t: (bi, t, 0)),
            pl.BlockSpec((1, tps, 8, LANES), lambda bi, t: (bi, t, 0, 0)),
        ],
        out_shape=[
            jax.ShapeDtypeStruct((b, n, LANES), I32),
            jax.ShapeDtypeStruct((b, n, LANES), F32),
            jax.ShapeDtypeStruct((b, nt, 8, LANES), I32),
        ],
        scratch_shapes=[pltpu.VMEM((1, LANES), I32)] + [pltpu.VMEM((1, LANES), F32)] * 3,
        compiler_params=_params(("parallel", "arbitrary")),
    )(aff, tri)


def _slot_index_kernel(ts_ref, posc_ref, idx_ref, *, nt, row_off, tile):
    bi, step = pl.program_id(0), pl.program_id(1)

    @pl.when(step == 0)
    def _():
        idx_ref[...] = jnp.zeros_like(idx_ref)

    tps = posc_ref.shape[1] // tile
    slot_rows = idx_ref.shape[2]
    lane = lax.broadcasted_iota(I32, (tile, 2 * LANES), 1)
    for u in range(tps):
        t = step * tps + u
        tok = (row_off + t * tile + lax.broadcasted_iota(I32, (tile, 1), 0)).astype(F32)
        for e in range(N_EXPERTS):
            h0 = jnp.minimum(ts_ref[(bi * nt + t) * N_EXPERTS + e] >> LANE_SHIFT, slot_rows - 2)
            hit = (posc_ref[0, u * tile:(u + 1) * tile, e:e + 1] - h0 * LANES) == lane
            vals = jnp.sum(jnp.where(hit, tok, 0.0), axis=0, keepdims=True)
            two_rows = jnp.concatenate([vals[:, :LANES], vals[:, LANES:]], axis=0).astype(I32)
            idx_ref[0, e, pl.ds(h0, 2), :] = idx_ref[0, e, pl.ds(h0, 2), :] + two_rows


def _slot_index(tstart, posc, *, row_off, n, cap):
    b = posc.shape[0]
    tile = ROUTE_TILE
    nt = n // tile
    slot_rows = max(cap, 2 * LANES) // LANES
    tps = min(TILES_PER_STEP, nt)
    grid_spec = pltpu.PrefetchScalarGridSpec(
        num_scalar_prefetch=1,
        grid=(b, nt // tps),
        in_specs=[pl.BlockSpec((1, tps * tile, LANES), lambda bi, t, ts: (bi, t, 0))],
        out_specs=pl.BlockSpec((1, N_EXPERTS, slot_rows, LANES), lambda bi, t, ts: (bi, 0, 0, 0)),
    )
    idx = pl.pallas_call(
        functools.partial(_slot_index_kernel, nt=nt, row_off=row_off, tile=tile),
        name="moe_slot_index",
        grid_spec=grid_spec,
        out_shape=jax.ShapeDtypeStruct((b, N_EXPERTS, slot_rows, LANES), I32),
        compiler_params=_params(("parallel", "arbitrary")),
    )(tstart, posc)
    return idx.reshape(b, N_EXPERTS, slot_rows * LANES)


def _gather_kernel(idx_ref, h_ref, o_ref, buf_ref, sem):
    bi = pl.program_id(1)
    n_rows = buf_ref.shape[0]

    def row_copy(src_row, dst_row, rows):
        return pltpu.make_async_copy(h_ref.at[bi, pl.ds(src_row, rows), :], buf_ref.at[pl.ds(dst_row, rows), :], sem)

    def issue(g, carry):
        for k in range(GATHER_UNROLL):
            s = g * GATHER_UNROLL + k
            row_copy(idx_ref[0, 0, s], s, 1).start(priority=k % 2)
        return carry

    lax.fori_loop(0, n_rows // GATHER_UNROLL, issue, 0)
    row_copy(0, 0, n_rows).wait()
    o_ref[0, 0] = _unpack_bf16_pairs(buf_ref[...])


def _gather(idx, h):
    b, _, half = h.shape
    d = 2 * half
    slots = idx.shape[2]
    assert slots % GATHER_UNROLL == 0
    return pl.pallas_call(
        _gather_kernel,
        name="moe_gather",
        grid=(N_EXPERTS, b),
        in_specs=[
            pl.BlockSpec((1, 1, slots), lambda ei, bi: (bi * N_EXPERTS + ei, 0, 0), memory_space=pltpu.SMEM),
            pl.BlockSpec(memory_space=pl.ANY),
        ],
        out_specs=pl.BlockSpec((1, 1, slots, d), lambda ei, bi: (ei, bi, 0, 0)),
        out_shape=jax.ShapeDtypeStruct((N_EXPERTS, b, slots, d), BF16),
        scratch_shapes=[pltpu.VMEM((slots, half), I32), pltpu.SemaphoreType.DMA(())],
        compiler_params=_params(("arbitrary", "arbitrary")),
    )(idx.reshape(b * N_EXPERTS, 1, slots), h)


def _ffn_kernel(x_ref, wg_ref, wu_ref, wd_ref, ye_ref, hm_ref, *, n_up):
    st = pl.program_id(1)
    bsz, rows, d = x_ref.shape[1:]
    tf = wg_ref.shape[3]

    @pl.when(st < n_up)
    def _():
        x = x_ref[0].reshape(bsz * rows, d)
        a = _dot(x, wg_ref[0, 0].astype(BF16))
        u = _dot(x, wu_ref[0, 0].astype(BF16))
        hm_ref[st] = (_silu(a) * u).astype(BF16)

    @pl.when(st >= n_up)
    def _():
        y = _dot(hm_ref[0], wd_ref[0, 0, 0:tf, :].astype(BF16))
        for c in range(1, n_up):
            y = y + _dot(hm_ref[c], wd_ref[0, 0, c * tf:(c + 1) * tf, :].astype(BF16))
        ye_ref[0] = y.reshape(bsz, rows, y.shape[1]).astype(ye_ref.dtype)


def _ffn(xe, w_gate, w_up, w_down, layer):
    e, b, rows, d = xe.shape
    f = w_gate.shape[3]
    tf = min(256, f)
    tdc = min(512, d)
    n_up, n_down = f // tf, d // tdc
    up = lambda st: jnp.minimum(st, n_up - 1)
    down = lambda st: jnp.maximum(st - n_up, 0)
    return pl.pallas_call(
        functools.partial(_ffn_kernel, n_up=n_up),
        name="moe_ffn",
        grid=(e, n_up + n_down),
        in_specs=[
            pl.BlockSpec((1, b, rows, d), lambda ei, st: (ei, 0, 0, 0)),
            pl.BlockSpec((1, 1, d, tf), lambda ei, st: (layer, ei, 0, up(st))),
            pl.BlockSpec((1, 1, d, tf), lambda ei, st: (layer, ei, 0, up(st))),
            pl.BlockSpec((1, 1, f, tdc), lambda ei, st: (layer, ei, 0, down(st))),
        ],
        out_specs=pl.BlockSpec((1, b, rows, tdc), lambda ei, st: (ei, 0, 0, down(st))),
        out_shape=jax.ShapeDtypeStruct((e, b, rows, d), BF16),
        scratch_shapes=[pltpu.VMEM((n_up, b * rows, tf), BF16)],
        compiler_params=_params(("parallel", "arbitrary")),
    )(xe, w_gate, w_up, w_down)


def _window_start(ts_ref, idx, cap_rows, win):
    a0 = jnp.minimum(ts_ref[idx] & (-SLOT_ALIGN), cap_rows - win)
    return pl.multiple_of(a0, SLOT_ALIGN)


def _combine_kernel(ts_ref, ye_ref, posc_ref, gate_ref, x_ref, g_ref, o_ref, *, nt, win, cap, n_starts, tile):
    bi, step = pl.program_id(0), pl.program_id(2)
    cap_rows = ye_ref.shape[2]
    tps = posc_ref.shape[1] // tile
    reg = COMBINE_REGION
    assert 2 * reg == LANES and reg <= cap_rows

    for u in range(tps):
        t = step * tps + u
        rows = slice(u * tile, (u + 1) * tile)
        base = (bi * nt + t) * N_EXPERTS
        starts, fits = [], None
        for e in range(N_EXPERTS):
            a0 = _window_start(ts_ref, base + e, cap_rows, reg)
            nxt = ts_ref[jnp.minimum(base + N_EXPERTS + e, n_starts - 1)]
            end = jnp.where(t + 1 < nt, nxt, cap)
            ok = end - a0 <= reg
            fits = ok if fits is None else jnp.logical_and(fits, ok)
            starts.append(a0)

        @pl.when(fits)
        def _():
            lane = lax.broadcasted_iota(I32, (tile, LANES), 1)
            upper = lane >= reg
            weights, slabs = [], []
            for p in range(N_EXPERTS // 2):
                e0, e1 = 2 * p, 2 * p + 1
                slot = jnp.where(upper, starts[e1] - reg, starts[e0]) + lane
                pcol = jnp.where(upper, posc_ref[0, rows, e1:e1 + 1], posc_ref[0, rows, e0:e0 + 1])
                gcol = jnp.where(upper, gate_ref[0, rows, e1:e1 + 1], gate_ref[0, rows, e0:e0 + 1])
                weights.append(jnp.where(pcol == slot, gcol, 0.0).astype(BF16))
                slabs += [ye_ref[e0, 0, pl.ds(starts[e0], reg), :], ye_ref[e1, 0, pl.ds(starts[e1], reg), :]]
            acc = _dot(jnp.concatenate(weights, axis=1), jnp.concatenate(slabs, axis=0))
            o_ref[0, rows, :] = x_ref[0, rows, :] + g_ref[0, 0] * acc

        @pl.when(jnp.logical_not(fits))
        def _():
            acc = jnp.zeros((tile, o_ref.shape[2]), F32)
            lane = lax.broadcasted_iota(I32, (tile, win), 1)
            for e in range(N_EXPERTS):
                a0 = _window_start(ts_ref, base + e, cap_rows, win)
                pcol = posc_ref[0, rows, e:e + 1]
                gcol = gate_ref[0, rows, e:e + 1]
                w = jnp.where(pcol == a0 + lane, gcol, 0.0).astype(BF16)
                acc = acc + _dot(w, ye_ref[e, 0, pl.ds(a0, win), :])
            o_ref[0, rows, :] = x_ref[0, rows, :] + g_ref[0, 0] * acc


def _combine(tstart, ye, posc, gate, x, g2, *, row_off, n, region, row0, cap_rows):
    b, s, d = x.shape
    tile = ROUTE_TILE
    nt = n // tile
    dc = 512 if d % 512 == 0 else d
    win = min(2 * tile, cap_rows)
    tps = min(TILES_PER_STEP, nt)
    rows = tps * tile
    assert row_off % rows == 0
    off = row_off // rows
    grid_spec = pltpu.PrefetchScalarGridSpec(
        num_scalar_prefetch=1,
        grid=(b, d // dc, nt // tps),
        in_specs=[
            pl.BlockSpec((N_EXPERTS, 1, cap_rows, dc), lambda bi, c, t, ts: (0, bi, row0 // cap_rows, c)),
            pl.BlockSpec((1, rows, LANES), lambda bi, c, t, ts: (bi, t, 0)),
            pl.BlockSpec((1, rows, LANES), lambda bi, c, t, ts: (bi, t, 0)),
            pl.BlockSpec((1, rows, dc), lambda bi, c, t, ts: (bi, off + t, c)),
            pl.BlockSpec((1, 1, 1, dc), lambda bi, c, t, ts: (bi, region, 0, c)),
        ],
        out_specs=pl.BlockSpec((1, rows, dc), lambda bi, c, t, ts: (bi, off + t, c)),
    )
    return pl.pallas_call(
        functools.partial(_combine_kernel, nt=nt, win=win, cap=CAPACITY_FACTOR * n // N_EXPERTS,
                          n_starts=b * nt * N_EXPERTS, tile=tile),
        name="moe_combine",
        grid_spec=grid_spec,
        out_shape=jax.ShapeDtypeStruct((b, s, d), F32),
        input_output_aliases={4: 0},
        compiler_params=_params(("parallel", "parallel", "arbitrary")),
    )(tstart, ye, posc, gate, x, g2)


def _moe(x, h, aff, g2, w_gate, w_up, w_down, layer, token_sets):
    routed, row0 = [], 0
    for row_off, n, region in token_sets:
        cap = CAPACITY_FACTOR * n // N_EXPERTS
        cap_rows = -(-cap // LANES) * LANES
        posc, gate, tst = _route(aff, row_off=row_off, n=n, cap=cap)
        tstart = tst[:, :, 0, :N_EXPERTS].reshape(-1)
        idx = _slot_index(tstart, posc, row_off=row_off, n=n, cap=cap)
        routed.append((row_off, n, region, row0, cap_rows, tstart, posc, gate, idx))
        row0 += cap_rows
    idx_all = jnp.concatenate([r[8][:, :, :r[4]] for r in routed], axis=2)
    ye = _ffn(_gather(idx_all, h), w_gate, w_up, w_down, layer)
    for row_off, n, region, r0, cap_rows, tstart, posc, gate, _ in routed:
        x = _combine(tstart, ye, posc, gate, x, g2, row_off=row_off, n=n, region=region, row0=r0, cap_rows=cap_rows)
    return x


def _rope_tables(n_lat, n_ctx, head_dim):
    quarter = head_dim // 4
    inv = ROPE_BASE ** (-jnp.arange(quarter, dtype=F32) / quarter)
    pos = jnp.arange(n_lat)
    row = (pos // GRID_W).astype(F32)[:, None] * inv[None, :]
    col = (pos % GRID_W).astype(F32)[:, None] * inv[None, :]
    cos = jnp.concatenate([jnp.cos(row), jnp.cos(row), jnp.cos(col), jnp.cos(col)], axis=1)
    sin = jnp.concatenate([-jnp.sin(row), jnp.sin(row), -jnp.sin(col), jnp.sin(col)], axis=1)
    cos = jnp.concatenate([cos, jnp.ones((n_ctx, head_dim), F32)], axis=0)
    sin = jnp.concatenate([sin, jnp.zeros((n_ctx, head_dim), F32)], axis=0)
    return cos, sin


def kernel(x, c, ctx, c_ctx, w_mod, b_mod, norm_gain, final_gain, ret_w_in, ret_w_out, ret_decay,
           win_w_qkv, win_w_o, win_sink, fno_w_o, router_w, exp_w_gate, exp_w_up, exp_w_down):
    b, n_lat, d = x.shape
    n_ctx = ctx.shape[1]
    depth = w_mod.shape[0]
    s = n_lat + n_ctx
    assert s % ROW_TILE == 0 and n_lat % n_ctx == 0
    assert n_lat % (FFT_INNER * 4) == 0 and b + 1 <= 8 and n_ctx % ROUTE_TILE == 0

    xs = jnp.concatenate([x, ctx], axis=1)
    cvec = jnp.zeros((8, d), F32).at[:b].set(c).at[b].set(c_ctx)
    mod = _modulation(cvec, w_mod, b_mod)

    def mod_pair(i, k):
        lat = mod[i, :b, k * d:(k + 1) * d]
        cx = jnp.broadcast_to(mod[i, b, k * d:(k + 1) * d], (b, d))
        return jnp.stack([lat, cx], axis=1).reshape(b, 2, 1, d)

    ret_dk = d // RET_HEADS
    ret_dv = 2 * ret_dk
    win_hd = d // WIN_HEADS
    ret_cos, ret_sin = _rope_tables(n_lat, n_ctx, ret_dk)
    win_cos, win_sin = _rope_tables(n_lat, n_ctx, win_hd)

    for i in range(depth):
        kind, j = i % N_MIXERS, i // N_MIXERS
        last = i == depth - 1
        sh1, sc1, g1, sh2, sc2, g2 = [mod_pair(i, k) for k in range(6)]
        h = _norm_mod(xs, norm_gain[i, 0], sh1, sc1, n_lat)
        if kind == 0:
            hk, hv = RET_HEADS * ret_dk, RET_HEADS * ret_dv
            w_in = ret_w_in[j].astype(BF16)
            qk = _mm_proj(h, w_in, 0, 2 * hk, PROJ_COL_TILE,
                          rope=(ret_cos, ret_sin, hk, ret_dk ** -0.5, ret_dk))
            vg = _mm_proj(h, w_in, 2 * hk, 3 * hv, PROJ_COL_TILE)
            scan = functools.partial(_ret_scan, n_lat=n_lat, n_ctx=n_ctx, heads=RET_HEADS, dk=ret_dk, dv=ret_dv)
            y_b = scan(qk, vg, ret_decay[j, 1:2], None, reverse=True)
            y = scan(qk, vg, ret_decay[j, 0:1], y_b, reverse=False)
            w_out = ret_w_out[j]
        elif kind == 1:
            nq, nkv = WIN_HEADS * win_hd, WIN_KV_HEADS * win_hd
            w_qkv = win_w_qkv[j].astype(BF16)
            qk = _mm_proj(h, w_qkv, 0, nq + nkv, COL_TILE, rope=(win_cos, win_sin, nq, 1.0, win_hd))
            v = _mm_proj(h, w_qkv, nq + nkv, nkv, COL_TILE)
            y = _win_attn(qk, v, win_sink[j], n_lat=n_lat, n_ctx=n_ctx, heads=WIN_HEADS,
                          kv_heads=WIN_KV_HEADS, hd=win_hd)
            w_out = win_w_o[j]
        else:
            cg = d // FOURIER_GROUPS
            cc, sc = _cos_sin(cg)
            w_ch = jnp.asarray(np.concatenate([cc, sc], axis=1) / math.sqrt(cg), BF16)
            uv = _mm_groups(h, w_ch, FOURIER_GROUPS)
            y = _fourier_positions(uv, n_lat, n_ctx, d)
            w_out = fno_w_o[j]
        xs = _mm_res(y, w_out.astype(BF16), xs, g1[:, 0], g1[:, 1], n_lat)
        h2, aff = _norm_mod(xs, norm_gain[i, 1], sh2, sc2, n_lat, router_w=router_w[i])
        token_sets = [(0, n_lat, 0)] + ([] if last else [(n_lat, n_ctx, 1)])
        xs = _moe(xs, h2, aff, g2, exp_w_gate, exp_w_up, exp_w_down, i, token_sets)

    zeros = jnp.zeros((b, 2, 1, d), F32)
    return _norm_mod(xs, final_gain, zeros, zeros, n_lat, out_dtype=F32, rows=n_lat)
```

```python
import functools
import math

import numpy as np
import jax
import jax.numpy as jnp
from jax import lax
from jax.experimental import pallas as pl
from jax.experimental.pallas import tpu as pltpu

F32 = jnp.float32
BF16 = jnp.bfloat16
I32 = jnp.int32

GRID_W = 64
N_MIXERS = 3
RET_HEADS = 8
RET_CHUNK = 128
WIN_HEADS = 16
WIN_KV_HEADS = 4
WIN_BLOCK = 128
FOURIER_GROUPS = 4
FFT_INNER = 64
N_EXPERTS = 16
CAPACITY_FACTOR = 2
ROPE_BASE = 10000.0
NORM_EPS = 1e-6
NEG_INF = -1e30

LANES = 128
ROUTE_TILE = 128
SLOT_ALIGN = 16
ROW_TILE = 768
MM_ROW_CHUNK = 256
COL_TILE = 512
RESIDENT_WEIGHT_BYTES = 8 * 1024 * 1024
PROJ_COL_TILE = 1024
NORM_TILES = (768, 512, 256)
VMEM_LIMIT_BYTES = 56 * 1024 * 1024


def _params(sem):
    return pltpu.CompilerParams(dimension_semantics=sem, vmem_limit_bytes=VMEM_LIMIT_BYTES)


def _dot(a, b):
    return jnp.dot(a, b, preferred_element_type=F32)


def _dot_nt(a, b):
    return lax.dot_general(a, b, (((1,), (1,)), ((), ())), preferred_element_type=F32)


def _split_bf16(x):
    hi = x.astype(BF16)
    lo = (x - hi.astype(F32)).astype(BF16)
    return hi, lo


def _silu(x):
    return x / (1.0 + jnp.exp(-x))


def _mod_kernel(c_ref, w_ref, b_ref, o_ref):
    s = _silu(c_ref[...])
    sh, sl = _split_bf16(s)
    wh, wl = _split_bf16(w_ref[0])
    o_ref[0] = _dot(sh, wh) + _dot(sl, wh) + _dot(sh, wl) + b_ref[0]


def _modulation(cvec, w_mod, b_mod):
    depth, d, n = w_mod.shape
    tn = 1024
    return pl.pallas_call(
        _mod_kernel,
        name="modulation",
        grid=(depth, n // tn),
        in_specs=[
            pl.BlockSpec((8, d), lambda i, j: (0, 0)),
            pl.BlockSpec((1, d, tn), lambda i, j: (i, 0, j)),
            pl.BlockSpec((1, 1, tn), lambda i, j: (i, 0, j)),
        ],
        out_specs=pl.BlockSpec((1, 8, tn), lambda i, j: (i, 0, j)),
        out_shape=jax.ShapeDtypeStruct((depth, 8, n), F32),
        compiler_params=_params(("parallel", "parallel")),
    )(cvec, w_mod, b_mod.reshape(depth, 1, n))


def _normed(x_ref, gain_ref, shift_ref, scale_ref, n_lat):
    x = x_ref[0]
    tr = x.shape[0]
    ms = jnp.mean(x * x, axis=-1, keepdims=True)
    y = x * lax.rsqrt(ms + NORM_EPS) * gain_ref[...]
    is_lat = pl.program_id(1) * tr + lax.broadcasted_iota(I32, (tr, 1), 0) < n_lat
    scale = jnp.where(is_lat, scale_ref[0, 0], scale_ref[0, 1])
    shift = jnp.where(is_lat, shift_ref[0, 0], shift_ref[0, 1])
    return y * (1.0 + scale) + shift


def _norm_mod_kernel(x_ref, gain_ref, shift_ref, scale_ref, o_ref, *, n_lat):
    o_ref[0] = _normed(x_ref, gain_ref, shift_ref, scale_ref, n_lat).astype(o_ref.dtype)


def _pack_bf16_pairs(h):
    half = h.shape[1] // 2
    bits = lax.bitcast_convert_type(h.astype(BF16).astype(F32), jnp.uint32)
    word = lax.shift_right_logical(bits[:, :half], jnp.uint32(16)) | (bits[:, half:] & jnp.uint32(0xFFFF0000))
    return lax.bitcast_convert_type(word, I32)


def _unpack_bf16_pairs(word):
    bits = lax.bitcast_convert_type(word, jnp.uint32)
    lo = lax.bitcast_convert_type(lax.shift_left(bits, jnp.uint32(16)), F32)
    hi = lax.bitcast_convert_type(bits & jnp.uint32(0xFFFF0000), F32)
    return jnp.concatenate([lo, hi], axis=1).astype(BF16)


def _norm_router_kernel(x_ref, gain_ref, shift_ref, scale_ref, wr_ref, o_ref, aff_ref, *, n_lat):
    h = _normed(x_ref, gain_ref, shift_ref, scale_ref, n_lat)
    o_ref[0] = _pack_bf16_pairs(h)
    hh, hl = _split_bf16(h)
    wh, wl = _split_bf16(wr_ref[...])
    logits = _dot(hh, wh) + _dot(hl, wh) + _dot(hh, wl)
    lane = lax.broadcasted_iota(I32, logits.shape, 1)
    valid = lane < N_EXPERTS
    logits = jnp.where(valid, logits, -jnp.inf)
    m = jnp.max(logits, axis=-1, keepdims=True)
    p = jnp.exp(logits - m)
    aff = p / jnp.sum(p, axis=-1, keepdims=True)
    aff_ref[0] = jnp.where(valid, aff, 0.0)


def _norm_mod(x, gain, shift, scale, n_lat, *, out_dtype=BF16, rows=None, router_w=None):
    b, s, d = x.shape
    rows = s if rows is None else rows
    tr = next(t for t in NORM_TILES if rows % t == 0 and s % t == 0)
    both = lambda bi, t: (bi, 0, 0, 0)
    in_specs = [
        pl.BlockSpec((1, tr, d), lambda bi, t: (bi, t, 0)),
        pl.BlockSpec((1, d), lambda bi, t: (0, 0)),
        pl.BlockSpec((1, 2, 1, d), both),
        pl.BlockSpec((1, 2, 1, d), both),
    ]
    args = [x, gain.reshape(1, d), shift, scale]
    out_specs = pl.BlockSpec((1, tr, d), lambda bi, t: (bi, t, 0))
    out_shape = jax.ShapeDtypeStruct((b, rows, d), out_dtype)
    kern = functools.partial(_norm_mod_kernel, n_lat=n_lat)
    if router_w is not None:
        wr = jnp.zeros((d, LANES), F32).at[:, :N_EXPERTS].set(router_w)
        in_specs.append(pl.BlockSpec((d, LANES), lambda bi, t: (0, 0)))
        args.append(wr)
        out_specs = [pl.BlockSpec((1, tr, d // 2), lambda bi, t: (bi, t, 0)),
                     pl.BlockSpec((1, tr, LANES), lambda bi, t: (bi, t, 0))]
        out_shape = [jax.ShapeDtypeStruct((b, rows, d // 2), I32),
                     jax.ShapeDtypeStruct((b, rows, LANES), F32)]
        kern = functools.partial(_norm_router_kernel, n_lat=n_lat)
    return pl.pallas_call(
        kern,
        name="norm_mod" if router_w is None else "norm_router",
        grid=(b, rows // tr),
        in_specs=in_specs,
        out_specs=out_specs,
        out_shape=out_shape,
        compiler_params=_params(("parallel", "parallel")),
    )(*args)


def _rope_partner(xs, quarter):
    if 2 * quarter == LANES:
        return pltpu.roll(xs, quarter, 1)
    back = pltpu.roll(xs, quarter, 1)
    fwd = pltpu.roll(xs, LANES - quarter, 1)
    lane = lax.broadcasted_iota(I32, xs.shape, 1)
    return jnp.where((lane % (2 * quarter)) < quarter, fwd, back)


def _row_chunks(tm):
    return [slice(r, r + MM_ROW_CHUNK) for r in range(0, tm, MM_ROW_CHUNK)]


def _mm_rope_kernel(a_ref, w_ref, cos_ref, sin_ref, o_ref, *, n_q, kscale, head_dim):
    sc = jnp.where(pl.program_id(2) >= n_q, kscale, 1.0).astype(F32)
    for rows in _row_chunks(a_ref.shape[1]):
        acc = _dot(a_ref[0, rows, :], w_ref[...])
        for s in range(acc.shape[1] // LANES):
            cols = slice(s * LANES, (s + 1) * LANES)
            off = (s * LANES) % head_dim
            xs = acc[:, cols]
            rot = xs * cos_ref[rows, off:off + LANES] + _rope_partner(xs, head_dim // 4) * sin_ref[rows, off:off + LANES]
            o_ref[0, rows, cols] = (rot * sc).astype(o_ref.dtype)


def _mm_plain_kernel(a_ref, w_ref, o_ref):
    for rows in _row_chunks(a_ref.shape[1]):
        o_ref[0, rows, :] = _dot(a_ref[0, rows, :], w_ref[...]).astype(o_ref.dtype)


def _mm_proj(a, w, col0, ncols, tn, rope=None):
    b, s, k = a.shape
    tm = ROW_TILE
    assert col0 % tn == 0 and ncols % tn == 0
    in_specs = [
        pl.BlockSpec((1, tm, k), lambda bi, i, j: (bi, i, 0)),
        pl.BlockSpec((k, tn), lambda bi, i, j: (0, col0 // tn + j)),
    ]
    args = [a, w]
    if rope is None:
        kern, name = _mm_plain_kernel, "mm_proj"
    else:
        cos_t, sin_t, n_q_cols, kscale, head_dim = rope
        assert tn % head_dim == 0 and n_q_cols % tn == 0
        kern = functools.partial(_mm_rope_kernel, n_q=n_q_cols // tn, kscale=kscale, head_dim=head_dim)
        name = "mm_proj_rope"
        in_specs += [pl.BlockSpec((tm, head_dim), lambda bi, i, j: (i, 0))] * 2
        args += [cos_t, sin_t]
    return pl.pallas_call(
        kern,
        name=name,
        grid=(b, s // tm, ncols // tn),
        in_specs=in_specs,
        out_specs=pl.BlockSpec((1, tm, tn), lambda bi, i, j: (bi, i, j)),
        out_shape=jax.ShapeDtypeStruct((b, s, ncols), BF16),
        compiler_params=_params(("parallel", "parallel", "arbitrary")),
    )(*args)


def _mm_res_kernel(a_ref, w_ref, x_ref, gl_ref, gc_ref, o_ref, *, n_lat):
    tm = a_ref.shape[1]
    for rows in _row_chunks(tm):
        acc = _dot(a_ref[0, rows, :], w_ref[...])
        row = pl.program_id(1) * tm + rows.start + lax.broadcasted_iota(I32, (acc.shape[0], 1), 0)
        gate = jnp.where(row < n_lat, gl_ref[0], gc_ref[0])
        o_ref[0, rows, :] = x_ref[0, rows, :] + gate * acc


def _mm_res(a, w, x, gate_lat, gate_ctx, n_lat):
    b, s, k = a.shape
    n = w.shape[1]
    tm = ROW_TILE
    tn = n if k * n * 2 <= RESIDENT_WEIGHT_BYTES else COL_TILE
    return pl.pallas_call(
        functools.partial(_mm_res_kernel, n_lat=n_lat),
        name="mm_residual",
        grid=(b, s // tm, n // tn),
        in_specs=[
            pl.BlockSpec((1, tm, k), lambda bi, i, j: (bi, i, 0)),
            pl.BlockSpec((k, tn), lambda bi, i, j: (0, j)),
            pl.BlockSpec((1, tm, tn), lambda bi, i, j: (bi, i, j)),
            pl.BlockSpec((1, 1, tn), lambda bi, i, j: (bi, 0, j)),
            pl.BlockSpec((1, 1, tn), lambda bi, i, j: (bi, 0, j)),
        ],
        out_specs=pl.BlockSpec((1, tm, tn), lambda bi, i, j: (bi, i, j)),
        out_shape=jax.ShapeDtypeStruct((b, s, n), F32),
        compiler_params=_params(("parallel", "parallel", "arbitrary")),
    )(a, w, x, gate_lat, gate_ctx)


def _mm_groups_kernel(a_ref, w_ref, o_ref):
    o_ref[0] = _dot(a_ref[0], w_ref[...]).astype(o_ref.dtype)


def _mm_groups(a, w, groups):
    b, s, d = a.shape
    cg = d // groups
    tm = ROW_TILE
    return pl.pallas_call(
        _mm_groups_kernel,
        name="mm_channel_dft",
        grid=(b, s // tm, 2 * groups),
        in_specs=[
            pl.BlockSpec((1, tm, cg), lambda bi, i, j: (bi, i, j % groups)),
            pl.BlockSpec((cg, cg), lambda bi, i, j: (0, j // groups)),
        ],
        out_specs=pl.BlockSpec((1, tm, cg), lambda bi, i, j: (bi, i, j)),
        out_shape=jax.ShapeDtypeStruct((b, s, 2 * d), F32),
        compiler_params=_params(("parallel", "parallel", "arbitrary")),
    )(a, w)


def _ret_scan_kernel(*refs, heads, dk, dv, reverse, add_in):
    if add_in:
        dec_ref, q_ref, k_ref, v_ref, g_ref, yin_ref, o_ref, s_ref, qd_ref, kd_ref, in_ref, cd_ref = refs
    else:
        dec_ref, q_ref, k_ref, v_ref, g_ref, o_ref, s_ref, qd_ref, kd_ref, in_ref, cd_ref = refs
        yin_ref = None
    c = q_ref.shape[1]
    j = pl.program_id(1)

    @pl.when(j == 0)
    def _():
        s_ref[...] = jnp.zeros_like(s_ref)
        m_col = lax.broadcasted_iota(I32, (c, LANES), 0).astype(F32)
        m_row = lax.broadcasted_iota(I32, (c, c), 0).astype(F32)
        n_row = lax.broadcasted_iota(I32, (c, c), 1).astype(F32)
        for h in range(heads):
            lg = -jnp.exp(dec_ref[:, h:h + 1])
            if reverse:
                q_pow, k_pow, diff = c - m_col, m_col, n_row - m_row
            else:
                q_pow, k_pow, diff = m_col + 1.0, c - 1.0 - m_col, m_row - n_row
            qd_ref[h] = jnp.exp(lg * q_pow)
            kd_ref[h] = jnp.exp(lg * k_pow)
            in_ref[h] = jnp.where(diff >= 0, jnp.exp(lg * jnp.maximum(diff, 0.0)), 0.0)
            cd_ref[h] = jnp.exp(jnp.broadcast_to(lg, (8, LANES)) * float(c))

    for h in range(heads):
        q = q_ref[0, :, h * dk:(h + 1) * dk]
        k = k_ref[0, :, h * dk:(h + 1) * dk]
        v = v_ref[0, :, h * dv:(h + 1) * dv]
        qdec = jnp.concatenate([qd_ref[h]] * (dk // LANES), axis=1)
        kdec = jnp.concatenate([kd_ref[h]] * (dk // LANES), axis=1)
        state = s_ref[h]
        cross = _dot((q.astype(F32) * qdec).astype(BF16), state.astype(BF16))
        scores = _dot_nt(q, k) * in_ref[h]
        o = cross + _dot(scores.astype(BF16), v)
        k_t = (k.astype(F32) * kdec).T.astype(BF16)
        s_ref[h] = state * cd_ref[h][0:1, 0:1] + _dot(k_t, v)
        mu = jnp.mean(o, axis=-1, keepdims=True)
        cen = o - mu
        var = jnp.mean(cen * cen, axis=-1, keepdims=True)
        g = g_ref[0, :, h * dv:(h + 1) * dv].astype(F32)
        y = cen * lax.rsqrt(var + NORM_EPS) * _silu(g)
        if add_in:
            y = y + yin_ref[0, :, h * dv:(h + 1) * dv].astype(F32)
        o_ref[0, :, h * dv:(h + 1) * dv] = y.astype(o_ref.dtype)


def _ret_scan(qk, vg, decay_row, y_in, *, n_lat, n_ctx, heads, dk, dv, reverse):
    b, s, _ = qk.shape
    c = RET_CHUNK
    lat_chunks, ctx_chunks = n_lat // c, n_ctx // c
    steps = lat_chunks + ctx_chunks
    hk, hv = heads * dk, heads * dv
    if reverse:
        chunk = lambda j: steps - 1 - j
    else:
        chunk = lambda j: jnp.where(j < ctx_chunks, lat_chunks + j, j - ctx_chunks)
    gate_blk = 2 if reverse else 1
    in_specs = [
        pl.BlockSpec((1, heads), lambda bi, j: (0, 0)),
        pl.BlockSpec((1, c, hk), lambda bi, j: (bi, chunk(j), 0)),
        pl.BlockSpec((1, c, hk), lambda bi, j: (bi, chunk(j), 1)),
        pl.BlockSpec((1, c, hv), lambda bi, j: (bi, chunk(j), 0)),
        pl.BlockSpec((1, c, hv), lambda bi, j: (bi, chunk(j), gate_blk)),
    ]
    args = [decay_row, qk, qk, vg, vg]
    if y_in is not None:
        in_specs.append(pl.BlockSpec((1, c, hv), lambda bi, j: (bi, chunk(j), 0)))
        args.append(y_in)
    kern = functools.partial(_ret_scan_kernel, heads=heads, dk=dk, dv=dv, reverse=reverse,
                             add_in=y_in is not None)
    return pl.pallas_call(
        kern,
        name="ret_scan_bwd" if reverse else "ret_scan_fwd",
        grid=(b, steps),
        in_specs=in_specs,
        out_specs=pl.BlockSpec((1, c, hv), lambda bi, j: (bi, chunk(j), 0)),
        out_shape=jax.ShapeDtypeStruct((b, s, hv), BF16),
        scratch_shapes=[
            pltpu.VMEM((heads, dk, dv), F32),
            pltpu.VMEM((heads, c, LANES), F32),
            pltpu.VMEM((heads, c, LANES), F32),
            pltpu.VMEM((heads, c, c), F32),
            pltpu.VMEM((heads, 8, LANES), F32),
        ],
        compiler_params=_params(("parallel", "arbitrary")),
    )(*args)


def _win_attn_kernel(sink_ref, q_ref, kc_ref, vc_ref, kp_ref, kq_ref, kn_ref, vp_ref, vq_ref, vn_ref,
                     o_ref, *, lat_tiles, heads, kv_heads, hd):
    qt = pl.program_id(1)
    blk = q_ref.shape[1]
    n_ctx = kc_ref.shape[1]
    grp = heads // kv_heads
    scale = hd ** -0.5
    rows = grp * blk
    qi = lax.broadcasted_iota(I32, (rows, blk), 0) % blk
    kj = lax.broadcasted_iota(I32, (rows, blk), 1)
    tq = qt + jnp.zeros((rows, blk), I32)
    ok_cur = tq < lat_tiles
    ok_prev = (kj >= qi) & ok_cur & (tq >= 1)
    ok_next = (kj <= qi) & (tq + 1 < lat_tiles)
    bias = jnp.concatenate(
        [jnp.zeros((rows, n_ctx), F32)]
        + [jnp.where(ok, 0.0, NEG_INF).astype(F32) for ok in (ok_prev, ok_cur, ok_next)], axis=1)
    head_row = lax.broadcasted_iota(I32, (rows, 1), 0) // blk
    for kv in range(kv_heads):
        cs = slice(kv * hd, (kv + 1) * hd)
        keys = jnp.concatenate([kc_ref[0, :, cs], kp_ref[0, :, cs], kq_ref[0, :, cs], kn_ref[0, :, cs]], axis=0)
        vals = jnp.concatenate([vc_ref[0, :, cs], vp_ref[0, :, cs], vq_ref[0, :, cs], vn_ref[0, :, cs]], axis=0)
        q = jnp.concatenate([q_ref[0, :, (kv * grp + g) * hd:(kv * grp + g + 1) * hd] for g in range(grp)], axis=0)
        sink = jnp.zeros((rows, 1), F32)
        for g in range(grp):
            h = kv * grp + g
            sink = jnp.where(head_row == g, sink_ref[:, h:h + 1], sink)
        s = _dot_nt(q, keys) * scale + bias
        m = jnp.maximum(jnp.max(s, axis=-1, keepdims=True), sink)
        p = jnp.exp(s - m)
        den = jnp.sum(p, axis=-1, keepdims=True) + jnp.exp(sink - m)
        o = _dot(p.astype(BF16), vals) / den
        for g in range(grp):
            h = kv * grp + g
            o_ref[0, :, h * hd:(h + 1) * hd] = o[g * blk:(g + 1) * blk].astype(o_ref.dtype)


def _win_attn(qk, v, sink, *, n_lat, n_ctx, heads, kv_heads, hd):
    b, s, _ = qk.shape
    blk = WIN_BLOCK
    lat_tiles = n_lat // blk
    tiles = s // blk
    kvw = kv_heads * hd
    k_col = (heads * hd) // kvw
    v_col = 0
    ctx_blk = n_lat // n_ctx
    prev = lambda t: jnp.maximum(t - 1, 0)
    nxt = lambda t: jnp.minimum(t + 1, tiles - 1)
    sink_row = jnp.zeros((1, LANES), F32).at[0, :heads].set(sink.astype(F32))
    kern = functools.partial(_win_attn_kernel, lat_tiles=lat_tiles, heads=heads, kv_heads=kv_heads, hd=hd)
    return pl.pallas_call(
        kern,
        name="win_attn",
        grid=(b, tiles),
        in_specs=[
            pl.BlockSpec((1, LANES), lambda bi, t: (0, 0)),
            pl.BlockSpec((1, blk, heads * hd), lambda bi, t: (bi, t, 0)),
            pl.BlockSpec((1, n_ctx, kvw), lambda bi, t: (bi, ctx_blk, k_col)),
            pl.BlockSpec((1, n_ctx, kvw), lambda bi, t: (bi, ctx_blk, v_col)),
            pl.BlockSpec((1, blk, kvw), lambda bi, t: (bi, prev(t), k_col)),
            pl.BlockSpec((1, blk, kvw), lambda bi, t: (bi, t, k_col)),
            pl.BlockSpec((1, blk, kvw), lambda bi, t: (bi, nxt(t), k_col)),
            pl.BlockSpec((1, blk, kvw), lambda bi, t: (bi, prev(t), v_col)),
            pl.BlockSpec((1, blk, kvw), lambda bi, t: (bi, t, v_col)),
            pl.BlockSpec((1, blk, kvw), lambda bi, t: (bi, nxt(t), v_col)),
        ],
        out_specs=pl.BlockSpec((1, blk, heads * hd), lambda bi, t: (bi, t, 0)),
        out_shape=jax.ShapeDtypeStruct((b, s, heads * hd), BF16),
        compiler_params=_params(("parallel", "parallel")),
    )(sink_row, qk, qk, v, qk, qk, qk, v, v, v)


SUBLANES = 8
PACKED_ROWS = 16


def _dft_stage_a_kernel(m_ref, u_ref, v_ref, re_ref, im_ref):
    n1, sub, tc = u_ref.shape[1:]
    rows = n1 * sub
    stacked = jnp.concatenate([u_ref[0].reshape(rows, tc), v_ref[0].reshape(rows, tc)], axis=0)
    out = _dot(m_ref[...], stacked.astype(BF16))
    re_ref[0] = out[:rows].reshape(n1, sub, tc)
    im_ref[0] = out[rows:].reshape(n1, sub, tc)


def _dft_stage_b_kernel(m_ref, re_ref, im_ref, twc_ref, tws_ref, o_ref):
    n2, sub, tc = o_ref.shape[1:]
    reps = tc // LANES
    ar, ai = re_ref[0], im_ref[0]
    twc = jnp.concatenate([twc_ref[...]] * reps, axis=1)
    tws = jnp.concatenate([tws_ref[...]] * reps, axis=1)
    stacked = jnp.concatenate([ar * twc + ai * tws, ai * twc - ar * tws], axis=0).astype(BF16)
    out = _dot(m_ref[...], stacked)
    o_ref[0] = out.reshape(n2, sub, tc).astype(o_ref.dtype)


def _dft_ctx_kernel(m_ref, x_ref, o_ref, *, d):
    x = x_ref[0]
    stacked = jnp.concatenate([x[:, :d], x[:, d:]], axis=0).astype(BF16)
    o_ref[0] = _dot(m_ref[...], stacked).astype(o_ref.dtype)


def _cos_sin(n):
    ang = 2.0 * np.pi * np.outer(np.arange(n), np.arange(n)) / float(n)
    return np.cos(ang), np.sin(ang)


def _fourier_positions(uv, n_lat, n_ctx, d):
    b, s, _ = uv.shape
    n2 = FFT_INNER
    n1 = n_lat // n2
    tc = min(512, d)
    ca, sa = _cos_sin(n1)
    eye = np.eye(SUBLANES)
    ka, ks = np.kron(ca, eye) / math.sqrt(n1), np.kron(sa, eye) / math.sqrt(n1)
    mat_a = jnp.asarray(np.block([[ka, -ks], [-ks, -ka]]), BF16)
    uv4 = uv.reshape(b, s // n2, n2, 2 * d)
    blk_a = (1, n1, SUBLANES, tc)
    a_re, a_im = pl.pallas_call(
        _dft_stage_a_kernel,
        name="dft_stage_a",
        grid=(b, n2 // SUBLANES, d // tc),
        in_specs=[
            pl.BlockSpec(mat_a.shape, lambda bi, cg, jc: (0, 0)),
            pl.BlockSpec(blk_a, lambda bi, cg, jc: (bi, 0, cg, jc)),
            pl.BlockSpec(blk_a, lambda bi, cg, jc: (bi, 0, cg, d // tc + jc)),
        ],
        out_specs=[pl.BlockSpec(blk_a, lambda bi, cg, jc: (bi, 0, cg, jc))] * 2,
        out_shape=[jax.ShapeDtypeStruct((b, n1, n2, d), F32)] * 2,
        compiler_params=_params(("parallel", "parallel", "parallel")),
    )(mat_a, uv4, uv4)
    cb, sb = _cos_sin(n2)
    eye = np.eye(PACKED_ROWS)
    kron_b = lambda m: np.einsum("kc,ab->kabc", m, eye).reshape(n2 * PACKED_ROWS, PACKED_ROWS * n2)
    mat_b = jnp.asarray(np.concatenate([kron_b(cb), kron_b(sb)], axis=1) / math.sqrt(n2), BF16)
    phi = 2.0 * np.pi * np.outer(np.arange(n1), np.arange(n2)) / float(n_lat)
    twc = jnp.asarray(np.repeat(np.cos(phi).reshape(-1, 1), LANES, axis=1), F32)
    tws = jnp.asarray(np.repeat(np.sin(phi).reshape(-1, 1), LANES, axis=1), F32)
    rows = PACKED_ROWS * n2
    y = pl.pallas_call(
        _dft_stage_b_kernel,
        name="dft_stage_b",
        grid=(b, n1 // PACKED_ROWS, d // tc),
        in_specs=[
            pl.BlockSpec(mat_b.shape, lambda bi, kb, jc: (0, 0)),
            pl.BlockSpec((1, rows, tc), lambda bi, kb, jc: (bi, kb, jc)),
            pl.BlockSpec((1, rows, tc), lambda bi, kb, jc: (bi, kb, jc)),
            pl.BlockSpec((rows, LANES), lambda bi, kb, jc: (kb, 0)),
            pl.BlockSpec((rows, LANES), lambda bi, kb, jc: (kb, 0)),
        ],
        out_specs=pl.BlockSpec((1, n2, PACKED_ROWS, tc), lambda bi, kb, jc: (bi, 0, kb, jc)),
        out_shape=jax.ShapeDtypeStruct((b, n2, n1, d), BF16),
        compiler_params=_params(("parallel", "parallel", "parallel")),
    )(mat_b, a_re.reshape(b, n_lat, d), a_im.reshape(b, n_lat, d), twc, tws)
    cc, sc = _cos_sin(n_ctx)
    mat_c = jnp.asarray(np.concatenate([cc, -sc], axis=1) / math.sqrt(n_ctx), BF16)
    y_ctx = pl.pallas_call(
        functools.partial(_dft_ctx_kernel, d=d),
        name="dft_ctx",
        grid=(b,),
        in_specs=[
            pl.BlockSpec(mat_c.shape, lambda bi: (0, 0)),
            pl.BlockSpec((1, n_ctx, 2 * d), lambda bi: (bi, n_lat // n_ctx, 0)),
        ],
        out_specs=pl.BlockSpec((1, n_ctx, d), lambda bi: (bi, 0, 0)),
        out_shape=jax.ShapeDtypeStruct((b, n_ctx, d), BF16),
        compiler_params=_params(("parallel",)),
    )(mat_c, uv)
    return jnp.concatenate([y.reshape(b, n_lat, d), y_ctx], axis=1)


LANE_SHIFT = 7
GATHER_UNROLL = 8
TILES_PER_STEP = 4
FFN_ROW_CHUNK = 384
COMBINE_REGION = 64


def _route_kernel(aff_ref, tri_ref, posc_ref, gate_ref, tst_ref,
                  thr_ref, need_ref, ctie_ref, cpos_ref, *, cap):
    step = pl.program_id(1)
    tile = tri_ref.shape[0]
    tps = posc_ref.shape[1] // tile

    @pl.when(step == 0)
    def _():
        def body(it, thr):
            bits = lax.bitcast_convert_type(aff_ref[0], I32)
            cand = thr | jnp.left_shift(jnp.int32(1), 30 - it)
            cnt = jnp.sum(jnp.where(bits >= cand, 1.0, 0.0), axis=0, keepdims=True)
            return jnp.where(cnt >= cap, cand, thr)

        thr = lax.fori_loop(0, 31, body, jnp.zeros((1, LANES), I32))
        bits = lax.bitcast_convert_type(aff_ref[0], I32)
        above = jnp.sum(jnp.where(bits > thr, 1.0, 0.0), axis=0, keepdims=True)
        thr_ref[...] = thr
        need_ref[...] = float(cap) - above
        ctie_ref[...] = jnp.zeros_like(ctie_ref)
        cpos_ref[...] = jnp.zeros_like(cpos_ref)

    thr = thr_ref[...]
    for u in range(tps):
        rows = slice(u * tile, (u + 1) * tile)
        a = aff_ref[0, pl.ds(pl.multiple_of((step * tps + u) * tile, tile), tile), :]
        bits = lax.bitcast_convert_type(a, I32)
        gt = bits > thr
        eq = bits == thr
        eqf = jnp.where(eq, 1.0, 0.0)
        tie_rank = _dot(tri_ref[...], eqf.astype(BF16)) + ctie_ref[...]
        sel = gt | (eq & (tie_rank < need_ref[...]))
        self_ = jnp.where(sel, 1.0, 0.0)
        start = cpos_ref[...]
        pos = jnp.where(sel, _dot(tri_ref[...], self_.astype(BF16)) + start, -1.0)
        posc_ref[0, rows, :] = pos.astype(I32)
        gate_ref[0, rows, :] = jnp.where(sel, a, 0.0)
        tst_ref[0, u] = jnp.broadcast_to(start, (8, LANES)).astype(I32)
        ctie_ref[...] = ctie_ref[...] + jnp.sum(eqf, axis=0, keepdims=True)
        cpos_ref[...] = start + jnp.sum(self_, axis=0, keepdims=True)


def _route(aff, *, row_off, n, cap):
    b = aff.shape[0]
    tile = ROUTE_TILE
    nt = n // tile
    tps = min(TILES_PER_STEP, nt)
    tri = jnp.asarray(np.tril(np.ones((tile, tile)), -1), BF16)
    return pl.pallas_call(
        functools.partial(_route_kernel, cap=cap),
        name="moe_route",
        grid=(b, nt // tps),
        in_specs=[
            pl.BlockSpec((1, n, LANES), lambda bi, t: (bi, row_off // n, 0)),
            pl.BlockSpec((tile, tile), lambda bi, t: (0, 0)),
        ],
        out_specs=[
            pl.BlockSpec((1, tps * tile, LANES), lambda bi, t: (bi, t, 0)),
            pl.BlockSpec((1, tps * tile, LANES), lambda bi, t: (bi, t, 0)),
            pl.BlockSpec((1, tps, 8, LANES), lambda bi, t: (bi, t, 0, 0)),
        ],
        out_shape=[
            jax.ShapeDtypeStruct((b, n, LANES), I32),
            jax.ShapeDtypeStruct((b, n, LANES), F32),
            jax.ShapeDtypeStruct((b, nt, 8, LANES), I32),
        ],
        scratch_shapes=[pltpu.VMEM((1, LANES), I32)] + [pltpu.VMEM((1, LANES), F32)] * 3,
        compiler_params=_params(("parallel", "arbitrary")),
    )(aff, tri)


def _slot_index_kernel(ts_ref, posc_ref, idx_ref, *, nt, row_off, tile):
    bi, step = pl.program_id(0), pl.program_id(1)

    @pl.when(step == 0)
    def _():
        idx_ref[...] = jnp.zeros_like(idx_ref)

    tps = posc_ref.shape[1] // tile
    slot_rows = idx_ref.shape[2]
    lane = lax.broadcasted_iota(I32, (tile, 2 * LANES), 1)
    for u in range(tps):
        t = step * tps + u
        tok = (row_off + t * tile + lax.broadcasted_iota(I32, (tile, 1), 0)).astype(F32)
        for e in range(N_EXPERTS):
            h0 = jnp.minimum(ts_ref[(bi * nt + t) * N_EXPERTS + e] >> LANE_SHIFT, slot_rows - 2)
            hit = (posc_ref[0, u * tile:(u + 1) * tile, e:e + 1] - h0 * LANES) == lane
            vals = jnp.sum(jnp.where(hit, tok, 0.0), axis=0, keepdims=True)
            two_rows = jnp.concatenate([vals[:, :LANES], vals[:, LANES:]], axis=0).astype(I32)
            idx_ref[0, e, pl.ds(h0, 2), :] = idx_ref[0, e, pl.ds(h0, 2), :] + two_rows


def _slot_index(tstart, posc, *, row_off, n, cap):
    b = posc.shape[0]
    tile = ROUTE_TILE
    nt = n // tile
    slot_rows = max(cap, 2 * LANES) // LANES
    tps = min(TILES_PER_STEP, nt)
    grid_spec = pltpu.PrefetchScalarGridSpec(
        num_scalar_prefetch=1,
        grid=(b, nt // tps),
        in_specs=[pl.BlockSpec((1, tps * tile, LANES), lambda bi, t, ts: (bi, t, 0))],
        out_specs=pl.BlockSpec((1, N_EXPERTS, slot_rows, LANES), lambda bi, t, ts: (bi, 0, 0, 0)),
    )
    idx = pl.pallas_call(
        functools.partial(_slot_index_kernel, nt=nt, row_off=row_off, tile=tile),
        name="moe_slot_index",
        grid_spec=grid_spec,
        out_shape=jax.ShapeDtypeStruct((b, N_EXPERTS, slot_rows, LANES), I32),
        compiler_params=_params(("parallel", "arbitrary")),
    )(tstart, posc)
    return idx.reshape(b, N_EXPERTS, slot_rows * LANES)


def _gather_kernel(idx_ref, h_ref, o_ref, buf_ref, sem):
    bi = pl.program_id(1)
    n_rows = buf_ref.shape[0]

    def row_copy(src_row, dst_row, rows):
        return pltpu.make_async_copy(h_ref.at[bi, pl.ds(src_row, rows), :], buf_ref.at[pl.ds(dst_row, rows), :], sem)

    def issue(g, carry):
        for k in range(GATHER_UNROLL):
            s = g * GATHER_UNROLL + k
            row_copy(idx_ref[0, 0, s], s, 1).start(priority=k % 2)
        return carry

    lax.fori_loop(0, n_rows // GATHER_UNROLL, issue, 0)
    row_copy(0, 0, n_rows).wait()
    o_ref[0, 0] = _unpack_bf16_pairs(buf_ref[...])


def _gather(idx, h):
    b, _, half = h.shape
    d = 2 * half
    slots = idx.shape[2]
    assert slots % GATHER_UNROLL == 0
    return pl.pallas_call(
        _gather_kernel,
        name="moe_gather",
        grid=(N_EXPERTS, b),
        in_specs=[
            pl.BlockSpec((1, 1, slots), lambda ei, bi: (bi * N_EXPERTS + ei, 0, 0), memory_space=pltpu.SMEM),
            pl.BlockSpec(memory_space=pl.ANY),
        ],
        out_specs=pl.BlockSpec((1, 1, slots, d), lambda ei, bi: (ei, bi, 0, 0)),
        out_shape=jax.ShapeDtypeStruct((N_EXPERTS, b, slots, d), BF16),
        scratch_shapes=[pltpu.VMEM((slots, half), I32), pltpu.SemaphoreType.DMA(())],
        compiler_params=_params(("arbitrary", "arbitrary")),
    )(idx.reshape(b * N_EXPERTS, 1, slots), h)


def _ffn_kernel(x_ref, wg_ref, wu_ref, wd_ref, ye_ref, hm_ref, *, n_up):
    st = pl.program_id(1)
    bsz, rows, d = x_ref.shape[1:]
    tf = wg_ref.shape[3]
    chunk = FFN_ROW_CHUNK if rows % FFN_ROW_CHUNK == 0 else rows
    spans = [(bi, r0) for bi in range(bsz) for r0 in range(0, rows, chunk)]

    @pl.when(st < n_up)
    def _():
        wg = wg_ref[0, 0].astype(BF16)
        wu = wu_ref[0, 0].astype(BF16)
        for bi, r0 in spans:
            x = x_ref[0, bi, r0:r0 + chunk, :]
            hm_ref[st, bi * rows + r0:bi * rows + r0 + chunk, :] = (_silu(_dot(x, wg)) * _dot(x, wu)).astype(BF16)

    @pl.when(st >= n_up)
    def _():
        wd = [wd_ref[0, 0, c * tf:(c + 1) * tf, :].astype(BF16) for c in range(n_up)]
        for bi, r0 in spans:
            m0 = bi * rows + r0
            y = _dot(hm_ref[0, m0:m0 + chunk, :], wd[0])
            for c in range(1, n_up):
                y = y + _dot(hm_ref[c, m0:m0 + chunk, :], wd[c])
            ye_ref[0, bi, r0:r0 + chunk, :] = y.astype(ye_ref.dtype)


def _ffn(xe, w_gate, w_up, w_down, layer):
    e, b, rows, d = xe.shape
    f = w_gate.shape[3]
    tf = min(512, f)
    tdc = min(512, d)
    n_up, n_down = f // tf, d // tdc
    up = lambda st: jnp.minimum(st, n_up - 1)
    down = lambda st: jnp.maximum(st - n_up, 0)
    return pl.pallas_call(
        functools.partial(_ffn_kernel, n_up=n_up),
        name="moe_ffn",
        grid=(e, n_up + n_down),
        in_specs=[
            pl.BlockSpec((1, b, rows, d), lambda ei, st: (ei, 0, 0, 0), pipeline_mode=pl.Buffered(1)),
            pl.BlockSpec((1, 1, d, tf), lambda ei, st: (layer, ei, 0, up(st))),
            pl.BlockSpec((1, 1, d, tf), lambda ei, st: (layer, ei, 0, up(st))),
            pl.BlockSpec((1, 1, f, tdc), lambda ei, st: (layer, ei, 0, down(st))),
        ],
        out_specs=pl.BlockSpec((1, b, rows, tdc), lambda ei, st: (ei, 0, 0, down(st))),
        out_shape=jax.ShapeDtypeStruct((e, b, rows, d), BF16),
        scratch_shapes=[pltpu.VMEM((n_up, b * rows, tf), BF16)],
        compiler_params=_params(("parallel", "arbitrary")),
    )(xe, w_gate, w_up, w_down)


def _window_start(ts_ref, idx, cap_rows, win):
    a0 = jnp.minimum(ts_ref[idx] & (-SLOT_ALIGN), cap_rows - win)
    return pl.multiple_of(a0, SLOT_ALIGN)


def _combine_kernel(ts_ref, ye_ref, posc_ref, gate_ref, x_ref, g_ref, o_ref, *, nt, win, cap, n_starts, tile):
    bi, step = pl.program_id(0), pl.program_id(2)
    cap_rows = ye_ref.shape[2]
    tps = posc_ref.shape[1] // tile
    reg = COMBINE_REGION
    assert 2 * reg == LANES and reg <= cap_rows

    for u in range(tps):
        t = step * tps + u
        rows = slice(u * tile, (u + 1) * tile)
        base = (bi * nt + t) * N_EXPERTS
        starts, fits = [], None
        for e in range(N_EXPERTS):
            a0 = _window_start(ts_ref, base + e, cap_rows, reg)
            nxt = ts_ref[jnp.minimum(base + N_EXPERTS + e, n_starts - 1)]
            end = jnp.where(t + 1 < nt, nxt, cap)
            ok = end - a0 <= reg
            fits = ok if fits is None else jnp.logical_and(fits, ok)
            starts.append(a0)

        lane = lax.broadcasted_iota(I32, (tile, LANES), 1)
        upper = lane >= reg
        weights, slabs = [], []
        for p in range(N_EXPERTS // 2):
            e0, e1 = 2 * p, 2 * p + 1
            slot = jnp.where(upper, starts[e1] - reg, starts[e0]) + lane
            pcol = jnp.where(upper, posc_ref[0, rows, e1:e1 + 1], posc_ref[0, rows, e0:e0 + 1])
            gcol = jnp.where(upper, gate_ref[0, rows, e1:e1 + 1], gate_ref[0, rows, e0:e0 + 1])
            weights.append(jnp.where(pcol == slot, gcol, 0.0).astype(BF16))
            slabs += [ye_ref[e0, 0, pl.ds(starts[e0], reg), :], ye_ref[e1, 0, pl.ds(starts[e1], reg), :]]
        acc = _dot(jnp.concatenate(weights, axis=1), jnp.concatenate(slabs, axis=0))
        o_ref[0, rows, :] = x_ref[0, rows, :] + g_ref[0, 0] * acc

        @pl.when(jnp.logical_not(fits))
        def _():
            acc = jnp.zeros((tile, o_ref.shape[2]), F32)
            lane = lax.broadcasted_iota(I32, (tile, win), 1)
            for e in range(N_EXPERTS):
                a0 = _window_start(ts_ref, base + e, cap_rows, win)
                pcol = posc_ref[0, rows, e:e + 1]
                gcol = gate_ref[0, rows, e:e + 1]
                w = jnp.where(pcol == a0 + lane, gcol, 0.0).astype(BF16)
                acc = acc + _dot(w, ye_ref[e, 0, pl.ds(a0, win), :])
            o_ref[0, rows, :] = x_ref[0, rows, :] + g_ref[0, 0] * acc


def _combine(tstart, ye, posc, gate, x, g2, *, row_off, n, region, row0, cap_rows):
    b, s, d = x.shape
    tile = ROUTE_TILE
    nt = n // tile
    dc = 512 if d % 512 == 0 else d
    win = min(2 * tile, cap_rows)
    tps = min(TILES_PER_STEP, nt)
    rows = tps * tile
    assert row_off % rows == 0
    off = row_off // rows
    grid_spec = pltpu.PrefetchScalarGridSpec(
        num_scalar_prefetch=1,
        grid=(b, d // dc, nt // tps),
        in_specs=[
            pl.BlockSpec((N_EXPERTS, 1, cap_rows, dc), lambda bi, c, t, ts: (0, bi, row0 // cap_rows, c)),
            pl.BlockSpec((1, rows, LANES), lambda bi, c, t, ts: (bi, t, 0)),
            pl.BlockSpec((1, rows, LANES), lambda bi, c, t, ts: (bi, t, 0)),
            pl.BlockSpec((1, rows, dc), lambda bi, c, t, ts: (bi, off + t, c)),
            pl.BlockSpec((1, 1, 1, dc), lambda bi, c, t, ts: (bi, region, 0, c)),
        ],
        out_specs=pl.BlockSpec((1, rows, dc), lambda bi, c, t, ts: (bi, off + t, c)),
    )
    return pl.pallas_call(
        functools.partial(_combine_kernel, nt=nt, win=win, cap=CAPACITY_FACTOR * n // N_EXPERTS,
                          n_starts=b * nt * N_EXPERTS, tile=tile),
        name="moe_combine",
        grid_spec=grid_spec,
        out_shape=jax.ShapeDtypeStruct((b, s, d), F32),
        input_output_aliases={4: 0},
        compiler_params=_params(("parallel", "parallel", "arbitrary")),
    )(tstart, ye, posc, gate, x, g2)


def _moe(x, h, aff, g2, w_gate, w_up, w_down, layer, token_sets):
    routed, row0 = [], 0
    for row_off, n, region in token_sets:
        cap = CAPACITY_FACTOR * n // N_EXPERTS
        cap_rows = -(-cap // LANES) * LANES
        posc, gate, tst = _route(aff, row_off=row_off, n=n, cap=cap)
        tstart = tst[:, :, 0, :N_EXPERTS].reshape(-1)
        idx = _slot_index(tstart, posc, row_off=row_off, n=n, cap=cap)
        routed.append((row_off, n, region, row0, cap_rows, tstart, posc, gate, idx))
        row0 += cap_rows
    idx_all = jnp.concatenate([r[8][:, :, :r[4]] for r in routed], axis=2)
    ye = _ffn(_gather(idx_all, h), w_gate, w_up, w_down, layer)
    for row_off, n, region, r0, cap_rows, tstart, posc, gate, _ in routed:
        x = _combine(tstart, ye, posc, gate, x, g2, row_off=row_off, n=n, region=region, row0=r0, cap_rows=cap_rows)
    return x


def _rope_tables(n_lat, n_ctx, head_dim):
    quarter = head_dim // 4
    inv = ROPE_BASE ** (-jnp.arange(quarter, dtype=F32) / quarter)
    pos = jnp.arange(n_lat)
    row = (pos // GRID_W).astype(F32)[:, None] * inv[None, :]
    col = (pos % GRID_W).astype(F32)[:, None] * inv[None, :]
    cos = jnp.concatenate([jnp.cos(row), jnp.cos(row), jnp.cos(col), jnp.cos(col)], axis=1)
    sin = jnp.concatenate([-jnp.sin(row), jnp.sin(row), -jnp.sin(col), jnp.sin(col)], axis=1)
    cos = jnp.concatenate([cos, jnp.ones((n_ctx, head_dim), F32)], axis=0)
    sin = jnp.concatenate([sin, jnp.zeros((n_ctx, head_dim), F32)], axis=0)
    return cos, sin


def kernel(x, c, ctx, c_ctx, w_mod, b_mod, norm_gain, final_gain, ret_w_in, ret_w_out, ret_decay,
           win_w_qkv, win_w_o, win_sink, fno_w_o, router_w, exp_w_gate, exp_w_up, exp_w_down):
    b, n_lat, d = x.shape
    n_ctx = ctx.shape[1]
    depth = w_mod.shape[0]
    s = n_lat + n_ctx
    assert s % ROW_TILE == 0 and n_lat % n_ctx == 0
    assert n_lat % (FFT_INNER * 4) == 0 and b + 1 <= 8 and n_ctx % ROUTE_TILE == 0

    xs = jnp.concatenate([x, ctx], axis=1)
    cvec = jnp.zeros((8, d), F32).at[:b].set(c).at[b].set(c_ctx)
    mod = _modulation(cvec, w_mod, b_mod)

    def mod_pair(i, k):
        lat = mod[i, :b, k * d:(k + 1) * d]
        cx = jnp.broadcast_to(mod[i, b, k * d:(k + 1) * d], (b, d))
        return jnp.stack([lat, cx], axis=1).reshape(b, 2, 1, d)

    ret_dk = d // RET_HEADS
    ret_dv = 2 * ret_dk
    win_hd = d // WIN_HEADS
    ret_cos, ret_sin = _rope_tables(n_lat, n_ctx, ret_dk)
    win_cos, win_sin = _rope_tables(n_lat, n_ctx, win_hd)

    for i in range(depth):
        kind, j = i % N_MIXERS, i // N_MIXERS
        last = i == depth - 1
        sh1, sc1, g1, sh2, sc2, g2 = [mod_pair(i, k) for k in range(6)]
        h = _norm_mod(xs, norm_gain[i, 0], sh1, sc1, n_lat)
        if kind == 0:
            hk, hv = RET_HEADS * ret_dk, RET_HEADS * ret_dv
            w_in = ret_w_in[j].astype(BF16)
            qk = _mm_proj(h, w_in, 0, 2 * hk, PROJ_COL_TILE,
                          rope=(ret_cos, ret_sin, hk, ret_dk ** -0.5, ret_dk))
            vg = _mm_proj(h, w_in, 2 * hk, 3 * hv, PROJ_COL_TILE)
            scan = functools.partial(_ret_scan, n_lat=n_lat, n_ctx=n_ctx, heads=RET_HEADS, dk=ret_dk, dv=ret_dv)
            y_b = scan(qk, vg, ret_decay[j, 1:2], None, reverse=True)
            y = scan(qk, vg, ret_decay[j, 0:1], y_b, reverse=False)
            w_out = ret_w_out[j]
        elif kind == 1:
            nq, nkv = WIN_HEADS * win_hd, WIN_KV_HEADS * win_hd
            w_qkv = win_w_qkv[j].astype(BF16)
            qk = _mm_proj(h, w_qkv, 0, nq + nkv, COL_TILE, rope=(win_cos, win_sin, nq, 1.0, win_hd))
            v = _mm_proj(h, w_qkv, nq + nkv, nkv, COL_TILE)
            y = _win_attn(qk, v, win_sink[j], n_lat=n_lat, n_ctx=n_ctx, heads=WIN_HEADS,
                          kv_heads=WIN_KV_HEADS, hd=win_hd)
            w_out = win_w_o[j]
        else:
            cg = d // FOURIER_GROUPS
            cc, sc = _cos_sin(cg)
            w_ch = jnp.asarray(np.concatenate([cc, sc], axis=1) / math.sqrt(cg), BF16)
            uv = _mm_groups(h, w_ch, FOURIER_GROUPS)
            y = _fourier_positions(uv, n_lat, n_ctx, d)
            w_out = fno_w_o[j]
        xs = _mm_res(y, w_out.astype(BF16), xs, g1[:, 0], g1[:, 1], n_lat)
        h2, aff = _norm_mod(xs, norm_gain[i, 1], sh2, sc2, n_lat, router_w=router_w[i])
        token_sets = [(0, n_lat, 0)] + ([] if last else [(n_lat, n_ctx, 1)])
        xs = _moe(xs, h2, aff, g2, exp_w_gate, exp_w_up, exp_w_down, i, token_sets)

    zeros = jnp.zeros((b, 2, 1, d), F32)
    return _norm_mod(xs, final_gain, zeros, zeros, n_lat, out_dtype=F32, rows=n_lat)
```

```python
import functools
import math

import numpy as np
import jax
import jax.numpy as jnp
from jax import lax
from jax.experimental import pallas as pl
from jax.experimental.pallas import tpu as pltpu

F32 = jnp.float32
BF16 = jnp.bfloat16
I32 = jnp.int32

GRID_W = 64
N_MIXERS = 3
RET_HEADS = 8
RET_CHUNK = 128
WIN_HEADS = 16
WIN_KV_HEADS = 4
WIN_BLOCK = 128
FOURIER_GROUPS = 4
FFT_INNER = 64
N_EXPERTS = 16
CAPACITY_FACTOR = 2
ROPE_BASE = 10000.0
NORM_EPS = 1e-6
NEG_INF = -1e30

LANES = 128
ROUTE_TILE = 128
SLOT_ALIGN = 16
ROW_TILE = 768
MM_ROW_CHUNK = 256
COL_TILE = 512
RESIDENT_WEIGHT_BYTES = 8 * 1024 * 1024
PROJ_COL_TILE = 2048
NORM_TILES = (768, 512, 256)
VMEM_LIMIT_BYTES = 56 * 1024 * 1024


def _params(sem):
    return pltpu.CompilerParams(dimension_semantics=sem, vmem_limit_bytes=VMEM_LIMIT_BYTES)


def _dot(a, b):
    return jnp.dot(a, b, preferred_element_type=F32)


def _dot_nt(a, b):
    return lax.dot_general(a, b, (((1,), (1,)), ((), ())), preferred_element_type=F32)


def _split_bf16(x):
    hi = x.astype(BF16)
    lo = (x - hi.astype(F32)).astype(BF16)
    return hi, lo


def _silu(x):
    return x / (1.0 + jnp.exp(-x))


def _mod_kernel(c_ref, w_ref, b_ref, o_ref):
    s = _silu(c_ref[...])
    sh, sl = _split_bf16(s)
    wh, wl = _split_bf16(w_ref[0])
    o_ref[0] = _dot(sh, wh) + _dot(sl, wh) + _dot(sh, wl) + b_ref[0]


def _modulation(cvec, w_mod, b_mod):
    depth, d, n = w_mod.shape
    tn = 1024
    return pl.pallas_call(
        _mod_kernel,
        name="modulation",
        grid=(depth, n // tn),
        in_specs=[
            pl.BlockSpec((8, d), lambda i, j: (0, 0)),
            pl.BlockSpec((1, d, tn), lambda i, j: (i, 0, j)),
            pl.BlockSpec((1, 1, tn), lambda i, j: (i, 0, j)),
        ],
        out_specs=pl.BlockSpec((1, 8, tn), lambda i, j: (i, 0, j)),
        out_shape=jax.ShapeDtypeStruct((depth, 8, n), F32),
        compiler_params=_params(("parallel", "parallel")),
    )(cvec, w_mod, b_mod.reshape(depth, 1, n))


def _normed(x_ref, gain_ref, shift_ref, scale_ref, n_lat):
    x = x_ref[0]
    tr = x.shape[0]
    ms = jnp.mean(x * x, axis=-1, keepdims=True)
    y = x * lax.rsqrt(ms + NORM_EPS) * gain_ref[...]
    is_lat = pl.program_id(1) * tr + lax.broadcasted_iota(I32, (tr, 1), 0) < n_lat
    scale = jnp.where(is_lat, scale_ref[0, 0], scale_ref[0, 1])
    shift = jnp.where(is_lat, shift_ref[0, 0], shift_ref[0, 1])
    return y * (1.0 + scale) + shift


def _norm_mod_kernel(x_ref, gain_ref, shift_ref, scale_ref, o_ref, *, n_lat):
    o_ref[0] = _normed(x_ref, gain_ref, shift_ref, scale_ref, n_lat).astype(o_ref.dtype)


def _pack_bf16_pairs(h):
    half = h.shape[1] // 2
    bits = lax.bitcast_convert_type(h.astype(BF16).astype(F32), jnp.uint32)
    word = lax.shift_right_logical(bits[:, :half], jnp.uint32(16)) | (bits[:, half:] & jnp.uint32(0xFFFF0000))
    return lax.bitcast_convert_type(word, I32)


def _unpack_bf16_pairs(word):
    bits = lax.bitcast_convert_type(word, jnp.uint32)
    lo = lax.bitcast_convert_type(lax.shift_left(bits, jnp.uint32(16)), F32)
    hi = lax.bitcast_convert_type(bits & jnp.uint32(0xFFFF0000), F32)
    return jnp.concatenate([lo, hi], axis=1).astype(BF16)


def _norm_router_kernel(x_ref, gain_ref, shift_ref, scale_ref, wr_ref, o_ref, aff_ref, *, n_lat):
    h = _normed(x_ref, gain_ref, shift_ref, scale_ref, n_lat)
    packed = _pack_bf16_pairs(h)
    o_ref[0] = packed.reshape(o_ref.shape[1:])
    hh, hl = _split_bf16(h)
    wh, wl = _split_bf16(wr_ref[...])
    logits = _dot(hh, wh) + _dot(hl, wh) + _dot(hh, wl)
    lane = lax.broadcasted_iota(I32, logits.shape, 1)
    valid = lane < N_EXPERTS
    logits = jnp.where(valid, logits, -jnp.inf)
    m = jnp.max(logits, axis=-1, keepdims=True)
    p = jnp.exp(logits - m)
    aff = p / jnp.sum(p, axis=-1, keepdims=True)
    aff_ref[0] = jnp.where(valid, aff, 0.0)


def _norm_mod(x, gain, shift, scale, n_lat, *, out_dtype=BF16, rows=None, router_w=None):
    b, s, d = x.shape
    rows = s if rows is None else rows
    tr = next(t for t in NORM_TILES if rows % t == 0 and s % t == 0)
    both = lambda bi, t: (bi, 0, 0, 0)
    in_specs = [
        pl.BlockSpec((1, tr, d), lambda bi, t: (bi, t, 0)),
        pl.BlockSpec((1, d), lambda bi, t: (0, 0)),
        pl.BlockSpec((1, 2, 1, d), both),
        pl.BlockSpec((1, 2, 1, d), both),
    ]
    args = [x, gain.reshape(1, d), shift, scale]
    out_specs = pl.BlockSpec((1, tr, d), lambda bi, t: (bi, t, 0))
    out_shape = jax.ShapeDtypeStruct((b, rows, d), out_dtype)
    kern = functools.partial(_norm_mod_kernel, n_lat=n_lat)
    if router_w is not None:
        wr = jnp.zeros((d, LANES), F32).at[:, :N_EXPERTS].set(router_w)
        in_specs.append(pl.BlockSpec((d, LANES), lambda bi, t: (0, 0)))
        args.append(wr)
        slab = (d // 2 // LANES, LANES)
        out_specs = [pl.BlockSpec((1, tr) + slab, lambda bi, t: (bi, t, 0, 0)),
                     pl.BlockSpec((1, tr, LANES), lambda bi, t: (bi, t, 0))]
        out_shape = [jax.ShapeDtypeStruct((b, rows) + slab, I32),
                     jax.ShapeDtypeStruct((b, rows, LANES), F32)]
        kern = functools.partial(_norm_router_kernel, n_lat=n_lat)
    return pl.pallas_call(
        kern,
        name="norm_mod" if router_w is None else "norm_router",
        grid=(b, rows // tr),
        in_specs=in_specs,
        out_specs=out_specs,
        out_shape=out_shape,
        compiler_params=_params(("parallel", "parallel")),
    )(*args)


def _rope_partner(xs, quarter):
    if 2 * quarter == LANES:
        return pltpu.roll(xs, quarter, 1)
    back = pltpu.roll(xs, quarter, 1)
    fwd = pltpu.roll(xs, LANES - quarter, 1)
    lane = lax.broadcasted_iota(I32, xs.shape, 1)
    return jnp.where((lane % (2 * quarter)) < quarter, fwd, back)


def _row_chunks(tm):
    return [slice(r, r + MM_ROW_CHUNK) for r in range(0, tm, MM_ROW_CHUNK)]


def _mm_rope_kernel(a_ref, w_ref, cos_ref, sin_ref, o_ref, *, n_q, kscale, head_dim):
    sc = jnp.where(pl.program_id(2) >= n_q, kscale, 1.0).astype(F32)
    for rows in _row_chunks(a_ref.shape[1]):
        acc = _dot(a_ref[0, rows, :], w_ref[...])
        for s in range(acc.shape[1] // LANES):
            cols = slice(s * LANES, (s + 1) * LANES)
            off = (s * LANES) % head_dim
            xs = acc[:, cols]
            rot = xs * cos_ref[rows, off:off + LANES] + _rope_partner(xs, head_dim // 4) * sin_ref[rows, off:off + LANES]
            o_ref[0, rows, cols] = (rot * sc).astype(o_ref.dtype)


def _mm_plain_kernel(a_ref, w_ref, o_ref):
    for rows in _row_chunks(a_ref.shape[1]):
        o_ref[0, rows, :] = _dot(a_ref[0, rows, :], w_ref[...]).astype(o_ref.dtype)


def _mm_proj(a, w, col0, ncols, tn, rope=None):
    b, s, k = a.shape
    tm = ROW_TILE
    assert col0 % tn == 0 and ncols % tn == 0
    in_specs = [
        pl.BlockSpec((1, tm, k), lambda bi, i, j: (bi, i, 0)),
        pl.BlockSpec((k, tn), lambda bi, i, j: (0, col0 // tn + j)),
    ]
    args = [a, w]
    if rope is None:
        kern, name = _mm_plain_kernel, "mm_proj"
    else:
        cos_t, sin_t, n_q_cols, kscale, head_dim = rope
        assert tn % head_dim == 0 and n_q_cols % tn == 0
        kern = functools.partial(_mm_rope_kernel, n_q=n_q_cols // tn, kscale=kscale, head_dim=head_dim)
        name = "mm_proj_rope"
        in_specs += [pl.BlockSpec((tm, head_dim), lambda bi, i, j: (i, 0))] * 2
        args += [cos_t, sin_t]
    return pl.pallas_call(
        kern,
        name=name,
        grid=(b, s // tm, ncols // tn),
        in_specs=in_specs,
        out_specs=pl.BlockSpec((1, tm, tn), lambda bi, i, j: (bi, i, j)),
        out_shape=jax.ShapeDtypeStruct((b, s, ncols), BF16),
        compiler_params=_params(("parallel", "parallel", "arbitrary")),
    )(*args)


def _mm_res_kernel(a_ref, w_ref, x_ref, gl_ref, gc_ref, o_ref, *, n_lat):
    tm = a_ref.shape[1]
    for rows in _row_chunks(tm):
        acc = _dot(a_ref[0, rows, :], w_ref[...])
        row = pl.program_id(1) * tm + rows.start + lax.broadcasted_iota(I32, (acc.shape[0], 1), 0)
        gate = jnp.where(row < n_lat, gl_ref[0], gc_ref[0])
        o_ref[0, rows, :] = x_ref[0, rows, :] + gate * acc


def _mm_res(a, w, x, gate_lat, gate_ctx, n_lat):
    b, s, k = a.shape
    n = w.shape[1]
    tm = ROW_TILE
    tn = n if k * n * 2 <= RESIDENT_WEIGHT_BYTES else COL_TILE
    return pl.pallas_call(
        functools.partial(_mm_res_kernel, n_lat=n_lat),
        name="mm_residual",
        grid=(b, s // tm, n // tn),
        in_specs=[
            pl.BlockSpec((1, tm, k), lambda bi, i, j: (bi, i, 0)),
            pl.BlockSpec((k, tn), lambda bi, i, j: (0, j)),
            pl.BlockSpec((1, tm, tn), lambda bi, i, j: (bi, i, j)),
            pl.BlockSpec((1, 1, tn), lambda bi, i, j: (bi, 0, j)),
            pl.BlockSpec((1, 1, tn), lambda bi, i, j: (bi, 0, j)),
        ],
        out_specs=pl.BlockSpec((1, tm, tn), lambda bi, i, j: (bi, i, j)),
        out_shape=jax.ShapeDtypeStruct((b, s, n), F32),
        compiler_params=_params(("parallel", "parallel", "arbitrary")),
    )(a, w, x, gate_lat, gate_ctx)


def _mm_groups_kernel(a_ref, w_ref, o_ref):
    o_ref[0] = _dot(a_ref[0], w_ref[...]).astype(o_ref.dtype)


def _mm_groups(a, w, groups):
    b, s, d = a.shape
    cg = d // groups
    tm = ROW_TILE
    return pl.pallas_call(
        _mm_groups_kernel,
        name="mm_channel_dft",
        grid=(b, s // tm, 2 * groups),
        in_specs=[
            pl.BlockSpec((1, tm, cg), lambda bi, i, j: (bi, i, j % groups)),
            pl.BlockSpec((cg, cg), lambda bi, i, j: (0, j // groups)),
        ],
        out_specs=pl.BlockSpec((1, tm, cg), lambda bi, i, j: (bi, i, j)),
        out_shape=jax.ShapeDtypeStruct((b, s, 2 * d), F32),
        compiler_params=_params(("parallel", "parallel", "arbitrary")),
    )(a, w)


def _ret_scan_kernel(*refs, heads, dk, dv, reverse, add_in):
    if add_in:
        dec_ref, q_ref, k_ref, v_ref, g_ref, yin_ref, o_ref, s_ref, qd_ref, kd_ref, in_ref, cd_ref = refs
    else:
        dec_ref, q_ref, k_ref, v_ref, g_ref, o_ref, s_ref, qd_ref, kd_ref, in_ref, cd_ref = refs
        yin_ref = None
    c = q_ref.shape[1]
    j = pl.program_id(1)

    @pl.when(j == 0)
    def _():
        s_ref[...] = jnp.zeros_like(s_ref)
        m_col = lax.broadcasted_iota(I32, (c, LANES), 0).astype(F32)
        m_row = lax.broadcasted_iota(I32, (c, c), 0).astype(F32)
        n_row = lax.broadcasted_iota(I32, (c, c), 1).astype(F32)
        for h in range(heads):
            lg = -jnp.exp(dec_ref[:, h:h + 1])
            if reverse:
                q_pow, k_pow, diff = c - m_col, m_col, n_row - m_row
            else:
                q_pow, k_pow, diff = m_col + 1.0, c - 1.0 - m_col, m_row - n_row
            qd_ref[h] = jnp.exp(lg * q_pow)
            kd_ref[h] = jnp.exp(lg * k_pow)
            in_ref[h] = jnp.where(diff >= 0, jnp.exp(lg * jnp.maximum(diff, 0.0)), 0.0)
            cd_ref[h] = jnp.exp(jnp.broadcast_to(lg, (8, LANES)) * float(c))

    for h in range(heads):
        q = q_ref[0, :, h * dk:(h + 1) * dk]
        k = k_ref[0, :, h * dk:(h + 1) * dk]
        v = v_ref[0, :, h * dv:(h + 1) * dv]
        qdec = jnp.concatenate([qd_ref[h]] * (dk // LANES), axis=1)
        kdec = jnp.concatenate([kd_ref[h]] * (dk // LANES), axis=1)
        state = s_ref[h]
        cross = _dot((q.astype(F32) * qdec).astype(BF16), state.astype(BF16))
        scores = _dot_nt(q, k) * in_ref[h]
        o = cross + _dot(scores.astype(BF16), v)
        k_t = (k.astype(F32) * kdec).T.astype(BF16)
        s_ref[h] = state * cd_ref[h][0:1, 0:1] + _dot(k_t, v)
        mu = jnp.mean(o, axis=-1, keepdims=True)
        cen = o - mu
        var = jnp.mean(cen * cen, axis=-1, keepdims=True)
        g = g_ref[0, :, h * dv:(h + 1) * dv].astype(F32)
        y = cen * lax.rsqrt(var + NORM_EPS) * _silu(g)
        if add_in:
            y = y + yin_ref[0, :, h * dv:(h + 1) * dv].astype(F32)
        o_ref[0, :, h * dv:(h + 1) * dv] = y.astype(o_ref.dtype)


def _ret_scan(qk, vg, decay_row, y_in, *, n_lat, n_ctx, heads, dk, dv, reverse):
    b, s, _ = qk.shape
    c = RET_CHUNK
    lat_chunks, ctx_chunks = n_lat // c, n_ctx // c
    steps = lat_chunks + ctx_chunks
    hk, hv = heads * dk, heads * dv
    if reverse:
        chunk = lambda j: steps - 1 - j
    else:
        chunk = lambda j: jnp.where(j < ctx_chunks, lat_chunks + j, j - ctx_chunks)
    gate_blk = 2 if reverse else 1
    in_specs = [
        pl.BlockSpec((1, heads), lambda bi, j: (0, 0)),
        pl.BlockSpec((1, c, hk), lambda bi, j: (bi, chunk(j), 0)),
        pl.BlockSpec((1, c, hk), lambda bi, j: (bi, chunk(j), 1)),
        pl.BlockSpec((1, c, hv), lambda bi, j: (bi, chunk(j), 0)),
        pl.BlockSpec((1, c, hv), lambda bi, j: (bi, chunk(j), gate_blk)),
    ]
    args = [decay_row, qk, qk, vg, vg]
    if y_in is not None:
        in_specs.append(pl.BlockSpec((1, c, hv), lambda bi, j: (bi, chunk(j), 0)))
        args.append(y_in)
    kern = functools.partial(_ret_scan_kernel, heads=heads, dk=dk, dv=dv, reverse=reverse,
                             add_in=y_in is not None)
    return pl.pallas_call(
        kern,
        name="ret_scan_bwd" if reverse else "ret_scan_fwd",
        grid=(b, steps),
        in_specs=in_specs,
        out_specs=pl.BlockSpec((1, c, hv), lambda bi, j: (bi, chunk(j), 0)),
        out_shape=jax.ShapeDtypeStruct((b, s, hv), BF16),
        scratch_shapes=[
            pltpu.VMEM((heads, dk, dv), F32),
            pltpu.VMEM((heads, c, LANES), F32),
            pltpu.VMEM((heads, c, LANES), F32),
            pltpu.VMEM((heads, c, c), F32),
            pltpu.VMEM((heads, 8, LANES), F32),
        ],
        compiler_params=_params(("parallel", "arbitrary")),
    )(*args)


def _win_attn_kernel(sink_ref, q_ref, kc_ref, vc_ref, kp_ref, kq_ref, kn_ref, vp_ref, vq_ref, vn_ref,
                     o_ref, *, lat_tiles, heads, kv_heads, hd):
    qt = pl.program_id(1)
    blk = q_ref.shape[1]
    n_ctx = kc_ref.shape[1]
    grp = heads // kv_heads
    scale = hd ** -0.5
    rows = grp * blk
    qi = lax.broadcasted_iota(I32, (rows, blk), 0) % blk
    kj = lax.broadcasted_iota(I32, (rows, blk), 1)
    tq = qt + jnp.zeros((rows, blk), I32)
    ok_cur = tq < lat_tiles
    ok_prev = (kj >= qi) & ok_cur & (tq >= 1)
    ok_next = (kj <= qi) & (tq + 1 < lat_tiles)
    bias = jnp.concatenate(
        [jnp.zeros((rows, n_ctx), F32)]
        + [jnp.where(ok, 0.0, NEG_INF).astype(F32) for ok in (ok_prev, ok_cur, ok_next)], axis=1)
    head_row = lax.broadcasted_iota(I32, (rows, 1), 0) // blk
    for kv in range(kv_heads):
        cs = slice(kv * hd, (kv + 1) * hd)
        keys = jnp.concatenate([kc_ref[0, :, cs], kp_ref[0, :, cs], kq_ref[0, :, cs], kn_ref[0, :, cs]], axis=0)
        vals = jnp.concatenate([vc_ref[0, :, cs], vp_ref[0, :, cs], vq_ref[0, :, cs], vn_ref[0, :, cs]], axis=0)
        q = jnp.concatenate([q_ref[0, :, (kv * grp + g) * hd:(kv * grp + g + 1) * hd] for g in range(grp)], axis=0)
        sink = jnp.zeros((rows, 1), F32)
        for g in range(grp):
            h = kv * grp + g
            sink = jnp.where(head_row == g, sink_ref[:, h:h + 1], sink)
        s = _dot_nt(q, keys) * scale + bias
        m = jnp.maximum(jnp.max(s, axis=-1, keepdims=True), sink)
        p = jnp.exp(s - m)
        den = jnp.sum(p, axis=-1, keepdims=True) + jnp.exp(sink - m)
        o = _dot(p.astype(BF16), vals) / den
        for g in range(grp):
            h = kv * grp + g
            o_ref[0, :, h * hd:(h + 1) * hd] = o[g * blk:(g + 1) * blk].astype(o_ref.dtype)


def _win_attn(qk, v, sink, *, n_lat, n_ctx, heads, kv_heads, hd):
    b, s, _ = qk.shape
    blk = WIN_BLOCK
    lat_tiles = n_lat // blk
    tiles = s // blk
    kvw = kv_heads * hd
    k_col = (heads * hd) // kvw
    v_col = 0
    ctx_blk = n_lat // n_ctx
    prev = lambda t: jnp.maximum(t - 1, 0)
    nxt = lambda t: jnp.minimum(t + 1, tiles - 1)
    sink_row = jnp.zeros((1, LANES), F32).at[0, :heads].set(sink.astype(F32))
    kern = functools.partial(_win_attn_kernel, lat_tiles=lat_tiles, heads=heads, kv_heads=kv_heads, hd=hd)
    return pl.pallas_call(
        kern,
        name="win_attn",
        grid=(b, tiles),
        in_specs=[
            pl.BlockSpec((1, LANES), lambda bi, t: (0, 0)),
            pl.BlockSpec((1, blk, heads * hd), lambda bi, t: (bi, t, 0)),
            pl.BlockSpec((1, n_ctx, kvw), lambda bi, t: (bi, ctx_blk, k_col)),
            pl.BlockSpec((1, n_ctx, kvw), lambda bi, t: (bi, ctx_blk, v_col)),
            pl.BlockSpec((1, blk, kvw), lambda bi, t: (bi, prev(t), k_col)),
            pl.BlockSpec((1, blk, kvw), lambda bi, t: (bi, t, k_col)),
            pl.BlockSpec((1, blk, kvw), lambda bi, t: (bi, nxt(t), k_col)),
            pl.BlockSpec((1, blk, kvw), lambda bi, t: (bi, prev(t), v_col)),
            pl.BlockSpec((1, blk, kvw), lambda bi, t: (bi, t, v_col)),
            pl.BlockSpec((1, blk, kvw), lambda bi, t: (bi, nxt(t), v_col)),
        ],
        out_specs=pl.BlockSpec((1, blk, heads * hd), lambda bi, t: (bi, t, 0)),
        out_shape=jax.ShapeDtypeStruct((b, s, heads * hd), BF16),
        compiler_params=_params(("parallel", "parallel")),
    )(sink_row, qk, qk, v, qk, qk, qk, v, v, v)


SUBLANES = 8
PACKED_ROWS = 16


def _dft_stage_a_kernel(m_ref, u_ref, v_ref, re_ref, im_ref):
    n1, sub, tc = u_ref.shape[1:]
    rows = n1 * sub
    stacked = jnp.concatenate([u_ref[0].reshape(rows, tc), v_ref[0].reshape(rows, tc)], axis=0)
    out = _dot(m_ref[...], stacked.astype(BF16))
    re_ref[0] = out[:rows].reshape(n1, sub, tc)
    im_ref[0] = out[rows:].reshape(n1, sub, tc)


def _dft_stage_b_kernel(m_ref, re_ref, im_ref, twc_ref, tws_ref, o_ref):
    n2, sub, tc = o_ref.shape[1:]
    reps = tc // LANES
    ar, ai = re_ref[0], im_ref[0]
    twc = jnp.concatenate([twc_ref[...]] * reps, axis=1)
    tws = jnp.concatenate([tws_ref[...]] * reps, axis=1)
    stacked = jnp.concatenate([ar * twc + ai * tws, ai * twc - ar * tws], axis=0).astype(BF16)
    out = _dot(m_ref[...], stacked)
    o_ref[0] = out.reshape(n2, sub, tc).astype(o_ref.dtype)


def _dft_ctx_kernel(m_ref, x_ref, o_ref, *, d):
    x = x_ref[0]
    stacked = jnp.concatenate([x[:, :d], x[:, d:]], axis=0).astype(BF16)
    o_ref[0] = _dot(m_ref[...], stacked).astype(o_ref.dtype)


def _cos_sin(n):
    ang = 2.0 * np.pi * np.outer(np.arange(n), np.arange(n)) / float(n)
    return np.cos(ang), np.sin(ang)


def _fourier_positions(uv, n_lat, n_ctx, d):
    b, s, _ = uv.shape
    n2 = FFT_INNER
    n1 = n_lat // n2
    tc = min(512, d)
    ca, sa = _cos_sin(n1)
    eye = np.eye(SUBLANES)
    ka, ks = np.kron(ca, eye) / math.sqrt(n1), np.kron(sa, eye) / math.sqrt(n1)
    mat_a = jnp.asarray(np.block([[ka, -ks], [-ks, -ka]]), BF16)
    uv4 = uv.reshape(b, s // n2, n2, 2 * d)
    blk_a = (1, n1, SUBLANES, tc)
    a_re, a_im = pl.pallas_call(
        _dft_stage_a_kernel,
        name="dft_stage_a",
        grid=(b, n2 // SUBLANES, d // tc),
        in_specs=[
            pl.BlockSpec(mat_a.shape, lambda bi, cg, jc: (0, 0)),
            pl.BlockSpec(blk_a, lambda bi, cg, jc: (bi, 0, cg, jc)),
            pl.BlockSpec(blk_a, lambda bi, cg, jc: (bi, 0, cg, d // tc + jc)),
        ],
        out_specs=[pl.BlockSpec(blk_a, lambda bi, cg, jc: (bi, 0, cg, jc))] * 2,
        out_shape=[jax.ShapeDtypeStruct((b, n1, n2, d), F32)] * 2,
        compiler_params=_params(("parallel", "parallel", "parallel")),
    )(mat_a, uv4, uv4)
    cb, sb = _cos_sin(n2)
    eye = np.eye(PACKED_ROWS)
    kron_b = lambda m: np.einsum("kc,ab->kabc", m, eye).reshape(n2 * PACKED_ROWS, PACKED_ROWS * n2)
    mat_b = jnp.asarray(np.concatenate([kron_b(cb), kron_b(sb)], axis=1) / math.sqrt(n2), BF16)
    phi = 2.0 * np.pi * np.outer(np.arange(n1), np.arange(n2)) / float(n_lat)
    twc = jnp.asarray(np.repeat(np.cos(phi).reshape(-1, 1), LANES, axis=1), F32)
    tws = jnp.asarray(np.repeat(np.sin(phi).reshape(-1, 1), LANES, axis=1), F32)
    rows = PACKED_ROWS * n2
    y = pl.pallas_call(
        _dft_stage_b_kernel,
        name="dft_stage_b",
        grid=(b, n1 // PACKED_ROWS, d // tc),
        in_specs=[
            pl.BlockSpec(mat_b.shape, lambda bi, kb, jc: (0, 0)),
            pl.BlockSpec((1, rows, tc), lambda bi, kb, jc: (bi, kb, jc)),
            pl.BlockSpec((1, rows, tc), lambda bi, kb, jc: (bi, kb, jc)),
            pl.BlockSpec((rows, LANES), lambda bi, kb, jc: (kb, 0)),
            pl.BlockSpec((rows, LANES), lambda bi, kb, jc: (kb, 0)),
        ],
        out_specs=pl.BlockSpec((1, n2, PACKED_ROWS, tc), lambda bi, kb, jc: (bi, 0, kb, jc)),
        out_shape=jax.ShapeDtypeStruct((b, n2, n1, d), BF16),
        compiler_params=_params(("parallel", "parallel", "parallel")),
    )(mat_b, a_re.reshape(b, n_lat, d), a_im.reshape(b, n_lat, d), twc, tws)
    cc, sc = _cos_sin(n_ctx)
    mat_c = jnp.asarray(np.concatenate([cc, -sc], axis=1) / math.sqrt(n_ctx), BF16)
    y_ctx = pl.pallas_call(
        functools.partial(_dft_ctx_kernel, d=d),
        name="dft_ctx",
        grid=(b,),
        in_specs=[
            pl.BlockSpec(mat_c.shape, lambda bi: (0, 0)),
            pl.BlockSpec((1, n_ctx, 2 * d), lambda bi: (bi, n_lat // n_ctx, 0)),
        ],
        out_specs=pl.BlockSpec((1, n_ctx, d), lambda bi: (bi, 0, 0)),
        out_shape=jax.ShapeDtypeStruct((b, n_ctx, d), BF16),
        compiler_params=_params(("parallel",)),
    )(mat_c, uv)
    return jnp.concatenate([y.reshape(b, n_lat, d), y_ctx], axis=1)


LANE_SHIFT = 7
GATHER_UNROLL = 8
TILES_PER_STEP = 4
FFN_ROW_CHUNK = 384
COMBINE_REGION = 64


def _route_kernel(aff_ref, tri_ref, posc_ref, gate_ref, tst_ref,
                  thr_ref, need_ref, ctie_ref, cpos_ref, *, cap):
    step = pl.program_id(1)
    tile = tri_ref.shape[0]
    tps = posc_ref.shape[1] // tile

    @pl.when(step == 0)
    def _():
        def body(it, thr):
            bits = lax.bitcast_convert_type(aff_ref[0], I32)
            cand = thr | jnp.left_shift(jnp.int32(1), 30 - it)
            cnt = jnp.sum(jnp.where(bits >= cand, 1.0, 0.0), axis=0, keepdims=True)
            return jnp.where(cnt >= cap, cand, thr)

        thr = lax.fori_loop(0, 31, body, jnp.zeros((1, LANES), I32))
        bits = lax.bitcast_convert_type(aff_ref[0], I32)
        above = jnp.sum(jnp.where(bits > thr, 1.0, 0.0), axis=0, keepdims=True)
        thr_ref[...] = thr
        need_ref[...] = float(cap) - above
        ctie_ref[...] = jnp.zeros_like(ctie_ref)
        cpos_ref[...] = jnp.zeros_like(cpos_ref)

    thr = thr_ref[...]
    for u in range(tps):
        rows = slice(u * tile, (u + 1) * tile)
        a = aff_ref[0, pl.ds(pl.multiple_of((step * tps + u) * tile, tile), tile), :]
        bits = lax.bitcast_convert_type(a, I32)
        gt = bits > thr
        eq = bits == thr
        eqf = jnp.where(eq, 1.0, 0.0)
        tie_rank = _dot(tri_ref[...], eqf.astype(BF16)) + ctie_ref[...]
        sel = gt | (eq & (tie_rank < need_ref[...]))
        self_ = jnp.where(sel, 1.0, 0.0)
        start = cpos_ref[...]
        pos = jnp.where(sel, _dot(tri_ref[...], self_.astype(BF16)) + start, -1.0)
        posc_ref[0, rows, :] = pos.astype(I32)
        gate_ref[0, rows, :] = jnp.where(sel, a, 0.0)
        tst_ref[0, u] = jnp.broadcast_to(start, (8, LANES)).astype(I32)
        ctie_ref[...] = ctie_ref[...] + jnp.sum(eqf, axis=0, keepdims=True)
        cpos_ref[...] = start + jnp.sum(self_, axis=0, keepdims=True)


def _route(aff, *, row_off, n, cap):
    b = aff.shape[0]
    tile = ROUTE_TILE
    nt = n // tile
    tps = min(TILES_PER_STEP, nt)
    tri = jnp.asarray(np.tril(np.ones((tile, tile)), -1), BF16)
    return pl.pallas_call(
        functools.partial(_route_kernel, cap=cap),
        name="moe_route",
        grid=(b, nt // tps),
        in_specs=[
            pl.BlockSpec((1, n, LANES), lambda bi, t: (bi, row_off // n, 0)),
            pl.BlockSpec((tile, tile), lambda bi, t: (0, 0)),
        ],
        out_specs=[
            pl.BlockSpec((1, tps * tile, LANES), lambda bi, t: (bi, t, 0)),
            pl.BlockSpec((1, tps * tile, LANES), lambda bi, t: (bi, t, 0)),
            pl.BlockSpec((1, tps, 8, LANES), lambda bi, t: (bi, t, 0, 0)),
        ],
        out_shape=[
            jax.ShapeDtypeStruct((b, n, LANES), I32),
            jax.ShapeDtypeStruct((b, n, LANES), F32),
            jax.ShapeDtypeStruct((b, nt, 8, LANES), I32),
        ],
        scratch_shapes=[pltpu.VMEM((1, LANES), I32)] + [pltpu.VMEM((1, LANES), F32)] * 3,
        compiler_params=_params(("parallel", "arbitrary")),
    )(aff, tri)


def _slot_index_kernel(ts_ref, posc_ref, idx_ref, *, nt, row_off, tile):
    bi, step = pl.program_id(0), pl.program_id(1)

    @pl.when(step == 0)
    def _():
        idx_ref[...] = jnp.zeros_like(idx_ref)

    tps = posc_ref.shape[1] // tile
    slot_rows = idx_ref.shape[2]
    lane = lax.broadcasted_iota(I32, (tile, 2 * LANES), 1)
    for u in range(tps):
        t = step * tps + u
        tok = (row_off + t * tile + lax.broadcasted_iota(I32, (tile, 1), 0)).astype(F32)
        for e in range(N_EXPERTS):
            h0 = jnp.minimum(ts_ref[(bi * nt + t) * N_EXPERTS + e] >> LANE_SHIFT, slot_rows - 2)
            hit = (posc_ref[0, u * tile:(u + 1) * tile, e:e + 1] - h0 * LANES) == lane
            vals = jnp.sum(jnp.where(hit, tok, 0.0), axis=0, keepdims=True)
            two_rows = jnp.concatenate([vals[:, :LANES], vals[:, LANES:]], axis=0).astype(I32)
            idx_ref[0, e, pl.ds(h0, 2), :] = idx_ref[0, e, pl.ds(h0, 2), :] + two_rows


def _slot_index(tstart, posc, *, row_off, n, cap):
    b = posc.shape[0]
    tile = ROUTE_TILE
    nt = n // tile
    slot_rows = max(cap, 2 * LANES) // LANES
    tps = min(TILES_PER_STEP, nt)
    grid_spec = pltpu.PrefetchScalarGridSpec(
        num_scalar_prefetch=1,
        grid=(b, nt // tps),
        in_specs=[pl.BlockSpec((1, tps * tile, LANES), lambda bi, t, ts: (bi, t, 0))],
        out_specs=pl.BlockSpec((1, N_EXPERTS, slot_rows, LANES), lambda bi, t, ts: (bi, 0, 0, 0)),
    )
    idx = pl.pallas_call(
        functools.partial(_slot_index_kernel, nt=nt, row_off=row_off, tile=tile),
        name="moe_slot_index",
        grid_spec=grid_spec,
        out_shape=jax.ShapeDtypeStruct((b, N_EXPERTS, slot_rows, LANES), I32),
        compiler_params=_params(("parallel", "arbitrary")),
    )(tstart, posc)
    return idx.reshape(b, N_EXPERTS, slot_rows * LANES)


def _gather_kernel(idx_ref, h_ref, o_ref, buf_ref, sem):
    bi = pl.program_id(1)
    n_rows = buf_ref.shape[0]

    def row_copy(src_row, dst_row, rows):
        return pltpu.make_async_copy(h_ref.at[bi, pl.ds(src_row, rows)], buf_ref.at[pl.ds(dst_row, rows)], sem)

    def issue(g, carry):
        for k in range(GATHER_UNROLL):
            s = g * GATHER_UNROLL + k
            row_copy(idx_ref[0, 0, s], s, 1).start(priority=k % 2)
        return carry

    lax.fori_loop(0, n_rows // GATHER_UNROLL, issue, 0)
    row_copy(0, 0, n_rows).wait()
    o_ref[0, 0] = _unpack_bf16_pairs(buf_ref[...].reshape(n_rows, -1))


def _gather(idx, h):
    b, _, sub, lanes = h.shape
    d = 2 * sub * lanes
    slots = idx.shape[2]
    assert slots % GATHER_UNROLL == 0
    return pl.pallas_call(
        _gather_kernel,
        name="moe_gather",
        grid=(N_EXPERTS, b),
        in_specs=[
            pl.BlockSpec((1, 1, slots), lambda ei, bi: (bi * N_EXPERTS + ei, 0, 0), memory_space=pltpu.SMEM),
            pl.BlockSpec(memory_space=pl.ANY),
        ],
        out_specs=pl.BlockSpec((1, 1, slots, d), lambda ei, bi: (ei, bi, 0, 0)),
        out_shape=jax.ShapeDtypeStruct((N_EXPERTS, b, slots, d), BF16),
        scratch_shapes=[pltpu.VMEM((slots, sub, lanes), I32), pltpu.SemaphoreType.DMA(())],
        compiler_params=_params(("arbitrary", "arbitrary")),
    )(idx.reshape(b * N_EXPERTS, 1, slots), h)


def _ffn_kernel(x_ref, wg_ref, wu_ref, wd_ref, ye_ref, hm_ref, *, n_up):
    st = pl.program_id(1)
    bsz, rows, d = x_ref.shape[1:]
    tf = wg_ref.shape[3]
    chunk = FFN_ROW_CHUNK if rows % FFN_ROW_CHUNK == 0 else rows
    spans = [(bi, r0) for bi in range(bsz) for r0 in range(0, rows, chunk)]

    @pl.when(st < n_up)
    def _():
        wg = wg_ref[0, 0].astype(BF16)
        wu = wu_ref[0, 0].astype(BF16)
        for bi, r0 in spans:
            x = x_ref[0, bi, r0:r0 + chunk, :]
            hm_ref[st, bi * rows + r0:bi * rows + r0 + chunk, :] = (_silu(_dot(x, wg)) * _dot(x, wu)).astype(BF16)

    @pl.when(st >= n_up)
    def _():
        wd = [wd_ref[0, 0, c * tf:(c + 1) * tf, :].astype(BF16) for c in range(n_up)]
        for bi, r0 in spans:
            m0 = bi * rows + r0
            y = _dot(hm_ref[0, m0:m0 + chunk, :], wd[0])
            for c in range(1, n_up):
                y = y + _dot(hm_ref[c, m0:m0 + chunk, :], wd[c])
            ye_ref[0, bi, r0:r0 + chunk, :] = y.astype(ye_ref.dtype)


def _ffn(xe, w_gate, w_up, w_down, layer):
    e, b, rows, d = xe.shape
    f = w_gate.shape[3]
    tf = min(512, f)
    tdc = min(512, d)
    n_up, n_down = f // tf, d // tdc
    up = lambda st: jnp.minimum(st, n_up - 1)
    down = lambda st: jnp.maximum(st - n_up, 0)
    return pl.pallas_call(
        functools.partial(_ffn_kernel, n_up=n_up),
        name="moe_ffn",
        grid=(e, n_up + n_down),
        in_specs=[
            pl.BlockSpec((1, b, rows, d), lambda ei, st: (ei, 0, 0, 0)),
            pl.BlockSpec((1, 1, d, tf), lambda ei, st: (layer, ei, 0, up(st))),
            pl.BlockSpec((1, 1, d, tf), lambda ei, st: (layer, ei, 0, up(st))),
            pl.BlockSpec((1, 1, f, tdc), lambda ei, st: (layer, ei, 0, down(st))),
        ],
        out_specs=pl.BlockSpec((1, b, rows, tdc), lambda ei, st: (ei, 0, 0, down(st))),
        out_shape=jax.ShapeDtypeStruct((e, b, rows, d), BF16),
        scratch_shapes=[pltpu.VMEM((n_up, b * rows, tf), BF16)],
        compiler_params=_params(("parallel", "arbitrary")),
    )(xe, w_gate, w_up, w_down)


def _window_start(ts_ref, idx, cap_rows, win):
    a0 = jnp.minimum(ts_ref[idx] & (-SLOT_ALIGN), cap_rows - win)
    return pl.multiple_of(a0, SLOT_ALIGN)


def _combine_kernel(ts_ref, ye_ref, posc_ref, gate_ref, x_ref, g_ref, o_ref, *, nt, win, cap, n_starts, tile):
    bi, step = pl.program_id(0), pl.program_id(2)
    cap_rows = ye_ref.shape[2]
    tps = posc_ref.shape[1] // tile
    reg = COMBINE_REGION
    assert 2 * reg == LANES and reg <= cap_rows

    for u in range(tps):
        t = step * tps + u
        rows = slice(u * tile, (u + 1) * tile)
        base = (bi * nt + t) * N_EXPERTS
        starts, fits = [], None
        for e in range(N_EXPERTS):
            a0 = _window_start(ts_ref, base + e, cap_rows, reg)
            nxt = ts_ref[jnp.minimum(base + N_EXPERTS + e, n_starts - 1)]
            end = jnp.where(t + 1 < nt, nxt, cap)
            ok = end - a0 <= reg
            fits = ok if fits is None else jnp.logical_and(fits, ok)
            starts.append(a0)

        lane = lax.broadcasted_iota(I32, (tile, LANES), 1)
        upper = lane >= reg
        weights, slabs = [], []
        for p in range(N_EXPERTS // 2):
            e0, e1 = 2 * p, 2 * p + 1
            slot = jnp.where(upper, starts[e1] - reg, starts[e0]) + lane
            pcol = jnp.where(upper, posc_ref[0, rows, e1:e1 + 1], posc_ref[0, rows, e0:e0 + 1])
            gcol = jnp.where(upper, gate_ref[0, rows, e1:e1 + 1], gate_ref[0, rows, e0:e0 + 1])
            weights.append(jnp.where(pcol == slot, gcol, 0.0).astype(BF16))
            slabs += [ye_ref[e0, 0, pl.ds(starts[e0], reg), :], ye_ref[e1, 0, pl.ds(starts[e1], reg), :]]
        acc = _dot(jnp.concatenate(weights, axis=1), jnp.concatenate(slabs, axis=0))
        o_ref[0, rows, :] = x_ref[0, rows, :] + g_ref[0, 0] * acc

        @pl.when(jnp.logical_not(fits))
        def _():
            acc = jnp.zeros((tile, o_ref.shape[2]), F32)
            lane = lax.broadcasted_iota(I32, (tile, win), 1)
            for e in range(N_EXPERTS):
                a0 = _window_start(ts_ref, base + e, cap_rows, win)
                pcol = posc_ref[0, rows, e:e + 1]
                gcol = gate_ref[0, rows, e:e + 1]
                w = jnp.where(pcol == a0 + lane, gcol, 0.0).astype(BF16)
                acc = acc + _dot(w, ye_ref[e, 0, pl.ds(a0, win), :])
            o_ref[0, rows, :] = x_ref[0, rows, :] + g_ref[0, 0] * acc


def _combine(tstart, ye, posc, gate, x, g2, *, row_off, n, region, row0, cap_rows):
    b, s, d = x.shape
    tile = ROUTE_TILE
    nt = n // tile
    dc = 512 if d % 512 == 0 else d
    win = min(2 * tile, cap_rows)
    tps = min(TILES_PER_STEP, nt)
    rows = tps * tile
    assert row_off % rows == 0
    off = row_off // rows
    grid_spec = pltpu.PrefetchScalarGridSpec(
        num_scalar_prefetch=1,
        grid=(b, d // dc, nt // tps),
        in_specs=[
            pl.BlockSpec((N_EXPERTS, 1, cap_rows, dc), lambda bi, c, t, ts: (0, bi, row0 // cap_rows, c)),
            pl.BlockSpec((1, rows, LANES), lambda bi, c, t, ts: (bi, t, 0)),
            pl.BlockSpec((1, rows, LANES), lambda bi, c, t, ts: (bi, t, 0)),
            pl.BlockSpec((1, rows, dc), lambda bi, c, t, ts: (bi, off + t, c)),
            pl.BlockSpec((1, 1, 1, dc), lambda bi, c, t, ts: (bi, region, 0, c)),
        ],
        out_specs=pl.BlockSpec((1, rows, dc), lambda bi, c, t, ts: (bi, off + t, c)),
    )
    return pl.pallas_call(
        functools.partial(_combine_kernel, nt=nt, win=win, cap=CAPACITY_FACTOR * n // N_EXPERTS,
                          n_starts=b * nt * N_EXPERTS, tile=tile),
        name="moe_combine",
        grid_spec=grid_spec,
        out_shape=jax.ShapeDtypeStruct((b, s, d), F32),
        input_output_aliases={4: 0},
        compiler_params=_params(("parallel", "parallel", "arbitrary")),
    )(tstart, ye, posc, gate, x, g2)


def _moe(x, h, aff, g2, w_gate, w_up, w_down, layer, token_sets):
    routed, row0 = [], 0
    for row_off, n, region in token_sets:
        cap = CAPACITY_FACTOR * n // N_EXPERTS
        cap_rows = -(-cap // LANES) * LANES
        posc, gate, tst = _route(aff, row_off=row_off, n=n, cap=cap)
        tstart = tst[:, :, 0, :N_EXPERTS].reshape(-1)
        idx = _slot_index(tstart, posc, row_off=row_off, n=n, cap=cap)
        routed.append((row_off, n, region, row0, cap_rows, tstart, posc, gate, idx))
        row0 += cap_rows
    idx_all = jnp.concatenate([r[8][:, :, :r[4]] for r in routed], axis=2)
    ye = _ffn(_gather(idx_all, h), w_gate, w_up, w_down, layer)
    for row_off, n, region, r0, cap_rows, tstart, posc, gate, _ in routed:
        x = _combine(tstart, ye, posc, gate, x, g2, row_off=row_off, n=n, region=region, row0=r0, cap_rows=cap_rows)
    return x


def _rope_tables(n_lat, n_ctx, head_dim):
    quarter = head_dim // 4
    inv = ROPE_BASE ** (-jnp.arange(quarter, dtype=F32) / quarter)
    pos = jnp.arange(n_lat)
    row = (pos // GRID_W).astype(F32)[:, None] * inv[None, :]
    col = (pos % GRID_W).astype(F32)[:, None] * inv[None, :]
    cos = jnp.concatenate([jnp.cos(row), jnp.cos(row), jnp.cos(col), jnp.cos(col)], axis=1)
    sin = jnp.concatenate([-jnp.sin(row), jnp.sin(row), -jnp.sin(col), jnp.sin(col)], axis=1)
    cos = jnp.concatenate([cos, jnp.ones((n_ctx, head_dim), F32)], axis=0)
    sin = jnp.concatenate([sin, jnp.zeros((n_ctx, head_dim), F32)], axis=0)
    return cos, sin


def kernel(x, c, ctx, c_ctx, w_mod, b_mod, norm_gain, final_gain, ret_w_in, ret_w_out, ret_decay,
           win_w_qkv, win_w_o, win_sink, fno_w_o, router_w, exp_w_gate, exp_w_up, exp_w_down):
    b, n_lat, d = x.shape
    n_ctx = ctx.shape[1]
    depth = w_mod.shape[0]
    s = n_lat + n_ctx
    assert s % ROW_TILE == 0 and n_lat % n_ctx == 0
    assert n_lat % (FFT_INNER * 4) == 0 and b + 1 <= 8 and n_ctx % ROUTE_TILE == 0

    xs = jnp.concatenate([x, ctx], axis=1)
    cvec = jnp.zeros((8, d), F32).at[:b].set(c).at[b].set(c_ctx)
    mod = _modulation(cvec, w_mod, b_mod)

    def mod_pair(i, k):
        lat = mod[i, :b, k * d:(k + 1) * d]
        cx = jnp.broadcast_to(mod[i, b, k * d:(k + 1) * d], (b, d))
        return jnp.stack([lat, cx], axis=1).reshape(b, 2, 1, d)

    ret_dk = d // RET_HEADS
    ret_dv = 2 * ret_dk
    win_hd = d // WIN_HEADS
    ret_cos, ret_sin = _rope_tables(n_lat, n_ctx, ret_dk)
    win_cos, win_sin = _rope_tables(n_lat, n_ctx, win_hd)

    for i in range(depth):
        kind, j = i % N_MIXERS, i // N_MIXERS
        last = i == depth - 1
        sh1, sc1, g1, sh2, sc2, g2 = [mod_pair(i, k) for k in range(6)]
        h = _norm_mod(xs, norm_gain[i, 0], sh1, sc1, n_lat)
        if kind == 0:
            hk, hv = RET_HEADS * ret_dk, RET_HEADS * ret_dv
            w_in = ret_w_in[j].astype(BF16)
            qk = _mm_proj(h, w_in, 0, 2 * hk, PROJ_COL_TILE,
                          rope=(ret_cos, ret_sin, hk, ret_dk ** -0.5, ret_dk))
            vg = _mm_proj(h, w_in, 2 * hk, 3 * hv, PROJ_COL_TILE)
            scan = functools.partial(_ret_scan, n_lat=n_lat, n_ctx=n_ctx, heads=RET_HEADS, dk=ret_dk, dv=ret_dv)
            y_b = scan(qk, vg, ret_decay[j, 1:2], None, reverse=True)
            y = scan(qk, vg, ret_decay[j, 0:1], y_b, reverse=False)
            w_out = ret_w_out[j]
        elif kind == 1:
            nq, nkv = WIN_HEADS * win_hd, WIN_KV_HEADS * win_hd
            w_qkv = win_w_qkv[j].astype(BF16)
            qk = _mm_proj(h, w_qkv, 0, nq + nkv, COL_TILE, rope=(win_cos, win_sin, nq, 1.0, win_hd))
            v = _mm_proj(h, w_qkv, nq + nkv, nkv, COL_TILE)
            y = _win_attn(qk, v, win_sink[j], n_lat=n_lat, n_ctx=n_ctx, heads=WIN_HEADS,
                          kv_heads=WIN_KV_HEADS, hd=win_hd)
            w_out = win_w_o[j]
        else:
            cg = d // FOURIER_GROUPS
            cc, sc = _cos_sin(cg)
            w_ch = jnp.asarray(np.concatenate([cc, sc], axis=1) / math.sqrt(cg), BF16)
            uv = _mm_groups(h, w_ch, FOURIER_GROUPS)
            y = _fourier_positions(uv, n_lat, n_ctx, d)
            w_out = fno_w_o[j]
        xs = _mm_res(y, w_out.astype(BF16), xs, g1[:, 0], g1[:, 1], n_lat)
        h2, aff = _norm_mod(xs, norm_gain[i, 1], sh2, sc2, n_lat, router_w=router_w[i])
        token_sets = [(0, n_lat, 0)] + ([] if last else [(n_lat, n_ctx, 1)])
        xs = _moe(xs, h2, aff, g2, exp_w_gate, exp_w_up, exp_w_down, i, token_sets)

    zeros = jnp.zeros((b, 2, 1, d), F32)
    return _norm_mod(xs, final_gain, zeros, zeros, n_lat, out_dtype=F32, rows=n_lat)
```

```python
import functools
import math

import numpy as np
import jax
import jax.numpy as jnp
from jax import lax
from jax.experimental import pallas as pl
from jax.experimental.pallas import tpu as pltpu

F32 = jnp.float32
BF16 = jnp.bfloat16
I32 = jnp.int32

GRID_W = 64
N_MIXERS = 3
RET_HEADS = 8
RET_CHUNK = 128
WIN_HEADS = 16
WIN_KV_HEADS = 4
WIN_BLOCK = 128
FOURIER_GROUPS = 4
FFT_INNER = 64
N_EXPERTS = 16
CAPACITY_FACTOR = 2
ROPE_BASE = 10000.0
NORM_EPS = 1e-6
NEG_INF = -1e30

LANES = 128
ROUTE_TILE = 128
SLOT_ALIGN = 16
ROW_TILE = 768
MM_ROW_CHUNK = 256
COL_TILE = 512
RESIDENT_WEIGHT_BYTES = 8 * 1024 * 1024
PROJ_COL_TILE = 2048
NORM_TILES = (768, 512, 256)
VMEM_LIMIT_BYTES = 56 * 1024 * 1024


def _params(sem):
    return pltpu.CompilerParams(dimension_semantics=sem, vmem_limit_bytes=VMEM_LIMIT_BYTES)


def _dot(a, b):
    return jnp.dot(a, b, preferred_element_type=F32)


def _dot_nt(a, b):
    return lax.dot_general(a, b, (((1,), (1,)), ((), ())), preferred_element_type=F32)


def _split_bf16(x):
    hi = x.astype(BF16)
    lo = (x - hi.astype(F32)).astype(BF16)
    return hi, lo


def _silu(x):
    return x / (1.0 + jnp.exp(-x))


def _mod_kernel(c_ref, w_ref, b_ref, o_ref):
    s = _silu(c_ref[...])
    sh, sl = _split_bf16(s)
    wh, wl = _split_bf16(w_ref[0])
    o_ref[0] = _dot(sh, wh) + _dot(sl, wh) + _dot(sh, wl) + b_ref[0]


def _modulation(cvec, w_mod, b_mod):
    depth, d, n = w_mod.shape
    tn = 1024
    return pl.pallas_call(
        _mod_kernel,
        name="modulation",
        grid=(depth, n // tn),
        in_specs=[
            pl.BlockSpec((8, d), lambda i, j: (0, 0)),
            pl.BlockSpec((1, d, tn), lambda i, j: (i, 0, j)),
            pl.BlockSpec((1, 1, tn), lambda i, j: (i, 0, j)),
        ],
        out_specs=pl.BlockSpec((1, 8, tn), lambda i, j: (i, 0, j)),
        out_shape=jax.ShapeDtypeStruct((depth, 8, n), F32),
        compiler_params=_params(("parallel", "parallel")),
    )(cvec, w_mod, b_mod.reshape(depth, 1, n))


def _normed(x_ref, gain_ref, shift_ref, scale_ref, n_lat):
    x = x_ref[0]
    tr = x.shape[0]
    ms = jnp.mean(x * x, axis=-1, keepdims=True)
    y = x * lax.rsqrt(ms + NORM_EPS) * gain_ref[...]
    is_lat = pl.program_id(1) * tr + lax.broadcasted_iota(I32, (tr, 1), 0) < n_lat
    scale = jnp.where(is_lat, scale_ref[0, 0], scale_ref[0, 1])
    shift = jnp.where(is_lat, shift_ref[0, 0], shift_ref[0, 1])
    return y * (1.0 + scale) + shift


def _norm_mod_kernel(x_ref, gain_ref, shift_ref, scale_ref, o_ref, *, n_lat):
    o_ref[0] = _normed(x_ref, gain_ref, shift_ref, scale_ref, n_lat).astype(o_ref.dtype)


def _pack_bf16_pairs(h):
    half = h.shape[1] // 2
    bits = lax.bitcast_convert_type(h.astype(BF16).astype(F32), jnp.uint32)
    word = lax.shift_right_logical(bits[:, :half], jnp.uint32(16)) | (bits[:, half:] & jnp.uint32(0xFFFF0000))
    return lax.bitcast_convert_type(word, I32)


def _unpack_bf16_pairs(word):
    bits = lax.bitcast_convert_type(word, jnp.uint32)
    lo = lax.bitcast_convert_type(lax.shift_left(bits, jnp.uint32(16)), F32)
    hi = lax.bitcast_convert_type(bits & jnp.uint32(0xFFFF0000), F32)
    return jnp.concatenate([lo, hi], axis=1).astype(BF16)


def _norm_router_kernel(x_ref, gain_ref, shift_ref, scale_ref, wr_ref, o_ref, aff_ref, *, n_lat):
    h = _normed(x_ref, gain_ref, shift_ref, scale_ref, n_lat)
    packed = _pack_bf16_pairs(h)
    o_ref[0] = packed.reshape(o_ref.shape[1:])
    hh, hl = _split_bf16(h)
    wh, wl = _split_bf16(wr_ref[...])
    logits = _dot(hh, wh) + _dot(hl, wh) + _dot(hh, wl)
    lane = lax.broadcasted_iota(I32, logits.shape, 1)
    valid = lane < N_EXPERTS
    logits = jnp.where(valid, logits, -jnp.inf)
    m = jnp.max(logits, axis=-1, keepdims=True)
    p = jnp.exp(logits - m)
    aff = p / jnp.sum(p, axis=-1, keepdims=True)
    aff_ref[0] = jnp.where(valid, aff, 0.0)


def _norm_mod(x, gain, shift, scale, n_lat, *, out_dtype=BF16, rows=None, router_w=None):
    b, s, d = x.shape
    rows = s if rows is None else rows
    tr = next(t for t in NORM_TILES if rows % t == 0 and s % t == 0)
    both = lambda bi, t: (bi, 0, 0, 0)
    in_specs = [
        pl.BlockSpec((1, tr, d), lambda bi, t: (bi, t, 0)),
        pl.BlockSpec((1, d), lambda bi, t: (0, 0)),
        pl.BlockSpec((1, 2, 1, d), both),
        pl.BlockSpec((1, 2, 1, d), both),
    ]
    args = [x, gain.reshape(1, d), shift, scale]
    out_specs = pl.BlockSpec((1, tr, d), lambda bi, t: (bi, t, 0))
    out_shape = jax.ShapeDtypeStruct((b, rows, d), out_dtype)
    kern = functools.partial(_norm_mod_kernel, n_lat=n_lat)
    if router_w is not None:
        wr = jnp.zeros((d, LANES), F32).at[:, :N_EXPERTS].set(router_w)
        in_specs.append(pl.BlockSpec((d, LANES), lambda bi, t: (0, 0)))
        args.append(wr)
        slab = (d // 2 // LANES, LANES)
        out_specs = [pl.BlockSpec((1, tr) + slab, lambda bi, t: (bi, t, 0, 0)),
                     pl.BlockSpec((1, tr, LANES), lambda bi, t: (bi, t, 0))]
        out_shape = [jax.ShapeDtypeStruct((b, rows) + slab, I32),
                     jax.ShapeDtypeStruct((b, rows, LANES), F32)]
        kern = functools.partial(_norm_router_kernel, n_lat=n_lat)
    return pl.pallas_call(
        kern,
        name="norm_mod" if router_w is None else "norm_router",
        grid=(b, rows // tr),
        in_specs=in_specs,
        out_specs=out_specs,
        out_shape=out_shape,
        compiler_params=_params(("parallel", "parallel")),
    )(*args)


def _rope_partner(xs, quarter):
    if 2 * quarter == LANES:
        return pltpu.roll(xs, quarter, 1)
    back = pltpu.roll(xs, quarter, 1)
    fwd = pltpu.roll(xs, LANES - quarter, 1)
    lane = lax.broadcasted_iota(I32, xs.shape, 1)
    return jnp.where((lane % (2 * quarter)) < quarter, fwd, back)


def _row_chunks(tm):
    return [slice(r, r + MM_ROW_CHUNK) for r in range(0, tm, MM_ROW_CHUNK)]


def _mm_rope_kernel(a_ref, w_ref, cos_ref, sin_ref, o_ref, *, n_q, qscale, kscale, head_dim):
    sc = jnp.where(pl.program_id(1) >= n_q, kscale, qscale).astype(F32)
    for rows in _row_chunks(a_ref.shape[1]):
        acc = _dot(a_ref[0, rows, :], w_ref[...])
        for s in range(acc.shape[1] // LANES):
            cols = slice(s * LANES, (s + 1) * LANES)
            off = (s * LANES) % head_dim
            xs = acc[:, cols]
            rot = xs * cos_ref[rows, off:off + LANES] + _rope_partner(xs, head_dim // 4) * sin_ref[rows, off:off + LANES]
            o_ref[0, rows, cols] = (rot * sc).astype(o_ref.dtype)


def _mm_plain_kernel(a_ref, w_ref, o_ref):
    for rows in _row_chunks(a_ref.shape[1]):
        o_ref[0, rows, :] = _dot(a_ref[0, rows, :], w_ref[...]).astype(o_ref.dtype)


def _mm_proj(a, w, col0, ncols, tn, rope=None):
    b, s, k = a.shape
    tm = ROW_TILE
    assert col0 % tn == 0 and ncols % tn == 0
    in_specs = [
        pl.BlockSpec((1, tm, k), lambda bi, j, i: (bi, i, 0)),
        pl.BlockSpec((k, tn), lambda bi, j, i: (0, col0 // tn + j)),
    ]
    args = [a, w]
    if rope is None:
        kern, name = _mm_plain_kernel, "mm_proj"
    else:
        cos_t, sin_t, n_q_cols, qscale, kscale, head_dim = rope
        assert tn % head_dim == 0 and n_q_cols % tn == 0
        kern = functools.partial(_mm_rope_kernel, n_q=n_q_cols // tn, qscale=qscale, kscale=kscale,
                                 head_dim=head_dim)
        name = "mm_proj_rope"
        in_specs += [pl.BlockSpec((tm, head_dim), lambda bi, j, i: (i, 0))] * 2
        args += [cos_t, sin_t]
    return pl.pallas_call(
        kern,
        name=name,
        grid=(b, ncols // tn, s // tm),
        in_specs=in_specs,
        out_specs=pl.BlockSpec((1, tm, tn), lambda bi, j, i: (bi, i, j)),
        out_shape=jax.ShapeDtypeStruct((b, s, ncols), BF16),
        compiler_params=_params(("parallel", "parallel", "arbitrary")),
    )(*args)


def _mm_res_kernel(a_ref, w_ref, x_ref, gl_ref, gc_ref, o_ref, *, n_lat):
    tm = a_ref.shape[1]
    for rows in _row_chunks(tm):
        acc = _dot(a_ref[0, rows, :], w_ref[...])
        row = pl.program_id(2) * tm + rows.start + lax.broadcasted_iota(I32, (acc.shape[0], 1), 0)
        gate = jnp.where(row < n_lat, gl_ref[0], gc_ref[0])
        o_ref[0, rows, :] = x_ref[0, rows, :] + gate * acc


def _mm_res(a, w, x, gate_lat, gate_ctx, n_lat):
    b, s, k = a.shape
    n = w.shape[1]
    tm = ROW_TILE
    tn = n
    while k * tn * 2 > RESIDENT_WEIGHT_BYTES:
        tn //= 2
    return pl.pallas_call(
        functools.partial(_mm_res_kernel, n_lat=n_lat),
        name="mm_residual",
        grid=(b, n // tn, s // tm),
        in_specs=[
            pl.BlockSpec((1, tm, k), lambda bi, j, i: (bi, i, 0)),
            pl.BlockSpec((k, tn), lambda bi, j, i: (0, j)),
            pl.BlockSpec((1, tm, tn), lambda bi, j, i: (bi, i, j)),
            pl.BlockSpec((1, 1, tn), lambda bi, j, i: (bi, 0, j)),
            pl.BlockSpec((1, 1, tn), lambda bi, j, i: (bi, 0, j)),
        ],
        out_specs=pl.BlockSpec((1, tm, tn), lambda bi, j, i: (bi, i, j)),
        out_shape=jax.ShapeDtypeStruct((b, s, n), F32),
        compiler_params=_params(("parallel", "parallel", "arbitrary")),
    )(a, w, x, gate_lat, gate_ctx)


def _mm_groups_kernel(a_ref, w_ref, o_ref):
    o_ref[0] = _dot(a_ref[0], w_ref[...]).astype(o_ref.dtype)


def _mm_groups(a, w, groups):
    b, s, d = a.shape
    cg = d // groups
    tm = ROW_TILE
    return pl.pallas_call(
        _mm_groups_kernel,
        name="mm_channel_dft",
        grid=(b, s // tm, 2 * groups),
        in_specs=[
            pl.BlockSpec((1, tm, cg), lambda bi, i, j: (bi, i, j % groups)),
            pl.BlockSpec((cg, cg), lambda bi, i, j: (0, j // groups)),
        ],
        out_specs=pl.BlockSpec((1, tm, cg), lambda bi, i, j: (bi, i, j)),
        out_shape=jax.ShapeDtypeStruct((b, s, 2 * d), F32),
        compiler_params=_params(("parallel", "parallel", "arbitrary")),
    )(a, w)


def _ret_scan_kernel(*refs, heads, dk, dv, reverse, add_in):
    if add_in:
        dec_ref, q_ref, k_ref, v_ref, g_ref, yin_ref, o_ref, s_ref, qd_ref, kd_ref, in_ref, cd_ref = refs
    else:
        dec_ref, q_ref, k_ref, v_ref, g_ref, o_ref, s_ref, qd_ref, kd_ref, in_ref, cd_ref = refs
        yin_ref = None
    c = q_ref.shape[1]
    j = pl.program_id(1)

    @pl.when(j == 0)
    def _():
        s_ref[...] = jnp.zeros_like(s_ref)
        m_col = lax.broadcasted_iota(I32, (c, LANES), 0).astype(F32)
        m_row = lax.broadcasted_iota(I32, (c, c), 0).astype(F32)
        n_row = lax.broadcasted_iota(I32, (c, c), 1).astype(F32)
        for h in range(heads):
            lg = -jnp.exp(dec_ref[:, h:h + 1])
            if reverse:
                q_pow, k_pow, diff = c - m_col, m_col, n_row - m_row
            else:
                q_pow, k_pow, diff = m_col + 1.0, c - 1.0 - m_col, m_row - n_row
            qd_ref[h] = jnp.exp(lg * q_pow)
            kd_ref[h] = jnp.exp(lg * k_pow)
            in_ref[h] = jnp.where(diff >= 0, jnp.exp(lg * jnp.maximum(diff, 0.0)), 0.0)
            cd_ref[h] = jnp.exp(jnp.broadcast_to(lg, (8, LANES)) * float(c))

    for h in range(heads):
        q = q_ref[0, :, h * dk:(h + 1) * dk]
        k = k_ref[0, :, h * dk:(h + 1) * dk]
        v = v_ref[0, :, h * dv:(h + 1) * dv]
        qdec = jnp.concatenate([qd_ref[h]] * (dk // LANES), axis=1)
        kdec = jnp.concatenate([kd_ref[h]] * (dk // LANES), axis=1)
        state = s_ref[h]
        cross = _dot((q.astype(F32) * qdec).astype(BF16), state.astype(BF16))
        scores = _dot_nt(q, k) * in_ref[h]
        o = cross + _dot(scores.astype(BF16), v)
        k_t = (k.astype(F32) * kdec).T.astype(BF16)
        s_ref[h] = state * cd_ref[h][0:1, 0:1] + _dot(k_t, v)
        mu = jnp.mean(o, axis=-1, keepdims=True)
        cen = o - mu
        var = jnp.mean(cen * cen, axis=-1, keepdims=True)
        g = g_ref[0, :, h * dv:(h + 1) * dv].astype(F32)
        y = cen * lax.rsqrt(var + NORM_EPS) * _silu(g)
        if add_in:
            y = y + yin_ref[0, :, h * dv:(h + 1) * dv].astype(F32)
        o_ref[0, :, h * dv:(h + 1) * dv] = y.astype(o_ref.dtype)


def _ret_scan(qk, vg, decay_row, y_in, *, n_lat, n_ctx, heads, dk, dv, reverse):
    b, s, _ = qk.shape
    c = RET_CHUNK
    lat_chunks, ctx_chunks = n_lat // c, n_ctx // c
    steps = lat_chunks + ctx_chunks
    hk, hv = heads * dk, heads * dv
    if reverse:
        chunk = lambda j: steps - 1 - j
    else:
        chunk = lambda j: jnp.where(j < ctx_chunks, lat_chunks + j, j - ctx_chunks)
    gate_blk = 2 if reverse else 1
    in_specs = [
        pl.BlockSpec((1, heads), lambda bi, j: (0, 0)),
        pl.BlockSpec((1, c, hk), lambda bi, j: (bi, chunk(j), 0)),
        pl.BlockSpec((1, c, hk), lambda bi, j: (bi, chunk(j), 1)),
        pl.BlockSpec((1, c, hv), lambda bi, j: (bi, chunk(j), 0)),
        pl.BlockSpec((1, c, hv), lambda bi, j: (bi, chunk(j), gate_blk)),
    ]
    args = [decay_row, qk, qk, vg, vg]
    if y_in is not None:
        in_specs.append(pl.BlockSpec((1, c, hv), lambda bi, j: (bi, chunk(j), 0)))
        args.append(y_in)
    kern = functools.partial(_ret_scan_kernel, heads=heads, dk=dk, dv=dv, reverse=reverse,
                             add_in=y_in is not None)
    return pl.pallas_call(
        kern,
        name="ret_scan_bwd" if reverse else "ret_scan_fwd",
        grid=(b, steps),
        in_specs=in_specs,
        out_specs=pl.BlockSpec((1, c, hv), lambda bi, j: (bi, chunk(j), 0)),
        out_shape=jax.ShapeDtypeStruct((b, s, hv), BF16),
        scratch_shapes=[
            pltpu.VMEM((heads, dk, dv), F32),
            pltpu.VMEM((heads, c, LANES), F32),
            pltpu.VMEM((heads, c, LANES), F32),
            pltpu.VMEM((heads, c, c), F32),
            pltpu.VMEM((heads, 8, LANES), F32),
        ],
        compiler_params=_params(("parallel", "arbitrary")),
    )(*args)


def _win_attn_kernel(sink_ref, q_ref, kc_ref, vc_ref, kp_ref, kq_ref, kn_ref, vp_ref, vq_ref, vn_ref,
                     o_ref, *, lat_tiles, heads, kv_heads, hd):
    qt = pl.program_id(1)
    blk = q_ref.shape[1]
    n_ctx = kc_ref.shape[1]
    grp = heads // kv_heads
    rows = grp * blk
    qi = lax.broadcasted_iota(I32, (rows, blk), 0) % blk
    kj = lax.broadcasted_iota(I32, (rows, blk), 1)
    tq = qt + jnp.zeros((rows, blk), I32)
    ok_cur = tq < lat_tiles
    ok_prev = (kj >= qi) & ok_cur & (tq >= 1)
    ok_next = (kj <= qi) & (tq + 1 < lat_tiles)
    bias = jnp.concatenate(
        [jnp.zeros((rows, n_ctx), F32)]
        + [jnp.where(ok, 0.0, NEG_INF).astype(F32) for ok in (ok_prev, ok_cur, ok_next)], axis=1)
    head_row = lax.broadcasted_iota(I32, (rows, 1), 0) // blk
    for kv in range(kv_heads):
        cs = slice(kv * hd, (kv + 1) * hd)
        keys = jnp.concatenate([kc_ref[0, :, cs], kp_ref[0, :, cs], kq_ref[0, :, cs], kn_ref[0, :, cs]], axis=0)
        vals = jnp.concatenate([vc_ref[0, :, cs], vp_ref[0, :, cs], vq_ref[0, :, cs], vn_ref[0, :, cs]], axis=0)
        q = jnp.concatenate([q_ref[0, :, (kv * grp + g) * hd:(kv * grp + g + 1) * hd] for g in range(grp)], axis=0)
        sink = jnp.zeros((rows, 1), F32)
        for g in range(grp):
            h = kv * grp + g
            sink = jnp.where(head_row == g, sink_ref[:, h:h + 1], sink)
        s = _dot_nt(q, keys) + bias
        m = jnp.maximum(jnp.max(s, axis=-1, keepdims=True), sink)
        p = jnp.exp(s - m)
        den = jnp.sum(p, axis=-1, keepdims=True) + jnp.exp(sink - m)
        o = _dot(p.astype(BF16), vals) / den
        for g in range(grp):
            h = kv * grp + g
            o_ref[0, :, h * hd:(h + 1) * hd] = o[g * blk:(g + 1) * blk].astype(o_ref.dtype)


def _win_attn(qk, v, sink, *, n_lat, n_ctx, heads, kv_heads, hd):
    b, s, _ = qk.shape
    blk = WIN_BLOCK
    lat_tiles = n_lat // blk
    tiles = s // blk
    kvw = kv_heads * hd
    k_col = (heads * hd) // kvw
    v_col = 0
    ctx_blk = n_lat // n_ctx
    prev = lambda t: jnp.maximum(t - 1, 0)
    nxt = lambda t: jnp.minimum(t + 1, tiles - 1)
    sink_row = jnp.zeros((1, LANES), F32).at[0, :heads].set(sink.astype(F32))
    kern = functools.partial(_win_attn_kernel, lat_tiles=lat_tiles, heads=heads, kv_heads=kv_heads, hd=hd)
    return pl.pallas_call(
        kern,
        name="win_attn",
        grid=(b, tiles),
        in_specs=[
            pl.BlockSpec((1, LANES), lambda bi, t: (0, 0)),
            pl.BlockSpec((1, blk, heads * hd), lambda bi, t: (bi, t, 0)),
            pl.BlockSpec((1, n_ctx, kvw), lambda bi, t: (bi, ctx_blk, k_col)),
            pl.BlockSpec((1, n_ctx, kvw), lambda bi, t: (bi, ctx_blk, v_col)),
            pl.BlockSpec((1, blk, kvw), lambda bi, t: (bi, prev(t), k_col)),
            pl.BlockSpec((1, blk, kvw), lambda bi, t: (bi, t, k_col)),
            pl.BlockSpec((1, blk, kvw), lambda bi, t: (bi, nxt(t), k_col)),
            pl.BlockSpec((1, blk, kvw), lambda bi, t: (bi, prev(t), v_col)),
            pl.BlockSpec((1, blk, kvw), lambda bi, t: (bi, t, v_col)),
            pl.BlockSpec((1, blk, kvw), lambda bi, t: (bi, nxt(t), v_col)),
        ],
        out_specs=pl.BlockSpec((1, blk, heads * hd), lambda bi, t: (bi, t, 0)),
        out_shape=jax.ShapeDtypeStruct((b, s, heads * hd), BF16),
        compiler_params=_params(("parallel", "parallel")),
    )(sink_row, qk, qk, v, qk, qk, qk, v, v, v)


SUBLANES = 8
PACKED_ROWS = 16


def _dft_stage_a_kernel(m_ref, u_ref, v_ref, re_ref, im_ref):
    n1, sub, tc = u_ref.shape[1:]
    rows = n1 * sub
    stacked = jnp.concatenate([u_ref[0].reshape(rows, tc), v_ref[0].reshape(rows, tc)], axis=0)
    out = _dot(m_ref[...], stacked.astype(BF16))
    re_ref[0] = out[:rows].reshape(n1, sub, tc)
    im_ref[0] = out[rows:].reshape(n1, sub, tc)


def _dft_stage_b_kernel(m_ref, re_ref, im_ref, twc_ref, tws_ref, o_ref):
    n2, sub, tc = o_ref.shape[1:]
    reps = tc // LANES
    ar, ai = re_ref[0], im_ref[0]
    twc = jnp.concatenate([twc_ref[...]] * reps, axis=1)
    tws = jnp.concatenate([tws_ref[...]] * reps, axis=1)
    stacked = jnp.concatenate([ar * twc + ai * tws, ai * twc - ar * tws], axis=0).astype(BF16)
    out = _dot(m_ref[...], stacked)
    o_ref[0] = out.reshape(n2, sub, tc).astype(o_ref.dtype)


def _dft_ctx_kernel(m_ref, x_ref, o_ref, *, d):
    x = x_ref[0]
    stacked = jnp.concatenate([x[:, :d], x[:, d:]], axis=0).astype(BF16)
    o_ref[0] = _dot(m_ref[...], stacked).astype(o_ref.dtype)


def _cos_sin(n):
    ang = 2.0 * np.pi * np.outer(np.arange(n), np.arange(n)) / float(n)
    return np.cos(ang), np.sin(ang)


def _fourier_positions(uv, n_lat, n_ctx, d):
    b, s, _ = uv.shape
    n2 = FFT_INNER
    n1 = n_lat // n2
    tc = min(512, d)
    ca, sa = _cos_sin(n1)
    eye = np.eye(SUBLANES)
    ka, ks = np.kron(ca, eye) / math.sqrt(n1), np.kron(sa, eye) / math.sqrt(n1)
    mat_a = jnp.asarray(np.block([[ka, -ks], [-ks, -ka]]), BF16)
    uv4 = uv.reshape(b, s // n2, n2, 2 * d)
    blk_a = (1, n1, SUBLANES, tc)
    a_re, a_im = pl.pallas_call(
        _dft_stage_a_kernel,
        name="dft_stage_a",
        grid=(b, n2 // SUBLANES, d // tc),
        in_specs=[
            pl.BlockSpec(mat_a.shape, lambda bi, cg, jc: (0, 0)),
            pl.BlockSpec(blk_a, lambda bi, cg, jc: (bi, 0, cg, jc)),
            pl.BlockSpec(blk_a, lambda bi, cg, jc: (bi, 0, cg, d // tc + jc)),
        ],
        out_specs=[pl.BlockSpec(blk_a, lambda bi, cg, jc: (bi, 0, cg, jc))] * 2,
        out_shape=[jax.ShapeDtypeStruct((b, n1, n2, d), F32)] * 2,
        compiler_params=_params(("parallel", "parallel", "parallel")),
    )(mat_a, uv4, uv4)
    cb, sb = _cos_sin(n2)
    eye = np.eye(PACKED_ROWS)
    kron_b = lambda m: np.einsum("kc,ab->kabc", m, eye).reshape(n2 * PACKED_ROWS, PACKED_ROWS * n2)
    mat_b = jnp.asarray(np.concatenate([kron_b(cb), kron_b(sb)], axis=1) / math.sqrt(n2), BF16)
    phi = 2.0 * np.pi * np.outer(np.arange(n1), np.arange(n2)) / float(n_lat)
    twc = jnp.asarray(np.repeat(np.cos(phi).reshape(-1, 1), LANES, axis=1), F32)
    tws = jnp.asarray(np.repeat(np.sin(phi).reshape(-1, 1), LANES, axis=1), F32)
    rows = PACKED_ROWS * n2
    y = pl.pallas_call(
        _dft_stage_b_kernel,
        name="dft_stage_b",
        grid=(b, n1 // PACKED_ROWS, d // tc),
        in_specs=[
            pl.BlockSpec(mat_b.shape, lambda bi, kb, jc: (0, 0)),
            pl.BlockSpec((1, rows, tc), lambda bi, kb, jc: (bi, kb, jc)),
            pl.BlockSpec((1, rows, tc), lambda bi, kb, jc: (bi, kb, jc)),
            pl.BlockSpec((rows, LANES), lambda bi, kb, jc: (kb, 0)),
            pl.BlockSpec((rows, LANES), lambda bi, kb, jc: (kb, 0)),
        ],
        out_specs=pl.BlockSpec((1, n2, PACKED_ROWS, tc), lambda bi, kb, jc: (bi, 0, kb, jc)),
        out_shape=jax.ShapeDtypeStruct((b, n2, n1, d), BF16),
        compiler_params=_params(("parallel", "parallel", "parallel")),
    )(mat_b, a_re.reshape(b, n_lat, d), a_im.reshape(b, n_lat, d), twc, tws)
    cc, sc = _cos_sin(n_ctx)
    mat_c = jnp.asarray(np.concatenate([cc, -sc], axis=1) / math.sqrt(n_ctx), BF16)
    y_ctx = pl.pallas_call(
        functools.partial(_dft_ctx_kernel, d=d),
        name="dft_ctx",
        grid=(b,),
        in_specs=[
            pl.BlockSpec(mat_c.shape, lambda bi: (0, 0)),
            pl.BlockSpec((1, n_ctx, 2 * d), lambda bi: (bi, n_lat // n_ctx, 0)),
        ],
        out_specs=pl.BlockSpec((1, n_ctx, d), lambda bi: (bi, 0, 0)),
        out_shape=jax.ShapeDtypeStruct((b, n_ctx, d), BF16),
        compiler_params=_params(("parallel",)),
    )(mat_c, uv)
    return jnp.concatenate([y.reshape(b, n_lat, d), y_ctx], axis=1)


LANE_SHIFT = 7
GATHER_UNROLL = 8
TILES_PER_STEP = 4
FFN_ROW_CHUNK = 384
COMBINE_REGION = 64


def _route_kernel(aff_ref, tri_ref, posc_ref, gate_ref, tst_ref,
                  thr_ref, need_ref, ctie_ref, cpos_ref, *, cap):
    step = pl.program_id(1)
    tile = tri_ref.shape[0]
    tps = posc_ref.shape[1] // tile

    @pl.when(step == 0)
    def _():
        def body(it, thr):
            bits = lax.bitcast_convert_type(aff_ref[0], I32)
            cand = thr | jnp.left_shift(jnp.int32(1), 30 - it)
            cnt = jnp.sum(jnp.where(bits >= cand, 1.0, 0.0), axis=0, keepdims=True)
            return jnp.where(cnt >= cap, cand, thr)

        thr = lax.fori_loop(0, 31, body, jnp.zeros((1, LANES), I32))
        bits = lax.bitcast_convert_type(aff_ref[0], I32)
        above = jnp.sum(jnp.where(bits > thr, 1.0, 0.0), axis=0, keepdims=True)
        thr_ref[...] = thr
        need_ref[...] = float(cap) - above
        ctie_ref[...] = jnp.zeros_like(ctie_ref)
        cpos_ref[...] = jnp.zeros_like(cpos_ref)

    thr = thr_ref[...]
    for u in range(tps):
        rows = slice(u * tile, (u + 1) * tile)
        a = aff_ref[0, pl.ds(pl.multiple_of((step * tps + u) * tile, tile), tile), :]
        bits = lax.bitcast_convert_type(a, I32)
        gt = bits > thr
        eq = bits == thr
        eqf = jnp.where(eq, 1.0, 0.0)
        tie_rank = _dot(tri_ref[...], eqf.astype(BF16)) + ctie_ref[...]
        sel = gt | (eq & (tie_rank < need_ref[...]))
        self_ = jnp.where(sel, 1.0, 0.0)
        start = cpos_ref[...]
        pos = jnp.where(sel, _dot(tri_ref[...], self_.astype(BF16)) + start, -1.0)
        posc_ref[0, rows, :] = pos.astype(I32)
        gate_ref[0, rows, :] = jnp.where(sel, a, 0.0)
        tst_ref[0, u] = jnp.broadcast_to(start, (8, LANES)).astype(I32)
        ctie_ref[...] = ctie_ref[...] + jnp.sum(eqf, axis=0, keepdims=True)
        cpos_ref[...] = start + jnp.sum(self_, axis=0, keepdims=True)


def _route(aff, *, row_off, n, cap):
    b = aff.shape[0]
    tile = ROUTE_TILE
    nt = n // tile
    tps = min(TILES_PER_STEP, nt)
    tri = jnp.asarray(np.tril(np.ones((tile, tile)), -1), BF16)
    return pl.pallas_call(
        functools.partial(_route_kernel, cap=cap),
        name="moe_route",
        grid=(b, nt // tps),
        in_specs=[
            pl.BlockSpec((1, n, LANES), lambda bi, t: (bi, row_off // n, 0)),
            pl.BlockSpec((tile, tile), lambda bi, t: (0, 0)),
        ],
        out_specs=[
            pl.BlockSpec((1, tps * tile, LANES), lambda bi, t: (bi, t, 0)),
            pl.BlockSpec((1, tps * tile, LANES), lambda bi, t: (bi, t, 0)),
            pl.BlockSpec((1, tps, 8, LANES), lambda bi, t: (bi, t, 0, 0)),
        ],
        out_shape=[
            jax.ShapeDtypeStruct((b, n, LANES), I32),
            jax.ShapeDtypeStruct((b, n, LANES), F32),
            jax.ShapeDtypeStruct((b, nt, 8, LANES), I32),
        ],
        scratch_shapes=[pltpu.VMEM((1, LANES), I32)] + [pltpu.VMEM((1, LANES), F32)] * 3,
        compiler_params=_params(("parallel", "arbitrary")),
    )(aff, tri)


def _slot_index_kernel(ts_ref, posc_ref, idx_ref, *, nt, row_off, tile):
    bi, step = pl.program_id(0), pl.program_id(1)

    @pl.when(step == 0)
    def _():
        idx_ref[...] = jnp.zeros_like(idx_ref)

    tps = posc_ref.shape[1] // tile
    slot_rows = idx_ref.shape[2]
    lane = lax.broadcasted_iota(I32, (tile, 2 * LANES), 1)
    for u in range(tps):
        t = step * tps + u
        tok = (row_off + t * tile + lax.broadcasted_iota(I32, (tile, 1), 0)).astype(F32)
        for e in range(N_EXPERTS):
            h0 = jnp.minimum(ts_ref[(bi * nt + t) * N_EXPERTS + e] >> LANE_SHIFT, slot_rows - 2)
            hit = (posc_ref[0, u * tile:(u + 1) * tile, e:e + 1] - h0 * LANES) == lane
            vals = jnp.sum(jnp.where(hit, tok, 0.0), axis=0, keepdims=True)
            two_rows = jnp.concatenate([vals[:, :LANES], vals[:, LANES:]], axis=0).astype(I32)
            idx_ref[0, e, pl.ds(h0, 2), :] = idx_ref[0, e, pl.ds(h0, 2), :] + two_rows


def _slot_index(tstart, posc, *, row_off, n, cap):
    b = posc.shape[0]
    tile = ROUTE_TILE
    nt = n // tile
    slot_rows = max(cap, 2 * LANES) // LANES
    tps = min(TILES_PER_STEP, nt)
    grid_spec = pltpu.PrefetchScalarGridSpec(
        num_scalar_prefetch=1,
        grid=(b, nt // tps),
        in_specs=[pl.BlockSpec((1, tps * tile, LANES), lambda bi, t, ts: (bi, t, 0))],
        out_specs=pl.BlockSpec((1, N_EXPERTS, slot_rows, LANES), lambda bi, t, ts: (bi, 0, 0, 0)),
    )
    idx = pl.pallas_call(
        functools.partial(_slot_index_kernel, nt=nt, row_off=row_off, tile=tile),
        name="moe_slot_index",
        grid_spec=grid_spec,
        out_shape=jax.ShapeDtypeStruct((b, N_EXPERTS, slot_rows, LANES), I32),
        compiler_params=_params(("parallel", "arbitrary")),
    )(tstart, posc)
    return idx.reshape(b, N_EXPERTS, slot_rows * LANES)


def _gather_kernel(idx_ref, h_ref, o_ref, buf_ref, sem):
    bi = pl.program_id(1)
    n_rows = buf_ref.shape[0]

    def row_copy(src_row, dst_row, rows):
        return pltpu.make_async_copy(h_ref.at[bi, pl.ds(src_row, rows)], buf_ref.at[pl.ds(dst_row, rows)], sem)

    def issue(g, carry):
        for k in range(GATHER_UNROLL):
            s = g * GATHER_UNROLL + k
            row_copy(idx_ref[0, 0, s], s, 1).start(priority=k % 2)
        return carry

    lax.fori_loop(0, n_rows // GATHER_UNROLL, issue, 0)
    row_copy(0, 0, n_rows).wait()
    o_ref[0, 0] = _unpack_bf16_pairs(buf_ref[...].reshape(n_rows, -1))


def _gather(idx, h):
    b, _, sub, lanes = h.shape
    d = 2 * sub * lanes
    slots = idx.shape[2]
    assert slots % GATHER_UNROLL == 0
    return pl.pallas_call(
        _gather_kernel,
        name="moe_gather",
        grid=(N_EXPERTS, b),
        in_specs=[
            pl.BlockSpec((1, 1, slots), lambda ei, bi: (bi * N_EXPERTS + ei, 0, 0), memory_space=pltpu.SMEM),
            pl.BlockSpec(memory_space=pl.ANY),
        ],
        out_specs=pl.BlockSpec((1, 1, slots, d), lambda ei, bi: (ei, bi, 0, 0)),
        out_shape=jax.ShapeDtypeStruct((N_EXPERTS, b, slots, d), BF16),
        scratch_shapes=[pltpu.VMEM((slots, sub, lanes), I32), pltpu.SemaphoreType.DMA(())],
        compiler_params=_params(("arbitrary", "arbitrary")),
    )(idx.reshape(b * N_EXPERTS, 1, slots), h)


def _ffn_kernel(x_ref, wg_ref, wu_ref, wd_ref, ye_ref, hm_ref, *, n_up):
    st = pl.program_id(1)
    bsz, rows, d = x_ref.shape[1:]
    tf = wg_ref.shape[3]
    chunk = FFN_ROW_CHUNK if rows % FFN_ROW_CHUNK == 0 else rows
    spans = [(bi, r0) for bi in range(bsz) for r0 in range(0, rows, chunk)]

    @pl.when(st < n_up)
    def _():
        wg = wg_ref[0, 0].astype(BF16)
        wu = wu_ref[0, 0].astype(BF16)
        for bi, r0 in spans:
            x = x_ref[0, bi, r0:r0 + chunk, :]
            hm_ref[st, bi * rows + r0:bi * rows + r0 + chunk, :] = (_silu(_dot(x, wg)) * _dot(x, wu)).astype(BF16)

    @pl.when(st >= n_up)
    def _():
        wd = [wd_ref[0, 0, c * tf:(c + 1) * tf, :].astype(BF16) for c in range(n_up)]
        for bi, r0 in spans:
            m0 = bi * rows + r0
            y = _dot(hm_ref[0, m0:m0 + chunk, :], wd[0])
            for c in range(1, n_up):
                y = y + _dot(hm_ref[c, m0:m0 + chunk, :], wd[c])
            ye_ref[0, bi, r0:r0 + chunk, :] = y.astype(ye_ref.dtype)


def _ffn(xe, w_gate, w_up, w_down, layer):
    e, b, rows, d = xe.shape
    f = w_gate.shape[3]
    tf = min(512, f)
    tdc = min(512, d)
    n_up, n_down = f // tf, d // tdc
    up = lambda st: jnp.minimum(st, n_up - 1)
    down = lambda st: jnp.maximum(st - n_up, 0)
    return pl.pallas_call(
        functools.partial(_ffn_kernel, n_up=n_up),
        name="moe_ffn",
        grid=(e, n_up + n_down),
        in_specs=[
            pl.BlockSpec((1, b, rows, d), lambda ei, st: (ei, 0, 0, 0)),
            pl.BlockSpec((1, 1, d, tf), lambda ei, st: (layer, ei, 0, up(st))),
            pl.BlockSpec((1, 1, d, tf), lambda ei, st: (layer, ei, 0, up(st))),
            pl.BlockSpec((1, 1, f, tdc), lambda ei, st: (layer, ei, 0, down(st))),
        ],
        out_specs=pl.BlockSpec((1, b, rows, tdc), lambda ei, st: (ei, 0, 0, down(st))),
        out_shape=jax.ShapeDtypeStruct((e, b, rows, d), BF16),
        scratch_shapes=[pltpu.VMEM((n_up, b * rows, tf), BF16)],
        compiler_params=_params(("parallel", "arbitrary")),
    )(xe, w_gate, w_up, w_down)


def _window_start(ts_ref, idx, cap_rows, win):
    a0 = jnp.minimum(ts_ref[idx] & (-SLOT_ALIGN), cap_rows - win)
    return pl.multiple_of(a0, SLOT_ALIGN)


def _combine_kernel(ts_ref, ye_ref, posc_ref, gate_ref, x_ref, g_ref, o_ref, *, nt, win, cap, n_starts, tile):
    bi, step = pl.program_id(0), pl.program_id(2)
    cap_rows = ye_ref.shape[2]
    tps = posc_ref.shape[1] // tile
    reg = COMBINE_REGION
    assert 2 * reg == LANES and reg <= cap_rows

    for u in range(tps):
        t = step * tps + u
        rows = slice(u * tile, (u + 1) * tile)
        base = (bi * nt + t) * N_EXPERTS
        starts, fits = [], None
        for e in range(N_EXPERTS):
            a0 = _window_start(ts_ref, base + e, cap_rows, reg)
            nxt = ts_ref[jnp.minimum(base + N_EXPERTS + e, n_starts - 1)]
            end = jnp.where(t + 1 < nt, nxt, cap)
            ok = end - a0 <= reg
            fits = ok if fits is None else jnp.logical_and(fits, ok)
            starts.append(a0)

        lane = lax.broadcasted_iota(I32, (tile, LANES), 1)
        upper = lane >= reg
        weights, slabs = [], []
        for p in range(N_EXPERTS // 2):
            e0, e1 = 2 * p, 2 * p + 1
            slot = jnp.where(upper, starts[e1] - reg, starts[e0]) + lane
            pcol = jnp.where(upper, posc_ref[0, rows, e1:e1 + 1], posc_ref[0, rows, e0:e0 + 1])
            gcol = jnp.where(upper, gate_ref[0, rows, e1:e1 + 1], gate_ref[0, rows, e0:e0 + 1])
            weights.append(jnp.where(pcol == slot, gcol, 0.0).astype(BF16))
            slabs += [ye_ref[e0, 0, pl.ds(starts[e0], reg), :], ye_ref[e1, 0, pl.ds(starts[e1], reg), :]]
        acc = _dot(jnp.concatenate(weights, axis=1), jnp.concatenate(slabs, axis=0))
        o_ref[0, rows, :] = x_ref[0, rows, :] + g_ref[0, 0] * acc

        @pl.when(jnp.logical_not(fits))
        def _():
            acc = jnp.zeros((tile, o_ref.shape[2]), F32)
            lane = lax.broadcasted_iota(I32, (tile, win), 1)
            for e in range(N_EXPERTS):
                a0 = _window_start(ts_ref, base + e, cap_rows, win)
                pcol = posc_ref[0, rows, e:e + 1]
                gcol = gate_ref[0, rows, e:e + 1]
                w = jnp.where(pcol == a0 + lane, gcol, 0.0).astype(BF16)
                acc = acc + _dot(w, ye_ref[e, 0, pl.ds(a0, win), :])
            o_ref[0, rows, :] = x_ref[0, rows, :] + g_ref[0, 0] * acc


def _combine(tstart, ye, posc, gate, x, g2, *, row_off, n, region, row0, cap_rows):
    b, s, d = x.shape
    tile = ROUTE_TILE
    nt = n // tile
    dc = 512 if d % 512 == 0 else d
    win = min(2 * tile, cap_rows)
    tps = min(TILES_PER_STEP, nt)
    rows = tps * tile
    assert row_off % rows == 0
    off = row_off // rows
    grid_spec = pltpu.PrefetchScalarGridSpec(
        num_scalar_prefetch=1,
        grid=(b, d // dc, nt // tps),
        in_specs=[
            pl.BlockSpec((N_EXPERTS, 1, cap_rows, dc), lambda bi, c, t, ts: (0, bi, row0 // cap_rows, c)),
            pl.BlockSpec((1, rows, LANES), lambda bi, c, t, ts: (bi, t, 0)),
            pl.BlockSpec((1, rows, LANES), lambda bi, c, t, ts: (bi, t, 0)),
            pl.BlockSpec((1, rows, dc), lambda bi, c, t, ts: (bi, off + t, c)),
            pl.BlockSpec((1, 1, 1, dc), lambda bi, c, t, ts: (bi, region, 0, c)),
        ],
        out_specs=pl.BlockSpec((1, rows, dc), lambda bi, c, t, ts: (bi, off + t, c)),
    )
    return pl.pallas_call(
        functools.partial(_combine_kernel, nt=nt, win=win, cap=CAPACITY_FACTOR * n // N_EXPERTS,
                          n_starts=b * nt * N_EXPERTS, tile=tile),
        name="moe_combine",
        grid_spec=grid_spec,
        out_shape=jax.ShapeDtypeStruct((b, s, d), F32),
        input_output_aliases={4: 0},
        compiler_params=_params(("parallel", "parallel", "arbitrary")),
    )(tstart, ye, posc, gate, x, g2)


def _moe(x, h, aff, g2, w_gate, w_up, w_down, layer, token_sets):
    routed, row0 = [], 0
    for row_off, n, region in token_sets:
        cap = CAPACITY_FACTOR * n // N_EXPERTS
        cap_rows = -(-cap // LANES) * LANES
        posc, gate, tst = _route(aff, row_off=row_off, n=n, cap=cap)
        tstart = tst[:, :, 0, :N_EXPERTS].reshape(-1)
        idx = _slot_index(tstart, posc, row_off=row_off, n=n, cap=cap)
        routed.append((row_off, n, region, row0, cap_rows, tstart, posc, gate, idx))
        row0 += cap_rows
    idx_all = jnp.concatenate([r[8][:, :, :r[4]] for r in routed], axis=2)
    ye = _ffn(_gather(idx_all, h), w_gate, w_up, w_down, layer)
    for row_off, n, region, r0, cap_rows, tstart, posc, gate, _ in routed:
        x = _combine(tstart, ye, posc, gate, x, g2, row_off=row_off, n=n, region=region, row0=r0, cap_rows=cap_rows)
    return x


def _rope_tables(n_lat, n_ctx, head_dim):
    quarter = head_dim // 4
    inv = ROPE_BASE ** (-jnp.arange(quarter, dtype=F32) / quarter)
    pos = jnp.arange(n_lat)
    row = (pos // GRID_W).astype(F32)[:, None] * inv[None, :]
    col = (pos % GRID_W).astype(F32)[:, None] * inv[None, :]
    cos = jnp.concatenate([jnp.cos(row), jnp.cos(row), jnp.cos(col), jnp.cos(col)], axis=1)
    sin = jnp.concatenate([-jnp.sin(row), jnp.sin(row), -jnp.sin(col), jnp.sin(col)], axis=1)
    cos = jnp.concatenate([cos, jnp.ones((n_ctx, head_dim), F32)], axis=0)
    sin = jnp.concatenate([sin, jnp.zeros((n_ctx, head_dim), F32)], axis=0)
    return cos, sin


def kernel(x, c, ctx, c_ctx, w_mod, b_mod, norm_gain, final_gain, ret_w_in, ret_w_out, ret_decay,
           win_w_qkv, win_w_o, win_sink, fno_w_o, router_w, exp_w_gate, exp_w_up, exp_w_down):
    b, n_lat, d = x.shape
    n_ctx = ctx.shape[1]
    depth = w_mod.shape[0]
    s = n_lat + n_ctx
    assert s % ROW_TILE == 0 and n_lat % n_ctx == 0
    assert n_lat % (FFT_INNER * 4) == 0 and b + 1 <= 8 and n_ctx % ROUTE_TILE == 0

    xs = jnp.concatenate([x, ctx], axis=1)
    cvec = jnp.zeros((8, d), F32).at[:b].set(c).at[b].set(c_ctx)
    mod = _modulation(cvec, w_mod, b_mod)

    def mod_pair(i, k):
        lat = mod[i, :b, k * d:(k + 1) * d]
        cx = jnp.broadcast_to(mod[i, b, k * d:(k + 1) * d], (b, d))
        return jnp.stack([lat, cx], axis=1).reshape(b, 2, 1, d)

    ret_dk = d // RET_HEADS
    ret_dv = 2 * ret_dk
    win_hd = d // WIN_HEADS
    ret_cos, ret_sin = _rope_tables(n_lat, n_ctx, ret_dk)
    win_cos, win_sin = _rope_tables(n_lat, n_ctx, win_hd)

    for i in range(depth):
        kind, j = i % N_MIXERS, i // N_MIXERS
        last = i == depth - 1
        sh1, sc1, g1, sh2, sc2, g2 = [mod_pair(i, k) for k in range(6)]
        h = _norm_mod(xs, norm_gain[i, 0], sh1, sc1, n_lat)
        if kind == 0:
            hk, hv = RET_HEADS * ret_dk, RET_HEADS * ret_dv
            w_in = ret_w_in[j].astype(BF16)
            qk = _mm_proj(h, w_in, 0, 2 * hk, PROJ_COL_TILE,
                          rope=(ret_cos, ret_sin, hk, 1.0, ret_dk ** -0.5, ret_dk))
            vg = _mm_proj(h, w_in, 2 * hk, 3 * hv, PROJ_COL_TILE)
            scan = functools.partial(_ret_scan, n_lat=n_lat, n_ctx=n_ctx, heads=RET_HEADS, dk=ret_dk, dv=ret_dv)
            y_b = scan(qk, vg, ret_decay[j, 1:2], None, reverse=True)
            y = scan(qk, vg, ret_decay[j, 0:1], y_b, reverse=False)
            w_out = ret_w_out[j]
        elif kind == 1:
            nq, nkv = WIN_HEADS * win_hd, WIN_KV_HEADS * win_hd
            w_qkv = win_w_qkv[j].astype(BF16)
            qk = _mm_proj(h, w_qkv, 0, nq + nkv, COL_TILE, rope=(win_cos, win_sin, nq, win_hd ** -0.5, 1.0, win_hd))
            v = _mm_proj(h, w_qkv, nq + nkv, nkv, COL_TILE)
            y = _win_attn(qk, v, win_sink[j], n_lat=n_lat, n_ctx=n_ctx, heads=WIN_HEADS,
                          kv_heads=WIN_KV_HEADS, hd=win_hd)
            w_out = win_w_o[j]
        else:
            cg = d // FOURIER_GROUPS
            cc, sc = _cos_sin(cg)
            w_ch = jnp.asarray(np.concatenate([cc, sc], axis=1) / math.sqrt(cg), BF16)
            uv = _mm_groups(h, w_ch, FOURIER_GROUPS)
            y = _fourier_positions(uv, n_lat, n_ctx, d)
            w_out = fno_w_o[j]
        xs = _mm_res(y, w_out.astype(BF16), xs, g1[:, 0], g1[:, 1], n_lat)
        h2, aff = _norm_mod(xs, norm_gain[i, 1], sh2, sc2, n_lat, router_w=router_w[i])
        token_sets = [(0, n_lat, 0)] + ([] if last else [(n_lat, n_ctx, 1)])
        xs = _moe(xs, h2, aff, g2, exp_w_gate, exp_w_up, exp_w_down, i, token_sets)

    zeros = jnp.zeros((b, 2, 1, d), F32)
    return _norm_mod(xs, final_gain, zeros, zeros, n_lat, out_dtype=F32, rows=n_lat)
```

```python
import functools
import math

import numpy as np
import jax
import jax.numpy as jnp
from jax import lax
from jax.experimental import pallas as pl
from jax.experimental.pallas import tpu as pltpu

F32 = jnp.float32
BF16 = jnp.bfloat16
I32 = jnp.int32

GRID_W = 64
N_MIXERS = 3
RET_HEADS = 8
RET_CHUNK = 256
WIN_HEADS = 16
WIN_KV_HEADS = 4
WIN_BLOCK = 128
FOURIER_GROUPS = 4
FFT_INNER = 64
N_EXPERTS = 16
CAPACITY_FACTOR = 2
ROPE_BASE = 10000.0
NORM_EPS = 1e-6
NEG_INF = -1e30

LANES = 128
ROUTE_TILE = 128
SLOT_ALIGN = 16
ROW_TILE = 768
MM_ROW_CHUNK = 256
COL_TILE = 512
RESIDENT_WEIGHT_BYTES = 8 * 1024 * 1024
PROJ_COL_TILE = 2048
NORM_TILES = (768, 512, 256)
VMEM_LIMIT_BYTES = 56 * 1024 * 1024


def _params(sem):
    return pltpu.CompilerParams(dimension_semantics=sem, vmem_limit_bytes=VMEM_LIMIT_BYTES)


def _dot(a, b):
    return jnp.dot(a, b, preferred_element_type=F32)


def _dot_nt(a, b):
    return lax.dot_general(a, b, (((1,), (1,)), ((), ())), preferred_element_type=F32)


def _split_bf16(x):
    hi = x.astype(BF16)
    lo = (x - hi.astype(F32)).astype(BF16)
    return hi, lo


def _silu(x):
    return x / (1.0 + jnp.exp(-x))


def _mod_kernel(c_ref, w_ref, b_ref, o_ref):
    s = _silu(c_ref[...])
    sh, sl = _split_bf16(s)
    wh, wl = _split_bf16(w_ref[0])
    o_ref[0] = _dot(sh, wh) + _dot(sl, wh) + _dot(sh, wl) + b_ref[0]


def _modulation(cvec, w_mod, b_mod):
    depth, d, n = w_mod.shape
    tn = 1024
    return pl.pallas_call(
        _mod_kernel,
        name="modulation",
        grid=(depth, n // tn),
        in_specs=[
            pl.BlockSpec((8, d), lambda i, j: (0, 0)),
            pl.BlockSpec((1, d, tn), lambda i, j: (i, 0, j)),
            pl.BlockSpec((1, 1, tn), lambda i, j: (i, 0, j)),
        ],
        out_specs=pl.BlockSpec((1, 8, tn), lambda i, j: (i, 0, j)),
        out_shape=jax.ShapeDtypeStruct((depth, 8, n), F32),
        compiler_params=_params(("parallel", "parallel")),
    )(cvec, w_mod, b_mod.reshape(depth, 1, n))


def _normed(x_ref, gain_ref, shift_ref, scale_ref, n_lat):
    x = x_ref[0]
    tr = x.shape[0]
    ms = jnp.mean(x * x, axis=-1, keepdims=True)
    y = x * lax.rsqrt(ms + NORM_EPS) * gain_ref[...]
    is_lat = pl.program_id(1) * tr + lax.broadcasted_iota(I32, (tr, 1), 0) < n_lat
    scale = jnp.where(is_lat, scale_ref[0, 0], scale_ref[0, 1])
    shift = jnp.where(is_lat, shift_ref[0, 0], shift_ref[0, 1])
    return y * (1.0 + scale) + shift


def _norm_mod_kernel(x_ref, gain_ref, shift_ref, scale_ref, o_ref, *, n_lat):
    o_ref[0] = _normed(x_ref, gain_ref, shift_ref, scale_ref, n_lat).astype(o_ref.dtype)


def _pack_bf16_pairs(h):
    half = h.shape[1] // 2
    bits = lax.bitcast_convert_type(h.astype(BF16).astype(F32), jnp.uint32)
    word = lax.shift_right_logical(bits[:, :half], jnp.uint32(16)) | (bits[:, half:] & jnp.uint32(0xFFFF0000))
    return lax.bitcast_convert_type(word, I32)


def _unpack_bf16_pairs(word):
    bits = lax.bitcast_convert_type(word, jnp.uint32)
    lo = lax.bitcast_convert_type(lax.shift_left(bits, jnp.uint32(16)), F32)
    hi = lax.bitcast_convert_type(bits & jnp.uint32(0xFFFF0000), F32)
    return jnp.concatenate([lo, hi], axis=1).astype(BF16)


def _norm_router_kernel(x_ref, gain_ref, shift_ref, scale_ref, wr_ref, o_ref, aff_ref, *, n_lat):
    h = _normed(x_ref, gain_ref, shift_ref, scale_ref, n_lat)
    packed = _pack_bf16_pairs(h)
    o_ref[0] = packed.reshape(o_ref.shape[1:])
    hh, hl = _split_bf16(h)
    wh, wl = _split_bf16(wr_ref[...])
    logits = _dot(hh, wh) + _dot(hl, wh) + _dot(hh, wl)
    lane = lax.broadcasted_iota(I32, logits.shape, 1)
    valid = lane < N_EXPERTS
    logits = jnp.where(valid, logits, -jnp.inf)
    m = jnp.max(logits, axis=-1, keepdims=True)
    p = jnp.exp(logits - m)
    aff = p / jnp.sum(p, axis=-1, keepdims=True)
    aff_ref[0] = jnp.where(valid, aff, 0.0)


def _norm_mod(x, gain, shift, scale, n_lat, *, out_dtype=BF16, rows=None, router_w=None):
    b, s, d = x.shape
    rows = s if rows is None else rows
    tr = next(t for t in NORM_TILES if rows % t == 0 and s % t == 0)
    both = lambda bi, t: (bi, 0, 0, 0)
    in_specs = [
        pl.BlockSpec((1, tr, d), lambda bi, t: (bi, t, 0)),
        pl.BlockSpec((1, d), lambda bi, t: (0, 0)),
        pl.BlockSpec((1, 2, 1, d), both),
        pl.BlockSpec((1, 2, 1, d), both),
    ]
    args = [x, gain.reshape(1, d), shift, scale]
    out_specs = pl.BlockSpec((1, tr, d), lambda bi, t: (bi, t, 0))
    out_shape = jax.ShapeDtypeStruct((b, rows, d), out_dtype)
    kern = functools.partial(_norm_mod_kernel, n_lat=n_lat)
    if router_w is not None:
        wr = jnp.zeros((d, LANES), F32).at[:, :N_EXPERTS].set(router_w)
        in_specs.append(pl.BlockSpec((d, LANES), lambda bi, t: (0, 0)))
        args.append(wr)
        slab = (d // 2 // LANES, LANES)
        out_specs = [pl.BlockSpec((1, tr) + slab, lambda bi, t: (bi, t, 0, 0)),
                     pl.BlockSpec((1, tr, LANES), lambda bi, t: (bi, t, 0))]
        out_shape = [jax.ShapeDtypeStruct((b, rows) + slab, I32),
                     jax.ShapeDtypeStruct((b, rows, LANES), F32)]
        kern = functools.partial(_norm_router_kernel, n_lat=n_lat)
    return pl.pallas_call(
        kern,
        name="norm_mod" if router_w is None else "norm_router",
        grid=(b, rows // tr),
        in_specs=in_specs,
        out_specs=out_specs,
        out_shape=out_shape,
        compiler_params=_params(("parallel", "parallel")),
    )(*args)


def _rope_partner(xs, quarter):
    if 2 * quarter == LANES:
        return pltpu.roll(xs, quarter, 1)
    back = pltpu.roll(xs, quarter, 1)
    fwd = pltpu.roll(xs, LANES - quarter, 1)
    lane = lax.broadcasted_iota(I32, xs.shape, 1)
    return jnp.where((lane % (2 * quarter)) < quarter, fwd, back)


def _row_chunks(tm):
    return [slice(r, r + MM_ROW_CHUNK) for r in range(0, tm, MM_ROW_CHUNK)]


def _mm_rope_kernel(a_ref, w_ref, cos_ref, sin_ref, o_ref, *, n_q, qscale, kscale, head_dim):
    sc = jnp.where(pl.program_id(1) >= n_q, kscale, qscale).astype(F32)
    for rows in _row_chunks(a_ref.shape[1]):
        acc = _dot(a_ref[0, rows, :], w_ref[...])
        for s in range(acc.shape[1] // LANES):
            cols = slice(s * LANES, (s + 1) * LANES)
            off = (s * LANES) % head_dim
            xs = acc[:, cols]
            rot = xs * cos_ref[rows, off:off + LANES] + _rope_partner(xs, head_dim // 4) * sin_ref[rows, off:off + LANES]
            o_ref[0, rows, cols] = (rot * sc).astype(o_ref.dtype)


def _mm_plain_kernel(a_ref, w_ref, o_ref):
    for rows in _row_chunks(a_ref.shape[1]):
        o_ref[0, rows, :] = _dot(a_ref[0, rows, :], w_ref[...]).astype(o_ref.dtype)


def _mm_proj(a, w, col0, ncols, tn, rope=None):
    b, s, k = a.shape
    tm = ROW_TILE
    assert col0 % tn == 0 and ncols % tn == 0
    in_specs = [
        pl.BlockSpec((1, tm, k), lambda bi, j, i: (bi, i, 0)),
        pl.BlockSpec((k, tn), lambda bi, j, i: (0, col0 // tn + j)),
    ]
    args = [a, w]
    if rope is None:
        kern, name = _mm_plain_kernel, "mm_proj"
    else:
        cos_t, sin_t, n_q_cols, qscale, kscale, head_dim = rope
        assert tn % head_dim == 0 and n_q_cols % tn == 0
        kern = functools.partial(_mm_rope_kernel, n_q=n_q_cols // tn, qscale=qscale, kscale=kscale,
                                 head_dim=head_dim)
        name = "mm_proj_rope"
        in_specs += [pl.BlockSpec((tm, head_dim), lambda bi, j, i: (i, 0))] * 2
        args += [cos_t, sin_t]
    return pl.pallas_call(
        kern,
        name=name,
        grid=(b, ncols // tn, s // tm),
        in_specs=in_specs,
        out_specs=pl.BlockSpec((1, tm, tn), lambda bi, j, i: (bi, i, j)),
        out_shape=jax.ShapeDtypeStruct((b, s, ncols), BF16),
        compiler_params=_params(("parallel", "parallel", "arbitrary")),
    )(*args)


def _mm_res_kernel(a_ref, w_ref, x_ref, gl_ref, gc_ref, o_ref, *, n_lat):
    tm = a_ref.shape[1]
    for rows in _row_chunks(tm):
        acc = _dot(a_ref[0, rows, :], w_ref[...])
        row = pl.program_id(2) * tm + rows.start + lax.broadcasted_iota(I32, (acc.shape[0], 1), 0)
        gate = jnp.where(row < n_lat, gl_ref[0], gc_ref[0])
        o_ref[0, rows, :] = x_ref[0, rows, :] + gate * acc


def _mm_res(a, w, x, gate_lat, gate_ctx, n_lat):
    b, s, k = a.shape
    n = w.shape[1]
    tm = ROW_TILE
    tn = n
    while k * tn * 2 > RESIDENT_WEIGHT_BYTES:
        tn //= 2
    return pl.pallas_call(
        functools.partial(_mm_res_kernel, n_lat=n_lat),
        name="mm_residual",
        grid=(b, n // tn, s // tm),
        in_specs=[
            pl.BlockSpec((1, tm, k), lambda bi, j, i: (bi, i, 0)),
            pl.BlockSpec((k, tn), lambda bi, j, i: (0, j)),
            pl.BlockSpec((1, tm, tn), lambda bi, j, i: (bi, i, j)),
            pl.BlockSpec((1, 1, tn), lambda bi, j, i: (bi, 0, j)),
            pl.BlockSpec((1, 1, tn), lambda bi, j, i: (bi, 0, j)),
        ],
        out_specs=pl.BlockSpec((1, tm, tn), lambda bi, j, i: (bi, i, j)),
        out_shape=jax.ShapeDtypeStruct((b, s, n), F32),
        compiler_params=_params(("parallel", "parallel", "arbitrary")),
    )(a, w, x, gate_lat, gate_ctx)


def _mm_groups_kernel(a_ref, w_ref, o_ref):
    o_ref[0] = _dot(a_ref[0], w_ref[...]).astype(o_ref.dtype)


def _mm_groups(a, w, groups):
    b, s, d = a.shape
    cg = d // groups
    tm = ROW_TILE
    return pl.pallas_call(
        _mm_groups_kernel,
        name="mm_channel_dft",
        grid=(b, s // tm, 2 * groups),
        in_specs=[
            pl.BlockSpec((1, tm, cg), lambda bi, i, j: (bi, i, j % groups)),
            pl.BlockSpec((cg, cg), lambda bi, i, j: (0, j // groups)),
        ],
        out_specs=pl.BlockSpec((1, tm, cg), lambda bi, i, j: (bi, i, j)),
        out_shape=jax.ShapeDtypeStruct((b, s, 2 * d), F32),
        compiler_params=_params(("parallel", "parallel", "arbitrary")),
    )(a, w)


def _ret_scan_kernel(*refs, heads, dk, dv, reverse, add_in):
    if add_in:
        dec_ref, q_ref, k_ref, v_ref, g_ref, yin_ref, o_ref, s_ref, qd_ref, kd_ref, in_ref, cd_ref = refs
    else:
        dec_ref, q_ref, k_ref, v_ref, g_ref, o_ref, s_ref, qd_ref, kd_ref, in_ref, cd_ref = refs
        yin_ref = None
    c = q_ref.shape[1]
    j = pl.program_id(1)

    @pl.when(j == 0)
    def _():
        s_ref[...] = jnp.zeros_like(s_ref)
        m_col = lax.broadcasted_iota(I32, (c, LANES), 0).astype(F32)
        m_row = lax.broadcasted_iota(I32, (c, c), 0).astype(F32)
        n_row = lax.broadcasted_iota(I32, (c, c), 1).astype(F32)
        for h in range(heads):
            lg = -jnp.exp(dec_ref[:, h:h + 1])
            if reverse:
                q_pow, k_pow, diff = c - m_col, m_col, n_row - m_row
            else:
                q_pow, k_pow, diff = m_col + 1.0, c - 1.0 - m_col, m_row - n_row
            qd_ref[h] = jnp.exp(lg * q_pow)
            kd_ref[h] = jnp.exp(lg * k_pow)
            in_ref[h] = jnp.where(diff >= 0, jnp.exp(lg * jnp.maximum(diff, 0.0)), 0.0)
            cd_ref[h] = jnp.exp(jnp.broadcast_to(lg, (8, LANES)) * float(c))

    for h in range(heads):
        q = q_ref[0, :, h * dk:(h + 1) * dk]
        k = k_ref[0, :, h * dk:(h + 1) * dk]
        v = v_ref[0, :, h * dv:(h + 1) * dv]
        qdec = jnp.concatenate([qd_ref[h]] * (dk // LANES), axis=1)
        kdec = jnp.concatenate([kd_ref[h]] * (dk // LANES), axis=1)
        state = s_ref[h]
        cross = _dot((q.astype(F32) * qdec).astype(BF16), state.astype(BF16))
        scores = _dot_nt(q, k) * in_ref[h]
        o = cross + _dot(scores.astype(BF16), v)
        k_t = (k.astype(F32) * kdec).T.astype(BF16)
        s_ref[h] = state * cd_ref[h][0:1, 0:1] + _dot(k_t, v)
        mu = jnp.mean(o, axis=-1, keepdims=True)
        cen = o - mu
        var = jnp.mean(cen * cen, axis=-1, keepdims=True)
        g = g_ref[0, :, h * dv:(h + 1) * dv].astype(F32)
        y = cen * lax.rsqrt(var + NORM_EPS) * _silu(g)
        if add_in:
            y = y + yin_ref[0, :, h * dv:(h + 1) * dv].astype(F32)
        o_ref[0, :, h * dv:(h + 1) * dv] = y.astype(o_ref.dtype)


def _ret_scan(qk, vg, decay_row, y_in, *, n_lat, n_ctx, heads, dk, dv, reverse):
    b, s, _ = qk.shape
    c = RET_CHUNK
    lat_chunks, ctx_chunks = n_lat // c, n_ctx // c
    steps = lat_chunks + ctx_chunks
    hk, hv = heads * dk, heads * dv
    if reverse:
        chunk = lambda j: steps - 1 - j
    else:
        chunk = lambda j: jnp.where(j < ctx_chunks, lat_chunks + j, j - ctx_chunks)
    gate_blk = 2 if reverse else 1
    in_specs = [
        pl.BlockSpec((1, heads), lambda bi, j: (0, 0)),
        pl.BlockSpec((1, c, hk), lambda bi, j: (bi, chunk(j), 0)),
        pl.BlockSpec((1, c, hk), lambda bi, j: (bi, chunk(j), 1)),
        pl.BlockSpec((1, c, hv), lambda bi, j: (bi, chunk(j), 0)),
        pl.BlockSpec((1, c, hv), lambda bi, j: (bi, chunk(j), gate_blk)),
    ]
    args = [decay_row, qk, qk, vg, vg]
    if y_in is not None:
        in_specs.append(pl.BlockSpec((1, c, hv), lambda bi, j: (bi, chunk(j), 0)))
        args.append(y_in)
    kern = functools.partial(_ret_scan_kernel, heads=heads, dk=dk, dv=dv, reverse=reverse,
                             add_in=y_in is not None)
    return pl.pallas_call(
        kern,
        name="ret_scan_bwd" if reverse else "ret_scan_fwd",
        grid=(b, steps),
        in_specs=in_specs,
        out_specs=pl.BlockSpec((1, c, hv), lambda bi, j: (bi, chunk(j), 0)),
        out_shape=jax.ShapeDtypeStruct((b, s, hv), BF16),
        scratch_shapes=[
            pltpu.VMEM((heads, dk, dv), F32),
            pltpu.VMEM((heads, c, LANES), F32),
            pltpu.VMEM((heads, c, LANES), F32),
            pltpu.VMEM((heads, c, c), F32),
            pltpu.VMEM((heads, 8, LANES), F32),
        ],
        compiler_params=_params(("parallel", "arbitrary")),
    )(*args)


def _win_attn_kernel(sink_ref, q_ref, kc_ref, vc_ref, kp_ref, kq_ref, kn_ref, vp_ref, vq_ref, vn_ref,
                     o_ref, *, lat_tiles, heads, kv_heads, hd):
    qt = pl.program_id(1)
    blk = q_ref.shape[1]
    n_ctx = kc_ref.shape[1]
    grp = heads // kv_heads
    rows = grp * blk
    qi = lax.broadcasted_iota(I32, (rows, blk), 0) % blk
    kj = lax.broadcasted_iota(I32, (rows, blk), 1)
    tq = qt + jnp.zeros((rows, blk), I32)
    ok_cur = tq < lat_tiles
    ok_prev = (kj >= qi) & ok_cur & (tq >= 1)
    ok_next = (kj <= qi) & (tq + 1 < lat_tiles)
    bias = jnp.concatenate(
        [jnp.zeros((rows, n_ctx), F32)]
        + [jnp.where(ok, 0.0, NEG_INF).astype(F32) for ok in (ok_prev, ok_cur, ok_next)], axis=1)
    head_row = lax.broadcasted_iota(I32, (rows, 1), 0) // blk
    for kv in range(kv_heads):
        cs = slice(kv * hd, (kv + 1) * hd)
        keys = jnp.concatenate([kc_ref[0, :, cs], kp_ref[0, :, cs], kq_ref[0, :, cs], kn_ref[0, :, cs]], axis=0)
        vals = jnp.concatenate([vc_ref[0, :, cs], vp_ref[0, :, cs], vq_ref[0, :, cs], vn_ref[0, :, cs]], axis=0)
        q = jnp.concatenate([q_ref[0, :, (kv * grp + g) * hd:(kv * grp + g + 1) * hd] for g in range(grp)], axis=0)
        sink = jnp.zeros((rows, 1), F32)
        for g in range(grp):
            h = kv * grp + g
            sink = jnp.where(head_row == g, sink_ref[:, h:h + 1], sink)
        s = _dot_nt(q, keys) + bias
        m = jnp.maximum(jnp.max(s, axis=-1, keepdims=True), sink)
        p = jnp.exp(s - m)
        den = jnp.sum(p, axis=-1, keepdims=True) + jnp.exp(sink - m)
        o = _dot(p.astype(BF16), vals) / den
        for g in range(grp):
            h = kv * grp + g
            o_ref[0, :, h * hd:(h + 1) * hd] = o[g * blk:(g + 1) * blk].astype(o_ref.dtype)


def _win_attn(qk, v, sink, *, n_lat, n_ctx, heads, kv_heads, hd):
    b, s, _ = qk.shape
    blk = WIN_BLOCK
    lat_tiles = n_lat // blk
    tiles = s // blk
    kvw = kv_heads * hd
    k_col = (heads * hd) // kvw
    v_col = 0
    ctx_blk = n_lat // n_ctx
    prev = lambda t: jnp.maximum(t - 1, 0)
    nxt = lambda t: jnp.minimum(t + 1, tiles - 1)
    sink_row = jnp.zeros((1, LANES), F32).at[0, :heads].set(sink.astype(F32))
    kern = functools.partial(_win_attn_kernel, lat_tiles=lat_tiles, heads=heads, kv_heads=kv_heads, hd=hd)
    return pl.pallas_call(
        kern,
        name="win_attn",
        grid=(b, tiles),
        in_specs=[
            pl.BlockSpec((1, LANES), lambda bi, t: (0, 0)),
            pl.BlockSpec((1, blk, heads * hd), lambda bi, t: (bi, t, 0)),
            pl.BlockSpec((1, n_ctx, kvw), lambda bi, t: (bi, ctx_blk, k_col)),
            pl.BlockSpec((1, n_ctx, kvw), lambda bi, t: (bi, ctx_blk, v_col)),
            pl.BlockSpec((1, blk, kvw), lambda bi, t: (bi, prev(t), k_col)),
            pl.BlockSpec((1, blk, kvw), lambda bi, t: (bi, t, k_col)),
            pl.BlockSpec((1, blk, kvw), lambda bi, t: (bi, nxt(t), k_col)),
            pl.BlockSpec((1, blk, kvw), lambda bi, t: (bi, prev(t), v_col)),
            pl.BlockSpec((1, blk, kvw), lambda bi, t: (bi, t, v_col)),
            pl.BlockSpec((1, blk, kvw), lambda bi, t: (bi, nxt(t), v_col)),
        ],
        out_specs=pl.BlockSpec((1, blk, heads * hd), lambda bi, t: (bi, t, 0)),
        out_shape=jax.ShapeDtypeStruct((b, s, heads * hd), BF16),
        compiler_params=_params(("parallel", "parallel")),
    )(sink_row, qk, qk, v, qk, qk, qk, v, v, v)


SUBLANES = 8
PACKED_ROWS = 16


def _dft_stage_a_kernel(m_ref, u_ref, v_ref, re_ref, im_ref):
    n1, sub, tc = u_ref.shape[1:]
    rows = n1 * sub
    stacked = jnp.concatenate([u_ref[0].reshape(rows, tc), v_ref[0].reshape(rows, tc)], axis=0)
    out = _dot(m_ref[...], stacked.astype(BF16))
    re_ref[0] = out[:rows].reshape(n1, sub, tc)
    im_ref[0] = out[rows:].reshape(n1, sub, tc)


def _dft_stage_b_kernel(m_ref, re_ref, im_ref, twc_ref, tws_ref, o_ref):
    n2, sub, tc = o_ref.shape[1:]
    reps = tc // LANES
    ar, ai = re_ref[0], im_ref[0]
    twc = jnp.concatenate([twc_ref[...]] * reps, axis=1)
    tws = jnp.concatenate([tws_ref[...]] * reps, axis=1)
    stacked = jnp.concatenate([ar * twc + ai * tws, ai * twc - ar * tws], axis=0).astype(BF16)
    out = _dot(m_ref[...], stacked)
    o_ref[0] = out.reshape(n2, sub, tc).astype(o_ref.dtype)


def _dft_ctx_kernel(m_ref, x_ref, o_ref, *, d):
    x = x_ref[0]
    stacked = jnp.concatenate([x[:, :d], x[:, d:]], axis=0).astype(BF16)
    o_ref[0] = _dot(m_ref[...], stacked).astype(o_ref.dtype)


def _cos_sin(n):
    ang = 2.0 * np.pi * np.outer(np.arange(n), np.arange(n)) / float(n)
    return np.cos(ang), np.sin(ang)


def _fourier_positions(uv, n_lat, n_ctx, d):
    b, s, _ = uv.shape
    n2 = FFT_INNER
    n1 = n_lat // n2
    tc = min(512, d)
    ca, sa = _cos_sin(n1)
    eye = np.eye(SUBLANES)
    ka, ks = np.kron(ca, eye) / math.sqrt(n1), np.kron(sa, eye) / math.sqrt(n1)
    mat_a = jnp.asarray(np.block([[ka, -ks], [-ks, -ka]]), BF16)
    uv4 = uv.reshape(b, s // n2, n2, 2 * d)
    blk_a = (1, n1, SUBLANES, tc)
    a_re, a_im = pl.pallas_call(
        _dft_stage_a_kernel,
        name="dft_stage_a",
        grid=(b, n2 // SUBLANES, d // tc),
        in_specs=[
            pl.BlockSpec(mat_a.shape, lambda bi, cg, jc: (0, 0)),
            pl.BlockSpec(blk_a, lambda bi, cg, jc: (bi, 0, cg, jc)),
            pl.BlockSpec(blk_a, lambda bi, cg, jc: (bi, 0, cg, d // tc + jc)),
        ],
        out_specs=[pl.BlockSpec(blk_a, lambda bi, cg, jc: (bi, 0, cg, jc))] * 2,
        out_shape=[jax.ShapeDtypeStruct((b, n1, n2, d), F32)] * 2,
        compiler_params=_params(("parallel", "parallel", "parallel")),
    )(mat_a, uv4, uv4)
    cb, sb = _cos_sin(n2)
    eye = np.eye(PACKED_ROWS)
    kron_b = lambda m: np.einsum("kc,ab->kabc", m, eye).reshape(n2 * PACKED_ROWS, PACKED_ROWS * n2)
    mat_b = jnp.asarray(np.concatenate([kron_b(cb), kron_b(sb)], axis=1) / math.sqrt(n2), BF16)
    phi = 2.0 * np.pi * np.outer(np.arange(n1), np.arange(n2)) / float(n_lat)
    twc = jnp.asarray(np.repeat(np.cos(phi).reshape(-1, 1), LANES, axis=1), F32)
    tws = jnp.asarray(np.repeat(np.sin(phi).reshape(-1, 1), LANES, axis=1), F32)
    rows = PACKED_ROWS * n2
    y = pl.pallas_call(
        _dft_stage_b_kernel,
        name="dft_stage_b",
        grid=(b, n1 // PACKED_ROWS, d // tc),
        in_specs=[
            pl.BlockSpec(mat_b.shape, lambda bi, kb, jc: (0, 0)),
            pl.BlockSpec((1, rows, tc), lambda bi, kb, jc: (bi, kb, jc)),
            pl.BlockSpec((1, rows, tc), lambda bi, kb, jc: (bi, kb, jc)),
            pl.BlockSpec((rows, LANES), lambda bi, kb, jc: (kb, 0)),
            pl.BlockSpec((rows, LANES), lambda bi, kb, jc: (kb, 0)),
        ],
        out_specs=pl.BlockSpec((1, n2, PACKED_ROWS, tc), lambda bi, kb, jc: (bi, 0, kb, jc)),
        out_shape=jax.ShapeDtypeStruct((b, n2, n1, d), BF16),
        compiler_params=_params(("parallel", "parallel", "parallel")),
    )(mat_b, a_re.reshape(b, n_lat, d), a_im.reshape(b, n_lat, d), twc, tws)
    cc, sc = _cos_sin(n_ctx)
    mat_c = jnp.asarray(np.concatenate([cc, -sc], axis=1) / math.sqrt(n_ctx), BF16)
    y_ctx = pl.pallas_call(
        functools.partial(_dft_ctx_kernel, d=d),
        name="dft_ctx",
        grid=(b,),
        in_specs=[
            pl.BlockSpec(mat_c.shape, lambda bi: (0, 0)),
            pl.BlockSpec((1, n_ctx, 2 * d), lambda bi: (bi, n_lat // n_ctx, 0)),
        ],
        out_specs=pl.BlockSpec((1, n_ctx, d), lambda bi: (bi, 0, 0)),
        out_shape=jax.ShapeDtypeStruct((b, n_ctx, d), BF16),
        compiler_params=_params(("parallel",)),
    )(mat_c, uv)
    return jnp.concatenate([y.reshape(b, n_lat, d), y_ctx], axis=1)


LANE_SHIFT = 7
GATHER_UNROLL = 16
TILES_PER_STEP = 8
FFN_ROW_CHUNK = 384
COMBINE_REGION = 64


def _route_kernel(aff_ref, tri_ref, posc_ref, gate_ref, tst_ref,
                  thr_ref, need_ref, ctie_ref, cpos_ref, *, cap):
    step = pl.program_id(1)
    tile = tri_ref.shape[0]
    tps = posc_ref.shape[1] // tile

    @pl.when(step == 0)
    def _():
        def body(it, thr):
            bits = lax.bitcast_convert_type(aff_ref[0], I32)
            cand = thr | jnp.left_shift(jnp.int32(1), 30 - it)
            cnt = jnp.sum(jnp.where(bits >= cand, 1.0, 0.0), axis=0, keepdims=True)
            return jnp.where(cnt >= cap, cand, thr)

        thr = lax.fori_loop(0, 31, body, jnp.zeros((1, LANES), I32))
        bits = lax.bitcast_convert_type(aff_ref[0], I32)
        above = jnp.sum(jnp.where(bits > thr, 1.0, 0.0), axis=0, keepdims=True)
        thr_ref[...] = thr
        need_ref[...] = float(cap) - above
        ctie_ref[...] = jnp.zeros_like(ctie_ref)
        cpos_ref[...] = jnp.zeros_like(cpos_ref)

    thr = thr_ref[...]
    for u in range(tps):
        rows = slice(u * tile, (u + 1) * tile)
        a = aff_ref[0, pl.ds(pl.multiple_of((step * tps + u) * tile, tile), tile), :]
        bits = lax.bitcast_convert_type(a, I32)
        gt = bits > thr
        eq = bits == thr
        eqf = jnp.where(eq, 1.0, 0.0)
        tie_rank = _dot(tri_ref[...], eqf.astype(BF16)) + ctie_ref[...]
        sel = gt | (eq & (tie_rank < need_ref[...]))
        self_ = jnp.where(sel, 1.0, 0.0)
        start = cpos_ref[...]
        pos = jnp.where(sel, _dot(tri_ref[...], self_.astype(BF16)) + start, -1.0)
        posc_ref[0, rows, :] = pos.astype(I32)
        gate_ref[0, rows, :] = jnp.where(sel, a, 0.0)
        tst_ref[0, u] = jnp.broadcast_to(start, (8, LANES)).astype(I32)
        ctie_ref[...] = ctie_ref[...] + jnp.sum(eqf, axis=0, keepdims=True)
        cpos_ref[...] = start + jnp.sum(self_, axis=0, keepdims=True)


def _route(aff, *, row_off, n, cap):
    b = aff.shape[0]
    tile = ROUTE_TILE
    nt = n // tile
    tps = min(TILES_PER_STEP, nt)
    tri = jnp.asarray(np.tril(np.ones((tile, tile)), -1), BF16)
    return pl.pallas_call(
        functools.partial(_route_kernel, cap=cap),
        name="moe_route",
        grid=(b, nt // tps),
        in_specs=[
            pl.BlockSpec((1, n, LANES), lambda bi, t: (bi, row_off // n, 0)),
            pl.BlockSpec((tile, tile), lambda bi, t: (0, 0)),
        ],
        out_specs=[
            pl.BlockSpec((1, tps * tile, LANES), lambda bi, t: (bi, t, 0)),
            pl.BlockSpec((1, tps * tile, LANES), lambda bi, t: (bi, t, 0)),
            pl.BlockSpec((1, tps, 8, LANES), lambda bi, t: (bi, t, 0, 0)),
        ],
        out_shape=[
            jax.ShapeDtypeStruct((b, n, LANES), I32),
            jax.ShapeDtypeStruct((b, n, LANES), F32),
            jax.ShapeDtypeStruct((b, nt, 8, LANES), I32),
        ],
        scratch_shapes=[pltpu.VMEM((1, LANES), I32)] + [pltpu.VMEM((1, LANES), F32)] * 3,
        compiler_params=_params(("parallel", "arbitrary")),
    )(aff, tri)


def _slot_index_kernel(ts_ref, posc_ref, idx_ref, *, nt, row_off, tile):
    bi, step = pl.program_id(0), pl.program_id(1)

    @pl.when(step == 0)
    def _():
        idx_ref[...] = jnp.zeros_like(idx_ref)

    tps = posc_ref.shape[1] // tile
    slot_rows = idx_ref.shape[2]
    lane = lax.broadcasted_iota(I32, (tile, 2 * LANES), 1)
    for u in range(tps):
        t = step * tps + u
        tok = (row_off + t * tile + lax.broadcasted_iota(I32, (tile, 1), 0)).astype(F32)
        for e in range(N_EXPERTS):
            h0 = jnp.minimum(ts_ref[(bi * nt + t) * N_EXPERTS + e] >> LANE_SHIFT, slot_rows - 2)
            hit = (posc_ref[0, u * tile:(u + 1) * tile, e:e + 1] - h0 * LANES) == lane
            vals = jnp.sum(jnp.where(hit, tok, 0.0), axis=0, keepdims=True)
            two_rows = jnp.concatenate([vals[:, :LANES], vals[:, LANES:]], axis=0).astype(I32)
            idx_ref[0, e, pl.ds(h0, 2), :] = idx_ref[0, e, pl.ds(h0, 2), :] + two_rows


def _slot_index(tstart, posc, *, row_off, n, cap):
    b = posc.shape[0]
    tile = ROUTE_TILE
    nt = n // tile
    slot_rows = max(cap, 2 * LANES) // LANES
    tps = min(TILES_PER_STEP, nt)
    grid_spec = pltpu.PrefetchScalarGridSpec(
        num_scalar_prefetch=1,
        grid=(b, nt // tps),
        in_specs=[pl.BlockSpec((1, tps * tile, LANES), lambda bi, t, ts: (bi, t, 0))],
        out_specs=pl.BlockSpec((1, N_EXPERTS, slot_rows, LANES), lambda bi, t, ts: (bi, 0, 0, 0)),
    )
    idx = pl.pallas_call(
        functools.partial(_slot_index_kernel, nt=nt, row_off=row_off, tile=tile),
        name="moe_slot_index",
        grid_spec=grid_spec,
        out_shape=jax.ShapeDtypeStruct((b, N_EXPERTS, slot_rows, LANES), I32),
        compiler_params=_params(("parallel", "arbitrary")),
    )(tstart, posc)
    return idx.reshape(b, N_EXPERTS, slot_rows * LANES)


def _gather_kernel(idx_ref, h_ref, o_ref, buf_ref, sem):
    bi = pl.program_id(1)
    n_rows = buf_ref.shape[0]

    def row_copy(src_row, dst_row, rows):
        return pltpu.make_async_copy(h_ref.at[bi, pl.ds(src_row, rows)], buf_ref.at[pl.ds(dst_row, rows)], sem)

    def issue(g, carry):
        for k in range(GATHER_UNROLL):
            s = g * GATHER_UNROLL + k
            row_copy(idx_ref[0, 0, s], s, 1).start(priority=k % 2)
        return carry

    lax.fori_loop(0, n_rows // GATHER_UNROLL, issue, 0)
    row_copy(0, 0, n_rows).wait()
    o_ref[0, 0] = _unpack_bf16_pairs(buf_ref[...].reshape(n_rows, -1))


def _gather(idx, h):
    b, _, sub, lanes = h.shape
    d = 2 * sub * lanes
    slots = idx.shape[2]
    assert slots % GATHER_UNROLL == 0
    return pl.pallas_call(
        _gather_kernel,
        name="moe_gather",
        grid=(N_EXPERTS, b),
        in_specs=[
            pl.BlockSpec((1, 1, slots), lambda ei, bi: (bi * N_EXPERTS + ei, 0, 0), memory_space=pltpu.SMEM),
            pl.BlockSpec(memory_space=pl.ANY),
        ],
        out_specs=pl.BlockSpec((1, 1, slots, d), lambda ei, bi: (ei, bi, 0, 0)),
        out_shape=jax.ShapeDtypeStruct((N_EXPERTS, b, slots, d), BF16),
        scratch_shapes=[pltpu.VMEM((slots, sub, lanes), I32), pltpu.SemaphoreType.DMA(())],
        compiler_params=_params(("arbitrary", "arbitrary")),
    )(idx.reshape(b * N_EXPERTS, 1, slots), h)


def _ffn_kernel(x_ref, wg_ref, wu_ref, wd_ref, ye_ref, hm_ref, *, n_up):
    st = pl.program_id(1)
    bsz, rows, d = x_ref.shape[1:]
    tf = wg_ref.shape[3]
    chunk = FFN_ROW_CHUNK if rows % FFN_ROW_CHUNK == 0 else rows
    spans = [(bi, r0) for bi in range(bsz) for r0 in range(0, rows, chunk)]

    @pl.when(st < n_up)
    def _():
        wg = wg_ref[0, 0].astype(BF16)
        wu = wu_ref[0, 0].astype(BF16)
        for bi, r0 in spans:
            x = x_ref[0, bi, r0:r0 + chunk, :]
            hm_ref[st, bi * rows + r0:bi * rows + r0 + chunk, :] = (_silu(_dot(x, wg)) * _dot(x, wu)).astype(BF16)

    @pl.when(st >= n_up)
    def _():
        wd = [wd_ref[0, 0, c * tf:(c + 1) * tf, :].astype(BF16) for c in range(n_up)]
        for bi, r0 in spans:
            m0 = bi * rows + r0
            y = _dot(hm_ref[0, m0:m0 + chunk, :], wd[0])
            for c in range(1, n_up):
                y = y + _dot(hm_ref[c, m0:m0 + chunk, :], wd[c])
            ye_ref[0, bi, r0:r0 + chunk, :] = y.astype(ye_ref.dtype)


def _ffn(xe, w_gate, w_up, w_down, layer):
    e, b, rows, d = xe.shape
    f = w_gate.shape[3]
    tf = min(512, f)
    tdc = min(512, d)
    n_up, n_down = f // tf, d // tdc
    up = lambda st: jnp.minimum(st, n_up - 1)
    down = lambda st: jnp.maximum(st - n_up, 0)
    return pl.pallas_call(
        functools.partial(_ffn_kernel, n_up=n_up),
        name="moe_ffn",
        grid=(e, n_up + n_down),
        in_specs=[
            pl.BlockSpec((1, b, rows, d), lambda ei, st: (ei, 0, 0, 0)),
            pl.BlockSpec((1, 1, d, tf), lambda ei, st: (layer, ei, 0, up(st))),
            pl.BlockSpec((1, 1, d, tf), lambda ei, st: (layer, ei, 0, up(st))),
            pl.BlockSpec((1, 1, f, tdc), lambda ei, st: (layer, ei, 0, down(st))),
        ],
        out_specs=pl.BlockSpec((1, b, rows, tdc), lambda ei, st: (ei, 0, 0, down(st))),
        out_shape=jax.ShapeDtypeStruct((e, b, rows, d), BF16),
        scratch_shapes=[pltpu.VMEM((n_up, b * rows, tf), BF16)],
        compiler_params=_params(("parallel", "arbitrary")),
    )(xe, w_gate, w_up, w_down)


def _window_start(ts_ref, idx, cap_rows, win):
    a0 = jnp.minimum(ts_ref[idx] & (-SLOT_ALIGN), cap_rows - win)
    return pl.multiple_of(a0, SLOT_ALIGN)


def _combine_kernel(ts_ref, ye_ref, posc_ref, gate_ref, x_ref, g_ref, o_ref, *, nt, win, cap, n_starts, tile):
    bi, step = pl.program_id(0), pl.program_id(2)
    cap_rows = ye_ref.shape[2]
    tps = posc_ref.shape[1] // tile
    reg = COMBINE_REGION
    assert 2 * reg == LANES and reg <= cap_rows

    for u in range(tps):
        t = step * tps + u
        rows = slice(u * tile, (u + 1) * tile)
        base = (bi * nt + t) * N_EXPERTS
        starts, fits = [], None
        for e in range(N_EXPERTS):
            a0 = _window_start(ts_ref, base + e, cap_rows, reg)
            nxt = ts_ref[jnp.minimum(base + N_EXPERTS + e, n_starts - 1)]
            end = jnp.where(t + 1 < nt, nxt, cap)
            ok = end - a0 <= reg
            fits = ok if fits is None else jnp.logical_and(fits, ok)
            starts.append(a0)

        lane = lax.broadcasted_iota(I32, (tile, LANES), 1)
        upper = lane >= reg
        weights, slabs = [], []
        for p in range(N_EXPERTS // 2):
            e0, e1 = 2 * p, 2 * p + 1
            slot = jnp.where(upper, starts[e1] - reg, starts[e0]) + lane
            pcol = jnp.where(upper, posc_ref[0, rows, e1:e1 + 1], posc_ref[0, rows, e0:e0 + 1])
            gcol = jnp.where(upper, gate_ref[0, rows, e1:e1 + 1], gate_ref[0, rows, e0:e0 + 1])
            weights.append(jnp.where(pcol == slot, gcol, 0.0).astype(BF16))
            slabs += [ye_ref[e0, 0, pl.ds(starts[e0], reg), :], ye_ref[e1, 0, pl.ds(starts[e1], reg), :]]
        acc = _dot(jnp.concatenate(weights, axis=1), jnp.concatenate(slabs, axis=0))
        o_ref[0, rows, :] = x_ref[0, rows, :] + g_ref[0, 0] * acc

        @pl.when(jnp.logical_not(fits))
        def _():
            acc = jnp.zeros((tile, o_ref.shape[2]), F32)
            lane = lax.broadcasted_iota(I32, (tile, win), 1)
            for e in range(N_EXPERTS):
                a0 = _window_start(ts_ref, base + e, cap_rows, win)
                pcol = posc_ref[0, rows, e:e + 1]
                gcol = gate_ref[0, rows, e:e + 1]
                w = jnp.where(pcol == a0 + lane, gcol, 0.0).astype(BF16)
                acc = acc + _dot(w, ye_ref[e, 0, pl.ds(a0, win), :])
            o_ref[0, rows, :] = x_ref[0, rows, :] + g_ref[0, 0] * acc


def _combine(tstart, ye, posc, gate, x, g2, *, row_off, n, region, row0, cap_rows):
    b, s, d = x.shape
    tile = ROUTE_TILE
    nt = n // tile
    dc = 512 if d % 512 == 0 else d
    win = min(2 * tile, cap_rows)
    tps = min(TILES_PER_STEP, nt)
    rows = tps * tile
    assert row_off % rows == 0
    off = row_off // rows
    grid_spec = pltpu.PrefetchScalarGridSpec(
        num_scalar_prefetch=1,
        grid=(b, d // dc, nt // tps),
        in_specs=[
            pl.BlockSpec((N_EXPERTS, 1, cap_rows, dc), lambda bi, c, t, ts: (0, bi, row0 // cap_rows, c)),
            pl.BlockSpec((1, rows, LANES), lambda bi, c, t, ts: (bi, t, 0)),
            pl.BlockSpec((1, rows, LANES), lambda bi, c, t, ts: (bi, t, 0)),
            pl.BlockSpec((1, rows, dc), lambda bi, c, t, ts: (bi, off + t, c)),
            pl.BlockSpec((1, 1, 1, dc), lambda bi, c, t, ts: (bi, region, 0, c)),
        ],
        out_specs=pl.BlockSpec((1, rows, dc), lambda bi, c, t, ts: (bi, off + t, c)),
    )
    return pl.pallas_call(
        functools.partial(_combine_kernel, nt=nt, win=win, cap=CAPACITY_FACTOR * n // N_EXPERTS,
                          n_starts=b * nt * N_EXPERTS, tile=tile),
        name="moe_combine",
        grid_spec=grid_spec,
        out_shape=jax.ShapeDtypeStruct((b, s, d), F32),
        input_output_aliases={4: 0},
        compiler_params=_params(("parallel", "parallel", "arbitrary")),
    )(tstart, ye, posc, gate, x, g2)


def _moe(x, h, aff, g2, w_gate, w_up, w_down, layer, token_sets):
    routed, row0 = [], 0
    for row_off, n, region in token_sets:
        cap = CAPACITY_FACTOR * n // N_EXPERTS
        cap_rows = -(-cap // LANES) * LANES
        posc, gate, tst = _route(aff, row_off=row_off, n=n, cap=cap)
        tstart = tst[:, :, 0, :N_EXPERTS].reshape(-1)
        idx = _slot_index(tstart, posc, row_off=row_off, n=n, cap=cap)
        routed.append((row_off, n, region, row0, cap_rows, tstart, posc, gate, idx))
        row0 += cap_rows
    idx_all = jnp.concatenate([r[8][:, :, :r[4]] for r in routed], axis=2)
    ye = _ffn(_gather(idx_all, h), w_gate, w_up, w_down, layer)
    for row_off, n, region, r0, cap_rows, tstart, posc, gate, _ in routed:
        x = _combine(tstart, ye, posc, gate, x, g2, row_off=row_off, n=n, region=region, row0=r0, cap_rows=cap_rows)
    return x


def _rope_tables(n_lat, n_ctx, head_dim):
    quarter = head_dim // 4
    inv = ROPE_BASE ** (-jnp.arange(quarter, dtype=F32) / quarter)
    pos = jnp.arange(n_lat)
    row = (pos // GRID_W).astype(F32)[:, None] * inv[None, :]
    col = (pos % GRID_W).astype(F32)[:, None] * inv[None, :]
    cos = jnp.concatenate([jnp.cos(row), jnp.cos(row), jnp.cos(col), jnp.cos(col)], axis=1)
    sin = jnp.concatenate([-jnp.sin(row), jnp.sin(row), -jnp.sin(col), jnp.sin(col)], axis=1)
    cos = jnp.concatenate([cos, jnp.ones((n_ctx, head_dim), F32)], axis=0)
    sin = jnp.concatenate([sin, jnp.zeros((n_ctx, head_dim), F32)], axis=0)
    return cos, sin


def kernel(x, c, ctx, c_ctx, w_mod, b_mod, norm_gain, final_gain, ret_w_in, ret_w_out, ret_decay,
           win_w_qkv, win_w_o, win_sink, fno_w_o, router_w, exp_w_gate, exp_w_up, exp_w_down):
    b, n_lat, d = x.shape
    n_ctx = ctx.shape[1]
    depth = w_mod.shape[0]
    s = n_lat + n_ctx
    assert s % ROW_TILE == 0 and n_lat % n_ctx == 0
    assert n_lat % (FFT_INNER * 4) == 0 and b + 1 <= 8 and n_ctx % ROUTE_TILE == 0

    xs = jnp.concatenate([x, ctx], axis=1)
    cvec = jnp.zeros((8, d), F32).at[:b].set(c).at[b].set(c_ctx)
    mod = _modulation(cvec, w_mod, b_mod)

    def mod_pair(i, k):
        lat = mod[i, :b, k * d:(k + 1) * d]
        cx = jnp.broadcast_to(mod[i, b, k * d:(k + 1) * d], (b, d))
        return jnp.stack([lat, cx], axis=1).reshape(b, 2, 1, d)

    ret_dk = d // RET_HEADS
    ret_dv = 2 * ret_dk
    win_hd = d // WIN_HEADS
    ret_cos, ret_sin = _rope_tables(n_lat, n_ctx, ret_dk)
    win_cos, win_sin = _rope_tables(n_lat, n_ctx, win_hd)

    for i in range(depth):
        kind, j = i % N_MIXERS, i // N_MIXERS
        last = i == depth - 1
        sh1, sc1, g1, sh2, sc2, g2 = [mod_pair(i, k) for k in range(6)]
        h = _norm_mod(xs, norm_gain[i, 0], sh1, sc1, n_lat)
        if kind == 0:
            hk, hv = RET_HEADS * ret_dk, RET_HEADS * ret_dv
            w_in = ret_w_in[j].astype(BF16)
            qk = _mm_proj(h, w_in, 0, 2 * hk, PROJ_COL_TILE,
                          rope=(ret_cos, ret_sin, hk, 1.0, ret_dk ** -0.5, ret_dk))
            vg = _mm_proj(h, w_in, 2 * hk, 3 * hv, PROJ_COL_TILE)
            scan = functools.partial(_ret_scan, n_lat=n_lat, n_ctx=n_ctx, heads=RET_HEADS, dk=ret_dk, dv=ret_dv)
            y_b = scan(qk, vg, ret_decay[j, 1:2], None, reverse=True)
            y = scan(qk, vg, ret_decay[j, 0:1], y_b, reverse=False)
            w_out = ret_w_out[j]
        elif kind == 1:
            nq, nkv = WIN_HEADS * win_hd, WIN_KV_HEADS * win_hd
            w_qkv = win_w_qkv[j].astype(BF16)
            qk = _mm_proj(h, w_qkv, 0, nq + nkv, COL_TILE, rope=(win_cos, win_sin, nq, win_hd ** -0.5, 1.0, win_hd))
            v = _mm_proj(h, w_qkv, nq + nkv, nkv, COL_TILE)
            y = _win_attn(qk, v, win_sink[j], n_lat=n_lat, n_ctx=n_ctx, heads=WIN_HEADS,
                          kv_heads=WIN_KV_HEADS, hd=win_hd)
            w_out = win_w_o[j]
        else:
            cg = d // FOURIER_GROUPS
            cc, sc = _cos_sin(cg)
            w_ch = jnp.asarray(np.concatenate([cc, sc], axis=1) / math.sqrt(cg), BF16)
            uv = _mm_groups(h, w_ch, FOURIER_GROUPS)
            y = _fourier_positions(uv, n_lat, n_ctx, d)
            w_out = fno_w_o[j]
        xs = _mm_res(y, w_out.astype(BF16), xs, g1[:, 0], g1[:, 1], n_lat)
        h2, aff = _norm_mod(xs, norm_gain[i, 1], sh2, sc2, n_lat, router_w=router_w[i])
        token_sets = [(0, n_lat, 0)] + ([] if last else [(n_lat, n_ctx, 1)])
        xs = _moe(xs, h2, aff, g2, exp_w_gate, exp_w_up, exp_w_down, i, token_sets)

    zeros = jnp.zeros((b, 2, 1, d), F32)
    return _norm_mod(xs, final_gain, zeros, zeros, n_lat, out_dtype=F32, rows=n_lat)
```

```python
import functools
import math

import numpy as np
import jax
import jax.numpy as jnp
from jax import lax
from jax.experimental import pallas as pl
from jax.experimental.pallas import tpu as pltpu

F32 = jnp.float32
BF16 = jnp.bfloat16
I32 = jnp.int32

GRID_W = 64
N_MIXERS = 3
RET_HEADS = 8
RET_CHUNK = 256
WIN_HEADS = 16
WIN_KV_HEADS = 4
WIN_BLOCK = 128
FOURIER_GROUPS = 4
FFT_INNER = 64
N_EXPERTS = 16
CAPACITY_FACTOR = 2
ROPE_BASE = 10000.0
NORM_EPS = 1e-6
NEG_INF = -1e30

LANES = 128
ROUTE_TILE = 128
SLOT_ALIGN = 16
ROW_TILE = 768
MM_ROW_CHUNK = 256
COL_TILE = 512
RESIDENT_WEIGHT_BYTES = 8 * 1024 * 1024
PROJ_COL_TILE = 2048
NORM_TILES = (768, 512, 256)
VMEM_LIMIT_BYTES = 56 * 1024 * 1024


def _params(sem):
    return pltpu.CompilerParams(dimension_semantics=sem, vmem_limit_bytes=VMEM_LIMIT_BYTES)


def _dot(a, b):
    return jnp.dot(a, b, preferred_element_type=F32)


def _dot_nt(a, b):
    return lax.dot_general(a, b, (((1,), (1,)), ((), ())), preferred_element_type=F32)


def _split_bf16(x):
    hi = x.astype(BF16)
    lo = (x - hi.astype(F32)).astype(BF16)
    return hi, lo


def _silu(x):
    return x / (1.0 + jnp.exp(-x))


def _mod_kernel(c_ref, w_ref, b_ref, o_ref):
    s = _silu(c_ref[...])
    sh, sl = _split_bf16(s)
    wh, wl = _split_bf16(w_ref[0])
    o_ref[0] = _dot(sh, wh) + _dot(sl, wh) + _dot(sh, wl) + b_ref[0]


def _modulation(cvec, w_mod, b_mod):
    depth, d, n = w_mod.shape
    tn = 1024
    return pl.pallas_call(
        _mod_kernel,
        name="modulation",
        grid=(depth, n // tn),
        in_specs=[
            pl.BlockSpec((8, d), lambda i, j: (0, 0)),
            pl.BlockSpec((1, d, tn), lambda i, j: (i, 0, j)),
            pl.BlockSpec((1, 1, tn), lambda i, j: (i, 0, j)),
        ],
        out_specs=pl.BlockSpec((1, 8, tn), lambda i, j: (i, 0, j)),
        out_shape=jax.ShapeDtypeStruct((depth, 8, n), F32),
        compiler_params=_params(("parallel", "parallel")),
    )(cvec, w_mod, b_mod.reshape(depth, 1, n))


def _normed(x_ref, gain_ref, shift_ref, scale_ref, n_lat):
    x = x_ref[0]
    tr = x.shape[0]
    ms = jnp.mean(x * x, axis=-1, keepdims=True)
    y = x * lax.rsqrt(ms + NORM_EPS) * gain_ref[...]
    is_lat = pl.program_id(1) * tr + lax.broadcasted_iota(I32, (tr, 1), 0) < n_lat
    scale = jnp.where(is_lat, scale_ref[0, 0], scale_ref[0, 1])
    shift = jnp.where(is_lat, shift_ref[0, 0], shift_ref[0, 1])
    return y * (1.0 + scale) + shift


def _norm_mod_kernel(x_ref, gain_ref, shift_ref, scale_ref, o_ref, *, n_lat):
    o_ref[0] = _normed(x_ref, gain_ref, shift_ref, scale_ref, n_lat).astype(o_ref.dtype)


def _pack_bf16_pairs(h):
    half = h.shape[1] // 2
    bits = lax.bitcast_convert_type(h.astype(BF16).astype(F32), jnp.uint32)
    word = lax.shift_right_logical(bits[:, :half], jnp.uint32(16)) | (bits[:, half:] & jnp.uint32(0xFFFF0000))
    return lax.bitcast_convert_type(word, I32)


def _unpack_bf16_pairs(word):
    bits = lax.bitcast_convert_type(word, jnp.uint32)
    lo = lax.bitcast_convert_type(lax.shift_left(bits, jnp.uint32(16)), F32)
    hi = lax.bitcast_convert_type(bits & jnp.uint32(0xFFFF0000), F32)
    return jnp.concatenate([lo, hi], axis=1).astype(BF16)


def _norm_router_kernel(x_ref, gain_ref, shift_ref, scale_ref, wr_ref, o_ref, aff_ref, *, n_lat):
    h = _normed(x_ref, gain_ref, shift_ref, scale_ref, n_lat)
    packed = _pack_bf16_pairs(h)
    o_ref[0] = packed.reshape(o_ref.shape[1:])
    hh, hl = _split_bf16(h)
    wh, wl = _split_bf16(wr_ref[...])
    logits = _dot(hh, wh) + _dot(hl, wh) + _dot(hh, wl)
    lane = lax.broadcasted_iota(I32, logits.shape, 1)
    valid = lane < N_EXPERTS
    logits = jnp.where(valid, logits, -jnp.inf)
    m = jnp.max(logits, axis=-1, keepdims=True)
    p = jnp.exp(logits - m)
    aff = p / jnp.sum(p, axis=-1, keepdims=True)
    aff_ref[0] = jnp.where(valid, aff, 0.0)


def _norm_mod(x, gain, shift, scale, n_lat, *, out_dtype=BF16, rows=None, router_w=None):
    b, s, d = x.shape
    rows = s if rows is None else rows
    tr = next(t for t in NORM_TILES if rows % t == 0 and s % t == 0)
    both = lambda bi, t: (bi, 0, 0, 0)
    in_specs = [
        pl.BlockSpec((1, tr, d), lambda bi, t: (bi, t, 0)),
        pl.BlockSpec((1, d), lambda bi, t: (0, 0)),
        pl.BlockSpec((1, 2, 1, d), both),
        pl.BlockSpec((1, 2, 1, d), both),
    ]
    args = [x, gain.reshape(1, d), shift, scale]
    out_specs = pl.BlockSpec((1, tr, d), lambda bi, t: (bi, t, 0))
    out_shape = jax.ShapeDtypeStruct((b, rows, d), out_dtype)
    kern = functools.partial(_norm_mod_kernel, n_lat=n_lat)
    if router_w is not None:
        wr = jnp.zeros((d, LANES), F32).at[:, :N_EXPERTS].set(router_w)
        in_specs.append(pl.BlockSpec((d, LANES), lambda bi, t: (0, 0)))
        args.append(wr)
        slab = (d // 2 // LANES, LANES)
        out_specs = [pl.BlockSpec((1, tr) + slab, lambda bi, t: (bi, t, 0, 0)),
                     pl.BlockSpec((1, tr, LANES), lambda bi, t: (bi, t, 0))]
        out_shape = [jax.ShapeDtypeStruct((b, rows) + slab, I32),
                     jax.ShapeDtypeStruct((b, rows, LANES), F32)]
        kern = functools.partial(_norm_router_kernel, n_lat=n_lat)
    return pl.pallas_call(
        kern,
        name="norm_mod" if router_w is None else "norm_router",
        grid=(b, rows // tr),
        in_specs=in_specs,
        out_specs=out_specs,
        out_shape=out_shape,
        compiler_params=_params(("parallel", "parallel")),
    )(*args)


def _rope_partner(xs, quarter):
    if 2 * quarter == LANES:
        return pltpu.roll(xs, quarter, 1)
    back = pltpu.roll(xs, quarter, 1)
    fwd = pltpu.roll(xs, LANES - quarter, 1)
    lane = lax.broadcasted_iota(I32, xs.shape, 1)
    return jnp.where((lane % (2 * quarter)) < quarter, fwd, back)


def _row_chunks(tm):
    return [slice(r, r + MM_ROW_CHUNK) for r in range(0, tm, MM_ROW_CHUNK)]


def _mm_rope_kernel(a_ref, w_ref, cos_ref, sin_ref, o_ref, *, n_q, qscale, kscale, head_dim):
    sc = jnp.where(pl.program_id(1) >= n_q, kscale, qscale).astype(F32)
    for rows in _row_chunks(a_ref.shape[1]):
        acc = _dot(a_ref[0, rows, :], w_ref[0])
        for s in range(acc.shape[1] // LANES):
            cols = slice(s * LANES, (s + 1) * LANES)
            off = (s * LANES) % head_dim
            xs = acc[:, cols]
            rot = xs * cos_ref[rows, off:off + LANES] + _rope_partner(xs, head_dim // 4) * sin_ref[rows, off:off + LANES]
            o_ref[0, rows, cols] = (rot * sc).astype(o_ref.dtype)


def _mm_plain_kernel(a_ref, w_ref, o_ref):
    for rows in _row_chunks(a_ref.shape[1]):
        o_ref[0, rows, :] = _dot(a_ref[0, rows, :], w_ref[0]).astype(o_ref.dtype)


def _mm_proj(a, w, layer, col0, ncols, tn, rope=None):
    b, s, k = a.shape
    tm = ROW_TILE
    assert col0 % tn == 0 and ncols % tn == 0
    in_specs = [
        pl.BlockSpec((1, tm, k), lambda bi, j, i: (bi, i, 0)),
        pl.BlockSpec((1, k, tn), lambda bi, j, i: (layer, 0, col0 // tn + j)),
    ]
    args = [a, w]
    if rope is None:
        kern, name = _mm_plain_kernel, "mm_proj"
    else:
        cos_t, sin_t, n_q_cols, qscale, kscale, head_dim = rope
        assert tn % head_dim == 0 and n_q_cols % tn == 0
        kern = functools.partial(_mm_rope_kernel, n_q=n_q_cols // tn, qscale=qscale, kscale=kscale,
                                 head_dim=head_dim)
        name = "mm_proj_rope"
        in_specs += [pl.BlockSpec((tm, head_dim), lambda bi, j, i: (i, 0))] * 2
        args += [cos_t, sin_t]
    return pl.pallas_call(
        kern,
        name=name,
        grid=(b, ncols // tn, s // tm),
        in_specs=in_specs,
        out_specs=pl.BlockSpec((1, tm, tn), lambda bi, j, i: (bi, i, j)),
        out_shape=jax.ShapeDtypeStruct((b, s, ncols), BF16),
        compiler_params=_params(("parallel", "parallel", "arbitrary")),
    )(*args)


def _mm_res_kernel(a_ref, w_ref, x_ref, gl_ref, gc_ref, o_ref, *, n_lat):
    tm = a_ref.shape[1]
    for rows in _row_chunks(tm):
        acc = _dot(a_ref[0, rows, :], w_ref[0])
        row = pl.program_id(2) * tm + rows.start + lax.broadcasted_iota(I32, (acc.shape[0], 1), 0)
        gate = jnp.where(row < n_lat, gl_ref[0], gc_ref[0])
        o_ref[0, rows, :] = x_ref[0, rows, :] + gate * acc


def _mm_res(a, w, layer, x, gate_lat, gate_ctx, n_lat):
    b, s, k = a.shape
    n = w.shape[2]
    tm = ROW_TILE
    tn = n
    while k * tn * 2 > RESIDENT_WEIGHT_BYTES:
        tn //= 2
    return pl.pallas_call(
        functools.partial(_mm_res_kernel, n_lat=n_lat),
        name="mm_residual",
        grid=(b, n // tn, s // tm),
        in_specs=[
            pl.BlockSpec((1, tm, k), lambda bi, j, i: (bi, i, 0)),
            pl.BlockSpec((1, k, tn), lambda bi, j, i: (layer, 0, j)),
            pl.BlockSpec((1, tm, tn), lambda bi, j, i: (bi, i, j)),
            pl.BlockSpec((1, 1, tn), lambda bi, j, i: (bi, 0, j)),
            pl.BlockSpec((1, 1, tn), lambda bi, j, i: (bi, 0, j)),
        ],
        out_specs=pl.BlockSpec((1, tm, tn), lambda bi, j, i: (bi, i, j)),
        out_shape=jax.ShapeDtypeStruct((b, s, n), F32),
        compiler_params=_params(("parallel", "parallel", "arbitrary")),
    )(a, w, x, gate_lat, gate_ctx)


def _mm_groups_kernel(a_ref, w_ref, o_ref):
    o_ref[0] = _dot(a_ref[0], w_ref[...]).astype(o_ref.dtype)


def _mm_groups(a, w, groups):
    b, s, d = a.shape
    cg = d // groups
    tm = ROW_TILE
    return pl.pallas_call(
        _mm_groups_kernel,
        name="mm_channel_dft",
        grid=(b, s // tm, 2 * groups),
        in_specs=[
            pl.BlockSpec((1, tm, cg), lambda bi, i, j: (bi, i, j % groups)),
            pl.BlockSpec((cg, cg), lambda bi, i, j: (0, j // groups)),
        ],
        out_specs=pl.BlockSpec((1, tm, cg), lambda bi, i, j: (bi, i, j)),
        out_shape=jax.ShapeDtypeStruct((b, s, 2 * d), F32),
        compiler_params=_params(("parallel", "parallel", "arbitrary")),
    )(a, w)


def _ret_scan_kernel(*refs, heads, dk, dv, reverse, add_in):
    if add_in:
        dec_ref, q_ref, k_ref, v_ref, g_ref, yin_ref, o_ref, s_ref, qd_ref, kd_ref, in_ref, cd_ref = refs
    else:
        dec_ref, q_ref, k_ref, v_ref, g_ref, o_ref, s_ref, qd_ref, kd_ref, in_ref, cd_ref = refs
        yin_ref = None
    c = q_ref.shape[1]
    j = pl.program_id(1)

    @pl.when(j == 0)
    def _():
        s_ref[...] = jnp.zeros_like(s_ref)
        m_col = lax.broadcasted_iota(I32, (c, LANES), 0).astype(F32)
        m_row = lax.broadcasted_iota(I32, (c, c), 0).astype(F32)
        n_row = lax.broadcasted_iota(I32, (c, c), 1).astype(F32)
        for h in range(heads):
            lg = -jnp.exp(dec_ref[:, h:h + 1])
            if reverse:
                q_pow, k_pow, diff = c - m_col, m_col, n_row - m_row
            else:
                q_pow, k_pow, diff = m_col + 1.0, c - 1.0 - m_col, m_row - n_row
            qd_ref[h] = jnp.exp(lg * q_pow)
            kd_ref[h] = jnp.exp(lg * k_pow)
            in_ref[h] = jnp.where(diff >= 0, jnp.exp(lg * jnp.maximum(diff, 0.0)), 0.0)
            cd_ref[h] = jnp.exp(jnp.broadcast_to(lg, (8, LANES)) * float(c))

    for h in range(heads):
        q = q_ref[0, :, h * dk:(h + 1) * dk]
        k = k_ref[0, :, h * dk:(h + 1) * dk]
        v = v_ref[0, :, h * dv:(h + 1) * dv]
        qdec = jnp.concatenate([qd_ref[h]] * (dk // LANES), axis=1)
        kdec = jnp.concatenate([kd_ref[h]] * (dk // LANES), axis=1)
        state = s_ref[h]
        cross = _dot((q.astype(F32) * qdec).astype(BF16), state.astype(BF16))
        scores = _dot_nt(q, k) * in_ref[h]
        o = cross + _dot(scores.astype(BF16), v)
        k_t = (k.astype(F32) * kdec).T.astype(BF16)
        s_ref[h] = state * cd_ref[h][0:1, 0:1] + _dot(k_t, v)
        mu = jnp.mean(o, axis=-1, keepdims=True)
        cen = o - mu
        var = jnp.mean(cen * cen, axis=-1, keepdims=True)
        g = g_ref[0, :, h * dv:(h + 1) * dv].astype(F32)
        y = cen * lax.rsqrt(var + NORM_EPS) * _silu(g)
        if add_in:
            y = y + yin_ref[0, :, h * dv:(h + 1) * dv].astype(F32)
        o_ref[0, :, h * dv:(h + 1) * dv] = y.astype(o_ref.dtype)


def _ret_scan(qk, vg, decay_row, y_in, *, n_lat, n_ctx, heads, dk, dv, reverse):
    b, s, _ = qk.shape
    c = RET_CHUNK
    lat_chunks, ctx_chunks = n_lat // c, n_ctx // c
    steps = lat_chunks + ctx_chunks
    hk, hv = heads * dk, heads * dv
    if reverse:
        chunk = lambda j: steps - 1 - j
    else:
        chunk = lambda j: jnp.where(j < ctx_chunks, lat_chunks + j, j - ctx_chunks)
    gate_blk = 2 if reverse else 1
    in_specs = [
        pl.BlockSpec((1, heads), lambda bi, j: (0, 0)),
        pl.BlockSpec((1, c, hk), lambda bi, j: (bi, chunk(j), 0)),
        pl.BlockSpec((1, c, hk), lambda bi, j: (bi, chunk(j), 1)),
        pl.BlockSpec((1, c, hv), lambda bi, j: (bi, chunk(j), 0)),
        pl.BlockSpec((1, c, hv), lambda bi, j: (bi, chunk(j), gate_blk)),
    ]
    args = [decay_row, qk, qk, vg, vg]
    if y_in is not None:
        in_specs.append(pl.BlockSpec((1, c, hv), lambda bi, j: (bi, chunk(j), 0)))
        args.append(y_in)
    kern = functools.partial(_ret_scan_kernel, heads=heads, dk=dk, dv=dv, reverse=reverse,
                             add_in=y_in is not None)
    return pl.pallas_call(
        kern,
        name="ret_scan_bwd" if reverse else "ret_scan_fwd",
        grid=(b, steps),
        in_specs=in_specs,
        out_specs=pl.BlockSpec((1, c, hv), lambda bi, j: (bi, chunk(j), 0)),
        out_shape=jax.ShapeDtypeStruct((b, s, hv), BF16),
        scratch_shapes=[
            pltpu.VMEM((heads, dk, dv), F32),
            pltpu.VMEM((heads, c, LANES), F32),
            pltpu.VMEM((heads, c, LANES), F32),
            pltpu.VMEM((heads, c, c), F32),
            pltpu.VMEM((heads, 8, LANES), F32),
        ],
        compiler_params=_params(("parallel", "arbitrary")),
    )(*args)


def _win_attn_kernel(sink_ref, q_ref, kc_ref, vc_ref, kp_ref, kq_ref, kn_ref, vp_ref, vq_ref, vn_ref,
                     o_ref, *, lat_tiles, heads, kv_heads, hd):
    qt = pl.program_id(1)
    blk = q_ref.shape[1]
    n_ctx = kc_ref.shape[1]
    grp = heads // kv_heads
    rows = grp * blk
    qi = lax.broadcasted_iota(I32, (rows, blk), 0) % blk
    kj = lax.broadcasted_iota(I32, (rows, blk), 1)
    tq = qt + jnp.zeros((rows, blk), I32)
    ok_cur = tq < lat_tiles
    ok_prev = (kj >= qi) & ok_cur & (tq >= 1)
    ok_next = (kj <= qi) & (tq + 1 < lat_tiles)
    bias = jnp.concatenate(
        [jnp.zeros((rows, n_ctx), F32)]
        + [jnp.where(ok, 0.0, NEG_INF).astype(F32) for ok in (ok_prev, ok_cur, ok_next)], axis=1)
    head_row = lax.broadcasted_iota(I32, (rows, 1), 0) // blk
    for kv in range(kv_heads):
        cs = slice(kv * hd, (kv + 1) * hd)
        keys = jnp.concatenate([kc_ref[0, :, cs], kp_ref[0, :, cs], kq_ref[0, :, cs], kn_ref[0, :, cs]], axis=0)
        vals = jnp.concatenate([vc_ref[0, :, cs], vp_ref[0, :, cs], vq_ref[0, :, cs], vn_ref[0, :, cs]], axis=0)
        q = jnp.concatenate([q_ref[0, :, (kv * grp + g) * hd:(kv * grp + g + 1) * hd] for g in range(grp)], axis=0)
        sink = jnp.zeros((rows, 1), F32)
        for g in range(grp):
            h = kv * grp + g
            sink = jnp.where(head_row == g, sink_ref[:, h:h + 1], sink)
        s = _dot_nt(q, keys) + bias
        m = jnp.maximum(jnp.max(s, axis=-1, keepdims=True), sink)
        p = jnp.exp(s - m)
        den = jnp.sum(p, axis=-1, keepdims=True) + jnp.exp(sink - m)
        o = _dot(p.astype(BF16), vals) / den
        for g in range(grp):
            h = kv * grp + g
            o_ref[0, :, h * hd:(h + 1) * hd] = o[g * blk:(g + 1) * blk].astype(o_ref.dtype)


def _win_attn(qk, v, sink, *, n_lat, n_ctx, heads, kv_heads, hd):
    b, s, _ = qk.shape
    blk = WIN_BLOCK
    lat_tiles = n_lat // blk
    tiles = s // blk
    kvw = kv_heads * hd
    k_col = (heads * hd) // kvw
    v_col = 0
    ctx_blk = n_lat // n_ctx
    prev = lambda t: jnp.maximum(t - 1, 0)
    nxt = lambda t: jnp.minimum(t + 1, tiles - 1)
    sink_row = jnp.zeros((1, LANES), F32).at[0, :heads].set(sink.astype(F32))
    kern = functools.partial(_win_attn_kernel, lat_tiles=lat_tiles, heads=heads, kv_heads=kv_heads, hd=hd)
    return pl.pallas_call(
        kern,
        name="win_attn",
        grid=(b, tiles),
        in_specs=[
            pl.BlockSpec((1, LANES), lambda bi, t: (0, 0)),
            pl.BlockSpec((1, blk, heads * hd), lambda bi, t: (bi, t, 0)),
            pl.BlockSpec((1, n_ctx, kvw), lambda bi, t: (bi, ctx_blk, k_col)),
            pl.BlockSpec((1, n_ctx, kvw), lambda bi, t: (bi, ctx_blk, v_col)),
            pl.BlockSpec((1, blk, kvw), lambda bi, t: (bi, prev(t), k_col)),
            pl.BlockSpec((1, blk, kvw), lambda bi, t: (bi, t, k_col)),
            pl.BlockSpec((1, blk, kvw), lambda bi, t: (bi, nxt(t), k_col)),
            pl.BlockSpec((1, blk, kvw), lambda bi, t: (bi, prev(t), v_col)),
            pl.BlockSpec((1, blk, kvw), lambda bi, t: (bi, t, v_col)),
            pl.BlockSpec((1, blk, kvw), lambda bi, t: (bi, nxt(t), v_col)),
        ],
        out_specs=pl.BlockSpec((1, blk, heads * hd), lambda bi, t: (bi, t, 0)),
        out_shape=jax.ShapeDtypeStruct((b, s, heads * hd), BF16),
        compiler_params=_params(("parallel", "parallel")),
    )(sink_row, qk, qk, v, qk, qk, qk, v, v, v)


SUBLANES = 8
PACKED_ROWS = 16


def _dft_stage_a_kernel(m_ref, u_ref, v_ref, re_ref, im_ref):
    n1, sub, tc = u_ref.shape[1:]
    rows = n1 * sub
    stacked = jnp.concatenate([u_ref[0].reshape(rows, tc), v_ref[0].reshape(rows, tc)], axis=0)
    out = _dot(m_ref[...], stacked.astype(BF16))
    re_ref[0] = out[:rows].reshape(n1, sub, tc)
    im_ref[0] = out[rows:].reshape(n1, sub, tc)


def _dft_stage_b_kernel(m_ref, re_ref, im_ref, twc_ref, tws_ref, o_ref):
    n2, sub, tc = o_ref.shape[1:]
    reps = tc // LANES
    ar, ai = re_ref[0], im_ref[0]
    twc = jnp.concatenate([twc_ref[...]] * reps, axis=1)
    tws = jnp.concatenate([tws_ref[...]] * reps, axis=1)
    stacked = jnp.concatenate([ar * twc + ai * tws, ai * twc - ar * tws], axis=0).astype(BF16)
    out = _dot(m_ref[...], stacked)
    o_ref[0] = out.reshape(n2, sub, tc).astype(o_ref.dtype)


def _dft_ctx_kernel(m_ref, x_ref, o_ref, *, d):
    x = x_ref[0]
    stacked = jnp.concatenate([x[:, :d], x[:, d:]], axis=0).astype(BF16)
    o_ref[0] = _dot(m_ref[...], stacked).astype(o_ref.dtype)


def _cos_sin(n):
    ang = 2.0 * np.pi * np.outer(np.arange(n), np.arange(n)) / float(n)
    return np.cos(ang), np.sin(ang)


def _fourier_positions(uv, n_lat, n_ctx, d):
    b, s, _ = uv.shape
    n2 = FFT_INNER
    n1 = n_lat // n2
    tc = min(512, d)
    ca, sa = _cos_sin(n1)
    eye = np.eye(SUBLANES)
    ka, ks = np.kron(ca, eye) / math.sqrt(n1), np.kron(sa, eye) / math.sqrt(n1)
    mat_a = jnp.asarray(np.block([[ka, -ks], [-ks, -ka]]), BF16)
    uv4 = uv.reshape(b, s // n2, n2, 2 * d)
    blk_a = (1, n1, SUBLANES, tc)
    a_re, a_im = pl.pallas_call(
        _dft_stage_a_kernel,
        name="dft_stage_a",
        grid=(b, n2 // SUBLANES, d // tc),
        in_specs=[
            pl.BlockSpec(mat_a.shape, lambda bi, cg, jc: (0, 0)),
            pl.BlockSpec(blk_a, lambda bi, cg, jc: (bi, 0, cg, jc)),
            pl.BlockSpec(blk_a, lambda bi, cg, jc: (bi, 0, cg, d // tc + jc)),
        ],
        out_specs=[pl.BlockSpec(blk_a, lambda bi, cg, jc: (bi, 0, cg, jc))] * 2,
        out_shape=[jax.ShapeDtypeStruct((b, n1, n2, d), F32)] * 2,
        compiler_params=_params(("parallel", "parallel", "parallel")),
    )(mat_a, uv4, uv4)
    cb, sb = _cos_sin(n2)
    eye = np.eye(PACKED_ROWS)
    kron_b = lambda m: np.einsum("kc,ab->kabc", m, eye).reshape(n2 * PACKED_ROWS, PACKED_ROWS * n2)
    mat_b = jnp.asarray(np.concatenate([kron_b(cb), kron_b(sb)], axis=1) / math.sqrt(n2), BF16)
    phi = 2.0 * np.pi * np.outer(np.arange(n1), np.arange(n2)) / float(n_lat)
    twc = jnp.asarray(np.repeat(np.cos(phi).reshape(-1, 1), LANES, axis=1), F32)
    tws = jnp.asarray(np.repeat(np.sin(phi).reshape(-1, 1), LANES, axis=1), F32)
    rows = PACKED_ROWS * n2
    y = pl.pallas_call(
        _dft_stage_b_kernel,
        name="dft_stage_b",
        grid=(b, n1 // PACKED_ROWS, d // tc),
        in_specs=[
            pl.BlockSpec(mat_b.shape, lambda bi, kb, jc: (0, 0)),
            pl.BlockSpec((1, rows, tc), lambda bi, kb, jc: (bi, kb, jc)),
            pl.BlockSpec((1, rows, tc), lambda bi, kb, jc: (bi, kb, jc)),
            pl.BlockSpec((rows, LANES), lambda bi, kb, jc: (kb, 0)),
            pl.BlockSpec((rows, LANES), lambda bi, kb, jc: (kb, 0)),
        ],
        out_specs=pl.BlockSpec((1, n2, PACKED_ROWS, tc), lambda bi, kb, jc: (bi, 0, kb, jc)),
        out_shape=jax.ShapeDtypeStruct((b, n2, n1, d), BF16),
        compiler_params=_params(("parallel", "parallel", "parallel")),
    )(mat_b, a_re.reshape(b, n_lat, d), a_im.reshape(b, n_lat, d), twc, tws)
    cc, sc = _cos_sin(n_ctx)
    mat_c = jnp.asarray(np.concatenate([cc, -sc], axis=1) / math.sqrt(n_ctx), BF16)
    y_ctx = pl.pallas_call(
        functools.partial(_dft_ctx_kernel, d=d),
        name="dft_ctx",
        grid=(b,),
        in_specs=[
            pl.BlockSpec(mat_c.shape, lambda bi: (0, 0)),
            pl.BlockSpec((1, n_ctx, 2 * d), lambda bi: (bi, n_lat // n_ctx, 0)),
        ],
        out_specs=pl.BlockSpec((1, n_ctx, d), lambda bi: (bi, 0, 0)),
        out_shape=jax.ShapeDtypeStruct((b, n_ctx, d), BF16),
        compiler_params=_params(("parallel",)),
    )(mat_c, uv)
    return jnp.concatenate([y.reshape(b, n_lat, d), y_ctx], axis=1)


LANE_SHIFT = 7
GATHER_UNROLL = 16
TILES_PER_STEP = 8
FFN_ROW_CHUNK = 384
COMBINE_REGION = 64


def _route_kernel(aff_ref, tri_ref, posc_ref, gate_ref, tst_ref,
                  thr_ref, need_ref, ctie_ref, cpos_ref, *, cap):
    step = pl.program_id(1)
    tile = tri_ref.shape[0]
    tps = posc_ref.shape[1] // tile

    @pl.when(step == 0)
    def _():
        def body(it, thr):
            bits = lax.bitcast_convert_type(aff_ref[0], I32)
            cand = thr | jnp.left_shift(jnp.int32(1), 30 - it)
            cnt = jnp.sum(jnp.where(bits >= cand, 1.0, 0.0), axis=0, keepdims=True)
            return jnp.where(cnt >= cap, cand, thr)

        thr = lax.fori_loop(0, 31, body, jnp.zeros((1, LANES), I32))
        bits = lax.bitcast_convert_type(aff_ref[0], I32)
        above = jnp.sum(jnp.where(bits > thr, 1.0, 0.0), axis=0, keepdims=True)
        thr_ref[...] = thr
        need_ref[...] = float(cap) - above
        ctie_ref[...] = jnp.zeros_like(ctie_ref)
        cpos_ref[...] = jnp.zeros_like(cpos_ref)

    thr = thr_ref[...]
    for u in range(tps):
        rows = slice(u * tile, (u + 1) * tile)
        a = aff_ref[0, pl.ds(pl.multiple_of((step * tps + u) * tile, tile), tile), :]
        bits = lax.bitcast_convert_type(a, I32)
        gt = bits > thr
        eq = bits == thr
        eqf = jnp.where(eq, 1.0, 0.0)
        tie_rank = _dot(tri_ref[...], eqf.astype(BF16)) + ctie_ref[...]
        sel = gt | (eq & (tie_rank < need_ref[...]))
        self_ = jnp.where(sel, 1.0, 0.0)
        start = cpos_ref[...]
        pos = jnp.where(sel, _dot(tri_ref[...], self_.astype(BF16)) + start, -1.0)
        posc_ref[0, rows, :] = pos.astype(I32)
        gate_ref[0, rows, :] = jnp.where(sel, a, 0.0)
        tst_ref[0, u] = jnp.broadcast_to(start, (8, LANES)).astype(I32)
        ctie_ref[...] = ctie_ref[...] + jnp.sum(eqf, axis=0, keepdims=True)
        cpos_ref[...] = start + jnp.sum(self_, axis=0, keepdims=True)


def _route(aff, *, row_off, n, cap):
    b = aff.shape[0]
    tile = ROUTE_TILE
    nt = n // tile
    tps = min(TILES_PER_STEP, nt)
    tri = jnp.asarray(np.tril(np.ones((tile, tile)), -1), BF16)
    return pl.pallas_call(
        functools.partial(_route_kernel, cap=cap),
        name="moe_route",
        grid=(b, nt // tps),
        in_specs=[
            pl.BlockSpec((1, n, LANES), lambda bi, t: (bi, row_off // n, 0)),
            pl.BlockSpec((tile, tile), lambda bi, t: (0, 0)),
        ],
        out_specs=[
            pl.BlockSpec((1, tps * tile, LANES), lambda bi, t: (bi, t, 0)),
            pl.BlockSpec((1, tps * tile, LANES), lambda bi, t: (bi, t, 0)),
            pl.BlockSpec((1, tps, 8, LANES), lambda bi, t: (bi, t, 0, 0)),
        ],
        out_shape=[
            jax.ShapeDtypeStruct((b, n, LANES), I32),
            jax.ShapeDtypeStruct((b, n, LANES), F32),
            jax.ShapeDtypeStruct((b, nt, 8, LANES), I32),
        ],
        scratch_shapes=[pltpu.VMEM((1, LANES), I32)] + [pltpu.VMEM((1, LANES), F32)] * 3,
        compiler_params=_params(("parallel", "arbitrary")),
    )(aff, tri)


def _slot_index_kernel(ts_ref, posc_ref, idx_ref, *, nt, row_off, tile):
    bi, step = pl.program_id(0), pl.program_id(1)

    @pl.when(step == 0)
    def _():
        idx_ref[...] = jnp.zeros_like(idx_ref)

    tps = posc_ref.shape[1] // tile
    slot_rows = idx_ref.shape[2]
    lane = lax.broadcasted_iota(I32, (tile, 2 * LANES), 1)
    for u in range(tps):
        t = step * tps + u
        tok = (row_off + t * tile + lax.broadcasted_iota(I32, (tile, 1), 0)).astype(F32)
        for e in range(N_EXPERTS):
            h0 = jnp.minimum(ts_ref[(bi * nt + t) * N_EXPERTS + e] >> LANE_SHIFT, slot_rows - 2)
            hit = (posc_ref[0, u * tile:(u + 1) * tile, e:e + 1] - h0 * LANES) == lane
            vals = jnp.sum(jnp.where(hit, tok, 0.0), axis=0, keepdims=True)
            two_rows = jnp.concatenate([vals[:, :LANES], vals[:, LANES:]], axis=0).astype(I32)
            idx_ref[0, e, pl.ds(h0, 2), :] = idx_ref[0, e, pl.ds(h0, 2), :] + two_rows


def _slot_index(tstart, posc, *, row_off, n, cap):
    b = posc.shape[0]
    tile = ROUTE_TILE
    nt = n // tile
    slot_rows = max(cap, 2 * LANES) // LANES
    tps = min(TILES_PER_STEP, nt)
    grid_spec = pltpu.PrefetchScalarGridSpec(
        num_scalar_prefetch=1,
        grid=(b, nt // tps),
        in_specs=[pl.BlockSpec((1, tps * tile, LANES), lambda bi, t, ts: (bi, t, 0))],
        out_specs=pl.BlockSpec((1, N_EXPERTS, slot_rows, LANES), lambda bi, t, ts: (bi, 0, 0, 0)),
    )
    idx = pl.pallas_call(
        functools.partial(_slot_index_kernel, nt=nt, row_off=row_off, tile=tile),
        name="moe_slot_index",
        grid_spec=grid_spec,
        out_shape=jax.ShapeDtypeStruct((b, N_EXPERTS, slot_rows, LANES), I32),
        compiler_params=_params(("parallel", "arbitrary")),
    )(tstart, posc)
    return idx.reshape(b, N_EXPERTS, slot_rows * LANES)


def _gather_kernel(idx_ref, h_ref, o_ref, buf_ref, sem):
    bi = pl.program_id(1)
    n_rows = buf_ref.shape[0]

    def row_copy(src_row, dst_row, rows):
        return pltpu.make_async_copy(h_ref.at[bi, pl.ds(src_row, rows)], buf_ref.at[pl.ds(dst_row, rows)], sem)

    def issue(g, carry):
        for k in range(GATHER_UNROLL):
            s = g * GATHER_UNROLL + k
            row_copy(idx_ref[0, 0, s], s, 1).start(priority=k % 2)
        return carry

    lax.fori_loop(0, n_rows // GATHER_UNROLL, issue, 0)
    row_copy(0, 0, n_rows).wait()
    o_ref[0, 0] = _unpack_bf16_pairs(buf_ref[...].reshape(n_rows, -1))


def _gather(idx, h):
    b, _, sub, lanes = h.shape
    d = 2 * sub * lanes
    slots = idx.shape[2]
    assert slots % GATHER_UNROLL == 0
    return pl.pallas_call(
        _gather_kernel,
        name="moe_gather",
        grid=(N_EXPERTS, b),
        in_specs=[
            pl.BlockSpec((1, 1, slots), lambda ei, bi: (bi * N_EXPERTS + ei, 0, 0), memory_space=pltpu.SMEM),
            pl.BlockSpec(memory_space=pl.ANY),
        ],
        out_specs=pl.BlockSpec((1, 1, slots, d), lambda ei, bi: (ei, bi, 0, 0)),
        out_shape=jax.ShapeDtypeStruct((N_EXPERTS, b, slots, d), BF16),
        scratch_shapes=[pltpu.VMEM((slots, sub, lanes), I32), pltpu.SemaphoreType.DMA(())],
        compiler_params=_params(("arbitrary", "arbitrary")),
    )(idx.reshape(b * N_EXPERTS, 1, slots), h)


def _ffn_kernel(x_ref, wg_ref, wu_ref, wd_ref, ye_ref, hm_ref, *, n_up):
    st = pl.program_id(1)
    bsz, rows, d = x_ref.shape[1:]
    tf = wg_ref.shape[3]
    chunk = FFN_ROW_CHUNK if rows % FFN_ROW_CHUNK == 0 else rows
    spans = [(bi, r0) for bi in range(bsz) for r0 in range(0, rows, chunk)]

    @pl.when(st < n_up)
    def _():
        wg = wg_ref[0, 0].astype(BF16)
        wu = wu_ref[0, 0].astype(BF16)
        for bi, r0 in spans:
            x = x_ref[0, bi, r0:r0 + chunk, :]
            hm_ref[st, bi * rows + r0:bi * rows + r0 + chunk, :] = (_silu(_dot(x, wg)) * _dot(x, wu)).astype(BF16)

    @pl.when(st >= n_up)
    def _():
        wd = [wd_ref[0, 0, c * tf:(c + 1) * tf, :].astype(BF16) for c in range(n_up)]
        for bi, r0 in spans:
            m0 = bi * rows + r0
            y = _dot(hm_ref[0, m0:m0 + chunk, :], wd[0])
            for c in range(1, n_up):
                y = y + _dot(hm_ref[c, m0:m0 + chunk, :], wd[c])
            ye_ref[0, bi, r0:r0 + chunk, :] = y.astype(ye_ref.dtype)


def _ffn(xe, w_gate, w_up, w_down, layer):
    e, b, rows, d = xe.shape
    f = w_gate.shape[3]
    tf = min(512, f)
    tdc = min(512, d)
    n_up, n_down = f // tf, d // tdc
    up = lambda st: jnp.minimum(st, n_up - 1)
    down = lambda st: jnp.maximum(st - n_up, 0)
    return pl.pallas_call(
        functools.partial(_ffn_kernel, n_up=n_up),
        name="moe_ffn",
        grid=(e, n_up + n_down),
        in_specs=[
            pl.BlockSpec((1, b, rows, d), lambda ei, st: (ei, 0, 0, 0)),
            pl.BlockSpec((1, 1, d, tf), lambda ei, st: (layer, ei, 0, up(st))),
            pl.BlockSpec((1, 1, d, tf), lambda ei, st: (layer, ei, 0, up(st))),
            pl.BlockSpec((1, 1, f, tdc), lambda ei, st: (layer, ei, 0, down(st))),
        ],
        out_specs=pl.BlockSpec((1, b, rows, tdc), lambda ei, st: (ei, 0, 0, down(st))),
        out_shape=jax.ShapeDtypeStruct((e, b, rows, d), BF16),
        scratch_shapes=[pltpu.VMEM((n_up, b * rows, tf), BF16)],
        compiler_params=_params(("parallel", "arbitrary")),
    )(xe, w_gate, w_up, w_down)


def _window_start(ts_ref, idx, cap_rows, win):
    a0 = jnp.minimum(ts_ref[idx] & (-SLOT_ALIGN), cap_rows - win)
    return pl.multiple_of(a0, SLOT_ALIGN)


def _combine_kernel(ts_ref, ye_ref, posc_ref, gate_ref, x_ref, g_ref, o_ref, *, nt, win, cap, n_starts, tile):
    bi, step = pl.program_id(0), pl.program_id(2)
    cap_rows = ye_ref.shape[2]
    tps = posc_ref.shape[1] // tile
    reg = COMBINE_REGION
    assert 2 * reg == LANES and reg <= cap_rows

    for u in range(tps):
        t = step * tps + u
        rows = slice(u * tile, (u + 1) * tile)
        base = (bi * nt + t) * N_EXPERTS
        starts, fits = [], None
        for e in range(N_EXPERTS):
            a0 = _window_start(ts_ref, base + e, cap_rows, reg)
            nxt = ts_ref[jnp.minimum(base + N_EXPERTS + e, n_starts - 1)]
            end = jnp.where(t + 1 < nt, nxt, cap)
            ok = end - a0 <= reg
            fits = ok if fits is None else jnp.logical_and(fits, ok)
            starts.append(a0)

        lane = lax.broadcasted_iota(I32, (tile, LANES), 1)
        upper = lane >= reg
        weights, slabs = [], []
        for p in range(N_EXPERTS // 2):
            e0, e1 = 2 * p, 2 * p + 1
            slot = jnp.where(upper, starts[e1] - reg, starts[e0]) + lane
            pcol = jnp.where(upper, posc_ref[0, rows, e1:e1 + 1], posc_ref[0, rows, e0:e0 + 1])
            gcol = jnp.where(upper, gate_ref[0, rows, e1:e1 + 1], gate_ref[0, rows, e0:e0 + 1])
            weights.append(jnp.where(pcol == slot, gcol, 0.0).astype(BF16))
            slabs += [ye_ref[e0, 0, pl.ds(starts[e0], reg), :], ye_ref[e1, 0, pl.ds(starts[e1], reg), :]]
        acc = _dot(jnp.concatenate(weights, axis=1), jnp.concatenate(slabs, axis=0))
        o_ref[0, rows, :] = x_ref[0, rows, :] + g_ref[0, 0] * acc

        @pl.when(jnp.logical_not(fits))
        def _():
            acc = jnp.zeros((tile, o_ref.shape[2]), F32)
            lane = lax.broadcasted_iota(I32, (tile, win), 1)
            for e in range(N_EXPERTS):
                a0 = _window_start(ts_ref, base + e, cap_rows, win)
                pcol = posc_ref[0, rows, e:e + 1]
                gcol = gate_ref[0, rows, e:e + 1]
                w = jnp.where(pcol == a0 + lane, gcol, 0.0).astype(BF16)
                acc = acc + _dot(w, ye_ref[e, 0, pl.ds(a0, win), :])
            o_ref[0, rows, :] = x_ref[0, rows, :] + g_ref[0, 0] * acc


def _combine(tstart, ye, posc, gate, x, g2, *, row_off, n, region, row0, cap_rows):
    b, s, d = x.shape
    tile = ROUTE_TILE
    nt = n // tile
    dc = 512 if d % 512 == 0 else d
    win = min(2 * tile, cap_rows)
    tps = min(TILES_PER_STEP, nt)
    rows = tps * tile
    assert row_off % rows == 0
    off = row_off // rows
    grid_spec = pltpu.PrefetchScalarGridSpec(
        num_scalar_prefetch=1,
        grid=(b, d // dc, nt // tps),
        in_specs=[
            pl.BlockSpec((N_EXPERTS, 1, cap_rows, dc), lambda bi, c, t, ts: (0, bi, row0 // cap_rows, c)),
            pl.BlockSpec((1, rows, LANES), lambda bi, c, t, ts: (bi, t, 0)),
            pl.BlockSpec((1, rows, LANES), lambda bi, c, t, ts: (bi, t, 0)),
            pl.BlockSpec((1, rows, dc), lambda bi, c, t, ts: (bi, off + t, c)),
            pl.BlockSpec((1, 1, 1, dc), lambda bi, c, t, ts: (bi, region, 0, c)),
        ],
        out_specs=pl.BlockSpec((1, rows, dc), lambda bi, c, t, ts: (bi, off + t, c)),
    )
    return pl.pallas_call(
        functools.partial(_combine_kernel, nt=nt, win=win, cap=CAPACITY_FACTOR * n // N_EXPERTS,
                          n_starts=b * nt * N_EXPERTS, tile=tile),
        name="moe_combine",
        grid_spec=grid_spec,
        out_shape=jax.ShapeDtypeStruct((b, s, d), F32),
        input_output_aliases={4: 0},
        compiler_params=_params(("parallel", "parallel", "arbitrary")),
    )(tstart, ye, posc, gate, x, g2)


def _moe(x, h, aff, g2, w_gate, w_up, w_down, layer, token_sets):
    routed, row0 = [], 0
    for row_off, n, region in token_sets:
        cap = CAPACITY_FACTOR * n // N_EXPERTS
        cap_rows = -(-cap // LANES) * LANES
        posc, gate, tst = _route(aff, row_off=row_off, n=n, cap=cap)
        tstart = tst[:, :, 0, :N_EXPERTS].reshape(-1)
        idx = _slot_index(tstart, posc, row_off=row_off, n=n, cap=cap)
        routed.append((row_off, n, region, row0, cap_rows, tstart, posc, gate, idx))
        row0 += cap_rows
    idx_all = jnp.concatenate([r[8][:, :, :r[4]] for r in routed], axis=2)
    ye = _ffn(_gather(idx_all, h), w_gate, w_up, w_down, layer)
    for row_off, n, region, r0, cap_rows, tstart, posc, gate, _ in routed:
        x = _combine(tstart, ye, posc, gate, x, g2, row_off=row_off, n=n, region=region, row0=r0, cap_rows=cap_rows)
    return x


def _rope_tables(n_lat, n_ctx, head_dim):
    quarter = head_dim // 4
    inv = ROPE_BASE ** (-jnp.arange(quarter, dtype=F32) / quarter)
    pos = jnp.arange(n_lat)
    row = (pos // GRID_W).astype(F32)[:, None] * inv[None, :]
    col = (pos % GRID_W).astype(F32)[:, None] * inv[None, :]
    cos = jnp.concatenate([jnp.cos(row), jnp.cos(row), jnp.cos(col), jnp.cos(col)], axis=1)
    sin = jnp.concatenate([-jnp.sin(row), jnp.sin(row), -jnp.sin(col), jnp.sin(col)], axis=1)
    cos = jnp.concatenate([cos, jnp.ones((n_ctx, head_dim), F32)], axis=0)
    sin = jnp.concatenate([sin, jnp.zeros((n_ctx, head_dim), F32)], axis=0)
    return cos, sin


def kernel(x, c, ctx, c_ctx, w_mod, b_mod, norm_gain, final_gain, ret_w_in, ret_w_out, ret_decay,
           win_w_qkv, win_w_o, win_sink, fno_w_o, router_w, exp_w_gate, exp_w_up, exp_w_down):
    b, n_lat, d = x.shape
    n_ctx = ctx.shape[1]
    depth = w_mod.shape[0]
    s = n_lat + n_ctx
    assert s % ROW_TILE == 0 and n_lat % n_ctx == 0
    assert n_lat % (FFT_INNER * 4) == 0 and b + 1 <= 8 and n_ctx % ROUTE_TILE == 0

    xs = jnp.concatenate([x, ctx], axis=1)
    cvec = jnp.zeros((8, d), F32).at[:b].set(c).at[b].set(c_ctx)
    mod = _modulation(cvec, w_mod, b_mod)

    def mod_pair(i, k):
        lat = mod[i, :b, k * d:(k + 1) * d]
        cx = jnp.broadcast_to(mod[i, b, k * d:(k + 1) * d], (b, d))
        return jnp.stack([lat, cx], axis=1).reshape(b, 2, 1, d)

    ret_dk = d // RET_HEADS
    ret_dv = 2 * ret_dk
    win_hd = d // WIN_HEADS
    ret_cos, ret_sin = _rope_tables(n_lat, n_ctx, ret_dk)
    win_cos, win_sin = _rope_tables(n_lat, n_ctx, win_hd)

    ret_w_in_bf, ret_w_out_bf = ret_w_in.astype(BF16), ret_w_out.astype(BF16)
    win_w_qkv_bf, win_w_o_bf, fno_w_o_bf = win_w_qkv.astype(BF16), win_w_o.astype(BF16), fno_w_o.astype(BF16)

    for i in range(depth):
        kind, j = i % N_MIXERS, i // N_MIXERS
        last = i == depth - 1
        sh1, sc1, g1, sh2, sc2, g2 = [mod_pair(i, k) for k in range(6)]
        h = _norm_mod(xs, norm_gain[i, 0], sh1, sc1, n_lat)
        if kind == 0:
            hk, hv = RET_HEADS * ret_dk, RET_HEADS * ret_dv
            qk = _mm_proj(h, ret_w_in_bf, j, 0, 2 * hk, PROJ_COL_TILE,
                          rope=(ret_cos, ret_sin, hk, 1.0, ret_dk ** -0.5, ret_dk))
            vg = _mm_proj(h, ret_w_in_bf, j, 2 * hk, 3 * hv, PROJ_COL_TILE)
            scan = functools.partial(_ret_scan, n_lat=n_lat, n_ctx=n_ctx, heads=RET_HEADS, dk=ret_dk, dv=ret_dv)
            y_b = scan(qk, vg, ret_decay[j, 1:2], None, reverse=True)
            y = scan(qk, vg, ret_decay[j, 0:1], y_b, reverse=False)
            w_out = ret_w_out_bf
        elif kind == 1:
            nq, nkv = WIN_HEADS * win_hd, WIN_KV_HEADS * win_hd
            qk = _mm_proj(h, win_w_qkv_bf, j, 0, nq + nkv, COL_TILE,
                          rope=(win_cos, win_sin, nq, win_hd ** -0.5, 1.0, win_hd))
            v = _mm_proj(h, win_w_qkv_bf, j, nq + nkv, nkv, COL_TILE)
            y = _win_attn(qk, v, win_sink[j], n_lat=n_lat, n_ctx=n_ctx, heads=WIN_HEADS,
                          kv_heads=WIN_KV_HEADS, hd=win_hd)
            w_out = win_w_o_bf
        else:
            cg = d // FOURIER_GROUPS
            cc, sc = _cos_sin(cg)
            w_ch = jnp.asarray(np.concatenate([cc, sc], axis=1) / math.sqrt(cg), BF16)
            uv = _mm_groups(h, w_ch, FOURIER_GROUPS)
            y = _fourier_positions(uv, n_lat, n_ctx, d)
            w_out = fno_w_o_bf
        xs = _mm_res(y, w_out, j, xs, g1[:, 0], g1[:, 1], n_lat)
        h2, aff = _norm_mod(xs, norm_gain[i, 1], sh2, sc2, n_lat, router_w=router_w[i])
        token_sets = [(0, n_lat, 0)] + ([] if last else [(n_lat, n_ctx, 1)])
        xs = _moe(xs, h2, aff, g2, exp_w_gate, exp_w_up, exp_w_down, i, token_sets)

    zeros = jnp.zeros((b, 2, 1, d), F32)
    return _norm_mod(xs, final_gain, zeros, zeros, n_lat, out_dtype=F32, rows=n_lat)
```

```python
import functools
import math

import numpy as np
import jax
import jax.numpy as jnp
from jax import lax
from jax.experimental import pallas as pl
from jax.experimental.pallas import tpu as pltpu

F32 = jnp.float32
BF16 = jnp.bfloat16
I32 = jnp.int32

GRID_W = 64
N_MIXERS = 3
RET_HEADS = 8
RET_CHUNK = 256
WIN_HEADS = 16
WIN_KV_HEADS = 4
WIN_BLOCK = 128
FOURIER_GROUPS = 4
FFT_INNER = 64
N_EXPERTS = 16
CAPACITY_FACTOR = 2
ROPE_BASE = 10000.0
NORM_EPS = 1e-6
NEG_INF = -1e30

LANES = 128
ROUTE_TILE = 128
SLOT_ALIGN = 16
ROW_TILE = 768
MM_ROW_CHUNK = 256
COL_TILE = 512
RESIDENT_WEIGHT_BYTES = 8 * 1024 * 1024
PROJ_COL_TILE = 2048
NORM_TILES = (768, 512, 256)
VMEM_LIMIT_BYTES = 56 * 1024 * 1024


def _params(sem):
    return pltpu.CompilerParams(dimension_semantics=sem, vmem_limit_bytes=VMEM_LIMIT_BYTES)


def _dot(a, b):
    return jnp.dot(a, b, preferred_element_type=F32)


def _dot_nt(a, b):
    return lax.dot_general(a, b, (((1,), (1,)), ((), ())), preferred_element_type=F32)


def _split_bf16(x):
    hi = x.astype(BF16)
    lo = (x - hi.astype(F32)).astype(BF16)
    return hi, lo


def _silu(x):
    return x / (1.0 + jnp.exp(-x))


def _mod_kernel(c_ref, w_ref, b_ref, o_ref):
    s = _silu(c_ref[...])
    sh, sl = _split_bf16(s)
    wh, wl = _split_bf16(w_ref[0])
    o_ref[0] = _dot(sh, wh) + _dot(sl, wh) + _dot(sh, wl) + b_ref[0]


def _modulation(cvec, w_mod, b_mod):
    depth, d, n = w_mod.shape
    tn = 1024
    return pl.pallas_call(
        _mod_kernel,
        name="modulation",
        grid=(depth, n // tn),
        in_specs=[
            pl.BlockSpec((8, d), lambda i, j: (0, 0)),
            pl.BlockSpec((1, d, tn), lambda i, j: (i, 0, j)),
            pl.BlockSpec((1, 1, tn), lambda i, j: (i, 0, j)),
        ],
        out_specs=pl.BlockSpec((1, 8, tn), lambda i, j: (i, 0, j)),
        out_shape=jax.ShapeDtypeStruct((depth, 8, n), F32),
        compiler_params=_params(("parallel", "parallel")),
    )(cvec, w_mod, b_mod.reshape(depth, 1, n))


def _normed(x_ref, gain_ref, shift_ref, scale_ref, n_lat):
    x = x_ref[0]
    tr = x.shape[0]
    ms = jnp.mean(x * x, axis=-1, keepdims=True)
    y = x * lax.rsqrt(ms + NORM_EPS) * gain_ref[...]
    is_lat = pl.program_id(1) * tr + lax.broadcasted_iota(I32, (tr, 1), 0) < n_lat
    scale = jnp.where(is_lat, scale_ref[0, 0], scale_ref[0, 1])
    shift = jnp.where(is_lat, shift_ref[0, 0], shift_ref[0, 1])
    return y * (1.0 + scale) + shift


def _norm_mod_kernel(x_ref, gain_ref, shift_ref, scale_ref, o_ref, *, n_lat):
    o_ref[0] = _normed(x_ref, gain_ref, shift_ref, scale_ref, n_lat).astype(o_ref.dtype)


def _pack_bf16_pairs(h):
    half = h.shape[1] // 2
    bits = lax.bitcast_convert_type(h.astype(BF16).astype(F32), jnp.uint32)
    word = lax.shift_right_logical(bits[:, :half], jnp.uint32(16)) | (bits[:, half:] & jnp.uint32(0xFFFF0000))
    return lax.bitcast_convert_type(word, I32)


def _unpack_bf16_pairs(word):
    bits = lax.bitcast_convert_type(word, jnp.uint32)
    lo = lax.bitcast_convert_type(lax.shift_left(bits, jnp.uint32(16)), F32)
    hi = lax.bitcast_convert_type(bits & jnp.uint32(0xFFFF0000), F32)
    return jnp.concatenate([lo, hi], axis=1).astype(BF16)


def _norm_router_kernel(x_ref, gain_ref, shift_ref, scale_ref, wr_ref, o_ref, aff_ref, *, n_lat):
    h = _normed(x_ref, gain_ref, shift_ref, scale_ref, n_lat)
    packed = _pack_bf16_pairs(h)
    o_ref[0] = packed.reshape(o_ref.shape[1:])
    hh, hl = _split_bf16(h)
    wh, wl = _split_bf16(wr_ref[...])
    logits = _dot(hh, wh) + _dot(hl, wh) + _dot(hh, wl)
    lane = lax.broadcasted_iota(I32, logits.shape, 1)
    valid = lane < N_EXPERTS
    logits = jnp.where(valid, logits, -jnp.inf)
    m = jnp.max(logits, axis=-1, keepdims=True)
    p = jnp.exp(logits - m)
    aff = p / jnp.sum(p, axis=-1, keepdims=True)
    aff_ref[0] = jnp.where(valid, aff, 0.0)


def _norm_mod(x, gain, shift, scale, n_lat, *, out_dtype=BF16, rows=None, router_w=None):
    b, s, d = x.shape
    rows = s if rows is None else rows
    tr = next(t for t in NORM_TILES if rows % t == 0 and s % t == 0)
    both = lambda bi, t: (bi, 0, 0, 0)
    in_specs = [
        pl.BlockSpec((1, tr, d), lambda bi, t: (bi, t, 0)),
        pl.BlockSpec((1, d), lambda bi, t: (0, 0)),
        pl.BlockSpec((1, 2, 1, d), both),
        pl.BlockSpec((1, 2, 1, d), both),
    ]
    args = [x, gain.reshape(1, d), shift, scale]
    out_specs = pl.BlockSpec((1, tr, d), lambda bi, t: (bi, t, 0))
    out_shape = jax.ShapeDtypeStruct((b, rows, d), out_dtype)
    kern = functools.partial(_norm_mod_kernel, n_lat=n_lat)
    if router_w is not None:
        wr = jnp.zeros((d, LANES), F32).at[:, :N_EXPERTS].set(router_w)
        in_specs.append(pl.BlockSpec((d, LANES), lambda bi, t: (0, 0)))
        args.append(wr)
        slab = (d // 2 // LANES, LANES)
        out_specs = [pl.BlockSpec((1, tr) + slab, lambda bi, t: (bi, t, 0, 0)),
                     pl.BlockSpec((1, tr, LANES), lambda bi, t: (bi, t, 0))]
        out_shape = [jax.ShapeDtypeStruct((b, rows) + slab, I32),
                     jax.ShapeDtypeStruct((b, rows, LANES), F32)]
        kern = functools.partial(_norm_router_kernel, n_lat=n_lat)
    return pl.pallas_call(
        kern,
        name="norm_mod" if router_w is None else "norm_router",
        grid=(b, rows // tr),
        in_specs=in_specs,
        out_specs=out_specs,
        out_shape=out_shape,
        compiler_params=_params(("parallel", "parallel")),
    )(*args)


def _rope_partner(xs, quarter):
    if 2 * quarter == LANES:
        return pltpu.roll(xs, quarter, 1)
    back = pltpu.roll(xs, quarter, 1)
    fwd = pltpu.roll(xs, LANES - quarter, 1)
    lane = lax.broadcasted_iota(I32, xs.shape, 1)
    return jnp.where((lane % (2 * quarter)) < quarter, fwd, back)


def _row_chunks(tm):
    return [slice(r, r + MM_ROW_CHUNK) for r in range(0, tm, MM_ROW_CHUNK)]


def _mm_rope_kernel(a_ref, w_ref, cos_ref, sin_ref, o_ref, *, n_q, qscale, kscale, head_dim):
    sc = jnp.where(pl.program_id(1) >= n_q, kscale, qscale).astype(F32)
    for rows in _row_chunks(a_ref.shape[1]):
        acc = _dot(a_ref[0, rows, :], w_ref[0])
        for s in range(acc.shape[1] // LANES):
            cols = slice(s * LANES, (s + 1) * LANES)
            off = (s * LANES) % head_dim
            xs = acc[:, cols]
            rot = xs * cos_ref[rows, off:off + LANES] + _rope_partner(xs, head_dim // 4) * sin_ref[rows, off:off + LANES]
            o_ref[0, rows, cols] = (rot * sc).astype(o_ref.dtype)


def _mm_plain_kernel(a_ref, w_ref, o_ref):
    for rows in _row_chunks(a_ref.shape[1]):
        o_ref[0, rows, :] = _dot(a_ref[0, rows, :], w_ref[0]).astype(o_ref.dtype)


def _mm_proj(a, w, layer, col0, ncols, tn, rope=None):
    b, s, k = a.shape
    tm = ROW_TILE
    assert col0 % tn == 0 and ncols % tn == 0
    in_specs = [
        pl.BlockSpec((1, tm, k), lambda bi, j, i: (bi, i, 0)),
        pl.BlockSpec((1, k, tn), lambda bi, j, i: (layer, 0, col0 // tn + j)),
    ]
    args = [a, w]
    if rope is None:
        kern, name = _mm_plain_kernel, "mm_proj"
    else:
        cos_t, sin_t, n_q_cols, qscale, kscale, head_dim = rope
        assert tn % head_dim == 0 and n_q_cols % tn == 0
        kern = functools.partial(_mm_rope_kernel, n_q=n_q_cols // tn, qscale=qscale, kscale=kscale,
                                 head_dim=head_dim)
        name = "mm_proj_rope"
        in_specs += [pl.BlockSpec((tm, head_dim), lambda bi, j, i: (i, 0))] * 2
        args += [cos_t, sin_t]
    return pl.pallas_call(
        kern,
        name=name,
        grid=(b, ncols // tn, s // tm),
        in_specs=in_specs,
        out_specs=pl.BlockSpec((1, tm, tn), lambda bi, j, i: (bi, i, j)),
        out_shape=jax.ShapeDtypeStruct((b, s, ncols), BF16),
        compiler_params=_params(("parallel", "parallel", "arbitrary")),
    )(*args)


def _mm_res_kernel(a_ref, w_ref, x_ref, gl_ref, gc_ref, o_ref, *, n_lat):
    tm = a_ref.shape[1]
    for rows in _row_chunks(tm):
        acc = _dot(a_ref[0, rows, :], w_ref[0])
        row = pl.program_id(2) * tm + rows.start + lax.broadcasted_iota(I32, (acc.shape[0], 1), 0)
        gate = jnp.where(row < n_lat, gl_ref[0], gc_ref[0])
        o_ref[0, rows, :] = x_ref[0, rows, :] + gate * acc


def _mm_res(a, w, layer, x, gate_lat, gate_ctx, n_lat):
    b, s, k = a.shape
    n = w.shape[2]
    tm = ROW_TILE
    tn = n
    while k * tn * 2 > RESIDENT_WEIGHT_BYTES:
        tn //= 2
    return pl.pallas_call(
        functools.partial(_mm_res_kernel, n_lat=n_lat),
        name="mm_residual",
        grid=(b, n // tn, s // tm),
        in_specs=[
            pl.BlockSpec((1, tm, k), lambda bi, j, i: (bi, i, 0)),
            pl.BlockSpec((1, k, tn), lambda bi, j, i: (layer, 0, j)),
            pl.BlockSpec((1, tm, tn), lambda bi, j, i: (bi, i, j)),
            pl.BlockSpec((1, 1, tn), lambda bi, j, i: (bi, 0, j)),
            pl.BlockSpec((1, 1, tn), lambda bi, j, i: (bi, 0, j)),
        ],
        out_specs=pl.BlockSpec((1, tm, tn), lambda bi, j, i: (bi, i, j)),
        out_shape=jax.ShapeDtypeStruct((b, s, n), F32),
        compiler_params=_params(("parallel", "parallel", "arbitrary")),
    )(a, w, x, gate_lat, gate_ctx)


def _mm_groups_kernel(a_ref, w_ref, o_ref, *, groups):
    cg = w_ref.shape[0]
    d = groups * cg
    for g in range(groups):
        uv = _dot(a_ref[0, :, g * cg:(g + 1) * cg], w_ref[...])
        o_ref[0, :, g * cg:(g + 1) * cg] = uv[:, :cg]
        o_ref[0, :, d + g * cg:d + (g + 1) * cg] = uv[:, cg:]


def _mm_groups(a, w, groups):
    b, s, d = a.shape
    cg = d // groups
    tm = ROW_TILE
    return pl.pallas_call(
        functools.partial(_mm_groups_kernel, groups=groups),
        name="mm_channel_dft",
        grid=(b, s // tm),
        in_specs=[
            pl.BlockSpec((1, tm, d), lambda bi, i: (bi, i, 0)),
            pl.BlockSpec((cg, 2 * cg), lambda bi, i: (0, 0)),
        ],
        out_specs=pl.BlockSpec((1, tm, 2 * d), lambda bi, i: (bi, i, 0)),
        out_shape=jax.ShapeDtypeStruct((b, s, 2 * d), F32),
        compiler_params=_params(("parallel", "parallel")),
    )(a, w)


def _ret_scan_kernel(*refs, heads, dk, dv, reverse, add_in):
    if add_in:
        dec_ref, q_ref, k_ref, v_ref, g_ref, yin_ref, o_ref, s_ref, qd_ref, kd_ref, in_ref, cd_ref = refs
    else:
        dec_ref, q_ref, k_ref, v_ref, g_ref, o_ref, s_ref, qd_ref, kd_ref, in_ref, cd_ref = refs
        yin_ref = None
    c = q_ref.shape[1]
    j = pl.program_id(1)

    @pl.when(j == 0)
    def _():
        s_ref[...] = jnp.zeros_like(s_ref)
        m_col = lax.broadcasted_iota(I32, (c, LANES), 0).astype(F32)
        m_row = lax.broadcasted_iota(I32, (c, c), 0).astype(F32)
        n_row = lax.broadcasted_iota(I32, (c, c), 1).astype(F32)
        for h in range(heads):
            lg = -jnp.exp(dec_ref[:, h:h + 1])
            if reverse:
                q_pow, k_pow, diff = c - m_col, m_col, n_row - m_row
            else:
                q_pow, k_pow, diff = m_col + 1.0, c - 1.0 - m_col, m_row - n_row
            qd_ref[h] = jnp.exp(lg * q_pow)
            kd_ref[h] = jnp.exp(lg * k_pow)
            in_ref[h] = jnp.where(diff >= 0, jnp.exp(lg * jnp.maximum(diff, 0.0)), 0.0)
            cd_ref[h] = jnp.exp(jnp.broadcast_to(lg, (8, LANES)) * float(c))

    for h in range(heads):
        q = q_ref[0, :, h * dk:(h + 1) * dk]
        k = k_ref[0, :, h * dk:(h + 1) * dk]
        v = v_ref[0, :, h * dv:(h + 1) * dv]
        qdec = jnp.concatenate([qd_ref[h]] * (dk // LANES), axis=1)
        kdec = jnp.concatenate([kd_ref[h]] * (dk // LANES), axis=1)
        state = s_ref[h]
        cross = _dot((q.astype(F32) * qdec).astype(BF16), state.astype(BF16))
        scores = _dot_nt(q, k) * in_ref[h]
        o = cross + _dot(scores.astype(BF16), v)
        k_t = (k.astype(F32) * kdec).T.astype(BF16)
        s_ref[h] = state * cd_ref[h][0:1, 0:1] + _dot(k_t, v)
        mu = jnp.mean(o, axis=-1, keepdims=True)
        cen = o - mu
        var = jnp.mean(cen * cen, axis=-1, keepdims=True)
        g = g_ref[0, :, h * dv:(h + 1) * dv].astype(F32)
        y = cen * lax.rsqrt(var + NORM_EPS) * _silu(g)
        if add_in:
            y = y + yin_ref[0, :, h * dv:(h + 1) * dv].astype(F32)
        o_ref[0, :, h * dv:(h + 1) * dv] = y.astype(o_ref.dtype)


def _ret_scan(qk, vg, decay_row, y_in, *, n_lat, n_ctx, heads, dk, dv, reverse):
    b, s, _ = qk.shape
    c = RET_CHUNK
    lat_chunks, ctx_chunks = n_lat // c, n_ctx // c
    steps = lat_chunks + ctx_chunks
    hk, hv = heads * dk, heads * dv
    if reverse:
        chunk = lambda j: steps - 1 - j
    else:
        chunk = lambda j: jnp.where(j < ctx_chunks, lat_chunks + j, j - ctx_chunks)
    gate_blk = 2 if reverse else 1
    in_specs = [
        pl.BlockSpec((1, heads), lambda bi, j: (0, 0)),
        pl.BlockSpec((1, c, hk), lambda bi, j: (bi, chunk(j), 0)),
        pl.BlockSpec((1, c, hk), lambda bi, j: (bi, chunk(j), 1)),
        pl.BlockSpec((1, c, hv), lambda bi, j: (bi, chunk(j), 0)),
        pl.BlockSpec((1, c, hv), lambda bi, j: (bi, chunk(j), gate_blk)),
    ]
    args = [decay_row, qk, qk, vg, vg]
    if y_in is not None:
        in_specs.append(pl.BlockSpec((1, c, hv), lambda bi, j: (bi, chunk(j), 0)))
        args.append(y_in)
    kern = functools.partial(_ret_scan_kernel, heads=heads, dk=dk, dv=dv, reverse=reverse,
                             add_in=y_in is not None)
    return pl.pallas_call(
        kern,
        name="ret_scan_bwd" if reverse else "ret_scan_fwd",
        grid=(b, steps),
        in_specs=in_specs,
        out_specs=pl.BlockSpec((1, c, hv), lambda bi, j: (bi, chunk(j), 0)),
        out_shape=jax.ShapeDtypeStruct((b, s, hv), BF16),
        scratch_shapes=[
            pltpu.VMEM((heads, dk, dv), F32),
            pltpu.VMEM((heads, c, LANES), F32),
            pltpu.VMEM((heads, c, LANES), F32),
            pltpu.VMEM((heads, c, c), F32),
            pltpu.VMEM((heads, 8, LANES), F32),
        ],
        compiler_params=_params(("parallel", "arbitrary")),
    )(*args)


def _win_attn_kernel(sink_ref, q_ref, kc_ref, vc_ref, kp_ref, kq_ref, kn_ref, vp_ref, vq_ref, vn_ref,
                     o_ref, *, lat_tiles, heads, kv_heads, hd):
    qt = pl.program_id(1)
    blk = q_ref.shape[1]
    n_ctx = kc_ref.shape[1]
    grp = heads // kv_heads
    rows = grp * blk
    qi = lax.broadcasted_iota(I32, (rows, blk), 0) % blk
    kj = lax.broadcasted_iota(I32, (rows, blk), 1)
    tq = qt + jnp.zeros((rows, blk), I32)
    ok_cur = tq < lat_tiles
    ok_prev = (kj >= qi) & ok_cur & (tq >= 1)
    ok_next = (kj <= qi) & (tq + 1 < lat_tiles)
    bias = jnp.concatenate(
        [jnp.zeros((rows, n_ctx), F32)]
        + [jnp.where(ok, 0.0, NEG_INF).astype(F32) for ok in (ok_prev, ok_cur, ok_next)], axis=1)
    head_row = lax.broadcasted_iota(I32, (rows, 1), 0) // blk
    for kv in range(kv_heads):
        cs = slice(kv * hd, (kv + 1) * hd)
        keys = jnp.concatenate([kc_ref[0, :, cs], kp_ref[0, :, cs], kq_ref[0, :, cs], kn_ref[0, :, cs]], axis=0)
        vals = jnp.concatenate([vc_ref[0, :, cs], vp_ref[0, :, cs], vq_ref[0, :, cs], vn_ref[0, :, cs]], axis=0)
        q = jnp.concatenate([q_ref[0, :, (kv * grp + g) * hd:(kv * grp + g + 1) * hd] for g in range(grp)], axis=0)
        sink = jnp.zeros((rows, 1), F32)
        for g in range(grp):
            h = kv * grp + g
            sink = jnp.where(head_row == g, sink_ref[:, h:h + 1], sink)
        s = _dot_nt(q, keys) + bias
        m = jnp.maximum(jnp.max(s, axis=-1, keepdims=True), sink)
        p = jnp.exp(s - m)
        den = jnp.sum(p, axis=-1, keepdims=True) + jnp.exp(sink - m)
        o = _dot(p.astype(BF16), vals) / den
        for g in range(grp):
            h = kv * grp + g
            o_ref[0, :, h * hd:(h + 1) * hd] = o[g * blk:(g + 1) * blk].astype(o_ref.dtype)


def _win_attn(qk, v, sink, *, n_lat, n_ctx, heads, kv_heads, hd):
    b, s, _ = qk.shape
    blk = WIN_BLOCK
    lat_tiles = n_lat // blk
    tiles = s // blk
    kvw = kv_heads * hd
    k_col = (heads * hd) // kvw
    v_col = 0
    ctx_blk = n_lat // n_ctx
    prev = lambda t: jnp.maximum(t - 1, 0)
    nxt = lambda t: jnp.minimum(t + 1, tiles - 1)
    sink_row = jnp.zeros((1, LANES), F32).at[0, :heads].set(sink.astype(F32))
    kern = functools.partial(_win_attn_kernel, lat_tiles=lat_tiles, heads=heads, kv_heads=kv_heads, hd=hd)
    return pl.pallas_call(
        kern,
        name="win_attn",
        grid=(b, tiles),
        in_specs=[
            pl.BlockSpec((1, LANES), lambda bi, t: (0, 0)),
            pl.BlockSpec((1, blk, heads * hd), lambda bi, t: (bi, t, 0)),
            pl.BlockSpec((1, n_ctx, kvw), lambda bi, t: (bi, ctx_blk, k_col)),
            pl.BlockSpec((1, n_ctx, kvw), lambda bi, t: (bi, ctx_blk, v_col)),
            pl.BlockSpec((1, blk, kvw), lambda bi, t: (bi, prev(t), k_col)),
            pl.BlockSpec((1, blk, kvw), lambda bi, t: (bi, t, k_col)),
            pl.BlockSpec((1, blk, kvw), lambda bi, t: (bi, nxt(t), k_col)),
            pl.BlockSpec((1, blk, kvw), lambda bi, t: (bi, prev(t), v_col)),
            pl.BlockSpec((1, blk, kvw), lambda bi, t: (bi, t, v_col)),
            pl.BlockSpec((1, blk, kvw), lambda bi, t: (bi, nxt(t), v_col)),
        ],
        out_specs=pl.BlockSpec((1, blk, heads * hd), lambda bi, t: (bi, t, 0)),
        out_shape=jax.ShapeDtypeStruct((b, s, heads * hd), BF16),
        compiler_params=_params(("parallel", "parallel")),
    )(sink_row, qk, qk, v, qk, qk, qk, v, v, v)


SUBLANES = 8
PACKED_ROWS = 16


def _dft_stage_a_kernel(m_ref, u_ref, v_ref, re_ref, im_ref):
    n1, sub, tc = u_ref.shape[1:]
    rows = n1 * sub
    stacked = jnp.concatenate([u_ref[0].reshape(rows, tc), v_ref[0].reshape(rows, tc)], axis=0)
    out = _dot(m_ref[...], stacked.astype(BF16))
    re_ref[0] = out[:rows].reshape(n1, sub, tc)
    im_ref[0] = out[rows:].reshape(n1, sub, tc)


def _dft_stage_b_kernel(m_ref, re_ref, im_ref, twc_ref, tws_ref, o_ref):
    n2, sub, tc = o_ref.shape[1:]
    reps = tc // LANES
    ar, ai = re_ref[0], im_ref[0]
    twc = jnp.concatenate([twc_ref[...]] * reps, axis=1)
    tws = jnp.concatenate([tws_ref[...]] * reps, axis=1)
    stacked = jnp.concatenate([ar * twc + ai * tws, ai * twc - ar * tws], axis=0).astype(BF16)
    out = _dot(m_ref[...], stacked)
    o_ref[0] = out.reshape(n2, sub, tc).astype(o_ref.dtype)


def _dft_ctx_kernel(m_ref, x_ref, o_ref, *, d):
    x = x_ref[0]
    stacked = jnp.concatenate([x[:, :d], x[:, d:]], axis=0).astype(BF16)
    o_ref[0] = _dot(m_ref[...], stacked).astype(o_ref.dtype)


def _cos_sin(n):
    ang = 2.0 * np.pi * np.outer(np.arange(n), np.arange(n)) / float(n)
    return np.cos(ang), np.sin(ang)


def _fourier_positions(uv, n_lat, n_ctx, d):
    b, s, _ = uv.shape
    n2 = FFT_INNER
    n1 = n_lat // n2
    tc = min(512, d)
    ca, sa = _cos_sin(n1)
    eye = np.eye(SUBLANES)
    ka, ks = np.kron(ca, eye) / math.sqrt(n1), np.kron(sa, eye) / math.sqrt(n1)
    mat_a = jnp.asarray(np.block([[ka, -ks], [-ks, -ka]]), BF16)
    uv4 = uv.reshape(b, s // n2, n2, 2 * d)
    blk_a = (1, n1, SUBLANES, tc)
    a_re, a_im = pl.pallas_call(
        _dft_stage_a_kernel,
        name="dft_stage_a",
        grid=(b, n2 // SUBLANES, d // tc),
        in_specs=[
            pl.BlockSpec(mat_a.shape, lambda bi, cg, jc: (0, 0)),
            pl.BlockSpec(blk_a, lambda bi, cg, jc: (bi, 0, cg, jc)),
            pl.BlockSpec(blk_a, lambda bi, cg, jc: (bi, 0, cg, d // tc + jc)),
        ],
        out_specs=[pl.BlockSpec(blk_a, lambda bi, cg, jc: (bi, 0, cg, jc))] * 2,
        out_shape=[jax.ShapeDtypeStruct((b, n1, n2, d), F32)] * 2,
        compiler_params=_params(("parallel", "parallel", "parallel")),
    )(mat_a, uv4, uv4)
    cb, sb = _cos_sin(n2)
    eye = np.eye(PACKED_ROWS)
    kron_b = lambda m: np.einsum("kc,ab->kabc", m, eye).reshape(n2 * PACKED_ROWS, PACKED_ROWS * n2)
    mat_b = jnp.asarray(np.concatenate([kron_b(cb), kron_b(sb)], axis=1) / math.sqrt(n2), BF16)
    phi = 2.0 * np.pi * np.outer(np.arange(n1), np.arange(n2)) / float(n_lat)
    twc = jnp.asarray(np.repeat(np.cos(phi).reshape(-1, 1), LANES, axis=1), F32)
    tws = jnp.asarray(np.repeat(np.sin(phi).reshape(-1, 1), LANES, axis=1), F32)
    rows = PACKED_ROWS * n2
    y = pl.pallas_call(
        _dft_stage_b_kernel,
        name="dft_stage_b",
        grid=(b, n1 // PACKED_ROWS, d // tc),
        in_specs=[
            pl.BlockSpec(mat_b.shape, lambda bi, kb, jc: (0, 0)),
            pl.BlockSpec((1, rows, tc), lambda bi, kb, jc: (bi, kb, jc)),
            pl.BlockSpec((1, rows, tc), lambda bi, kb, jc: (bi, kb, jc)),
            pl.BlockSpec((rows, LANES), lambda bi, kb, jc: (kb, 0)),
            pl.BlockSpec((rows, LANES), lambda bi, kb, jc: (kb, 0)),
        ],
        out_specs=pl.BlockSpec((1, n2, PACKED_ROWS, tc), lambda bi, kb, jc: (bi, 0, kb, jc)),
        out_shape=jax.ShapeDtypeStruct((b, n2, n1, d), BF16),
        compiler_params=_params(("parallel", "parallel", "parallel")),
    )(mat_b, a_re.reshape(b, n_lat, d), a_im.reshape(b, n_lat, d), twc, tws)
    cc, sc = _cos_sin(n_ctx)
    mat_c = jnp.asarray(np.concatenate([cc, -sc], axis=1) / math.sqrt(n_ctx), BF16)
    y_ctx = pl.pallas_call(
        functools.partial(_dft_ctx_kernel, d=d),
        name="dft_ctx",
        grid=(b,),
        in_specs=[
            pl.BlockSpec(mat_c.shape, lambda bi: (0, 0)),
            pl.BlockSpec((1, n_ctx, 2 * d), lambda bi: (bi, n_lat // n_ctx, 0)),
        ],
        out_specs=pl.BlockSpec((1, n_ctx, d), lambda bi: (bi, 0, 0)),
        out_shape=jax.ShapeDtypeStruct((b, n_ctx, d), BF16),
        compiler_params=_params(("parallel",)),
    )(mat_c, uv)
    return jnp.concatenate([y.reshape(b, n_lat, d), y_ctx], axis=1)


LANE_SHIFT = 7
GATHER_UNROLL = 16
TILES_PER_STEP = 8
FFN_ROW_CHUNK = 384
COMBINE_REGION = 64


def _route_kernel(aff_ref, tri_ref, posc_ref, gate_ref, tst_ref,
                  thr_ref, need_ref, ctie_ref, cpos_ref, *, cap):
    step = pl.program_id(1)
    tile = tri_ref.shape[0]
    tps = posc_ref.shape[1] // tile

    @pl.when(step == 0)
    def _():
        def body(it, thr):
            bits = lax.bitcast_convert_type(aff_ref[0], I32)
            cand = thr | jnp.left_shift(jnp.int32(1), 30 - it)
            cnt = jnp.sum(jnp.where(bits >= cand, 1.0, 0.0), axis=0, keepdims=True)
            return jnp.where(cnt >= cap, cand, thr)

        thr = lax.fori_loop(0, 31, body, jnp.zeros((1, LANES), I32))
        bits = lax.bitcast_convert_type(aff_ref[0], I32)
        above = jnp.sum(jnp.where(bits > thr, 1.0, 0.0), axis=0, keepdims=True)
        thr_ref[...] = thr
        need_ref[...] = float(cap) - above
        ctie_ref[...] = jnp.zeros_like(ctie_ref)
        cpos_ref[...] = jnp.zeros_like(cpos_ref)

    thr = thr_ref[...]
    for u in range(tps):
        rows = slice(u * tile, (u + 1) * tile)
        a = aff_ref[0, pl.ds(pl.multiple_of((step * tps + u) * tile, tile), tile), :]
        bits = lax.bitcast_convert_type(a, I32)
        gt = bits > thr
        eq = bits == thr
        eqf = jnp.where(eq, 1.0, 0.0)
        tie_rank = _dot(tri_ref[...], eqf.astype(BF16)) + ctie_ref[...]
        sel = gt | (eq & (tie_rank < need_ref[...]))
        self_ = jnp.where(sel, 1.0, 0.0)
        start = cpos_ref[...]
        pos = jnp.where(sel, _dot(tri_ref[...], self_.astype(BF16)) + start, -1.0)
        posc_ref[0, rows, :] = pos.astype(I32)
        gate_ref[0, rows, :] = jnp.where(sel, a, 0.0)
        tst_ref[0, u] = jnp.broadcast_to(start, (8, LANES)).astype(I32)
        ctie_ref[...] = ctie_ref[...] + jnp.sum(eqf, axis=0, keepdims=True)
        cpos_ref[...] = start + jnp.sum(self_, axis=0, keepdims=True)


def _route(aff, *, row_off, n, cap):
    b = aff.shape[0]
    tile = ROUTE_TILE
    nt = n // tile
    tps = min(TILES_PER_STEP, nt)
    tri = jnp.asarray(np.tril(np.ones((tile, tile)), -1), BF16)
    return pl.pallas_call(
        functools.partial(_route_kernel, cap=cap),
        name="moe_route",
        grid=(b, nt // tps),
        in_specs=[
            pl.BlockSpec((1, n, LANES), lambda bi, t: (bi, row_off // n, 0)),
            pl.BlockSpec((tile, tile), lambda bi, t: (0, 0)),
        ],
        out_specs=[
            pl.BlockSpec((1, tps * tile, LANES), lambda bi, t: (bi, t, 0)),
            pl.BlockSpec((1, tps * tile, LANES), lambda bi, t: (bi, t, 0)),
            pl.BlockSpec((1, tps, 8, LANES), lambda bi, t: (bi, t, 0, 0)),
        ],
        out_shape=[
            jax.ShapeDtypeStruct((b, n, LANES), I32),
            jax.ShapeDtypeStruct((b, n, LANES), F32),
            jax.ShapeDtypeStruct((b, nt, 8, LANES), I32),
        ],
        scratch_shapes=[pltpu.VMEM((1, LANES), I32)] + [pltpu.VMEM((1, LANES), F32)] * 3,
        compiler_params=_params(("parallel", "arbitrary")),
    )(aff, tri)


def _slot_index_kernel(ts_ref, posc_ref, idx_ref, *, nt, row_off, tile):
    bi, step = pl.program_id(0), pl.program_id(1)

    @pl.when(step == 0)
    def _():
        idx_ref[...] = jnp.zeros_like(idx_ref)

    tps = posc_ref.shape[1] // tile
    slot_rows = idx_ref.shape[2]
    lane = lax.broadcasted_iota(I32, (tile, 2 * LANES), 1)
    for u in range(tps):
        t = step * tps + u
        tok = (row_off + t * tile + lax.broadcasted_iota(I32, (tile, 1), 0)).astype(F32)
        for e in range(N_EXPERTS):
            h0 = jnp.minimum(ts_ref[(bi * nt + t) * N_EXPERTS + e] >> LANE_SHIFT, slot_rows - 2)
            hit = (posc_ref[0, u * tile:(u + 1) * tile, e:e + 1] - h0 * LANES) == lane
            vals = jnp.sum(jnp.where(hit, tok, 0.0), axis=0, keepdims=True)
            two_rows = jnp.concatenate([vals[:, :LANES], vals[:, LANES:]], axis=0).astype(I32)
            idx_ref[0, e, pl.ds(h0, 2), :] = idx_ref[0, e, pl.ds(h0, 2), :] + two_rows


def _slot_index(tstart, posc, *, row_off, n, cap):
    b = posc.shape[0]
    tile = ROUTE_TILE
    nt = n // tile
    slot_rows = max(cap, 2 * LANES) // LANES
    tps = min(TILES_PER_STEP, nt)
    grid_spec = pltpu.PrefetchScalarGridSpec(
        num_scalar_prefetch=1,
        grid=(b, nt // tps),
        in_specs=[pl.BlockSpec((1, tps * tile, LANES), lambda bi, t, ts: (bi, t, 0))],
        out_specs=pl.BlockSpec((1, N_EXPERTS, slot_rows, LANES), lambda bi, t, ts: (bi, 0, 0, 0)),
    )
    idx = pl.pallas_call(
        functools.partial(_slot_index_kernel, nt=nt, row_off=row_off, tile=tile),
        name="moe_slot_index",
        grid_spec=grid_spec,
        out_shape=jax.ShapeDtypeStruct((b, N_EXPERTS, slot_rows, LANES), I32),
        compiler_params=_params(("parallel", "arbitrary")),
    )(tstart, posc)
    return idx.reshape(b, N_EXPERTS, slot_rows * LANES)


def _gather_kernel(idx_ref, h_ref, o_ref, buf_ref, sem):
    bi = pl.program_id(1)
    n_rows = buf_ref.shape[0]

    def row_copy(src_row, dst_row, rows):
        return pltpu.make_async_copy(h_ref.at[bi, pl.ds(src_row, rows)], buf_ref.at[pl.ds(dst_row, rows)], sem)

    def issue(g, carry):
        for k in range(GATHER_UNROLL):
            s = g * GATHER_UNROLL + k
            row_copy(idx_ref[0, 0, s], s, 1).start(priority=k % 2)
        return carry

    lax.fori_loop(0, n_rows // GATHER_UNROLL, issue, 0)
    row_copy(0, 0, n_rows).wait()
    o_ref[0, 0] = _unpack_bf16_pairs(buf_ref[...].reshape(n_rows, -1))


def _gather(idx, h):
    b, _, sub, lanes = h.shape
    d = 2 * sub * lanes
    slots = idx.shape[2]
    assert slots % GATHER_UNROLL == 0
    return pl.pallas_call(
        _gather_kernel,
        name="moe_gather",
        grid=(N_EXPERTS, b),
        in_specs=[
            pl.BlockSpec((1, 1, slots), lambda ei, bi: (bi * N_EXPERTS + ei, 0, 0), memory_space=pltpu.SMEM),
            pl.BlockSpec(memory_space=pl.ANY),
        ],
        out_specs=pl.BlockSpec((1, 1, slots, d), lambda ei, bi: (ei, bi, 0, 0)),
        out_shape=jax.ShapeDtypeStruct((N_EXPERTS, b, slots, d), BF16),
        scratch_shapes=[pltpu.VMEM((slots, sub, lanes), I32), pltpu.SemaphoreType.DMA(())],
        compiler_params=_params(("arbitrary", "arbitrary")),
    )(idx.reshape(b * N_EXPERTS, 1, slots), h)


def _ffn_kernel(x_ref, wg_ref, wu_ref, wd_ref, ye_ref, hm_ref, *, n_up):
    st = pl.program_id(1)
    bsz, rows, d = x_ref.shape[1:]
    tf = wg_ref.shape[3]
    chunk = FFN_ROW_CHUNK if rows % FFN_ROW_CHUNK == 0 else rows
    spans = [(bi, r0) for bi in range(bsz) for r0 in range(0, rows, chunk)]

    @pl.when(st < n_up)
    def _():
        wg = wg_ref[0, 0].astype(BF16)
        wu = wu_ref[0, 0].astype(BF16)
        for bi, r0 in spans:
            x = x_ref[0, bi, r0:r0 + chunk, :]
            hm_ref[st, bi * rows + r0:bi * rows + r0 + chunk, :] = (_silu(_dot(x, wg)) * _dot(x, wu)).astype(BF16)

    @pl.when(st >= n_up)
    def _():
        wd = [wd_ref[0, 0, c * tf:(c + 1) * tf, :].astype(BF16) for c in range(n_up)]
        for bi, r0 in spans:
            m0 = bi * rows + r0
            y = _dot(hm_ref[0, m0:m0 + chunk, :], wd[0])
            for c in range(1, n_up):
                y = y + _dot(hm_ref[c, m0:m0 + chunk, :], wd[c])
            ye_ref[0, bi, r0:r0 + chunk, :] = y.astype(ye_ref.dtype)


def _ffn(xe, w_gate, w_up, w_down, layer):
    e, b, rows, d = xe.shape
    f = w_gate.shape[3]
    tf = min(512, f)
    tdc = min(512, d)
    n_up, n_down = f // tf, d // tdc
    up = lambda st: jnp.minimum(st, n_up - 1)
    down = lambda st: jnp.maximum(st - n_up, 0)
    return pl.pallas_call(
        functools.partial(_ffn_kernel, n_up=n_up),
        name="moe_ffn",
        grid=(e, n_up + n_down),
        in_specs=[
            pl.BlockSpec((1, b, rows, d), lambda ei, st: (ei, 0, 0, 0)),
            pl.BlockSpec((1, 1, d, tf), lambda ei, st: (layer, ei, 0, up(st))),
            pl.BlockSpec((1, 1, d, tf), lambda ei, st: (layer, ei, 0, up(st))),
            pl.BlockSpec((1, 1, f, tdc), lambda ei, st: (layer, ei, 0, down(st))),
        ],
        out_specs=pl.BlockSpec((1, b, rows, tdc), lambda ei, st: (ei, 0, 0, down(st))),
        out_shape=jax.ShapeDtypeStruct((e, b, rows, d), BF16),
        scratch_shapes=[pltpu.VMEM((n_up, b * rows, tf), BF16)],
        compiler_params=_params(("parallel", "arbitrary")),
    )(xe, w_gate, w_up, w_down)


def _window_start(ts_ref, idx, cap_rows, win):
    a0 = jnp.minimum(ts_ref[idx] & (-SLOT_ALIGN), cap_rows - win)
    return pl.multiple_of(a0, SLOT_ALIGN)


def _combine_kernel(ts_ref, ye_ref, posc_ref, gate_ref, x_ref, g_ref, o_ref, *, nt, win, cap, n_starts, tile):
    bi, step = pl.program_id(0), pl.program_id(2)
    cap_rows = ye_ref.shape[2]
    tps = posc_ref.shape[1] // tile
    reg = COMBINE_REGION
    assert 2 * reg == LANES and reg <= cap_rows

    for u in range(tps):
        t = step * tps + u
        rows = slice(u * tile, (u + 1) * tile)
        base = (bi * nt + t) * N_EXPERTS
        starts, fits = [], None
        for e in range(N_EXPERTS):
            a0 = _window_start(ts_ref, base + e, cap_rows, reg)
            nxt = ts_ref[jnp.minimum(base + N_EXPERTS + e, n_starts - 1)]
            end = jnp.where(t + 1 < nt, nxt, cap)
            ok = end - a0 <= reg
            fits = ok if fits is None else jnp.logical_and(fits, ok)
            starts.append(a0)

        lane = lax.broadcasted_iota(I32, (tile, LANES), 1)
        upper = lane >= reg
        weights, slabs = [], []
        for p in range(N_EXPERTS // 2):
            e0, e1 = 2 * p, 2 * p + 1
            slot = jnp.where(upper, starts[e1] - reg, starts[e0]) + lane
            pcol = jnp.where(upper, posc_ref[0, rows, e1:e1 + 1], posc_ref[0, rows, e0:e0 + 1])
            gcol = jnp.where(upper, gate_ref[0, rows, e1:e1 + 1], gate_ref[0, rows, e0:e0 + 1])
            weights.append(jnp.where(pcol == slot, gcol, 0.0).astype(BF16))
            slabs += [ye_ref[e0, 0, pl.ds(starts[e0], reg), :], ye_ref[e1, 0, pl.ds(starts[e1], reg), :]]
        acc = _dot(jnp.concatenate(weights, axis=1), jnp.concatenate(slabs, axis=0))
        o_ref[0, rows, :] = x_ref[0, rows, :] + g_ref[0, 0] * acc

        @pl.when(jnp.logical_not(fits))
        def _():
            acc = jnp.zeros((tile, o_ref.shape[2]), F32)
            lane = lax.broadcasted_iota(I32, (tile, win), 1)
            for e in range(N_EXPERTS):
                a0 = _window_start(ts_ref, base + e, cap_rows, win)
                pcol = posc_ref[0, rows, e:e + 1]
                gcol = gate_ref[0, rows, e:e + 1]
                w = jnp.where(pcol == a0 + lane, gcol, 0.0).astype(BF16)
                acc = acc + _dot(w, ye_ref[e, 0, pl.ds(a0, win), :])
            o_ref[0, rows, :] = x_ref[0, rows, :] + g_ref[0, 0] * acc


def _combine(tstart, ye, posc, gate, x, g2, *, row_off, n, region, row0, cap_rows):
    b, s, d = x.shape
    tile = ROUTE_TILE
    nt = n // tile
    dc = 512 if d % 512 == 0 else d
    win = min(2 * tile, cap_rows)
    tps = min(TILES_PER_STEP, nt)
    rows = tps * tile
    assert row_off % rows == 0
    off = row_off // rows
    grid_spec = pltpu.PrefetchScalarGridSpec(
        num_scalar_prefetch=1,
        grid=(b, d // dc, nt // tps),
        in_specs=[
            pl.BlockSpec((N_EXPERTS, 1, cap_rows, dc), lambda bi, c, t, ts: (0, bi, row0 // cap_rows, c)),
            pl.BlockSpec((1, rows, LANES), lambda bi, c, t, ts: (bi, t, 0)),
            pl.BlockSpec((1, rows, LANES), lambda bi, c, t, ts: (bi, t, 0)),
            pl.BlockSpec((1, rows, dc), lambda bi, c, t, ts: (bi, off + t, c)),
            pl.BlockSpec((1, 1, 1, dc), lambda bi, c, t, ts: (bi, region, 0, c)),
        ],
        out_specs=pl.BlockSpec((1, rows, dc), lambda bi, c, t, ts: (bi, off + t, c)),
    )
    return pl.pallas_call(
        functools.partial(_combine_kernel, nt=nt, win=win, cap=CAPACITY_FACTOR * n // N_EXPERTS,
                          n_starts=b * nt * N_EXPERTS, tile=tile),
        name="moe_combine",
        grid_spec=grid_spec,
        out_shape=jax.ShapeDtypeStruct((b, s, d), F32),
        input_output_aliases={4: 0},
        compiler_params=_params(("parallel", "parallel", "arbitrary")),
    )(tstart, ye, posc, gate, x, g2)


def _moe(x, h, aff, g2, w_gate, w_up, w_down, layer, token_sets):
    routed, row0 = [], 0
    for row_off, n, region in token_sets:
        cap = CAPACITY_FACTOR * n // N_EXPERTS
        cap_rows = -(-cap // LANES) * LANES
        posc, gate, tst = _route(aff, row_off=row_off, n=n, cap=cap)
        tstart = tst[:, :, 0, :N_EXPERTS].reshape(-1)
        idx = _slot_index(tstart, posc, row_off=row_off, n=n, cap=cap)
        routed.append((row_off, n, region, row0, cap_rows, tstart, posc, gate, idx))
        row0 += cap_rows
    idx_all = jnp.concatenate([r[8][:, :, :r[4]] for r in routed], axis=2)
    ye = _ffn(_gather(idx_all, h), w_gate, w_up, w_down, layer)
    for row_off, n, region, r0, cap_rows, tstart, posc, gate, _ in routed:
        x = _combine(tstart, ye, posc, gate, x, g2, row_off=row_off, n=n, region=region, row0=r0, cap_rows=cap_rows)
    return x


def _rope_tables(n_lat, n_ctx, head_dim):
    quarter = head_dim // 4
    inv = ROPE_BASE ** (-jnp.arange(quarter, dtype=F32) / quarter)
    pos = jnp.arange(n_lat)
    row = (pos // GRID_W).astype(F32)[:, None] * inv[None, :]
    col = (pos % GRID_W).astype(F32)[:, None] * inv[None, :]
    cos = jnp.concatenate([jnp.cos(row), jnp.cos(row), jnp.cos(col), jnp.cos(col)], axis=1)
    sin = jnp.concatenate([-jnp.sin(row), jnp.sin(row), -jnp.sin(col), jnp.sin(col)], axis=1)
    cos = jnp.concatenate([cos, jnp.ones((n_ctx, head_dim), F32)], axis=0)
    sin = jnp.concatenate([sin, jnp.zeros((n_ctx, head_dim), F32)], axis=0)
    return cos, sin


def kernel(x, c, ctx, c_ctx, w_mod, b_mod, norm_gain, final_gain, ret_w_in, ret_w_out, ret_decay,
           win_w_qkv, win_w_o, win_sink, fno_w_o, router_w, exp_w_gate, exp_w_up, exp_w_down):
    b, n_lat, d = x.shape
    n_ctx = ctx.shape[1]
    depth = w_mod.shape[0]
    s = n_lat + n_ctx
    assert s % ROW_TILE == 0 and n_lat % n_ctx == 0
    assert n_lat % (FFT_INNER * 4) == 0 and b + 1 <= 8 and n_ctx % ROUTE_TILE == 0

    xs = jnp.concatenate([x, ctx], axis=1)
    cvec = jnp.zeros((8, d), F32).at[:b].set(c).at[b].set(c_ctx)
    mod = _modulation(cvec, w_mod, b_mod)

    def mod_pair(i, k):
        lat = mod[i, :b, k * d:(k + 1) * d]
        cx = jnp.broadcast_to(mod[i, b, k * d:(k + 1) * d], (b, d))
        return jnp.stack([lat, cx], axis=1).reshape(b, 2, 1, d)

    ret_dk = d // RET_HEADS
    ret_dv = 2 * ret_dk
    win_hd = d // WIN_HEADS
    ret_cos, ret_sin = _rope_tables(n_lat, n_ctx, ret_dk)
    win_cos, win_sin = _rope_tables(n_lat, n_ctx, win_hd)

    ret_w_in_bf, ret_w_out_bf = ret_w_in.astype(BF16), ret_w_out.astype(BF16)
    win_w_qkv_bf, win_w_o_bf, fno_w_o_bf = win_w_qkv.astype(BF16), win_w_o.astype(BF16), fno_w_o.astype(BF16)

    for i in range(depth):
        kind, j = i % N_MIXERS, i // N_MIXERS
        last = i == depth - 1
        sh1, sc1, g1, sh2, sc2, g2 = [mod_pair(i, k) for k in range(6)]
        h = _norm_mod(xs, norm_gain[i, 0], sh1, sc1, n_lat)
        if kind == 0:
            hk, hv = RET_HEADS * ret_dk, RET_HEADS * ret_dv
            qk = _mm_proj(h, ret_w_in_bf, j, 0, 2 * hk, PROJ_COL_TILE,
                          rope=(ret_cos, ret_sin, hk, 1.0, ret_dk ** -0.5, ret_dk))
            vg = _mm_proj(h, ret_w_in_bf, j, 2 * hk, 3 * hv, PROJ_COL_TILE)
            scan = functools.partial(_ret_scan, n_lat=n_lat, n_ctx=n_ctx, heads=RET_HEADS, dk=ret_dk, dv=ret_dv)
            y_b = scan(qk, vg, ret_decay[j, 1:2], None, reverse=True)
            y = scan(qk, vg, ret_decay[j, 0:1], y_b, reverse=False)
            w_out = ret_w_out_bf
        elif kind == 1:
            nq, nkv = WIN_HEADS * win_hd, WIN_KV_HEADS * win_hd
            qk = _mm_proj(h, win_w_qkv_bf, j, 0, nq + nkv, COL_TILE,
                          rope=(win_cos, win_sin, nq, win_hd ** -0.5, 1.0, win_hd))
            v = _mm_proj(h, win_w_qkv_bf, j, nq + nkv, nkv, COL_TILE)
            y = _win_attn(qk, v, win_sink[j], n_lat=n_lat, n_ctx=n_ctx, heads=WIN_HEADS,
                          kv_heads=WIN_KV_HEADS, hd=win_hd)
            w_out = win_w_o_bf
        else:
            cg = d // FOURIER_GROUPS
            cc, sc = _cos_sin(cg)
            w_ch = jnp.asarray(np.concatenate([cc, sc], axis=1) / math.sqrt(cg), BF16)
            uv = _mm_groups(h, w_ch, FOURIER_GROUPS)
            y = _fourier_positions(uv, n_lat, n_ctx, d)
            w_out = fno_w_o_bf
        xs = _mm_res(y, w_out, j, xs, g1[:, 0], g1[:, 1], n_lat)
        h2, aff = _norm_mod(xs, norm_gain[i, 1], sh2, sc2, n_lat, router_w=router_w[i])
        token_sets = [(0, n_lat, 0)] + ([] if last else [(n_lat, n_ctx, 1)])
        xs = _moe(xs, h2, aff, g2, exp_w_gate, exp_w_up, exp_w_down, i, token_sets)

    zeros = jnp.zeros((b, 2, 1, d), F32)
    return _norm_mod(xs, final_gain, zeros, zeros, n_lat, out_dtype=F32, rows=n_lat)
```

```python
import functools
import math

import numpy as np
import jax
import jax.numpy as jnp
from jax import lax
from jax.experimental import pallas as pl
from jax.experimental.pallas import tpu as pltpu

F32 = jnp.float32
BF16 = jnp.bfloat16
I32 = jnp.int32

GRID_W = 64
N_MIXERS = 3
RET_HEADS = 8
RET_CHUNK = 256
WIN_HEADS = 16
WIN_KV_HEADS = 4
WIN_BLOCK = 128
FOURIER_GROUPS = 4
FFT_INNER = 64
N_EXPERTS = 16
CAPACITY_FACTOR = 2
ROPE_BASE = 10000.0
NORM_EPS = 1e-6
NEG_INF = -1e30

LANES = 128
ROUTE_TILE = 128
SLOT_ALIGN = 16
ROW_TILE = 768
MM_ROW_CHUNK = 256
COL_TILE = 512
RESIDENT_WEIGHT_BYTES = 8 * 1024 * 1024
PROJ_COL_TILE = 2048
NORM_TILES = (768, 512, 256)
VMEM_LIMIT_BYTES = 56 * 1024 * 1024


def _params(sem):
    return pltpu.CompilerParams(dimension_semantics=sem, vmem_limit_bytes=VMEM_LIMIT_BYTES)


def _dot(a, b):
    return jnp.dot(a, b, preferred_element_type=F32)


def _dot_nt(a, b):
    return lax.dot_general(a, b, (((1,), (1,)), ((), ())), preferred_element_type=F32)


def _split_bf16(x):
    hi = x.astype(BF16)
    lo = (x - hi.astype(F32)).astype(BF16)
    return hi, lo


def _silu(x):
    return x / (1.0 + jnp.exp(-x))


def _mod_kernel(c_ref, w_ref, b_ref, o_ref):
    s = _silu(c_ref[...])
    sh, sl = _split_bf16(s)
    wh, wl = _split_bf16(w_ref[0])
    o_ref[0] = _dot(sh, wh) + _dot(sl, wh) + _dot(sh, wl) + b_ref[0]


def _modulation(cvec, w_mod, b_mod):
    depth, d, n = w_mod.shape
    tn = 1024
    return pl.pallas_call(
        _mod_kernel,
        name="modulation",
        grid=(depth, n // tn),
        in_specs=[
            pl.BlockSpec((8, d), lambda i, j: (0, 0)),
            pl.BlockSpec((1, d, tn), lambda i, j: (i, 0, j)),
            pl.BlockSpec((1, 1, tn), lambda i, j: (i, 0, j)),
        ],
        out_specs=pl.BlockSpec((1, 8, tn), lambda i, j: (i, 0, j)),
        out_shape=jax.ShapeDtypeStruct((depth, 8, n), F32),
        compiler_params=_params(("parallel", "parallel")),
    )(cvec, w_mod, b_mod.reshape(depth, 1, n))


def _normed(x_ref, gain_ref, shift_ref, scale_ref, n_lat):
    x = x_ref[0]
    tr = x.shape[0]
    ms = jnp.mean(x * x, axis=-1, keepdims=True)
    y = x * lax.rsqrt(ms + NORM_EPS) * gain_ref[...]
    is_lat = pl.program_id(1) * tr + lax.broadcasted_iota(I32, (tr, 1), 0) < n_lat
    scale = jnp.where(is_lat, scale_ref[0, 0], scale_ref[0, 1])
    shift = jnp.where(is_lat, shift_ref[0, 0], shift_ref[0, 1])
    return y * (1.0 + scale) + shift


def _norm_mod_kernel(x_ref, gain_ref, shift_ref, scale_ref, o_ref, *, n_lat):
    o_ref[0] = _normed(x_ref, gain_ref, shift_ref, scale_ref, n_lat).astype(o_ref.dtype)


def _pack_bf16_pairs(h):
    half = h.shape[1] // 2
    bits = lax.bitcast_convert_type(h.astype(BF16).astype(F32), jnp.uint32)
    word = lax.shift_right_logical(bits[:, :half], jnp.uint32(16)) | (bits[:, half:] & jnp.uint32(0xFFFF0000))
    return lax.bitcast_convert_type(word, I32)


def _unpack_bf16_pairs(word):
    bits = lax.bitcast_convert_type(word, jnp.uint32)
    lo = lax.bitcast_convert_type(lax.shift_left(bits, jnp.uint32(16)), F32)
    hi = lax.bitcast_convert_type(bits & jnp.uint32(0xFFFF0000), F32)
    return jnp.concatenate([lo, hi], axis=1).astype(BF16)


def _norm_router_kernel(x_ref, gain_ref, shift_ref, scale_ref, wr_ref, o_ref, aff_ref, *, n_lat):
    h = _normed(x_ref, gain_ref, shift_ref, scale_ref, n_lat)
    packed = _pack_bf16_pairs(h)
    o_ref[0] = packed.reshape(o_ref.shape[1:])
    hh, hl = _split_bf16(h)
    wh, wl = _split_bf16(wr_ref[...])
    logits = _dot(hh, wh) + _dot(hl, wh) + _dot(hh, wl)
    lane = lax.broadcasted_iota(I32, logits.shape, 1)
    valid = lane < N_EXPERTS
    logits = jnp.where(valid, logits, -jnp.inf)
    m = jnp.max(logits, axis=-1, keepdims=True)
    p = jnp.exp(logits - m)
    aff = p / jnp.sum(p, axis=-1, keepdims=True)
    aff_ref[0] = jnp.where(valid, aff, 0.0)


def _norm_mod(x, gain, shift, scale, n_lat, *, out_dtype=BF16, rows=None, router_w=None):
    b, s, d = x.shape
    rows = s if rows is None else rows
    tr = next(t for t in NORM_TILES if rows % t == 0 and s % t == 0)
    both = lambda bi, t: (bi, 0, 0, 0)
    in_specs = [
        pl.BlockSpec((1, tr, d), lambda bi, t: (bi, t, 0)),
        pl.BlockSpec((1, d), lambda bi, t: (0, 0)),
        pl.BlockSpec((1, 2, 1, d), both),
        pl.BlockSpec((1, 2, 1, d), both),
    ]
    args = [x, gain.reshape(1, d), shift, scale]
    out_specs = pl.BlockSpec((1, tr, d), lambda bi, t: (bi, t, 0))
    out_shape = jax.ShapeDtypeStruct((b, rows, d), out_dtype)
    kern = functools.partial(_norm_mod_kernel, n_lat=n_lat)
    if router_w is not None:
        wr = jnp.zeros((d, LANES), F32).at[:, :N_EXPERTS].set(router_w)
        in_specs.append(pl.BlockSpec((d, LANES), lambda bi, t: (0, 0)))
        args.append(wr)
        slab = (d // 2 // LANES, LANES)
        out_specs = [pl.BlockSpec((1, tr) + slab, lambda bi, t: (bi, t, 0, 0)),
                     pl.BlockSpec((1, tr, LANES), lambda bi, t: (bi, t, 0))]
        out_shape = [jax.ShapeDtypeStruct((b, rows) + slab, I32),
                     jax.ShapeDtypeStruct((b, rows, LANES), F32)]
        kern = functools.partial(_norm_router_kernel, n_lat=n_lat)
    return pl.pallas_call(
        kern,
        name="norm_mod" if router_w is None else "norm_router",
        grid=(b, rows // tr),
        in_specs=in_specs,
        out_specs=out_specs,
        out_shape=out_shape,
        compiler_params=_params(("parallel", "parallel")),
    )(*args)


def _rope_partner(xs, quarter):
    if 2 * quarter == LANES:
        return pltpu.roll(xs, quarter, 1)
    back = pltpu.roll(xs, quarter, 1)
    fwd = pltpu.roll(xs, LANES - quarter, 1)
    lane = lax.broadcasted_iota(I32, xs.shape, 1)
    return jnp.where((lane % (2 * quarter)) < quarter, fwd, back)


def _row_chunks(tm):
    return [slice(r, r + MM_ROW_CHUNK) for r in range(0, tm, MM_ROW_CHUNK)]


def _mm_rope_kernel(a_ref, w_ref, cos_ref, sin_ref, o_ref, *, n_q, qscale, kscale, head_dim):
    sc = jnp.where(pl.program_id(1) >= n_q, kscale, qscale).astype(F32)
    for rows in _row_chunks(a_ref.shape[1]):
        acc = _dot(a_ref[0, rows, :], w_ref[0])
        for s in range(acc.shape[1] // LANES):
            cols = slice(s * LANES, (s + 1) * LANES)
            off = (s * LANES) % head_dim
            xs = acc[:, cols]
            rot = xs * cos_ref[rows, off:off + LANES] + _rope_partner(xs, head_dim // 4) * sin_ref[rows, off:off + LANES]
            o_ref[0, rows, cols] = (rot * sc).astype(o_ref.dtype)


def _mm_plain_kernel(a_ref, w_ref, o_ref):
    for rows in _row_chunks(a_ref.shape[1]):
        o_ref[0, rows, :] = _dot(a_ref[0, rows, :], w_ref[0]).astype(o_ref.dtype)


def _mm_proj(a, w, layer, col0, ncols, tn, rope=None):
    b, s, k = a.shape
    tm = ROW_TILE
    assert col0 % tn == 0 and ncols % tn == 0
    in_specs = [
        pl.BlockSpec((1, tm, k), lambda bi, j, i: (bi, i, 0)),
        pl.BlockSpec((1, k, tn), lambda bi, j, i: (layer, 0, col0 // tn + j)),
    ]
    args = [a, w]
    if rope is None:
        kern, name = _mm_plain_kernel, "mm_proj"
    else:
        cos_t, sin_t, n_q_cols, qscale, kscale, head_dim = rope
        assert tn % head_dim == 0 and n_q_cols % tn == 0
        kern = functools.partial(_mm_rope_kernel, n_q=n_q_cols // tn, qscale=qscale, kscale=kscale,
                                 head_dim=head_dim)
        name = "mm_proj_rope"
        in_specs += [pl.BlockSpec((tm, head_dim), lambda bi, j, i: (i, 0))] * 2
        args += [cos_t, sin_t]
    return pl.pallas_call(
        kern,
        name=name,
        grid=(b, ncols // tn, s // tm),
        in_specs=in_specs,
        out_specs=pl.BlockSpec((1, tm, tn), lambda bi, j, i: (bi, i, j)),
        out_shape=jax.ShapeDtypeStruct((b, s, ncols), BF16),
        compiler_params=_params(("parallel", "parallel", "arbitrary")),
    )(*args)


def _mm_res_kernel(a_ref, w_ref, x_ref, gl_ref, gc_ref, o_ref, *, n_lat):
    tm = a_ref.shape[1]
    for rows in _row_chunks(tm):
        acc = _dot(a_ref[0, rows, :], w_ref[0])
        row = pl.program_id(2) * tm + rows.start + lax.broadcasted_iota(I32, (acc.shape[0], 1), 0)
        gate = jnp.where(row < n_lat, gl_ref[0], gc_ref[0])
        o_ref[0, rows, :] = x_ref[0, rows, :] + gate * acc


def _mm_res(a, w, layer, x, gate_lat, gate_ctx, n_lat):
    b, s, k = a.shape
    n = w.shape[2]
    tm = ROW_TILE
    tn = n
    while k * tn * 2 > RESIDENT_WEIGHT_BYTES:
        tn //= 2
    return pl.pallas_call(
        functools.partial(_mm_res_kernel, n_lat=n_lat),
        name="mm_residual",
        grid=(b, n // tn, s // tm),
        in_specs=[
            pl.BlockSpec((1, tm, k), lambda bi, j, i: (bi, i, 0)),
            pl.BlockSpec((1, k, tn), lambda bi, j, i: (layer, 0, j)),
            pl.BlockSpec((1, tm, tn), lambda bi, j, i: (bi, i, j)),
            pl.BlockSpec((1, 1, tn), lambda bi, j, i: (bi, 0, j)),
            pl.BlockSpec((1, 1, tn), lambda bi, j, i: (bi, 0, j)),
        ],
        out_specs=pl.BlockSpec((1, tm, tn), lambda bi, j, i: (bi, i, j)),
        out_shape=jax.ShapeDtypeStruct((b, s, n), F32),
        compiler_params=_params(("parallel", "parallel", "arbitrary")),
    )(a, w, x, gate_lat, gate_ctx)


def _mm_groups_kernel(a_ref, w_ref, o_ref, *, groups):
    cg = w_ref.shape[0]
    d = groups * cg
    for g in range(groups):
        uv = _dot(a_ref[0, :, g * cg:(g + 1) * cg], w_ref[...])
        o_ref[0, :, g * cg:(g + 1) * cg] = uv[:, :cg]
        o_ref[0, :, d + g * cg:d + (g + 1) * cg] = uv[:, cg:]


def _mm_groups(a, w, groups):
    b, s, d = a.shape
    cg = d // groups
    tm = ROW_TILE
    return pl.pallas_call(
        functools.partial(_mm_groups_kernel, groups=groups),
        name="mm_channel_dft",
        grid=(b, s // tm),
        in_specs=[
            pl.BlockSpec((1, tm, d), lambda bi, i: (bi, i, 0)),
            pl.BlockSpec((cg, 2 * cg), lambda bi, i: (0, 0)),
        ],
        out_specs=pl.BlockSpec((1, tm, 2 * d), lambda bi, i: (bi, i, 0)),
        out_shape=jax.ShapeDtypeStruct((b, s, 2 * d), F32),
        compiler_params=_params(("parallel", "parallel")),
    )(a, w)


def _ret_scan_kernel(*refs, heads, dk, dv, reverse, add_in):
    if add_in:
        dec_ref, q_ref, k_ref, v_ref, g_ref, yin_ref, o_ref, s_ref, qd_ref, kd_ref, in_ref, cd_ref = refs
    else:
        dec_ref, q_ref, k_ref, v_ref, g_ref, o_ref, s_ref, qd_ref, kd_ref, in_ref, cd_ref = refs
        yin_ref = None
    c = q_ref.shape[1]
    j = pl.program_id(1)

    @pl.when(j == 0)
    def _():
        s_ref[...] = jnp.zeros_like(s_ref)
        m_col = lax.broadcasted_iota(I32, (c, LANES), 0).astype(F32)
        m_row = lax.broadcasted_iota(I32, (c, c), 0).astype(F32)
        n_row = lax.broadcasted_iota(I32, (c, c), 1).astype(F32)
        for h in range(heads):
            lg = -jnp.exp(dec_ref[:, h:h + 1])
            if reverse:
                q_pow, k_pow, diff = c - m_col, m_col, n_row - m_row
            else:
                q_pow, k_pow, diff = m_col + 1.0, c - 1.0 - m_col, m_row - n_row
            qd_ref[h] = jnp.exp(lg * q_pow)
            kd_ref[h] = jnp.exp(lg * k_pow)
            in_ref[h] = jnp.where(diff >= 0, jnp.exp(lg * jnp.maximum(diff, 0.0)), 0.0)
            cd_ref[h] = jnp.exp(jnp.broadcast_to(lg, (8, LANES)) * float(c))

    for h in range(heads):
        q = q_ref[0, :, h * dk:(h + 1) * dk]
        k = k_ref[0, :, h * dk:(h + 1) * dk]
        v = v_ref[0, :, h * dv:(h + 1) * dv]
        qdec = jnp.concatenate([qd_ref[h]] * (dk // LANES), axis=1)
        kdec = jnp.concatenate([kd_ref[h]] * (dk // LANES), axis=1)
        state = s_ref[h]
        cross = _dot((q.astype(F32) * qdec).astype(BF16), state.astype(BF16))
        scores = _dot_nt(q, k) * in_ref[h]
        o = cross + _dot(scores.astype(BF16), v)
        k_t = (k.astype(F32) * kdec).T.astype(BF16)
        s_ref[h] = state * cd_ref[h][0:1, 0:1] + _dot(k_t, v)
        mu = jnp.mean(o, axis=-1, keepdims=True)
        cen = o - mu
        var = jnp.mean(cen * cen, axis=-1, keepdims=True)
        g = g_ref[0, :, h * dv:(h + 1) * dv].astype(F32)
        y = cen * lax.rsqrt(var + NORM_EPS) * _silu(g)
        if add_in:
            y = y + yin_ref[0, :, h * dv:(h + 1) * dv].astype(F32)
        o_ref[0, :, h * dv:(h + 1) * dv] = y.astype(o_ref.dtype)


def _ret_scan(qk, vg, decay_row, y_in, *, n_lat, n_ctx, heads, dk, dv, reverse):
    b, s, _ = qk.shape
    c = RET_CHUNK
    lat_chunks, ctx_chunks = n_lat // c, n_ctx // c
    steps = lat_chunks + ctx_chunks
    hk, hv = heads * dk, heads * dv
    if reverse:
        chunk = lambda j: steps - 1 - j
    else:
        chunk = lambda j: jnp.where(j < ctx_chunks, lat_chunks + j, j - ctx_chunks)
    gate_blk = 2 if reverse else 1
    in_specs = [
        pl.BlockSpec((1, heads), lambda bi, j: (0, 0)),
        pl.BlockSpec((1, c, hk), lambda bi, j: (bi, chunk(j), 0)),
        pl.BlockSpec((1, c, hk), lambda bi, j: (bi, chunk(j), 1)),
        pl.BlockSpec((1, c, hv), lambda bi, j: (bi, chunk(j), 0)),
        pl.BlockSpec((1, c, hv), lambda bi, j: (bi, chunk(j), gate_blk)),
    ]
    args = [decay_row, qk, qk, vg, vg]
    if y_in is not None:
        in_specs.append(pl.BlockSpec((1, c, hv), lambda bi, j: (bi, chunk(j), 0)))
        args.append(y_in)
    kern = functools.partial(_ret_scan_kernel, heads=heads, dk=dk, dv=dv, reverse=reverse,
                             add_in=y_in is not None)
    return pl.pallas_call(
        kern,
        name="ret_scan_bwd" if reverse else "ret_scan_fwd",
        grid=(b, steps),
        in_specs=in_specs,
        out_specs=pl.BlockSpec((1, c, hv), lambda bi, j: (bi, chunk(j), 0)),
        out_shape=jax.ShapeDtypeStruct((b, s, hv), BF16),
        scratch_shapes=[
            pltpu.VMEM((heads, dk, dv), F32),
            pltpu.VMEM((heads, c, LANES), F32),
            pltpu.VMEM((heads, c, LANES), F32),
            pltpu.VMEM((heads, c, c), F32),
            pltpu.VMEM((heads, 8, LANES), F32),
        ],
        compiler_params=_params(("parallel", "arbitrary")),
    )(*args)


def _win_attn_kernel(sink_ref, q_ref, kc_ref, vc_ref, kp_ref, kq_ref, kn_ref, vp_ref, vq_ref, vn_ref,
                     o_ref, *, lat_tiles, heads, kv_heads, hd):
    qt = pl.program_id(1)
    blk = q_ref.shape[1]
    n_ctx = kc_ref.shape[1]
    grp = heads // kv_heads
    rows = grp * blk
    qi = lax.broadcasted_iota(I32, (rows, blk), 0) % blk
    kj = lax.broadcasted_iota(I32, (rows, blk), 1)
    tq = qt + jnp.zeros((rows, blk), I32)
    ok_cur = tq < lat_tiles
    ok_prev = (kj >= qi) & ok_cur & (tq >= 1)
    ok_next = (kj <= qi) & (tq + 1 < lat_tiles)
    bias = jnp.concatenate([jnp.where(ok, 0.0, NEG_INF).astype(F32) for ok in (ok_prev, ok_cur, ok_next)], axis=1)
    head_row = lax.broadcasted_iota(I32, (rows, 1), 0) // blk
    for kv in range(kv_heads):
        cs = slice(kv * hd, (kv + 1) * hd)
        keys = jnp.concatenate([kp_ref[0, :, cs], kq_ref[0, :, cs], kn_ref[0, :, cs]], axis=0)
        vals = jnp.concatenate([vp_ref[0, :, cs], vq_ref[0, :, cs], vn_ref[0, :, cs]], axis=0)
        q = jnp.concatenate([q_ref[0, :, (kv * grp + g) * hd:(kv * grp + g + 1) * hd] for g in range(grp)], axis=0)
        sink = jnp.zeros((rows, 1), F32)
        for g in range(grp):
            h = kv * grp + g
            sink = jnp.where(head_row == g, sink_ref[:, h:h + 1], sink)
        s_ctx = _dot_nt(q, kc_ref[0, :, cs])
        s_loc = _dot_nt(q, keys) + bias
        m = jnp.maximum(jnp.maximum(jnp.max(s_ctx, axis=-1, keepdims=True), jnp.max(s_loc, axis=-1, keepdims=True)), sink)
        p_ctx = jnp.exp(s_ctx - m)
        p_loc = jnp.exp(s_loc - m)
        den = jnp.sum(p_ctx, axis=-1, keepdims=True) + jnp.sum(p_loc, axis=-1, keepdims=True) + jnp.exp(sink - m)
        o = (_dot(p_ctx.astype(BF16), vc_ref[0, :, cs]) + _dot(p_loc.astype(BF16), vals)) / den
        for g in range(grp):
            h = kv * grp + g
            o_ref[0, :, h * hd:(h + 1) * hd] = o[g * blk:(g + 1) * blk].astype(o_ref.dtype)


def _win_attn(qk, v, sink, *, n_lat, n_ctx, heads, kv_heads, hd):
    b, s, _ = qk.shape
    blk = WIN_BLOCK
    lat_tiles = n_lat // blk
    tiles = s // blk
    kvw = kv_heads * hd
    k_col = (heads * hd) // kvw
    v_col = 0
    ctx_blk = n_lat // n_ctx
    prev = lambda t: jnp.maximum(t - 1, 0)
    nxt = lambda t: jnp.minimum(t + 1, tiles - 1)
    sink_row = jnp.zeros((1, LANES), F32).at[0, :heads].set(sink.astype(F32))
    kern = functools.partial(_win_attn_kernel, lat_tiles=lat_tiles, heads=heads, kv_heads=kv_heads, hd=hd)
    return pl.pallas_call(
        kern,
        name="win_attn",
        grid=(b, tiles),
        in_specs=[
            pl.BlockSpec((1, LANES), lambda bi, t: (0, 0)),
            pl.BlockSpec((1, blk, heads * hd), lambda bi, t: (bi, t, 0)),
            pl.BlockSpec((1, n_ctx, kvw), lambda bi, t: (bi, ctx_blk, k_col)),
            pl.BlockSpec((1, n_ctx, kvw), lambda bi, t: (bi, ctx_blk, v_col)),
            pl.BlockSpec((1, blk, kvw), lambda bi, t: (bi, prev(t), k_col)),
            pl.BlockSpec((1, blk, kvw), lambda bi, t: (bi, t, k_col)),
            pl.BlockSpec((1, blk, kvw), lambda bi, t: (bi, nxt(t), k_col)),
            pl.BlockSpec((1, blk, kvw), lambda bi, t: (bi, prev(t), v_col)),
            pl.BlockSpec((1, blk, kvw), lambda bi, t: (bi, t, v_col)),
            pl.BlockSpec((1, blk, kvw), lambda bi, t: (bi, nxt(t), v_col)),
        ],
        out_specs=pl.BlockSpec((1, blk, heads * hd), lambda bi, t: (bi, t, 0)),
        out_shape=jax.ShapeDtypeStruct((b, s, heads * hd), BF16),
        compiler_params=_params(("parallel", "parallel")),
    )(sink_row, qk, qk, v, qk, qk, qk, v, v, v)


SUBLANES = 8
PACKED_ROWS = 16


def _dft_stage_a_kernel(m_ref, u_ref, v_ref, re_ref, im_ref):
    n1, sub, tc = u_ref.shape[1:]
    rows = n1 * sub
    stacked = jnp.concatenate([u_ref[0].reshape(rows, tc), v_ref[0].reshape(rows, tc)], axis=0)
    out = _dot(m_ref[...], stacked.astype(BF16))
    re_ref[0] = out[:rows].reshape(n1, sub, tc)
    im_ref[0] = out[rows:].reshape(n1, sub, tc)


def _dft_stage_b_kernel(m_ref, re_ref, im_ref, twc_ref, tws_ref, o_ref):
    n2, sub, tc = o_ref.shape[1:]
    reps = tc // LANES
    ar, ai = re_ref[0], im_ref[0]
    twc = jnp.concatenate([twc_ref[...]] * reps, axis=1)
    tws = jnp.concatenate([tws_ref[...]] * reps, axis=1)
    stacked = jnp.concatenate([ar * twc + ai * tws, ai * twc - ar * tws], axis=0).astype(BF16)
    out = _dot(m_ref[...], stacked)
    o_ref[0] = out.reshape(n2, sub, tc).astype(o_ref.dtype)


def _dft_ctx_kernel(m_ref, x_ref, o_ref, *, d):
    x = x_ref[0]
    stacked = jnp.concatenate([x[:, :d], x[:, d:]], axis=0).astype(BF16)
    o_ref[0] = _dot(m_ref[...], stacked).astype(o_ref.dtype)


def _cos_sin(n):
    ang = 2.0 * np.pi * np.outer(np.arange(n), np.arange(n)) / float(n)
    return np.cos(ang), np.sin(ang)


def _fourier_positions(uv, n_lat, n_ctx, d):
    b, s, _ = uv.shape
    n2 = FFT_INNER
    n1 = n_lat // n2
    tc = min(512, d)
    ca, sa = _cos_sin(n1)
    eye = np.eye(SUBLANES)
    ka, ks = np.kron(ca, eye) / math.sqrt(n1), np.kron(sa, eye) / math.sqrt(n1)
    mat_a = jnp.asarray(np.block([[ka, -ks], [-ks, -ka]]), BF16)
    uv4 = uv.reshape(b, s // n2, n2, 2 * d)
    blk_a = (1, n1, SUBLANES, tc)
    a_re, a_im = pl.pallas_call(
        _dft_stage_a_kernel,
        name="dft_stage_a",
        grid=(b, n2 // SUBLANES, d // tc),
        in_specs=[
            pl.BlockSpec(mat_a.shape, lambda bi, cg, jc: (0, 0)),
            pl.BlockSpec(blk_a, lambda bi, cg, jc: (bi, 0, cg, jc)),
            pl.BlockSpec(blk_a, lambda bi, cg, jc: (bi, 0, cg, d // tc + jc)),
        ],
        out_specs=[pl.BlockSpec(blk_a, lambda bi, cg, jc: (bi, 0, cg, jc))] * 2,
        out_shape=[jax.ShapeDtypeStruct((b, n1, n2, d), F32)] * 2,
        compiler_params=_params(("parallel", "parallel", "parallel")),
    )(mat_a, uv4, uv4)
    cb, sb = _cos_sin(n2)
    eye = np.eye(PACKED_ROWS)
    kron_b = lambda m: np.einsum("kc,ab->kabc", m, eye).reshape(n2 * PACKED_ROWS, PACKED_ROWS * n2)
    mat_b = jnp.asarray(np.concatenate([kron_b(cb), kron_b(sb)], axis=1) / math.sqrt(n2), BF16)
    phi = 2.0 * np.pi * np.outer(np.arange(n1), np.arange(n2)) / float(n_lat)
    twc = jnp.asarray(np.repeat(np.cos(phi).reshape(-1, 1), LANES, axis=1), F32)
    tws = jnp.asarray(np.repeat(np.sin(phi).reshape(-1, 1), LANES, axis=1), F32)
    rows = PACKED_ROWS * n2
    y = pl.pallas_call(
        _dft_stage_b_kernel,
        name="dft_stage_b",
        grid=(b, n1 // PACKED_ROWS, d // tc),
        in_specs=[
            pl.BlockSpec(mat_b.shape, lambda bi, kb, jc: (0, 0)),
            pl.BlockSpec((1, rows, tc), lambda bi, kb, jc: (bi, kb, jc)),
            pl.BlockSpec((1, rows, tc), lambda bi, kb, jc: (bi, kb, jc)),
            pl.BlockSpec((rows, LANES), lambda bi, kb, jc: (kb, 0)),
            pl.BlockSpec((rows, LANES), lambda bi, kb, jc: (kb, 0)),
        ],
        out_specs=pl.BlockSpec((1, n2, PACKED_ROWS, tc), lambda bi, kb, jc: (bi, 0, kb, jc)),
        out_shape=jax.ShapeDtypeStruct((b, n2, n1, d), BF16),
        compiler_params=_params(("parallel", "parallel", "parallel")),
    )(mat_b, a_re.reshape(b, n_lat, d), a_im.reshape(b, n_lat, d), twc, tws)
    cc, sc = _cos_sin(n_ctx)
    mat_c = jnp.asarray(np.concatenate([cc, -sc], axis=1) / math.sqrt(n_ctx), BF16)
    y_ctx = pl.pallas_call(
        functools.partial(_dft_ctx_kernel, d=d),
        name="dft_ctx",
        grid=(b,),
        in_specs=[
            pl.BlockSpec(mat_c.shape, lambda bi: (0, 0)),
            pl.BlockSpec((1, n_ctx, 2 * d), lambda bi: (bi, n_lat // n_ctx, 0)),
        ],
        out_specs=pl.BlockSpec((1, n_ctx, d), lambda bi: (bi, 0, 0)),
        out_shape=jax.ShapeDtypeStruct((b, n_ctx, d), BF16),
        compiler_params=_params(("parallel",)),
    )(mat_c, uv)
    return jnp.concatenate([y.reshape(b, n_lat, d), y_ctx], axis=1)


LANE_SHIFT = 7
GATHER_UNROLL = 16
TILES_PER_STEP = 8
FFN_ROW_CHUNK = 384
COMBINE_REGION = 64


def _route_kernel(aff_ref, tri_ref, posc_ref, gate_ref, tst_ref,
                  thr_ref, need_ref, ctie_ref, cpos_ref, *, cap):
    step = pl.program_id(1)
    tile = tri_ref.shape[0]
    tps = posc_ref.shape[1] // tile

    @pl.when(step == 0)
    def _():
        def body(it, thr):
            bits = lax.bitcast_convert_type(aff_ref[0], I32)
            cand = thr | jnp.left_shift(jnp.int32(1), 30 - it)
            cnt = jnp.sum(jnp.where(bits >= cand, 1.0, 0.0), axis=0, keepdims=True)
            return jnp.where(cnt >= cap, cand, thr)

        thr = lax.fori_loop(0, 31, body, jnp.zeros((1, LANES), I32))
        bits = lax.bitcast_convert_type(aff_ref[0], I32)
        above = jnp.sum(jnp.where(bits > thr, 1.0, 0.0), axis=0, keepdims=True)
        thr_ref[...] = thr
        need_ref[...] = float(cap) - above
        ctie_ref[...] = jnp.zeros_like(ctie_ref)
        cpos_ref[...] = jnp.zeros_like(cpos_ref)

    thr = thr_ref[...]
    for u in range(tps):
        rows = slice(u * tile, (u + 1) * tile)
        a = aff_ref[0, pl.ds(pl.multiple_of((step * tps + u) * tile, tile), tile), :]
        bits = lax.bitcast_convert_type(a, I32)
        gt = bits > thr
        eq = bits == thr
        eqf = jnp.where(eq, 1.0, 0.0)
        tie_rank = _dot(tri_ref[...], eqf.astype(BF16)) + ctie_ref[...]
        sel = gt | (eq & (tie_rank < need_ref[...]))
        self_ = jnp.where(sel, 1.0, 0.0)
        start = cpos_ref[...]
        pos = jnp.where(sel, _dot(tri_ref[...], self_.astype(BF16)) + start, -1.0)
        posc_ref[0, rows, :] = pos.astype(I32)
        gate_ref[0, rows, :] = jnp.where(sel, a, 0.0)
        tst_ref[0, u] = jnp.broadcast_to(start, (8, LANES)).astype(I32)
        ctie_ref[...] = ctie_ref[...] + jnp.sum(eqf, axis=0, keepdims=True)
        cpos_ref[...] = start + jnp.sum(self_, axis=0, keepdims=True)


def _route(aff, *, row_off, n, cap):
    b = aff.shape[0]
    tile = ROUTE_TILE
    nt = n // tile
    tps = min(TILES_PER_STEP, nt)
    tri = jnp.asarray(np.tril(np.ones((tile, tile)), -1), BF16)
    return pl.pallas_call(
        functools.partial(_route_kernel, cap=cap),
        name="moe_route",
        grid=(b, nt // tps),
        in_specs=[
            pl.BlockSpec((1, n, LANES), lambda bi, t: (bi, row_off // n, 0)),
            pl.BlockSpec((tile, tile), lambda bi, t: (0, 0)),
        ],
        out_specs=[
            pl.BlockSpec((1, tps * tile, LANES), lambda bi, t: (bi, t, 0)),
            pl.BlockSpec((1, tps * tile, LANES), lambda bi, t: (bi, t, 0)),
            pl.BlockSpec((1, tps, 8, LANES), lambda bi, t: (bi, t, 0, 0)),
        ],
        out_shape=[
            jax.ShapeDtypeStruct((b, n, LANES), I32),
            jax.ShapeDtypeStruct((b, n, LANES), F32),
            jax.ShapeDtypeStruct((b, nt, 8, LANES), I32),
        ],
        scratch_shapes=[pltpu.VMEM((1, LANES), I32)] + [pltpu.VMEM((1, LANES), F32)] * 3,
        compiler_params=_params(("parallel", "arbitrary")),
    )(aff, tri)


def _slot_index_kernel(ts_ref, posc_ref, idx_ref, *, nt, row_off, tile):
    bi, step = pl.program_id(0), pl.program_id(1)

    @pl.when(step == 0)
    def _():
        idx_ref[...] = jnp.zeros_like(idx_ref)

    tps = posc_ref.shape[1] // tile
    slot_rows = idx_ref.shape[2]
    lane = lax.broadcasted_iota(I32, (tile, 2 * LANES), 1)
    for u in range(tps):
        t = step * tps + u
        tok = (row_off + t * tile + lax.broadcasted_iota(I32, (tile, 1), 0)).astype(F32)
        for e in range(N_EXPERTS):
            h0 = jnp.minimum(ts_ref[(bi * nt + t) * N_EXPERTS + e] >> LANE_SHIFT, slot_rows - 2)
            hit = (posc_ref[0, u * tile:(u + 1) * tile, e:e + 1] - h0 * LANES) == lane
            vals = jnp.sum(jnp.where(hit, tok, 0.0), axis=0, keepdims=True)
            two_rows = jnp.concatenate([vals[:, :LANES], vals[:, LANES:]], axis=0).astype(I32)
            idx_ref[0, e, pl.ds(h0, 2), :] = idx_ref[0, e, pl.ds(h0, 2), :] + two_rows


def _slot_index(tstart, posc, *, row_off, n, cap):
    b = posc.shape[0]
    tile = ROUTE_TILE
    nt = n // tile
    slot_rows = max(cap, 2 * LANES) // LANES
    tps = min(TILES_PER_STEP, nt)
    grid_spec = pltpu.PrefetchScalarGridSpec(
        num_scalar_prefetch=1,
        grid=(b, nt // tps),
        in_specs=[pl.BlockSpec((1, tps * tile, LANES), lambda bi, t, ts: (bi, t, 0))],
        out_specs=pl.BlockSpec((1, N_EXPERTS, slot_rows, LANES), lambda bi, t, ts: (bi, 0, 0, 0)),
    )
    idx = pl.pallas_call(
        functools.partial(_slot_index_kernel, nt=nt, row_off=row_off, tile=tile),
        name="moe_slot_index",
        grid_spec=grid_spec,
        out_shape=jax.ShapeDtypeStruct((b, N_EXPERTS, slot_rows, LANES), I32),
        compiler_params=_params(("parallel", "arbitrary")),
    )(tstart, posc)
    return idx.reshape(b, N_EXPERTS, slot_rows * LANES)


def _gather_kernel(idx_ref, h_ref, o_ref, buf_ref, sem):
    bi = pl.program_id(1)
    n_rows = buf_ref.shape[0]

    def row_copy(src_row, dst_row, rows):
        return pltpu.make_async_copy(h_ref.at[bi, pl.ds(src_row, rows)], buf_ref.at[pl.ds(dst_row, rows)], sem)

    def issue(g, carry):
        for k in range(GATHER_UNROLL):
            s = g * GATHER_UNROLL + k
            row_copy(idx_ref[0, 0, s], s, 1).start(priority=k % 2)
        return carry

    lax.fori_loop(0, n_rows // GATHER_UNROLL, issue, 0)
    row_copy(0, 0, n_rows).wait()
    o_ref[0, 0] = _unpack_bf16_pairs(buf_ref[...].reshape(n_rows, -1))


def _gather(idx, h):
    b, _, sub, lanes = h.shape
    d = 2 * sub * lanes
    slots = idx.shape[2]
    assert slots % GATHER_UNROLL == 0
    return pl.pallas_call(
        _gather_kernel,
        name="moe_gather",
        grid=(N_EXPERTS, b),
        in_specs=[
            pl.BlockSpec((1, 1, slots), lambda ei, bi: (bi * N_EXPERTS + ei, 0, 0), memory_space=pltpu.SMEM),
            pl.BlockSpec(memory_space=pl.ANY),
        ],
        out_specs=pl.BlockSpec((1, 1, slots, d), lambda ei, bi: (ei, bi, 0, 0)),
        out_shape=jax.ShapeDtypeStruct((N_EXPERTS, b, slots, d), BF16),
        scratch_shapes=[pltpu.VMEM((slots, sub, lanes), I32), pltpu.SemaphoreType.DMA(())],
        compiler_params=_params(("arbitrary", "arbitrary")),
    )(idx.reshape(b * N_EXPERTS, 1, slots), h)


def _ffn_kernel(x_ref, wg_ref, wu_ref, wd_ref, ye_ref, hm_ref, *, n_up):
    st = pl.program_id(1)
    bsz, rows, d = x_ref.shape[1:]
    tf = wg_ref.shape[3]
    chunk = FFN_ROW_CHUNK if rows % FFN_ROW_CHUNK == 0 else rows
    spans = [(bi, r0) for bi in range(bsz) for r0 in range(0, rows, chunk)]

    @pl.when(st < n_up)
    def _():
        wg = wg_ref[0, 0].astype(BF16)
        wu = wu_ref[0, 0].astype(BF16)
        for bi, r0 in spans:
            x = x_ref[0, bi, r0:r0 + chunk, :]
            hm_ref[st, bi * rows + r0:bi * rows + r0 + chunk, :] = (_silu(_dot(x, wg)) * _dot(x, wu)).astype(BF16)

    @pl.when(st >= n_up)
    def _():
        wd = [wd_ref[0, 0, c * tf:(c + 1) * tf, :].astype(BF16) for c in range(n_up)]
        for bi, r0 in spans:
            m0 = bi * rows + r0
            y = _dot(hm_ref[0, m0:m0 + chunk, :], wd[0])
            for c in range(1, n_up):
                y = y + _dot(hm_ref[c, m0:m0 + chunk, :], wd[c])
            ye_ref[0, bi, r0:r0 + chunk, :] = y.astype(ye_ref.dtype)


def _ffn(xe, w_gate, w_up, w_down, layer):
    e, b, rows, d = xe.shape
    f = w_gate.shape[3]
    tf = min(512, f)
    tdc = min(512, d)
    n_up, n_down = f // tf, d // tdc
    up = lambda st: jnp.minimum(st, n_up - 1)
    down = lambda st: jnp.maximum(st - n_up, 0)
    return pl.pallas_call(
        functools.partial(_ffn_kernel, n_up=n_up),
        name="moe_ffn",
        grid=(e, n_up + n_down),
        in_specs=[
            pl.BlockSpec((1, b, rows, d), lambda ei, st: (ei, 0, 0, 0)),
            pl.BlockSpec((1, 1, d, tf), lambda ei, st: (layer, ei, 0, up(st))),
            pl.BlockSpec((1, 1, d, tf), lambda ei, st: (layer, ei, 0, up(st))),
            pl.BlockSpec((1, 1, f, tdc), lambda ei, st: (layer, ei, 0, down(st))),
        ],
        out_specs=pl.BlockSpec((1, b, rows, tdc), lambda ei, st: (ei, 0, 0, down(st))),
        out_shape=jax.ShapeDtypeStruct((e, b, rows, d), BF16),
        scratch_shapes=[pltpu.VMEM((n_up, b * rows, tf), BF16)],
        compiler_params=_params(("parallel", "arbitrary")),
    )(xe, w_gate, w_up, w_down)


def _window_start(ts_ref, idx, cap_rows, win):
    a0 = jnp.minimum(ts_ref[idx] & (-SLOT_ALIGN), cap_rows - win)
    return pl.multiple_of(a0, SLOT_ALIGN)


def _combine_kernel(ts_ref, ye_ref, posc_ref, gate_ref, x_ref, g_ref, o_ref, *, nt, win, cap, n_starts, tile):
    bi, step = pl.program_id(0), pl.program_id(2)
    cap_rows = ye_ref.shape[2]
    tps = posc_ref.shape[1] // tile
    reg = COMBINE_REGION
    assert 2 * reg == LANES and reg <= cap_rows

    for u in range(tps):
        t = step * tps + u
        rows = slice(u * tile, (u + 1) * tile)
        base = (bi * nt + t) * N_EXPERTS
        starts, fits = [], None
        for e in range(N_EXPERTS):
            a0 = _window_start(ts_ref, base + e, cap_rows, reg)
            nxt = ts_ref[jnp.minimum(base + N_EXPERTS + e, n_starts - 1)]
            end = jnp.where(t + 1 < nt, nxt, cap)
            ok = end - a0 <= reg
            fits = ok if fits is None else jnp.logical_and(fits, ok)
            starts.append(a0)

        lane = lax.broadcasted_iota(I32, (tile, LANES), 1)
        upper = lane >= reg
        weights, slabs = [], []
        for p in range(N_EXPERTS // 2):
            e0, e1 = 2 * p, 2 * p + 1
            slot = jnp.where(upper, starts[e1] - reg, starts[e0]) + lane
            pcol = jnp.where(upper, posc_ref[0, rows, e1:e1 + 1], posc_ref[0, rows, e0:e0 + 1])
            gcol = jnp.where(upper, gate_ref[0, rows, e1:e1 + 1], gate_ref[0, rows, e0:e0 + 1])
            weights.append(jnp.where(pcol == slot, gcol, 0.0).astype(BF16))
            slabs += [ye_ref[e0, 0, pl.ds(starts[e0], reg), :], ye_ref[e1, 0, pl.ds(starts[e1], reg), :]]
        acc = _dot(jnp.concatenate(weights, axis=1), jnp.concatenate(slabs, axis=0))
        o_ref[0, rows, :] = x_ref[0, rows, :] + g_ref[0, 0] * acc

        @pl.when(jnp.logical_not(fits))
        def _():
            acc = jnp.zeros((tile, o_ref.shape[2]), F32)
            lane = lax.broadcasted_iota(I32, (tile, win), 1)
            for e in range(N_EXPERTS):
                a0 = _window_start(ts_ref, base + e, cap_rows, win)
                pcol = posc_ref[0, rows, e:e + 1]
                gcol = gate_ref[0, rows, e:e + 1]
                w = jnp.where(pcol == a0 + lane, gcol, 0.0).astype(BF16)
                acc = acc + _dot(w, ye_ref[e, 0, pl.ds(a0, win), :])
            o_ref[0, rows, :] = x_ref[0, rows, :] + g_ref[0, 0] * acc


def _combine(tstart, ye, posc, gate, x, g2, *, row_off, n, region, row0, cap_rows):
    b, s, d = x.shape
    tile = ROUTE_TILE
    nt = n // tile
    dc = 512 if d % 512 == 0 else d
    win = min(2 * tile, cap_rows)
    tps = min(TILES_PER_STEP, nt)
    rows = tps * tile
    assert row_off % rows == 0
    off = row_off // rows
    grid_spec = pltpu.PrefetchScalarGridSpec(
        num_scalar_prefetch=1,
        grid=(b, d // dc, nt // tps),
        in_specs=[
            pl.BlockSpec((N_EXPERTS, 1, cap_rows, dc), lambda bi, c, t, ts: (0, bi, row0 // cap_rows, c)),
            pl.BlockSpec((1, rows, LANES), lambda bi, c, t, ts: (bi, t, 0)),
            pl.BlockSpec((1, rows, LANES), lambda bi, c, t, ts: (bi, t, 0)),
            pl.BlockSpec((1, rows, dc), lambda bi, c, t, ts: (bi, off + t, c)),
            pl.BlockSpec((1, 1, 1, dc), lambda bi, c, t, ts: (bi, region, 0, c)),
        ],
        out_specs=pl.BlockSpec((1, rows, dc), lambda bi, c, t, ts: (bi, off + t, c)),
    )
    return pl.pallas_call(
        functools.partial(_combine_kernel, nt=nt, win=win, cap=CAPACITY_FACTOR * n // N_EXPERTS,
                          n_starts=b * nt * N_EXPERTS, tile=tile),
        name="moe_combine",
        grid_spec=grid_spec,
        out_shape=jax.ShapeDtypeStruct((b, s, d), F32),
        input_output_aliases={4: 0},
        compiler_params=_params(("parallel", "parallel", "arbitrary")),
    )(tstart, ye, posc, gate, x, g2)


def _moe(x, h, aff, g2, w_gate, w_up, w_down, layer, token_sets):
    routed, row0 = [], 0
    for row_off, n, region in token_sets:
        cap = CAPACITY_FACTOR * n // N_EXPERTS
        cap_rows = -(-cap // LANES) * LANES
        posc, gate, tst = _route(aff, row_off=row_off, n=n, cap=cap)
        tstart = tst[:, :, 0, :N_EXPERTS].reshape(-1)
        idx = _slot_index(tstart, posc, row_off=row_off, n=n, cap=cap)
        routed.append((row_off, n, region, row0, cap_rows, tstart, posc, gate, idx))
        row0 += cap_rows
    idx_all = jnp.concatenate([r[8][:, :, :r[4]] for r in routed], axis=2)
    ye = _ffn(_gather(idx_all, h), w_gate, w_up, w_down, layer)
    for row_off, n, region, r0, cap_rows, tstart, posc, gate, _ in routed:
        x = _combine(tstart, ye, posc, gate, x, g2, row_off=row_off, n=n, region=region, row0=r0, cap_rows=cap_rows)
    return x


def _rope_tables(n_lat, n_ctx, head_dim):
    quarter = head_dim // 4
    inv = ROPE_BASE ** (-jnp.arange(quarter, dtype=F32) / quarter)
    pos = jnp.arange(n_lat)
    row = (pos // GRID_W).astype(F32)[:, None] * inv[None, :]
    col = (pos % GRID_W).astype(F32)[:, None] * inv[None, :]
    cos = jnp.concatenate([jnp.cos(row), jnp.cos(row), jnp.cos(col), jnp.cos(col)], axis=1)
    sin = jnp.concatenate([-jnp.sin(row), jnp.sin(row), -jnp.sin(col), jnp.sin(col)], axis=1)
    cos = jnp.concatenate([cos, jnp.ones((n_ctx, head_dim), F32)], axis=0)
    sin = jnp.concatenate([sin, jnp.zeros((n_ctx, head_dim), F32)], axis=0)
    return cos, sin


def kernel(x, c, ctx, c_ctx, w_mod, b_mod, norm_gain, final_gain, ret_w_in, ret_w_out, ret_decay,
           win_w_qkv, win_w_o, win_sink, fno_w_o, router_w, exp_w_gate, exp_w_up, exp_w_down):
    b, n_lat, d = x.shape
    n_ctx = ctx.shape[1]
    depth = w_mod.shape[0]
    s = n_lat + n_ctx
    assert s % ROW_TILE == 0 and n_lat % n_ctx == 0
    assert n_lat % (FFT_INNER * 4) == 0 and b + 1 <= 8 and n_ctx % ROUTE_TILE == 0

    xs = jnp.concatenate([x, ctx], axis=1)
    cvec = jnp.zeros((8, d), F32).at[:b].set(c).at[b].set(c_ctx)
    mod = _modulation(cvec, w_mod, b_mod)

    def mod_pair(i, k):
        lat = mod[i, :b, k * d:(k + 1) * d]
        cx = jnp.broadcast_to(mod[i, b, k * d:(k + 1) * d], (b, d))
        return jnp.stack([lat, cx], axis=1).reshape(b, 2, 1, d)

    ret_dk = d // RET_HEADS
    ret_dv = 2 * ret_dk
    win_hd = d // WIN_HEADS
    ret_cos, ret_sin = _rope_tables(n_lat, n_ctx, ret_dk)
    win_cos, win_sin = _rope_tables(n_lat, n_ctx, win_hd)

    ret_w_in_bf, ret_w_out_bf = ret_w_in.astype(BF16), ret_w_out.astype(BF16)
    win_w_qkv_bf, win_w_o_bf, fno_w_o_bf = win_w_qkv.astype(BF16), win_w_o.astype(BF16), fno_w_o.astype(BF16)

    for i in range(depth):
        kind, j = i % N_MIXERS, i // N_MIXERS
        last = i == depth - 1
        sh1, sc1, g1, sh2, sc2, g2 = [mod_pair(i, k) for k in range(6)]
        h = _norm_mod(xs, norm_gain[i, 0], sh1, sc1, n_lat)
        if kind == 0:
            hk, hv = RET_HEADS * ret_dk, RET_HEADS * ret_dv
            qk = _mm_proj(h, ret_w_in_bf, j, 0, 2 * hk, PROJ_COL_TILE,
                          rope=(ret_cos, ret_sin, hk, 1.0, ret_dk ** -0.5, ret_dk))
            vg = _mm_proj(h, ret_w_in_bf, j, 2 * hk, 3 * hv, PROJ_COL_TILE)
            scan = functools.partial(_ret_scan, n_lat=n_lat, n_ctx=n_ctx, heads=RET_HEADS, dk=ret_dk, dv=ret_dv)
            y_b = scan(qk, vg, ret_decay[j, 1:2], None, reverse=True)
            y = scan(qk, vg, ret_decay[j, 0:1], y_b, reverse=False)
            w_out = ret_w_out_bf
        elif kind == 1:
            nq, nkv = WIN_HEADS * win_hd, WIN_KV_HEADS * win_hd
            qk = _mm_proj(h, win_w_qkv_bf, j, 0, nq + nkv, COL_TILE,
                          rope=(win_cos, win_sin, nq, win_hd ** -0.5, 1.0, win_hd))
            v = _mm_proj(h, win_w_qkv_bf, j, nq + nkv, nkv, COL_TILE)
            y = _win_attn(qk, v, win_sink[j], n_lat=n_lat, n_ctx=n_ctx, heads=WIN_HEADS,
                          kv_heads=WIN_KV_HEADS, hd=win_hd)
            w_out = win_w_o_bf
        else:
            cg = d // FOURIER_GROUPS
            cc, sc = _cos_sin(cg)
            w_ch = jnp.asarray(np.concatenate([cc, sc], axis=1) / math.sqrt(cg), BF16)
            uv = _mm_groups(h, w_ch, FOURIER_GROUPS)
            y = _fourier_positions(uv, n_lat, n_ctx, d)
            w_out = fno_w_o_bf
        xs = _mm_res(y, w_out, j, xs, g1[:, 0], g1[:, 1], n_lat)
        h2, aff = _norm_mod(xs, norm_gain[i, 1], sh2, sc2, n_lat, router_w=router_w[i])
        token_sets = [(0, n_lat, 0)] + ([] if last else [(n_lat, n_ctx, 1)])
        xs = _moe(xs, h2, aff, g2, exp_w_gate, exp_w_up, exp_w_down, i, token_sets)

    zeros = jnp.zeros((b, 2, 1, d), F32)
    return _norm_mod(xs, final_gain, zeros, zeros, n_lat, out_dtype=F32, rows=n_lat)
```
